```python
import math
import jax, jax.numpy as jnp
from jax import lax
import numpy as np

D_MODEL = 1024
BATCH = 2
SEQ = 8192
DEPTH = 1
DEC_BATCH = 32
DEC_SEQ = 4
PAST_LEN = 16384
PAGE_SIZE = 128

RET_HEADS = 4
RET_DK = 128
RET_DV = 256
RET_CHUNK = 128
ROPE_BASE = 10000.0
ATT_GROUPS = ((128, 1), (512, 4), (2048, 16))
N_GROUPS = 3
ATT_HEADS = 8
ATT_DH = 64
Q_BLOCK = 128
REL_BUCKETS = 32
REL_MAX_DIST = 2048
EPS = 1e-6

RET_QK_W = RET_HEADS * RET_DK
RET_V_W = RET_HEADS * RET_DV
ATT_W = N_GROUPS * ATT_HEADS * ATT_DH
ATT_OUT_W = ATT_HEADS * ATT_DH
IN_SPLITS = (RET_QK_W, RET_QK_W, RET_V_W, RET_V_W, ATT_W, ATT_W, ATT_W, ATT_OUT_W, D_MODEL, D_MODEL)
IN_W = 2 * RET_QK_W + 2 * RET_V_W + 3 * ATT_W + ATT_OUT_W + 2 * D_MODEL

kernel_name = 'retention_dilated_attn_hybrid_step'


def rms_norm(x, g):
    x32 = x.astype(jnp.float32)
    y = x32 * lax.rsqrt(jnp.mean(x32 * x32, axis=-1, keepdims=True) + EPS)
    return (y * g.astype(jnp.float32)).astype(x.dtype)


def head_rms(x, g):
    x32 = x.astype(jnp.float32)
    y = x32 * lax.rsqrt(jnp.mean(x32 * x32, axis=-1, keepdims=True) + EPS)
    return (y * g.astype(jnp.float32)).astype(x.dtype)


def head_group_norm(o, g):
    mu = jnp.mean(o, axis=-1, keepdims=True)
    var = jnp.mean(jnp.square(o - mu), axis=-1, keepdims=True)
    return (o - mu) * lax.rsqrt(var + EPS) * g.astype(jnp.float32).reshape(RET_HEADS, RET_DV)


def split_cols(p):
    out, s = [], 0
    for w in IN_SPLITS:
        out.append(p[..., s:s + w])
        s += w
    return out


def rotary(x, pos):
    half = x.shape[-1] // 2
    inv = ROPE_BASE ** (-jnp.arange(half, dtype=jnp.float32) / half)
    ang = pos[:, None] * inv[None, :]
    cos = jnp.cos(ang)[None, :, None, :]
    sin = jnp.sin(ang)[None, :, None, :]
    x32 = x.astype(jnp.float32)
    x1, x2 = x32[..., :half], x32[..., half:]
    return jnp.concatenate([x1 * cos - x2 * sin, x1 * sin + x2 * cos], axis=-1).astype(x.dtype)


def retention_scan(q, k, v, state0):
    B, T, H, _ = q.shape
    C = RET_CHUNK if T % RET_CHUNK == 0 else T
    n = T // C
    log_g = jnp.log1p(-(2.0 ** (-5.0 - jnp.arange(H, dtype=jnp.float32))))
    i = jnp.arange(C, dtype=jnp.float32)
    diff = i[:, None] - i[None, :]
    decay = jnp.where(diff[None] >= 0, jnp.exp(jnp.maximum(diff, 0.0)[None] * log_g[:, None, None]), 0.0)
    q_decay = jnp.exp((i + 1.0)[:, None] * log_g[None, :])
    k_decay = jnp.exp((C - 1.0 - i)[:, None] * log_g[None, :])
    chunk_decay = jnp.exp(C * log_g)

    def step(S, qkv):
        qc, kc, vc = qkv
        scores = jnp.einsum('bqhd,bkhd->bhqk', qc, kc) * decay[None]
        o = (jnp.einsum('bhqk,bkhe->bqhe', scores, vc)
             + jnp.einsum('bqhd,bhde->bqhe', qc, S) * q_decay[None, :, :, None])
        S = S * chunk_decay[None, :, None, None] + jnp.einsum('bkhd,bkhe->bhde', kc * k_decay[None, :, :, None], vc)
        return S, o

    xs = tuple(a.astype(jnp.float32).reshape(B, n, C, H, a.shape[-1]).transpose(1, 0, 2, 3, 4) for a in (q, k, v))
    S, o = lax.scan(step, state0.astype(jnp.float32), xs)
    o = o.transpose(1, 0, 2, 3, 4).reshape(B, T, H, RET_DV)
    return o, S


def t5_bucket(dist):
    max_exact = REL_BUCKETS // 2
    d = jnp.maximum(dist.astype(jnp.float32), 1.0)
    large = max_exact + (jnp.log(d / max_exact) / math.log(REL_MAX_DIST / max_exact)
                         * (REL_BUCKETS - max_exact)).astype(jnp.int32)
    large = jnp.minimum(large, REL_BUCKETS - 1)
    return jnp.where(dist < max_exact, dist, large)


def group_bias(rel_bias, g, win, dil):
    dist = dil * jnp.arange(win // dil + 1, dtype=jnp.int32)
    b = rel_bias[t5_bucket(dist)][:, g * ATT_HEADS:(g + 1) * ATT_HEADS]
    return b.astype(jnp.float32).T


def dilated_group(q, kv_ext, start, dil, bias):
    tq = q.shape[1]
    M = bias.shape[1]
    idx = start + jnp.arange(tq)[:, None] - dil * jnp.arange(M)[None, :]
    valid = idx >= 0
    kv = jnp.take(kv_ext, jnp.maximum(idx, 0), axis=1)
    logits = jnp.einsum('bqhd,bqmhd->bhqm', q, kv[:, :, :, 0]).astype(jnp.float32) + bias[None, :, None, :]
    logits = jnp.where(valid[None, None], logits, -jnp.inf)
    lse = jax.nn.logsumexp(logits, axis=-1)
    p = jnp.exp(logits - lse[..., None]).astype(kv.dtype)
    o = jnp.einsum('bhqm,bqmhd->bqhd', p, kv[:, :, :, 1])
    return o, lse.transpose(0, 2, 1)


def dilated_attention(q, kv_exts, buf_lens, rel_bias):
    B, T = q.shape[:2]
    biases = [group_bias(rel_bias, g, w, d) for g, (w, d) in enumerate(ATT_GROUPS)]

    def attend(q_blk, t0):
        outs, lses = [], []
        for g, (win, dil) in enumerate(ATT_GROUPS):
            o, lse = dilated_group(q_blk[:, :, g], kv_exts[g], buf_lens[g] + t0, dil, biases[g])
            outs.append(o)
            lses.append(lse)
        w = jax.nn.softmax(jnp.stack(lses, axis=2), axis=2)
        o = jnp.einsum('bqgh,bqghd->bqhd', w, jnp.stack(outs, axis=2).astype(jnp.float32))
        return o.astype(q_blk.dtype)

    if T > Q_BLOCK and T % Q_BLOCK == 0:
        nb = T // Q_BLOCK
        qb = q.reshape(B, nb, Q_BLOCK, N_GROUPS, ATT_HEADS, ATT_DH).transpose(1, 0, 2, 3, 4, 5)
        out = lax.map(lambda a: attend(a[0], a[1]), (qb, jnp.arange(nb, dtype=jnp.int32) * Q_BLOCK))
        return out.transpose(1, 0, 2, 3, 4).reshape(B, T, ATT_HEADS, ATT_DH)
    return attend(q, 0)


def mixer_layer(x, pos0, ret_state0, kv_bufs, rel_bias, w_norm, w_in, q_norm, k_norm,
                ret_norm, w_proj_ret, w_proj_att, w_out):
    B, T, _ = x.shape
    h = rms_norm(x, w_norm)
    rq, rk, rv, rg, aq, ak, av, ag, ga, gb = split_cols(h @ w_in)
    pos = pos0 + jnp.arange(T, dtype=jnp.float32)
    rq = rotary(rq.reshape(B, T, RET_HEADS, RET_DK), pos)
    rk = rotary(rk.reshape(B, T, RET_HEADS, RET_DK), pos) * (RET_DK ** -0.5)
    rv = rv.reshape(B, T, RET_HEADS, RET_DV)
    ret_o, ret_state = retention_scan(rq, rk, rv, ret_state0)
    ret_o = head_group_norm(ret_o, ret_norm).reshape(B, T, RET_V_W).astype(x.dtype)
    o_a = (jax.nn.silu(rg) * ret_o) @ w_proj_ret
    aq = head_rms(aq.reshape(B, T, N_GROUPS, ATT_HEADS, ATT_DH), q_norm) * (ATT_DH ** -0.5)
    ak = head_rms(ak.reshape(B, T, N_GROUPS, ATT_HEADS, ATT_DH), k_norm)
    av = av.reshape(B, T, N_GROUPS, ATT_HEADS, ATT_DH)
    kv_new = jnp.stack([ak, av], axis=3)
    kv_exts, buf_lens, kv_rows = [], [], []
    for g, (win, dil) in enumerate(ATT_GROUPS):
        new_g = kv_new[:, :, g]
        if kv_bufs is None:
            kv_exts.append(new_g)
            buf_lens.append(0)
            kv_rows.append(new_g[:, T - min(win, T):])
        else:
            buf = kv_bufs[g].astype(new_g.dtype)
            kv_exts.append(jnp.concatenate([buf, new_g], axis=1))
            buf_lens.append(buf.shape[1])
            kv_rows.append(new_g)
    att_o = dilated_attention(aq, kv_exts, buf_lens, rel_bias).reshape(B, T, ATT_OUT_W)
    o_b = (jax.nn.silu(ag) * att_o) @ w_proj_att
    merged = jax.nn.sigmoid(ga) * o_a + jax.nn.sigmoid(gb) * o_b
    return x + merged @ w_out, ret_state, kv_rows


def setup_inputs(seed: int = 0) -> dict:
    key = jax.random.key(seed)
    ks = jax.random.split(key, 16)
    f32 = jnp.float32
    nrm = lambda k, s: jax.random.normal(k, s, f32)
    caches = [nrm(ks[2 + g], (DEPTH, DEC_BATCH, min(w, PAST_LEN), 2, ATT_HEADS, ATT_DH))
              for g, (w, d) in enumerate(ATT_GROUPS)]
    return {
        'x_prompt': nrm(ks[0], (BATCH, SEQ, D_MODEL)),
        'x_sample': nrm(ks[1], (DEC_BATCH, DEC_SEQ, D_MODEL)),
        'cache_kv_w128': caches[0],
        'cache_kv_w512': caches[1],
        'cache_kv_w2048': caches[2],
        'state_retention': 0.5 * nrm(ks[5], (DEPTH, DEC_BATCH, RET_HEADS, RET_DK, RET_DV)),
        'w_norm': 1.0 + 0.02 * nrm(ks[6], (DEPTH, D_MODEL)),
        'w_in': nrm(ks[7], (DEPTH, D_MODEL, IN_W)) * D_MODEL ** -0.5,
        'q_norm': 1.0 + 0.02 * nrm(ks[8], (DEPTH, ATT_DH)),
        'k_norm': 1.0 + 0.02 * nrm(ks[9], (DEPTH, ATT_DH)),
        'rel_bias': 0.5 * nrm(ks[10], (REL_BUCKETS, N_GROUPS * ATT_HEADS)),
        'ret_norm': 1.0 + 0.02 * nrm(ks[11], (DEPTH, RET_V_W)),
        'w_proj_ret': nrm(ks[12], (DEPTH, RET_V_W, D_MODEL)) * RET_V_W ** -0.5,
        'w_proj_att': nrm(ks[13], (DEPTH, ATT_OUT_W, D_MODEL)) * ATT_OUT_W ** -0.5,
        'w_out': nrm(ks[14], (DEPTH, D_MODEL, D_MODEL)) * D_MODEL ** -0.5,
    }


def reference(x_prompt, x_sample, cache_kv_w128, cache_kv_w512, cache_kv_w2048, state_retention,
              w_norm, w_in, q_norm, k_norm, rel_bias, ret_norm, w_proj_ret, w_proj_att, w_out):
    y_p, y_s = x_prompt, x_sample
    rs_p, rs_s = [], []
    kvp = [[], [], []]
    kvs = [[], [], []]
    for l in range(DEPTH):
        lw = (w_norm[l], w_in[l], q_norm[l], k_norm[l], ret_norm[l], w_proj_ret[l], w_proj_att[l], w_out[l])
        zero_state = jnp.zeros((x_prompt.shape[0], RET_HEADS, RET_DK, RET_DV), jnp.float32)
        y_p, st_p, rows_p = mixer_layer(y_p, 0, zero_state, None, rel_bias, *lw)
        bufs = (cache_kv_w128[l], cache_kv_w512[l], cache_kv_w2048[l])
        y_s, st_s, rows_s = mixer_layer(y_s, PAST_LEN, state_retention[l], bufs, rel_bias, *lw)
        rs_p.append(st_p)
        rs_s.append(st_s)
        for g in range(N_GROUPS):
            kvp[g].append(rows_p[g])
            kvs[g].append(rows_s[g])
    ret_state_prompt = jnp.stack(rs_p)
    ret_state_sample = jnp.stack(rs_s)
    kv_w128_prompt = jnp.stack(kvp[0])
    kv_w512_prompt = jnp.stack(kvp[1])
    kv_w2048_prompt = jnp.stack(kvp[2])
    kv_w128_sample = jnp.stack(kvs[0])
    kv_w512_sample = jnp.stack(kvs[1])
    kv_w2048_sample = jnp.stack(kvs[2])
    return (y_p, y_s, ret_state_prompt, ret_state_sample, kv_w128_prompt, kv_w512_prompt, kv_w2048_prompt, kv_w128_sample, kv_w512_sample, kv_w2048_sample)
```

```python
import functools
import math

import jax
import jax.numpy as jnp
from jax import lax
from jax.experimental import pallas as pl
from jax.experimental.pallas import tpu as pltpu

D_MODEL = 1024
PAST_LEN = 16384
RET_HEADS = 4
RET_DK = 128
RET_DV = 256
RET_CHUNK = 128
ROPE_BASE = 10000.0
ATT_GROUPS = ((128, 1), (512, 4), (2048, 16))
N_GROUPS = 3
ATT_HEADS = 8
ATT_DH = 64
REL_BUCKETS = 32
REL_MAX_DIST = 2048
EPS = 1e-6

RET_QK_W = RET_HEADS * RET_DK
RET_V_W = RET_HEADS * RET_DV
ATT_W = N_GROUPS * ATT_HEADS * ATT_DH
ATT_OUT_W = ATT_HEADS * ATT_DH
IN_W = 2 * RET_QK_W + 2 * RET_V_W + 3 * ATT_W + ATT_OUT_W + 2 * D_MODEL

LANES = 128
TILE = 512
RET_OUT_W = 2 * RET_QK_W + 2 * RET_V_W
QKV_W = 3 * ATT_OUT_W
GATE_W = 2 * D_MODEL + ATT_OUT_W
ATT_BLOCK = 128
N_KEYS = ATT_BLOCK + 1
SAMPLE_PAD = 8
NEG = -1e30
VMEM_LIMIT = 48 * 1024 * 1024

F32 = jnp.float32
BF16 = jnp.bfloat16


def _nt_dot(a, b):
    return lax.dot_general(a, b, (((1,), (1,)), ((), ())), preferred_element_type=F32)


def _dot(a, b):
    return jnp.dot(a, b, preferred_element_type=F32)


def _proj_schedule():
    t = lambda col: col // TILE
    roles, w_tiles = [], []
    for k, (col, epi) in enumerate([(0, "rot_q"), (RET_QK_W, "rot_k")]):
        roles.append((0, k, epi)); w_tiles.append(t(col))
    for k in range(2 * RET_V_W // TILE):
        roles.append((0, 2 + k, "plain")); w_tiles.append(t(2 * RET_QK_W) + k)
    att0 = 2 * RET_QK_W + 2 * RET_V_W
    for g in range(N_GROUPS):
        for kind, epi in enumerate(("norm_q", "norm_k", "plain")):
            roles.append((1 + g, kind, epi)); w_tiles.append(t(att0 + kind * ATT_W + g * ATT_OUT_W))
    gate0 = att0 + 3 * ATT_W
    for k in range(2 * D_MODEL // TILE):
        roles.append((4, k, "plain")); w_tiles.append(t(gate0 + ATT_OUT_W) + k)
    roles.append((4, 2 * D_MODEL // TILE, "plain")); w_tiles.append(t(gate0))
    return roles, w_tiles


_ROLE, _W_TILE = _proj_schedule()
_N_STEPS = len(_ROLE)
_FIRST_STEP = [min(j for j, r in enumerate(_ROLE) if r[0] == o) for o in range(5)]
_N_TILES = [sum(1 for r in _ROLE if r[0] == o) for o in range(5)]


def _in_proj_kernel(x_ref, wn_ref, w_ref, cos_ref, sin_ref, qg_ref, kg_ref, seg_ref,
                    ret_ref, a0_ref, a1_ref, a2_ref, gate_ref, h_ref, y_ref, *, dils):
    j = pl.program_id(1)
    out_refs = (ret_ref, a0_ref, a1_ref, a2_ref, gate_ref)

    @pl.when(j == 0)
    def _():
        x = x_ref[...]
        ms = jnp.mean(x * x, axis=-1, keepdims=True)
        h_ref[...] = (x * lax.rsqrt(ms + EPS) * wn_ref[...]).astype(BF16)

    y = _dot(h_ref[...], w_ref[...])
    tm = y.shape[0]

    def rotary(scale):
        cos, sin = cos_ref[...], sin_ref[...]
        parts = []
        for hh in range(TILE // RET_DK):
            yh = y[:, hh * RET_DK:(hh + 1) * RET_DK]
            parts.append((yh * cos + pltpu.roll(yh, RET_DK // 2, axis=1) * sin) * scale)
        return jnp.concatenate(parts, axis=1)

    def head_rms(gain):
        y2 = (y * y).astype(BF16)
        half = TILE // 2
        ms = jnp.concatenate([_dot(y2[:, :half], seg_ref[...]), _dot(y2[:, half:], seg_ref[...])], axis=1)
        return y * lax.rsqrt(ms + EPS) * gain

    epilogues = {
        "plain": lambda: y,
        "rot_q": lambda: rotary(1.0),
        "rot_k": lambda: rotary(RET_DK ** -0.5),
        "norm_q": lambda: head_rms(qg_ref[...]),
        "norm_k": lambda: head_rms(kg_ref[...]),
    }

    for jj, (out_idx, _, epi) in enumerate(_ROLE):
        @pl.when(j == jj)
        def _(out_idx=out_idx, epi=epi):
            val = epilogues[epi]()
            o_ref = out_refs[out_idx]
            dil = dils[out_idx - 1] if 1 <= out_idx <= N_GROUPS else 0
            if dil == 0:
                o_ref[...] = val
            elif dil == 1:
                o_ref[0, 0] = val
            else:
                for c in range(TILE // LANES):
                    cs = slice(c * LANES, (c + 1) * LANES)
                    y_ref[c] = val[:, cs]
                    for r in range(dil):
                        o_ref[0, r, :, cs] = y_ref[c, pl.ds(r, tm // dil, stride=dil), :]


def _in_proj(x2d, seq_len, dils, wn, w_bf, cos_t, sin_t, qg, kg, seg, tm):
    n = x2d.shape[0]
    batch = n // seq_len
    per_seq = seq_len // tm
    n_pos_blocks = cos_t.shape[0] // tm

    def w_map(i, j):
        jr = j - _FIRST_STEP[1]
        att_tile = _W_TILE[_FIRST_STEP[1]] + (jr % 3) * N_GROUPS + jr // 3
        gate_tile = jnp.where(j == _N_STEPS - 1, _W_TILE[-1], j + 1)
        return (0, jnp.where(j < _FIRST_STEP[1], j, jnp.where(j < _FIRST_STEP[4], att_tile, gate_tile)))

    for jj in range(_N_STEPS):
        jr = jj - _FIRST_STEP[1]
        expect = jj if jj < _FIRST_STEP[1] else (
            _W_TILE[_FIRST_STEP[1]] + (jr % 3) * N_GROUPS + jr // 3 if jj < _FIRST_STEP[4]
            else (_W_TILE[-1] if jj == _N_STEPS - 1 else jj + 1))
        assert expect == _W_TILE[jj], (jj, expect, _W_TILE[jj])

    def col(o, j):
        return jnp.clip(j - _FIRST_STEP[o], 0, _N_TILES[o] - 1)

    flat_spec = lambda o: pl.BlockSpec((tm, TILE), lambda i, j: (i, col(o, j)))
    qkv_spec = lambda o, d: pl.BlockSpec((1, d, tm // d, TILE),
                                         lambda i, j: (i // per_seq, 0, i % per_seq, col(o, j)))
    return pl.pallas_call(
        functools.partial(_in_proj_kernel, dils=dils),
        grid=(n // tm, _N_STEPS),
        in_specs=[
            pl.BlockSpec((tm, D_MODEL), lambda i, j: (i, 0)),
            pl.BlockSpec((1, D_MODEL), lambda i, j: (0, 0)),
            pl.BlockSpec((D_MODEL, TILE), w_map),
            pl.BlockSpec((tm, RET_DK), lambda i, j: (i % n_pos_blocks, 0)),
            pl.BlockSpec((tm, RET_DK), lambda i, j: (i % n_pos_blocks, 0)),
            pl.BlockSpec((1, TILE), lambda i, j: (0, 0)),
            pl.BlockSpec((1, TILE), lambda i, j: (0, 0)),
            pl.BlockSpec((TILE // 2, TILE // 2), lambda i, j: (0, 0)),
        ],
        out_specs=[flat_spec(0)] + [qkv_spec(1 + g, dils[g]) for g in range(N_GROUPS)] + [flat_spec(4)],
        out_shape=[jax.ShapeDtypeStruct((n, RET_OUT_W), F32)]
        + [jax.ShapeDtypeStruct((batch, d, seq_len // d, QKV_W), F32) for d in dils]
        + [jax.ShapeDtypeStruct((n, GATE_W), F32)],
        scratch_shapes=[pltpu.VMEM((tm, D_MODEL), BF16), pltpu.VMEM((TILE // LANES, tm, LANES), F32)],
        compiler_params=pltpu.CompilerParams(
            dimension_semantics=("parallel", "arbitrary"), vmem_limit_bytes=VMEM_LIMIT),
        name="in_proj",
    )(x2d, wn, w_bf, cos_t, sin_t, qg, kg, seg)


def _group_norm_gate(o, gain, gate):
    mu = jnp.mean(o, axis=-1, keepdims=True)
    d = o - mu
    var = jnp.mean(d * d, axis=-1, keepdims=True)
    return gate * jax.nn.sigmoid(gate) * (d * lax.rsqrt(var + EPS) * gain)


def _retention_head(q, k_keys, v_keys, state, dec, qd, kd, cd):
    qb = q.astype(BF16)
    vb = v_keys.astype(BF16)
    scores = _nt_dot(qb, k_keys.astype(BF16)) * dec
    o = _dot(scores.astype(BF16), vb) + _dot(qb, state.astype(BF16)) * qd
    kt = jnp.transpose(k_keys * kd).astype(BF16)
    new_state = state * cd + _dot(kt, vb)
    return o, new_state


def _retention_prompt_kernel(q_ref, k_ref, v_ref, g_ref, dec_ref, qd_ref, kd_ref, cd_ref, gn_ref,
                             o_ref, s_ref):
    @pl.when(pl.program_id(1) == 0)
    def _():
        s_ref[...] = jnp.zeros_like(s_ref)

    for h in range(RET_HEADS):
        ks = slice(h * RET_DK, (h + 1) * RET_DK)
        vs = slice(h * RET_DV, (h + 1) * RET_DV)
        o, new_state = _retention_head(q_ref[0, :, ks], k_ref[0, :, ks], v_ref[0, :, vs], s_ref[0, h],
                                       dec_ref[h], qd_ref[h], kd_ref[h], cd_ref[h])
        s_ref[0, h] = new_state
        o_ref[0, :, vs] = _group_norm_gate(o, gn_ref[:, vs], g_ref[0, :, vs])


def _retention_specs(rows, idx):
    return [
        pl.BlockSpec((1, rows, RET_QK_W), lambda *a: (*idx(*a), 0)),
        pl.BlockSpec((1, rows, RET_QK_W), lambda *a: (*idx(*a), 1)),
        pl.BlockSpec((1, rows, RET_V_W), lambda *a: (*idx(*a), 1)),
        pl.BlockSpec((1, rows, RET_V_W), lambda *a: (*idx(*a), 2)),
    ]


def _table_specs(tables, ndim_grid):
    return [pl.BlockSpec(t.shape, lambda *a, nd=t.ndim: (0,) * nd) for t in tables]


def _retention_prompt(ret, tables, gn):
    b, t, _ = ret.shape
    c = RET_CHUNK
    return pl.pallas_call(
        _retention_prompt_kernel,
        grid=(b, t // c),
        in_specs=_retention_specs(c, lambda bi, ci: (bi, ci)) + _table_specs(tables + (gn,), 2),
        out_specs=[
            pl.BlockSpec((1, c, RET_V_W), lambda bi, ci: (bi, ci, 0)),
            pl.BlockSpec((1, RET_HEADS, RET_DK, RET_DV), lambda bi, ci: (bi, 0, 0, 0)),
        ],
        out_shape=[
            jax.ShapeDtypeStruct((b, t, RET_V_W), F32),
            jax.ShapeDtypeStruct((b, RET_HEADS, RET_DK, RET_DV), F32),
        ],
        compiler_params=pltpu.CompilerParams(
            dimension_semantics=("parallel", "arbitrary"), vmem_limit_bytes=VMEM_LIMIT),
        name="retention_prompt",
    )(ret, ret, ret, ret, *tables, gn)


def _retention_sample_kernel(q_ref, k_ref, v_ref, g_ref, s_in_ref, dec_ref, qd_ref, kd_ref, cd_ref,
                             gn_ref, o_ref, s_out_ref, kpad_ref, vpad_ref):
    @pl.when(pl.program_id(0) == 0)
    def _():
        kpad_ref[...] = jnp.zeros_like(kpad_ref)
        vpad_ref[...] = jnp.zeros_like(vpad_ref)

    kpad_ref[0:SAMPLE_PAD, :] = k_ref[0]
    vpad_ref[0:SAMPLE_PAD, :] = v_ref[0]
    for h in range(RET_HEADS):
        ks = slice(h * RET_DK, (h + 1) * RET_DK)
        vs = slice(h * RET_DV, (h + 1) * RET_DV)
        o, new_state = _retention_head(q_ref[0, :, ks], kpad_ref[:, ks], vpad_ref[:, vs], s_in_ref[0, h],
                                       dec_ref[h], qd_ref[h], kd_ref[h], cd_ref[h])
        s_out_ref[0, h] = new_state
        o_ref[0, :, vs] = _group_norm_gate(o, gn_ref[:, vs], g_ref[0, :, vs])


def _retention_sample(ret, state, tables, gn):
    b, p, _ = ret.shape
    state_spec = pl.BlockSpec((1, RET_HEADS, RET_DK, RET_DV), lambda bi: (bi, 0, 0, 0))
    return pl.pallas_call(
        _retention_sample_kernel,
        grid=(b,),
        in_specs=_retention_specs(p, lambda bi: (bi, 0)) + [state_spec] + _table_specs(tables + (gn,), 1),
        out_specs=[pl.BlockSpec((1, p, RET_V_W), lambda bi: (bi, 0, 0)), state_spec],
        out_shape=[
            jax.ShapeDtypeStruct((b, p, RET_V_W), F32),
            jax.ShapeDtypeStruct(state.shape, F32),
        ],
        scratch_shapes=[pltpu.VMEM((RET_CHUNK, RET_QK_W), F32), pltpu.VMEM((RET_CHUNK, RET_V_W), F32)],
        compiler_params=pltpu.CompilerParams(
            dimension_semantics=("arbitrary",), vmem_limit_bytes=VMEM_LIMIT),
        name="retention_sample",
    )(ret, ret, ret, ret, state, *tables, gn)


def _attn_prompt_kernel(q_ref, kp_ref, kc_ref, vp_ref, vc_ref, bias_ref, o_ref, lse_ref):
    jb = pl.program_id(2)
    pair_w = 2 * ATT_DH
    lane = lax.broadcasted_iota(jnp.int32, (ATT_BLOCK, pair_w), 1)
    low_half = lane < ATT_DH
    col = lax.broadcasted_iota(jnp.int32, (1, 2 * ATT_BLOCK), 1)
    no_prev = jnp.where(jnp.logical_and(col < ATT_BLOCK, jb == 0), NEG, 0.0).astype(F32)

    for p in range(ATT_HEADS // 2):
        ps = slice(p * pair_w, (p + 1) * pair_w)
        qp = q_ref[0, 0, :, ps]
        kcat = jnp.concatenate([kp_ref[0, 0, :, ps], kc_ref[0, 0, :, ps]], axis=0).astype(BF16)
        vcat = jnp.concatenate([vp_ref[0, 0, :, ps], vc_ref[0, 0, :, ps]], axis=0).astype(BF16)
        outs, lses = [], []
        for hh in range(2):
            keep = low_half if hh == 0 else jnp.logical_not(low_half)
            qm = jnp.where(keep, qp, 0.0).astype(BF16)
            s = _nt_dot(qm, kcat) + bias_ref[2 * p + hh] + no_prev
            m = jnp.max(s, axis=-1, keepdims=True)
            e = jnp.exp(s - m)
            l = jnp.sum(e, axis=-1, keepdims=True)
            outs.append(_dot(e.astype(BF16), vcat) / l)
            lses.append(m + jnp.log(l))
        o_ref[0, 0, :, ps] = jnp.where(low_half, outs[0], outs[1])
        lse_ref[0, 0, :, ps] = jnp.where(low_half, lses[0], lses[1])


def _attn_prompt_group(qkv, bias):
    b, dil, tr, _ = qkv.shape
    blk = (1, 1, ATT_BLOCK, ATT_OUT_W)
    cur = lambda c: pl.BlockSpec(blk, lambda bi, r, j: (bi, r, j, c))
    prev = lambda c: pl.BlockSpec(blk, lambda bi, r, j: (bi, r, jnp.maximum(j - 1, 0), c))
    res_shape = jax.ShapeDtypeStruct((b, dil, tr, ATT_OUT_W), F32)
    return pl.pallas_call(
        _attn_prompt_kernel,
        grid=(b, dil, tr // ATT_BLOCK),
        in_specs=[cur(0), prev(1), cur(1), prev(2), cur(2),
                  pl.BlockSpec(bias.shape, lambda bi, r, j: (0, 0, 0))],
        out_specs=[cur(0), cur(0)],
        out_shape=[res_shape, res_shape],
        compiler_params=pltpu.CompilerParams(
            dimension_semantics=("parallel", "parallel", "arbitrary"), vmem_limit_bytes=VMEM_LIMIT),
        name=f"attn_prompt_d{dil}",
    )(qkv, qkv, qkv, qkv, qkv, bias)


def _attn_sample_kernel(a0_ref, a1_ref, a2_ref, c0_ref, c1_ref, c2_ref, b0_ref, b1_ref, b2_ref, bn_ref,
                        o_ref, kn_ref, vn_ref):
    qkv_refs = (a0_ref, a1_ref, a2_ref)
    cache_refs = (c0_ref, c1_ref, c2_ref)
    bias_refs = (b0_ref, b1_ref, b2_ref)

    @pl.when(pl.program_id(0) == 0)
    def _():
        kn_ref[...] = jnp.zeros_like(kn_ref)
        vn_ref[...] = jnp.zeros_like(vn_ref)

    for g in range(N_GROUPS):
        gs = slice(g * ATT_OUT_W, (g + 1) * ATT_OUT_W)
        kn_ref[0:SAMPLE_PAD, gs] = qkv_refs[g][0, :, ATT_OUT_W:2 * ATT_OUT_W]
        vn_ref[0:SAMPLE_PAD, gs] = qkv_refs[g][0, :, 2 * ATT_OUT_W:3 * ATT_OUT_W]

    for h in range(ATT_HEADS):
        logits, values = [], []
        for g in range(N_GROUPS):
            hs = slice(g * ATT_OUT_W + h * ATT_DH, g * ATT_OUT_W + (h + 1) * ATT_DH)
            qh = qkv_refs[g][0, :, h * ATT_DH:(h + 1) * ATT_DH].astype(BF16)
            logits.append(_dot(qh, cache_refs[g][0, 0, h].astype(BF16)) + bias_refs[g][h])
            values.append((cache_refs[g][0, 1, h].astype(BF16), True))
            logits.append(_nt_dot(qh, kn_ref[:, hs].astype(BF16)) + bn_ref[g, h])
            values.append((vn_ref[:, hs].astype(BF16), False))
        m = functools.reduce(jnp.maximum, [jnp.max(x, axis=-1, keepdims=True) for x in logits])
        es = [jnp.exp(x - m) for x in logits]
        l = functools.reduce(jnp.add, [jnp.sum(e, axis=-1, keepdims=True) for e in es])
        acc = jnp.zeros((SAMPLE_PAD, ATT_DH), F32)
        for e, (v, transposed) in zip(es, values):
            eb = e.astype(BF16)
            acc = acc + (_nt_dot(eb, v) if transposed else _dot(eb, v))
        o_ref[0, :, h * ATT_DH:(h + 1) * ATT_DH] = acc / l


def _attn_sample(qkvs, caches_t, biases, bias_new):
    b, p, _ = qkvs[0].shape
    return pl.pallas_call(
        _attn_sample_kernel,
        grid=(b,),
        in_specs=[pl.BlockSpec((1, p, QKV_W), lambda bi: (bi, 0, 0)) for _ in qkvs]
        + [pl.BlockSpec((1,) + c.shape[1:], lambda bi: (bi, 0, 0, 0, 0)) for c in caches_t]
        + [pl.BlockSpec(x.shape, lambda bi: (0, 0, 0)) for x in biases]
        + [pl.BlockSpec(bias_new.shape, lambda bi: (0, 0, 0, 0))],
        out_specs=pl.BlockSpec((1, p, ATT_OUT_W), lambda bi: (bi, 0, 0)),
        out_shape=jax.ShapeDtypeStruct((b, p, ATT_OUT_W), F32),
        scratch_shapes=[pltpu.VMEM((ATT_BLOCK, ATT_W), F32), pltpu.VMEM((ATT_BLOCK, ATT_W), F32)],
        compiler_params=pltpu.CompilerParams(
            dimension_semantics=("arbitrary",), vmem_limit_bytes=VMEM_LIMIT),
        name="attn_sample",
    )(*qkvs, *caches_t, *biases, bias_new)


def _out_proj_kernel(*refs, dils):
    x_ref, ret_ref, ga_ref, gb_ref, ag_ref, wr_ref, wa_ref, wo_ref = refs[:8]
    if dils is None:
        att_ref, o_ref = refs[8:]
        att = att_ref[0]
    else:
        group_refs = refs[8:8 + 2 * N_GROUPS]
        o_ref = refs[8 + 2 * N_GROUPS]
        scratch = refs[9 + 2 * N_GROUPS:]
        tm = x_ref.shape[1]
        os, lses = [], []
        for g, dil in enumerate(dils):
            og_ref, lg_ref = group_refs[2 * g], group_refs[2 * g + 1]
            if dil == 1:
                os.append(og_ref[0, 0]); lses.append(lg_ref[0, 0])
                continue
            so_ref, sl_ref = scratch[2 * (g - 1)], scratch[2 * (g - 1) + 1]
            n_chunks = ATT_OUT_W // LANES
            for c in range(n_chunks):
                cs = slice(c * LANES, (c + 1) * LANES)
                for r in range(dil):
                    so_ref[c, pl.ds(r, tm // dil, stride=dil), :] = og_ref[0, r, :, cs]
                    sl_ref[c, pl.ds(r, tm // dil, stride=dil), :] = lg_ref[0, r, :, cs]
            os.append(jnp.concatenate([so_ref[c] for c in range(n_chunks)], axis=1))
            lses.append(jnp.concatenate([sl_ref[c] for c in range(n_chunks)], axis=1))
        mx = functools.reduce(jnp.maximum, lses)
        ws = [jnp.exp(l - mx) for l in lses]
        att = functools.reduce(jnp.add, [w * o for w, o in zip(ws, os)]) / functools.reduce(jnp.add, ws)
    ag = ag_ref[0]
    u = (ag * jax.nn.sigmoid(ag) * att).astype(BF16)
    o_b = _dot(u, wa_ref[...])
    o_a = _dot(ret_ref[0].astype(BF16), wr_ref[...])
    merged = jax.nn.sigmoid(ga_ref[0]) * o_a + jax.nn.sigmoid(gb_ref[0]) * o_b
    o_ref[0] = x_ref[0] + _dot(merged.astype(BF16), wo_ref[...])


def _out_proj(x, ret, gates, wr, wa, wo, tm, att=None, groups=None):
    b, t, _ = x.shape
    row = lambda w, c: pl.BlockSpec((1, tm, w), lambda bi, i: (bi, i, c))
    full = lambda a: pl.BlockSpec(a.shape, lambda bi, i: (0, 0))
    in_specs = [row(D_MODEL, 0), row(RET_V_W, 0), row(D_MODEL, 0), row(D_MODEL, 1),
                row(ATT_OUT_W, 2 * D_MODEL // ATT_OUT_W), full(wr), full(wa), full(wo)]
    args = [x, ret, gates, gates, gates, wr, wa, wo]
    scratch = []
    if groups is None:
        dils = None
        in_specs.append(row(ATT_OUT_W, 0))
        args.append(att)
    else:
        dils = tuple(o.shape[1] for o, _ in groups)
        for (o, lse), d in zip(groups, dils):
            spec = pl.BlockSpec((1, d, tm // d, ATT_OUT_W), lambda bi, i: (bi, 0, i, 0))
            in_specs += [spec, spec]
            args += [o, lse]
            if d > 1:
                scratch += [pltpu.VMEM((ATT_OUT_W // LANES, tm, LANES), F32)] * 2
    return pl.pallas_call(
        functools.partial(_out_proj_kernel, dils=dils),
        grid=(b, t // tm),
        in_specs=in_specs,
        out_specs=row(D_MODEL, 0),
        out_shape=jax.ShapeDtypeStruct((b, t, D_MODEL), F32),
        scratch_shapes=scratch,
        compiler_params=pltpu.CompilerParams(
            dimension_semantics=("parallel", "parallel"), vmem_limit_bytes=VMEM_LIMIT),
        name="out_proj",
    )(*args)


def _rotary_tables(pos):
    half = RET_DK // 2
    inv = ROPE_BASE ** (-jnp.arange(half, dtype=F32) / half)
    ang = pos[:, None] * inv[None, :]
    cos, sin = jnp.cos(ang), jnp.sin(ang)
    return jnp.concatenate([cos, cos], axis=1), jnp.concatenate([-sin, sin], axis=1)


def _retention_tables(c, rows):
    log_g = jnp.log1p(-(2.0 ** (-5.0 - jnp.arange(RET_HEADS, dtype=F32))))
    i = jnp.arange(c, dtype=F32)
    diff = i[:, None] - i[None, :]
    decay = jnp.where(diff[None] >= 0, jnp.exp(jnp.maximum(diff, 0.0)[None] * log_g[:, None, None]), 0.0)
    q_decay = jnp.exp((i + 1.0)[None, :] * log_g[:, None])
    k_decay = jnp.exp((c - 1.0 - i)[None, :] * log_g[:, None])
    chunk_decay = jnp.exp(c * log_g)
    dec = jnp.zeros((RET_HEADS, rows, RET_CHUNK), F32).at[:, :c, :c].set(decay)
    qd = jnp.zeros((RET_HEADS, rows, 1), F32).at[:, :c, 0].set(q_decay)
    kd = jnp.zeros((RET_HEADS, RET_CHUNK, 1), F32).at[:, :c, 0].set(k_decay)
    cd = jnp.broadcast_to(chunk_decay[:, None, None], (RET_HEADS, 1, RET_DV))
    return dec, qd, kd, cd


def _t5_bucket(dist):
    max_exact = REL_BUCKETS // 2
    d = jnp.maximum(dist.astype(F32), 1.0)
    large = max_exact + (jnp.log(d / max_exact) / math.log(REL_MAX_DIST / max_exact)
                         * (REL_BUCKETS - max_exact)).astype(jnp.int32)
    large = jnp.minimum(large, REL_BUCKETS - 1)
    return jnp.where(dist < max_exact, dist, large)


def _group_bias(rel_bias, g, dil):
    dist = dil * jnp.arange(N_KEYS, dtype=jnp.int32)
    return rel_bias[_t5_bucket(dist)][:, g * ATT_HEADS:(g + 1) * ATT_HEADS].astype(F32).T


def _slot_bias(tb, m, ok):
    return jnp.where(ok[None], tb[:, jnp.clip(m, 0, N_KEYS - 1)], NEG)


def _prompt_bias(tb):
    m = jnp.arange(ATT_BLOCK)[:, None] + ATT_BLOCK - jnp.arange(2 * ATT_BLOCK)[None, :]
    return _slot_bias(tb, m, jnp.logical_and(m >= 0, m < N_KEYS))


def _sample_bias(tb, win, dil):
    i = jnp.arange(SAMPLE_PAD)[:, None]
    dc = win + i - jnp.arange(win)[None, :]
    bias_c = _slot_bias(tb, dc // dil, jnp.logical_and(dc % dil == 0, dc // dil < N_KEYS))
    n = jnp.arange(ATT_BLOCK)[None, :]
    dn = i - n
    ok_n = jnp.logical_and(jnp.logical_and(dn >= 0, dn % dil == 0), n < SAMPLE_PAD)
    return bias_c, _slot_bias(tb, dn // dil, ok_n)


def _kv_rows(qkv, n_rows):
    b, d, tr, _ = qkv.shape
    per = n_rows // d
    kv = qkv[:, :, tr - per:, ATT_OUT_W:]
    return jnp.transpose(kv, (0, 2, 1, 3)).reshape(b, n_rows, 2, ATT_HEADS, ATT_DH)


def kernel(x_prompt, x_sample, cache_kv_w128, cache_kv_w512, cache_kv_w2048, state_retention,
           w_norm, w_in, q_norm, k_norm, rel_bias, ret_norm, w_proj_ret, w_proj_att, w_out):
    assert w_in.shape[0] == 1
    bp, t, _ = x_prompt.shape
    bs, ts, _ = x_sample.shape
    dils = tuple(d for _, d in ATT_GROUPS)
    assert t % (ATT_BLOCK * max(dils)) == 0 and ts <= SAMPLE_PAD
    caches = (cache_kv_w128[0], cache_kv_w512[0], cache_kv_w2048[0])
    for cch, (win, _) in zip(caches, ATT_GROUPS):
        assert cch.shape[1] == win and win <= PAST_LEN

    wn = w_norm[0].reshape(1, D_MODEL)
    w_in_bf = w_in[0].astype(BF16)
    qg = jnp.tile(q_norm[0] * (ATT_DH ** -0.5), TILE // ATT_DH).reshape(1, TILE)
    kg = jnp.tile(k_norm[0], TILE // ATT_DH).reshape(1, TILE)
    gn = ret_norm[0].reshape(1, RET_V_W)
    wr = w_proj_ret[0].astype(BF16)
    wa = w_proj_att[0].astype(BF16)
    wo = w_out[0].astype(BF16)
    hid = jnp.arange(TILE // 2) // ATT_DH
    seg_mean = jnp.where(hid[:, None] == hid[None, :], 1.0 / ATT_DH, 0.0).astype(BF16)
    group_bias = [_group_bias(rel_bias, g, d) for g, d in enumerate(dils)]

    cos_p, sin_p = _rotary_tables(jnp.arange(t, dtype=F32))
    ret_in, *qkv_p, gates_p = _in_proj(x_prompt.reshape(bp * t, D_MODEL), t, dils, wn, w_in_bf,
                                       cos_p, sin_p, qg, kg, seg_mean, tm=512)
    ret_p, state_p = _retention_prompt(ret_in.reshape(bp, t, RET_OUT_W),
                                       _retention_tables(RET_CHUNK, RET_CHUNK), gn)
    groups = [_attn_prompt_group(qkv_p[g], _prompt_bias(group_bias[g])) for g in range(N_GROUPS)]
    y_p = _out_proj(x_prompt, ret_p, gates_p.reshape(bp, t, GATE_W), wr, wa, wo, tm=256, groups=groups)

    pad = SAMPLE_PAD
    ns = bs * pad
    xs = jnp.pad(x_sample, ((0, 0), (0, pad - ts), (0, 0))).reshape(ns, D_MODEL)
    cos_s, sin_s = _rotary_tables(jnp.tile(PAST_LEN + jnp.arange(pad, dtype=F32), bs))
    ret_in_s, *qkv_s, gates_s = _in_proj(xs, ns, (1,) * N_GROUPS, wn, w_in_bf, cos_s, sin_s, qg, kg,
                                         seg_mean, tm=ns)
    ret_s, state_s = _retention_sample(ret_in_s.reshape(bs, pad, RET_OUT_W), state_retention[0],
                                       _retention_tables(ts, pad), gn)
    qkv_s = [a.reshape(bs, pad, QKV_W) for a in qkv_s]
    caches_t = [jnp.transpose(c, (0, 2, 3, 4, 1)) for c in caches]
    sb = [_sample_bias(group_bias[g], win, d) for g, (win, d) in enumerate(ATT_GROUPS)]
    att_s = _attn_sample(qkv_s, caches_t, [c for c, _ in sb], jnp.stack([n for _, n in sb]))
    y_s = _out_proj(xs.reshape(1, ns, D_MODEL), ret_s.reshape(1, ns, RET_V_W), gates_s.reshape(1, ns, GATE_W),
                    wr, wa, wo, tm=ns, att=att_s.reshape(1, ns, ATT_OUT_W))
    y_s = y_s.reshape(bs, pad, D_MODEL)[:, :ts]

    kv_p = [_kv_rows(qkv_p[g], min(win, t))[None] for g, (win, _) in enumerate(ATT_GROUPS)]
    kv_s = [_kv_rows(a[:, None, :ts], ts)[None] for a in qkv_s]
    return (y_p, y_s, state_p[None], state_s[None], kv_p[0], kv_p[1], kv_p[2], kv_s[0], kv_s[1], kv_s[2])
```

```python
import functools
import math

import jax
import jax.numpy as jnp
from jax import lax
from jax.experimental import pallas as pl
from jax.experimental.pallas import tpu as pltpu

D_MODEL = 1024
PAST_LEN = 16384
RET_HEADS = 4
RET_DK = 128
RET_DV = 256
RET_CHUNK = 128
ROPE_BASE = 10000.0
ATT_GROUPS = ((128, 1), (512, 4), (2048, 16))
N_GROUPS = 3
ATT_HEADS = 8
ATT_DH = 64
REL_BUCKETS = 32
REL_MAX_DIST = 2048
EPS = 1e-6

RET_QK_W = RET_HEADS * RET_DK
RET_V_W = RET_HEADS * RET_DV
ATT_W = N_GROUPS * ATT_HEADS * ATT_DH
ATT_OUT_W = ATT_HEADS * ATT_DH
IN_W = 2 * RET_QK_W + 2 * RET_V_W + 3 * ATT_W + ATT_OUT_W + 2 * D_MODEL

LANES = 128
TILE = 512
RET_OUT_W = 2 * RET_QK_W + 2 * RET_V_W
QKV_W = 3 * ATT_OUT_W
GATE_W = 2 * D_MODEL + ATT_OUT_W
ATT_BLOCK = 128
N_KEYS = ATT_BLOCK + 1
SAMPLE_PAD = 8
NEG = -1e30
VMEM_LIMIT = 48 * 1024 * 1024

F32 = jnp.float32
BF16 = jnp.bfloat16


def _nt_dot(a, b):
    return lax.dot_general(a, b, (((1,), (1,)), ((), ())), preferred_element_type=F32)


def _dot(a, b):
    return jnp.dot(a, b, preferred_element_type=F32)


def _proj_schedule():
    sched = [(0, 0, 0, "rot_q"), (RET_QK_W, 0, RET_QK_W, "rot_k")]
    for k in range(2 * RET_V_W // TILE):
        sched.append((2 * RET_QK_W + k * TILE, 0, 2 * RET_QK_W + k * TILE, "plain"))
    att0 = 2 * RET_QK_W + 2 * RET_V_W
    for g in range(N_GROUPS):
        for kind, epi in enumerate(("norm_q", "norm_k", "plain")):
            sched.append((att0 + kind * ATT_W + g * ATT_OUT_W, 1 + g, kind * ATT_OUT_W, epi))
    gate0 = att0 + 3 * ATT_W
    for k in range(2 * D_MODEL // TILE):
        sched.append((gate0 + ATT_OUT_W + k * TILE, 4, k * TILE, "plain"))
    sched.append((gate0, 4, 2 * D_MODEL, "plain"))
    return sched


_SCHEDULE = _proj_schedule()


def _in_proj_kernel(x_ref, wn_ref, w_ref, cos_ref, sin_ref, qg_ref, kg_ref, seg_ref,
                    ret_ref, a0_ref, a1_ref, a2_ref, gate_ref, tail_ref, h_ref, y_ref, *, dils):
    out_refs = (ret_ref, a0_ref, a1_ref, a2_ref, gate_ref)
    x = x_ref[...]
    tm = x.shape[0]
    ms = jnp.mean(x * x, axis=-1, keepdims=True)
    h_ref[...] = (x * lax.rsqrt(ms + EPS) * wn_ref[...]).astype(BF16)

    def rotary(y, scale):
        cos, sin = cos_ref[...], sin_ref[...]
        parts = []
        for hh in range(TILE // RET_DK):
            yh = y[:, hh * RET_DK:(hh + 1) * RET_DK]
            parts.append((yh * cos + pltpu.roll(yh, RET_DK // 2, axis=1) * sin) * scale)
        return jnp.concatenate(parts, axis=1)

    def head_rms(y, gain):
        y2 = (y * y).astype(BF16)
        half = TILE // 2
        ms = jnp.concatenate([_dot(y2[:, :half], seg_ref[...]), _dot(y2[:, half:], seg_ref[...])], axis=1)
        return y * lax.rsqrt(ms + EPS) * gain

    epilogues = {
        "plain": lambda y: y,
        "rot_q": lambda y: rotary(y, 1.0),
        "rot_k": lambda y: rotary(y, RET_DK ** -0.5),
        "norm_q": lambda y: head_rms(y, qg_ref[...]),
        "norm_k": lambda y: head_rms(y, kg_ref[...]),
    }

    for w_col, out_idx, out_col, epi in _SCHEDULE:
        val = epilogues[epi](_dot(h_ref[...], w_ref[:, w_col:w_col + TILE]))
        o_ref = out_refs[out_idx]
        ocs = slice(out_col, out_col + TILE)
        if not 1 <= out_idx <= N_GROUPS:
            o_ref[:, ocs] = val.astype(o_ref.dtype)
            continue
        g, kind = out_idx - 1, out_col // ATT_OUT_W
        if kind > 0:
            tcol = (2 * g + kind - 1) * ATT_OUT_W
            tail_ref[0, :, tcol:tcol + ATT_OUT_W] = val
        dil = dils[g]
        if dil == 1:
            o_ref[0, 0, :, ocs] = val.astype(o_ref.dtype)
            continue
        for c in range(TILE // LANES):
            y_ref[c] = val[:, c * LANES:(c + 1) * LANES]
            for r in range(dil):
                o_ref[0, r, :, out_col + c * LANES:out_col + (c + 1) * LANES] = (
                    y_ref[c, pl.ds(r, tm // dil, stride=dil), :].astype(o_ref.dtype))


def _in_proj(x2d, seq_len, dils, tail_rows, out_dtype, wn, w_bf, cos_t, sin_t, qg, kg, seg, tm):
    n = x2d.shape[0]
    batch = n // seq_len
    per_seq = seq_len // tm
    n_pos_blocks = cos_t.shape[0] // tm
    first_tail = (seq_len - tail_rows) // tm
    const = lambda shape: pl.BlockSpec(shape, lambda i: (0, 0))
    qkv_spec = lambda d: pl.BlockSpec((1, d, tm // d, QKV_W), lambda i: (i // per_seq, 0, i % per_seq, 0))
    tail_spec = pl.BlockSpec((1, tm, 2 * ATT_W),
                             lambda i: (i // per_seq, jnp.maximum(i % per_seq - first_tail, 0), 0))
    return pl.pallas_call(
        functools.partial(_in_proj_kernel, dils=dils),
        grid=(n // tm,),
        in_specs=[
            pl.BlockSpec((tm, D_MODEL), lambda i: (i, 0)),
            const((1, D_MODEL)),
            pl.BlockSpec(w_bf.shape, lambda i: (0, 0), pipeline_mode=pl.Buffered(1)),
            pl.BlockSpec((tm, RET_DK), lambda i: (i % n_pos_blocks, 0)),
            pl.BlockSpec((tm, RET_DK), lambda i: (i % n_pos_blocks, 0)),
            const((1, TILE)), const((1, TILE)), const((TILE // 2, TILE // 2)),
        ],
        out_specs=[pl.BlockSpec((tm, RET_OUT_W), lambda i: (i, 0))]
        + [qkv_spec(d) for d in dils]
        + [pl.BlockSpec((tm, GATE_W), lambda i: (i, 0)), tail_spec],
        out_shape=[jax.ShapeDtypeStruct((n, RET_OUT_W), out_dtype)]
        + [jax.ShapeDtypeStruct((batch, d, seq_len // d, QKV_W), out_dtype) for d in dils]
        + [jax.ShapeDtypeStruct((n, GATE_W), out_dtype),
           jax.ShapeDtypeStruct((batch, tail_rows, 2 * ATT_W), F32)],
        scratch_shapes=[pltpu.VMEM((tm, D_MODEL), BF16), pltpu.VMEM((TILE // LANES, tm, LANES), F32)],
        compiler_params=pltpu.CompilerParams(
            dimension_semantics=("arbitrary",), vmem_limit_bytes=VMEM_LIMIT),
        name="in_proj",
    )(x2d, wn, w_bf, cos_t, sin_t, qg, kg, seg)


def _group_norm_gate(o, gain, gate):
    gate = gate.astype(F32)
    mu = jnp.mean(o, axis=-1, keepdims=True)
    d = o - mu
    var = jnp.mean(d * d, axis=-1, keepdims=True)
    return gate * jax.nn.sigmoid(gate) * (d * lax.rsqrt(var + EPS) * gain)


def _retention_head(q, k_keys, v_keys, state, dec, qd, kd, cd):
    qb = q.astype(BF16)
    vb = v_keys.astype(BF16)
    scores = _nt_dot(qb, k_keys.astype(BF16)) * dec
    o = _dot(scores.astype(BF16), vb) + _dot(qb, state.astype(BF16)) * qd
    kt = jnp.transpose(k_keys * kd).astype(BF16)
    new_state = state * cd + _dot(kt, vb)
    return o, new_state


def _retention_prompt_kernel(q_ref, k_ref, v_ref, g_ref, dec_ref, qd_ref, kd_ref, cd_ref, gn_ref,
                             o_ref, s_ref):
    @pl.when(pl.program_id(1) == 0)
    def _():
        s_ref[...] = jnp.zeros_like(s_ref)

    for h in range(RET_HEADS):
        ks = slice(h * RET_DK, (h + 1) * RET_DK)
        vs = slice(h * RET_DV, (h + 1) * RET_DV)
        o, new_state = _retention_head(q_ref[0, :, ks], k_ref[0, :, ks], v_ref[0, :, vs], s_ref[0, h],
                                       dec_ref[h], qd_ref[h], kd_ref[h], cd_ref[h])
        s_ref[0, h] = new_state
        o_ref[0, :, vs] = _group_norm_gate(o, gn_ref[:, vs], g_ref[0, :, vs]).astype(o_ref.dtype)


def _retention_specs(rows, idx):
    return [
        pl.BlockSpec((1, rows, RET_QK_W), lambda *a: (*idx(*a), 0)),
        pl.BlockSpec((1, rows, RET_QK_W), lambda *a: (*idx(*a), 1)),
        pl.BlockSpec((1, rows, RET_V_W), lambda *a: (*idx(*a), 1)),
        pl.BlockSpec((1, rows, RET_V_W), lambda *a: (*idx(*a), 2)),
    ]


def _table_specs(tables, ndim_grid):
    return [pl.BlockSpec(t.shape, lambda *a, nd=t.ndim: (0,) * nd) for t in tables]


def _retention_prompt(ret, tables, gn):
    b, t, _ = ret.shape
    c = RET_CHUNK
    return pl.pallas_call(
        _retention_prompt_kernel,
        grid=(b, t // c),
        in_specs=_retention_specs(c, lambda bi, ci: (bi, ci)) + _table_specs(tables + (gn,), 2),
        out_specs=[
            pl.BlockSpec((1, c, RET_V_W), lambda bi, ci: (bi, ci, 0)),
            pl.BlockSpec((1, RET_HEADS, RET_DK, RET_DV), lambda bi, ci: (bi, 0, 0, 0)),
        ],
        out_shape=[
            jax.ShapeDtypeStruct((b, t, RET_V_W), ret.dtype),
            jax.ShapeDtypeStruct((b, RET_HEADS, RET_DK, RET_DV), F32),
        ],
        compiler_params=pltpu.CompilerParams(
            dimension_semantics=("parallel", "arbitrary"), vmem_limit_bytes=VMEM_LIMIT),
        name="retention_prompt",
    )(ret, ret, ret, ret, *tables, gn)


def _retention_sample_kernel(q_ref, k_ref, v_ref, g_ref, s_in_ref, dec_ref, qd_ref, kd_ref, cd_ref,
                             gn_ref, o_ref, s_out_ref, kpad_ref, vpad_ref):
    @pl.when(pl.program_id(0) == 0)
    def _():
        kpad_ref[...] = jnp.zeros_like(kpad_ref)
        vpad_ref[...] = jnp.zeros_like(vpad_ref)

    kpad_ref[0:SAMPLE_PAD, :] = k_ref[0]
    vpad_ref[0:SAMPLE_PAD, :] = v_ref[0]
    for h in range(RET_HEADS):
        ks = slice(h * RET_DK, (h + 1) * RET_DK)
        vs = slice(h * RET_DV, (h + 1) * RET_DV)
        o, new_state = _retention_head(q_ref[0, :, ks], kpad_ref[:, ks], vpad_ref[:, vs], s_in_ref[0, h],
                                       dec_ref[h], qd_ref[h], kd_ref[h], cd_ref[h])
        s_out_ref[0, h] = new_state
        o_ref[0, :, vs] = _group_norm_gate(o, gn_ref[:, vs], g_ref[0, :, vs]).astype(o_ref.dtype)


def _retention_sample(ret, state, tables, gn):
    b, p, _ = ret.shape
    state_spec = pl.BlockSpec((1, RET_HEADS, RET_DK, RET_DV), lambda bi: (bi, 0, 0, 0))
    return pl.pallas_call(
        _retention_sample_kernel,
        grid=(b,),
        in_specs=_retention_specs(p, lambda bi: (bi, 0)) + [state_spec] + _table_specs(tables + (gn,), 1),
        out_specs=[pl.BlockSpec((1, p, RET_V_W), lambda bi: (bi, 0, 0)), state_spec],
        out_shape=[
            jax.ShapeDtypeStruct((b, p, RET_V_W), ret.dtype),
            jax.ShapeDtypeStruct(state.shape, F32),
        ],
        scratch_shapes=[pltpu.VMEM((RET_CHUNK, RET_QK_W), F32), pltpu.VMEM((RET_CHUNK, RET_V_W), F32)],
        compiler_params=pltpu.CompilerParams(
            dimension_semantics=("arbitrary",), vmem_limit_bytes=VMEM_LIMIT),
        name="retention_sample",
    )(ret, ret, ret, ret, state, *tables, gn)


def _attn_prompt_kernel(q_ref, kp_ref, kc_ref, vp_ref, vc_ref, bias_ref, o_ref, lse_ref):
    jb = pl.program_id(2)
    pair_w = 2 * ATT_DH
    lane = lax.broadcasted_iota(jnp.int32, (ATT_BLOCK, pair_w), 1)
    low_half = lane < ATT_DH
    col = lax.broadcasted_iota(jnp.int32, (1, 2 * ATT_BLOCK), 1)
    no_prev = jnp.where(jnp.logical_and(col < ATT_BLOCK, jb == 0), NEG, 0.0).astype(F32)

    for p in range(ATT_HEADS // 2):
        ps = slice(p * pair_w, (p + 1) * pair_w)
        qp = q_ref[0, 0, :, ps]
        kcat = jnp.concatenate([kp_ref[0, 0, :, ps], kc_ref[0, 0, :, ps]], axis=0).astype(BF16)
        vcat = jnp.concatenate([vp_ref[0, 0, :, ps], vc_ref[0, 0, :, ps]], axis=0).astype(BF16)
        outs, lses = [], []
        for hh in range(2):
            keep = low_half if hh == 0 else jnp.logical_not(low_half)
            qm = jnp.where(keep, qp, 0.0).astype(BF16)
            s = _nt_dot(qm, kcat) + bias_ref[2 * p + hh] + no_prev
            m = jnp.max(s, axis=-1, keepdims=True)
            e = jnp.exp(s - m)
            l = jnp.sum(e, axis=-1, keepdims=True)
            outs.append(_dot(e.astype(BF16), vcat) / l)
            lses.append(m + jnp.log(l))
        o_ref[0, 0, :, ps] = jnp.where(low_half, outs[0], outs[1])
        lse_ref[0, 0, :, ps] = jnp.where(low_half, lses[0], lses[1])


def _attn_prompt_group(qkv, bias):
    b, dil, tr, _ = qkv.shape
    blk = (1, 1, ATT_BLOCK, ATT_OUT_W)
    cur = lambda c: pl.BlockSpec(blk, lambda bi, r, j: (bi, r, j, c))
    prev = lambda c: pl.BlockSpec(blk, lambda bi, r, j: (bi, r, jnp.maximum(j - 1, 0), c))
    res_shape = jax.ShapeDtypeStruct((b, dil, tr, ATT_OUT_W), F32)
    return pl.pallas_call(
        _attn_prompt_kernel,
        grid=(b, dil, tr // ATT_BLOCK),
        in_specs=[cur(0), prev(1), cur(1), prev(2), cur(2),
                  pl.BlockSpec(bias.shape, lambda bi, r, j: (0, 0, 0))],
        out_specs=[cur(0), cur(0)],
        out_shape=[res_shape, res_shape],
        compiler_params=pltpu.CompilerParams(
            dimension_semantics=("parallel", "parallel", "arbitrary"), vmem_limit_bytes=VMEM_LIMIT),
        name=f"attn_prompt_d{dil}",
    )(qkv, qkv, qkv, qkv, qkv, bias)


def _attn_sample_kernel(a0_ref, a1_ref, a2_ref, c0_ref, c1_ref, c2_ref, b0_ref, b1_ref, b2_ref, bn_ref,
                        o_ref, kn_ref, vn_ref):
    qkv_refs = (a0_ref, a1_ref, a2_ref)
    cache_refs = (c0_ref, c1_ref, c2_ref)
    bias_refs = (b0_ref, b1_ref, b2_ref)

    @pl.when(pl.program_id(0) == 0)
    def _():
        kn_ref[...] = jnp.zeros_like(kn_ref)
        vn_ref[...] = jnp.zeros_like(vn_ref)

    for g in range(N_GROUPS):
        gs = slice(g * ATT_OUT_W, (g + 1) * ATT_OUT_W)
        kn_ref[0:SAMPLE_PAD, gs] = qkv_refs[g][0, :, ATT_OUT_W:2 * ATT_OUT_W]
        vn_ref[0:SAMPLE_PAD, gs] = qkv_refs[g][0, :, 2 * ATT_OUT_W:3 * ATT_OUT_W]

    for h in range(ATT_HEADS):
        logits, values = [], []
        for g in range(N_GROUPS):
            hs = slice(g * ATT_OUT_W + h * ATT_DH, g * ATT_OUT_W + (h + 1) * ATT_DH)
            qh = qkv_refs[g][0, :, h * ATT_DH:(h + 1) * ATT_DH].astype(BF16)
            logits.append(_dot(qh, cache_refs[g][0, 0, h].astype(BF16)) + bias_refs[g][h])
            values.append((cache_refs[g][0, 1, h].astype(BF16), True))
            logits.append(_nt_dot(qh, kn_ref[:, hs].astype(BF16)) + bn_ref[g, h])
            values.append((vn_ref[:, hs].astype(BF16), False))
        m = functools.reduce(jnp.maximum, [jnp.max(x, axis=-1, keepdims=True) for x in logits])
        es = [jnp.exp(x - m) for x in logits]
        l = functools.reduce(jnp.add, [jnp.sum(e, axis=-1, keepdims=True) for e in es])
        acc = jnp.zeros((SAMPLE_PAD, ATT_DH), F32)
        for e, (v, transposed) in zip(es, values):
            eb = e.astype(BF16)
            acc = acc + (_nt_dot(eb, v) if transposed else _dot(eb, v))
        o_ref[0, :, h * ATT_DH:(h + 1) * ATT_DH] = acc / l


def _attn_sample(qkvs, caches_t, biases, bias_new):
    b, p, _ = qkvs[0].shape
    return pl.pallas_call(
        _attn_sample_kernel,
        grid=(b,),
        in_specs=[pl.BlockSpec((1, p, QKV_W), lambda bi: (bi, 0, 0)) for _ in qkvs]
        + [pl.BlockSpec((1,) + c.shape[1:], lambda bi: (bi, 0, 0, 0, 0)) for c in caches_t]
        + [pl.BlockSpec(x.shape, lambda bi: (0, 0, 0)) for x in biases]
        + [pl.BlockSpec(bias_new.shape, lambda bi: (0, 0, 0, 0))],
        out_specs=pl.BlockSpec((1, p, ATT_OUT_W), lambda bi: (bi, 0, 0)),
        out_shape=jax.ShapeDtypeStruct((b, p, ATT_OUT_W), F32),
        scratch_shapes=[pltpu.VMEM((ATT_BLOCK, ATT_W), F32), pltpu.VMEM((ATT_BLOCK, ATT_W), F32)],
        compiler_params=pltpu.CompilerParams(
            dimension_semantics=("arbitrary",), vmem_limit_bytes=VMEM_LIMIT),
        name="attn_sample",
    )(*qkvs, *caches_t, *biases, bias_new)


def _out_proj_kernel(*refs, dils):
    x_ref, ret_ref, ga_ref, gb_ref, ag_ref, wr_ref, wa_ref, wo_ref = refs[:8]
    if dils is None:
        att_ref, o_ref = refs[8:]
        att = att_ref[0]
    else:
        group_refs = refs[8:8 + 2 * N_GROUPS]
        o_ref = refs[8 + 2 * N_GROUPS]
        scratch = refs[9 + 2 * N_GROUPS:]
        tm = x_ref.shape[1]
        os, lses = [], []
        for g, dil in enumerate(dils):
            og_ref, lg_ref = group_refs[2 * g], group_refs[2 * g + 1]
            if dil == 1:
                os.append(og_ref[0, 0]); lses.append(lg_ref[0, 0])
                continue
            so_ref, sl_ref = scratch[2 * (g - 1)], scratch[2 * (g - 1) + 1]
            n_chunks = ATT_OUT_W // LANES
            for c in range(n_chunks):
                cs = slice(c * LANES, (c + 1) * LANES)
                for r in range(dil):
                    so_ref[c, pl.ds(r, tm // dil, stride=dil), :] = og_ref[0, r, :, cs]
                    sl_ref[c, pl.ds(r, tm // dil, stride=dil), :] = lg_ref[0, r, :, cs]
            os.append(jnp.concatenate([so_ref[c] for c in range(n_chunks)], axis=1))
            lses.append(jnp.concatenate([sl_ref[c] for c in range(n_chunks)], axis=1))
        mx = functools.reduce(jnp.maximum, lses)
        ws = [jnp.exp(l - mx) for l in lses]
        att = functools.reduce(jnp.add, [w * o for w, o in zip(ws, os)]) / functools.reduce(jnp.add, ws)
    ag = ag_ref[0].astype(F32)
    u = (ag * jax.nn.sigmoid(ag) * att).astype(BF16)
    o_b = _dot(u, wa_ref[...])
    o_a = _dot(ret_ref[0].astype(BF16), wr_ref[...])
    merged = jax.nn.sigmoid(ga_ref[0].astype(F32)) * o_a + jax.nn.sigmoid(gb_ref[0].astype(F32)) * o_b
    o_ref[0] = x_ref[0] + _dot(merged.astype(BF16), wo_ref[...])


def _out_proj(x, ret, gates, wr, wa, wo, tm, att=None, groups=None):
    b, t, _ = x.shape
    row = lambda w, c: pl.BlockSpec((1, tm, w), lambda bi, i: (bi, i, c))
    full = lambda a: pl.BlockSpec(a.shape, lambda bi, i: (0, 0))
    in_specs = [row(D_MODEL, 0), row(RET_V_W, 0), row(D_MODEL, 0), row(D_MODEL, 1),
                row(ATT_OUT_W, 2 * D_MODEL // ATT_OUT_W), full(wr), full(wa), full(wo)]
    args = [x, ret, gates, gates, gates, wr, wa, wo]
    scratch = []
    if groups is None:
        dils = None
        in_specs.append(row(ATT_OUT_W, 0))
        args.append(att)
    else:
        dils = tuple(o.shape[1] for o, _ in groups)
        for (o, lse), d in zip(groups, dils):
            spec = pl.BlockSpec((1, d, tm // d, ATT_OUT_W), lambda bi, i: (bi, 0, i, 0))
            in_specs += [spec, spec]
            args += [o, lse]
            if d > 1:
                scratch += [pltpu.VMEM((ATT_OUT_W // LANES, tm, LANES), F32)] * 2
    return pl.pallas_call(
        functools.partial(_out_proj_kernel, dils=dils),
        grid=(b, t // tm),
        in_specs=in_specs,
        out_specs=row(D_MODEL, 0),
        out_shape=jax.ShapeDtypeStruct((b, t, D_MODEL), F32),
        scratch_shapes=scratch,
        compiler_params=pltpu.CompilerParams(
            dimension_semantics=("parallel", "parallel"), vmem_limit_bytes=VMEM_LIMIT),
        name="out_proj",
    )(*args)


def _rotary_tables(pos):
    half = RET_DK // 2
    inv = ROPE_BASE ** (-jnp.arange(half, dtype=F32) / half)
    ang = pos[:, None] * inv[None, :]
    cos, sin = jnp.cos(ang), jnp.sin(ang)
    return jnp.concatenate([cos, cos], axis=1), jnp.concatenate([-sin, sin], axis=1)


def _retention_tables(c, rows):
    log_g = jnp.log1p(-(2.0 ** (-5.0 - jnp.arange(RET_HEADS, dtype=F32))))
    i = jnp.arange(c, dtype=F32)
    diff = i[:, None] - i[None, :]
    decay = jnp.where(diff[None] >= 0, jnp.exp(jnp.maximum(diff, 0.0)[None] * log_g[:, None, None]), 0.0)
    q_decay = jnp.exp((i + 1.0)[None, :] * log_g[:, None])
    k_decay = jnp.exp((c - 1.0 - i)[None, :] * log_g[:, None])
    chunk_decay = jnp.exp(c * log_g)
    dec = jnp.zeros((RET_HEADS, rows, RET_CHUNK), F32).at[:, :c, :c].set(decay)
    qd = jnp.zeros((RET_HEADS, rows, 1), F32).at[:, :c, 0].set(q_decay)
    kd = jnp.zeros((RET_HEADS, RET_CHUNK, 1), F32).at[:, :c, 0].set(k_decay)
    cd = jnp.broadcast_to(chunk_decay[:, None, None], (RET_HEADS, 1, RET_DV))
    return dec, qd, kd, cd


def _t5_bucket(dist):
    max_exact = REL_BUCKETS // 2
    d = jnp.maximum(dist.astype(F32), 1.0)
    large = max_exact + (jnp.log(d / max_exact) / math.log(REL_MAX_DIST / max_exact)
                         * (REL_BUCKETS - max_exact)).astype(jnp.int32)
    large = jnp.minimum(large, REL_BUCKETS - 1)
    return jnp.where(dist < max_exact, dist, large)


def _group_bias(rel_bias, g, dil):
    dist = dil * jnp.arange(N_KEYS, dtype=jnp.int32)
    return rel_bias[_t5_bucket(dist)][:, g * ATT_HEADS:(g + 1) * ATT_HEADS].astype(F32).T


def _toeplitz(f, n_rows, n_cols):
    heads, period = f.shape
    assert period == n_rows + n_cols - 1
    g = jnp.roll(f, -(n_rows - 1), axis=1)
    flat = jnp.tile(g, (1, n_rows))[:, :n_rows * (period - 1)]
    return flat.reshape(heads, n_rows, period - 1)[:, :, :n_cols]


def _neg(heads, n):
    return jnp.full((heads, n), NEG, F32)


def _dilate(v, dil):
    heads, n = v.shape
    return jnp.stack([v] + [_neg(heads, n)] * (dil - 1), axis=-1).reshape(heads, n * dil)


def _prompt_bias(tb):
    heads = tb.shape[0]
    f = jnp.concatenate([_neg(heads, ATT_BLOCK - 1), tb[:, ::-1], _neg(heads, ATT_BLOCK - 1)], axis=1)
    return _toeplitz(f, ATT_BLOCK, 2 * ATT_BLOCK)


def _sample_bias(tb, win, dil):
    heads = tb.shape[0]
    p = SAMPLE_PAD
    f_c = jnp.concatenate([_neg(heads, p - 1), _dilate(tb[:, :0:-1], dil)], axis=1)
    bias_c = _toeplitz(f_c, p, win)
    slots = _dilate(tb[:, :p], dil)[:, :p]
    f_n = jnp.concatenate([slots[:, ::-1], _neg(heads, p - 1)], axis=1)
    bias_n = _toeplitz(f_n, p, p)
    return bias_c, jnp.pad(bias_n, ((0, 0), (0, 0), (0, ATT_BLOCK - p)), constant_values=NEG)


def _kv_rows(tail, g, n_rows):
    b, rows, _ = tail.shape
    kv = tail[:, rows - n_rows:, 2 * g * ATT_OUT_W:(2 * g + 2) * ATT_OUT_W]
    return kv.reshape(b, n_rows, 2, ATT_HEADS, ATT_DH)


def kernel(x_prompt, x_sample, cache_kv_w128, cache_kv_w512, cache_kv_w2048, state_retention,
           w_norm, w_in, q_norm, k_norm, rel_bias, ret_norm, w_proj_ret, w_proj_att, w_out):
    assert w_in.shape[0] == 1
    bp, t, _ = x_prompt.shape
    bs, ts, _ = x_sample.shape
    dils = tuple(d for _, d in ATT_GROUPS)
    assert t % (ATT_BLOCK * max(dils)) == 0 and ts <= SAMPLE_PAD
    caches = (cache_kv_w128[0], cache_kv_w512[0], cache_kv_w2048[0])
    for cch, (win, _) in zip(caches, ATT_GROUPS):
        assert cch.shape[1] == win and win <= PAST_LEN

    wn = w_norm[0].reshape(1, D_MODEL)
    w_in_bf = w_in[0].astype(BF16)
    qg = jnp.tile(q_norm[0] * (ATT_DH ** -0.5), TILE // ATT_DH).reshape(1, TILE)
    kg = jnp.tile(k_norm[0], TILE // ATT_DH).reshape(1, TILE)
    gn = ret_norm[0].reshape(1, RET_V_W)
    wr = w_proj_ret[0].astype(BF16)
    wa = w_proj_att[0].astype(BF16)
    wo = w_out[0].astype(BF16)
    hid = jnp.arange(TILE // 2) // ATT_DH
    seg_mean = jnp.where(hid[:, None] == hid[None, :], 1.0 / ATT_DH, 0.0).astype(BF16)
    group_bias = [_group_bias(rel_bias, g, d) for g, d in enumerate(dils)]

    cos_p, sin_p = _rotary_tables(jnp.arange(t, dtype=F32))
    tail_p_rows = min(max(w for w, _ in ATT_GROUPS), t)
    ret_in, *qkv_p, gates_p, tail_p = _in_proj(x_prompt.reshape(bp * t, D_MODEL), t, dils, tail_p_rows, BF16,
                                               wn, w_in_bf, cos_p, sin_p, qg, kg, seg_mean, tm=256)
    ret_p, state_p = _retention_prompt(ret_in.reshape(bp, t, RET_OUT_W),
                                       _retention_tables(RET_CHUNK, RET_CHUNK), gn)
    groups = [_attn_prompt_group(qkv_p[g], _prompt_bias(group_bias[g])) for g in range(N_GROUPS)]
    y_p = _out_proj(x_prompt, ret_p, gates_p.reshape(bp, t, GATE_W), wr, wa, wo, tm=256, groups=groups)

    pad = SAMPLE_PAD
    ns = bs * pad
    xs = jnp.pad(x_sample, ((0, 0), (0, pad - ts), (0, 0))).reshape(ns, D_MODEL)
    cos_s, sin_s = _rotary_tables(jnp.tile(PAST_LEN + jnp.arange(pad, dtype=F32), bs))
    ret_in_s, *qkv_s, gates_s, tail_s = _in_proj(xs, ns, (1,) * N_GROUPS, ns, F32, wn, w_in_bf, cos_s, sin_s,
                                                 qg, kg, seg_mean, tm=ns)
    ret_s, state_s = _retention_sample(ret_in_s.reshape(bs, pad, RET_OUT_W), state_retention[0],
                                       _retention_tables(ts, pad), gn)
    qkv_s = [a.reshape(bs, pad, QKV_W) for a in qkv_s]
    caches_t = [jnp.transpose(c, (0, 2, 3, 4, 1)) for c in caches]
    sb = [_sample_bias(group_bias[g], win, d) for g, (win, d) in enumerate(ATT_GROUPS)]
    att_s = _attn_sample(qkv_s, caches_t, [c for c, _ in sb], jnp.stack([n for _, n in sb]))
    y_s = _out_proj(xs.reshape(1, ns, D_MODEL), ret_s.reshape(1, ns, RET_V_W), gates_s.reshape(1, ns, GATE_W),
                    wr, wa, wo, tm=ns, att=att_s.reshape(1, ns, ATT_OUT_W))
    y_s = y_s.reshape(bs, pad, D_MODEL)[:, :ts]

    kv_p = [_kv_rows(tail_p, g, min(win, t))[None] for g, (win, _) in enumerate(ATT_GROUPS)]
    tail_s = tail_s.reshape(bs, pad, 2 * ATT_W)[:, :ts]
    kv_s = [_kv_rows(tail_s, g, ts)[None] for g in range(N_GROUPS)]
    return (y_p, y_s, state_p[None], state_s[None], kv_p[0], kv_p[1], kv_p[2], kv_s[0], kv_s[1], kv_s[2])
```

```python
import functools
import math

import jax
import jax.numpy as jnp
from jax import lax
from jax.experimental import pallas as pl
from jax.experimental.pallas import tpu as pltpu

D_MODEL = 1024
PAST_LEN = 16384
RET_HEADS = 4
RET_DK = 128
RET_DV = 256
RET_CHUNK = 128
ROPE_BASE = 10000.0
ATT_GROUPS = ((128, 1), (512, 4), (2048, 16))
N_GROUPS = 3
ATT_HEADS = 8
ATT_DH = 64
REL_BUCKETS = 32
REL_MAX_DIST = 2048
EPS = 1e-6

RET_QK_W = RET_HEADS * RET_DK
RET_V_W = RET_HEADS * RET_DV
ATT_W = N_GROUPS * ATT_HEADS * ATT_DH
ATT_OUT_W = ATT_HEADS * ATT_DH
IN_W = 2 * RET_QK_W + 2 * RET_V_W + 3 * ATT_W + ATT_OUT_W + 2 * D_MODEL

LANES = 128
TILE = 512
RET_OUT_W = 2 * RET_QK_W + 2 * RET_V_W
QKV_W = 3 * ATT_OUT_W
GATE_W = 2 * D_MODEL + ATT_OUT_W
ATT_BLOCK = 128
N_KEYS = ATT_BLOCK + 1
SAMPLE_PAD = 8
NEG = -1e30
VMEM_LIMIT = 48 * 1024 * 1024

F32 = jnp.float32
BF16 = jnp.bfloat16


def _nt_dot(a, b):
    return lax.dot_general(a, b, (((1,), (1,)), ((), ())), preferred_element_type=F32)


def _dot(a, b):
    return jnp.dot(a, b, preferred_element_type=F32)


def _proj_schedule():
    sched = [(0, 0, 0, "rot_q"), (RET_QK_W, 0, RET_QK_W, "rot_k")]
    for k in range(2 * RET_V_W // TILE):
        sched.append((2 * RET_QK_W + k * TILE, 0, 2 * RET_QK_W + k * TILE, "plain"))
    att0 = 2 * RET_QK_W + 2 * RET_V_W
    for g in range(N_GROUPS):
        for kind, epi in enumerate(("norm_q", "norm_k", "plain")):
            sched.append((att0 + kind * ATT_W + g * ATT_OUT_W, 1 + g, kind * ATT_OUT_W, epi))
    gate0 = att0 + 3 * ATT_W
    for k in range(2 * D_MODEL // TILE):
        sched.append((gate0 + ATT_OUT_W + k * TILE, 4, k * TILE, "plain"))
    sched.append((gate0, 4, 2 * D_MODEL, "plain"))
    return sched


_SCHEDULE = _proj_schedule()


def _in_proj_kernel(x_ref, wn_ref, w_ref, cos_ref, sin_ref, qg_ref, kg_ref, seg_ref,
                    ret_ref, a0_ref, a1_ref, a2_ref, gate_ref, tail_ref, h_ref, y_ref, *, dils):
    out_refs = (ret_ref, a0_ref, a1_ref, a2_ref, gate_ref)
    x = x_ref[...]
    tm = x.shape[0]
    ms = jnp.mean(x * x, axis=-1, keepdims=True)
    h_ref[...] = (x * lax.rsqrt(ms + EPS) * wn_ref[...]).astype(BF16)

    def rotary(y, scale):
        cos, sin = cos_ref[...], sin_ref[...]
        parts = []
        for hh in range(TILE // RET_DK):
            yh = y[:, hh * RET_DK:(hh + 1) * RET_DK]
            parts.append((yh * cos + pltpu.roll(yh, RET_DK // 2, axis=1) * sin) * scale)
        return jnp.concatenate(parts, axis=1)

    def head_rms(y, gain):
        y2 = (y * y).astype(BF16)
        half = TILE // 2
        ms = jnp.concatenate([_dot(y2[:, :half], seg_ref[...]), _dot(y2[:, half:], seg_ref[...])], axis=1)
        return y * lax.rsqrt(ms + EPS) * gain

    epilogues = {
        "plain": lambda y: y,
        "rot_q": lambda y: rotary(y, 1.0),
        "rot_k": lambda y: rotary(y, RET_DK ** -0.5),
        "norm_q": lambda y: head_rms(y, qg_ref[...]),
        "norm_k": lambda y: head_rms(y, kg_ref[...]),
    }

    for w_col, out_idx, out_col, epi in _SCHEDULE:
        val = epilogues[epi](_dot(h_ref[...], w_ref[:, w_col:w_col + TILE]))
        o_ref = out_refs[out_idx]
        ocs = slice(out_col, out_col + TILE)
        if not 1 <= out_idx <= N_GROUPS:
            o_ref[:, ocs] = val.astype(o_ref.dtype)
            continue
        g, kind = out_idx - 1, out_col // ATT_OUT_W
        if kind > 0:
            tcol = (2 * g + kind - 1) * ATT_OUT_W
            tail_ref[0, :, tcol:tcol + ATT_OUT_W] = val
        dil = dils[g]
        if dil == 1:
            o_ref[0, 0, :, ocs] = val.astype(o_ref.dtype)
            continue
        for c in range(TILE // LANES):
            y_ref[c] = val[:, c * LANES:(c + 1) * LANES]
            for r in range(dil):
                o_ref[0, r, :, out_col + c * LANES:out_col + (c + 1) * LANES] = (
                    y_ref[c, pl.ds(r, tm // dil, stride=dil), :].astype(o_ref.dtype))


def _in_proj(x2d, seq_len, dils, tail_rows, out_dtype, wn, w_bf, cos_t, sin_t, qg, kg, seg, tm):
    n = x2d.shape[0]
    batch = n // seq_len
    per_seq = seq_len // tm
    n_pos_blocks = cos_t.shape[0] // tm
    first_tail = (seq_len - tail_rows) // tm
    const = lambda shape: pl.BlockSpec(shape, lambda i: (0, 0))
    qkv_spec = lambda d: pl.BlockSpec((1, d, tm // d, QKV_W), lambda i: (i // per_seq, 0, i % per_seq, 0))
    tail_spec = pl.BlockSpec((1, tm, 2 * ATT_W),
                             lambda i: (i // per_seq, jnp.maximum(i % per_seq - first_tail, 0), 0))
    return pl.pallas_call(
        functools.partial(_in_proj_kernel, dils=dils),
        grid=(n // tm,),
        in_specs=[
            pl.BlockSpec((tm, D_MODEL), lambda i: (i, 0)),
            const((1, D_MODEL)),
            pl.BlockSpec(w_bf.shape, lambda i: (0, 0), pipeline_mode=pl.Buffered(1)),
            pl.BlockSpec((tm, RET_DK), lambda i: (i % n_pos_blocks, 0)),
            pl.BlockSpec((tm, RET_DK), lambda i: (i % n_pos_blocks, 0)),
            const((1, TILE)), const((1, TILE)), const((TILE // 2, TILE // 2)),
        ],
        out_specs=[pl.BlockSpec((tm, RET_OUT_W), lambda i: (i, 0))]
        + [qkv_spec(d) for d in dils]
        + [pl.BlockSpec((tm, GATE_W), lambda i: (i, 0)), tail_spec],
        out_shape=[jax.ShapeDtypeStruct((n, RET_OUT_W), out_dtype)]
        + [jax.ShapeDtypeStruct((batch, d, seq_len // d, QKV_W), out_dtype) for d in dils]
        + [jax.ShapeDtypeStruct((n, GATE_W), out_dtype),
           jax.ShapeDtypeStruct((batch, tail_rows, 2 * ATT_W), F32)],
        scratch_shapes=[pltpu.VMEM((tm, D_MODEL), BF16), pltpu.VMEM((TILE // LANES, tm, LANES), F32)],
        compiler_params=pltpu.CompilerParams(
            dimension_semantics=("arbitrary",), vmem_limit_bytes=VMEM_LIMIT),
        name="in_proj",
    )(x2d, wn, w_bf, cos_t, sin_t, qg, kg, seg)


def _group_norm_gate(o, gain, gate):
    gate = gate.astype(F32)
    mu = jnp.mean(o, axis=-1, keepdims=True)
    d = o - mu
    var = jnp.mean(d * d, axis=-1, keepdims=True)
    return gate * jax.nn.sigmoid(gate) * (d * lax.rsqrt(var + EPS) * gain)


def _retention_head(q, k_keys, v_keys, state, dec, qd, kd, cd):
    qb = q.astype(BF16)
    vb = v_keys.astype(BF16)
    scores = _nt_dot(qb, k_keys.astype(BF16)) * dec
    o = _dot(scores.astype(BF16), vb) + _dot(qb, state.astype(BF16)) * qd
    kt = jnp.transpose(k_keys * kd).astype(BF16)
    new_state = state * cd + _dot(kt, vb)
    return o, new_state


def _retention_prompt_kernel(q_ref, k_ref, v_ref, g_ref, dec_ref, qd_ref, kd_ref, cd_ref, gn_ref,
                             o_ref, s_ref):
    @pl.when(pl.program_id(1) == 0)
    def _():
        s_ref[...] = jnp.zeros_like(s_ref)

    for h in range(RET_HEADS):
        ks = slice(h * RET_DK, (h + 1) * RET_DK)
        vs = slice(h * RET_DV, (h + 1) * RET_DV)
        o, new_state = _retention_head(q_ref[0, :, ks], k_ref[0, :, ks], v_ref[0, :, vs], s_ref[0, h],
                                       dec_ref[h], qd_ref[h], kd_ref[h], cd_ref[h])
        s_ref[0, h] = new_state
        o_ref[0, :, vs] = _group_norm_gate(o, gn_ref[:, vs], g_ref[0, :, vs]).astype(o_ref.dtype)


def _retention_specs(rows, idx):
    return [
        pl.BlockSpec((1, rows, RET_QK_W), lambda *a: (*idx(*a), 0)),
        pl.BlockSpec((1, rows, RET_QK_W), lambda *a: (*idx(*a), 1)),
        pl.BlockSpec((1, rows, RET_V_W), lambda *a: (*idx(*a), 1)),
        pl.BlockSpec((1, rows, RET_V_W), lambda *a: (*idx(*a), 2)),
    ]


def _table_specs(tables, ndim_grid):
    return [pl.BlockSpec(t.shape, lambda *a, nd=t.ndim: (0,) * nd) for t in tables]


def _retention_prompt(ret, tables, gn):
    b, t, _ = ret.shape
    c = RET_CHUNK
    return pl.pallas_call(
        _retention_prompt_kernel,
        grid=(b, t // c),
        in_specs=_retention_specs(c, lambda bi, ci: (bi, ci)) + _table_specs(tables + (gn,), 2),
        out_specs=[
            pl.BlockSpec((1, c, RET_V_W), lambda bi, ci: (bi, ci, 0)),
            pl.BlockSpec((1, RET_HEADS, RET_DK, RET_DV), lambda bi, ci: (bi, 0, 0, 0)),
        ],
        out_shape=[
            jax.ShapeDtypeStruct((b, t, RET_V_W), ret.dtype),
            jax.ShapeDtypeStruct((b, RET_HEADS, RET_DK, RET_DV), F32),
        ],
        compiler_params=pltpu.CompilerParams(
            dimension_semantics=("parallel", "arbitrary"), vmem_limit_bytes=VMEM_LIMIT),
        name="retention_prompt",
    )(ret, ret, ret, ret, *tables, gn)


def _retention_sample_kernel(q_ref, k_ref, v_ref, g_ref, s_in_ref, dec_ref, qd_ref, kd_ref, cd_ref,
                             gn_ref, o_ref, s_out_ref, kpad_ref, vpad_ref):
    @pl.when(pl.program_id(0) == 0)
    def _():
        kpad_ref[...] = jnp.zeros_like(kpad_ref)
        vpad_ref[...] = jnp.zeros_like(vpad_ref)

    kpad_ref[0:SAMPLE_PAD, :] = k_ref[0]
    vpad_ref[0:SAMPLE_PAD, :] = v_ref[0]
    for h in range(RET_HEADS):
        ks = slice(h * RET_DK, (h + 1) * RET_DK)
        vs = slice(h * RET_DV, (h + 1) * RET_DV)
        o, new_state = _retention_head(q_ref[0, :, ks], kpad_ref[:, ks], vpad_ref[:, vs], s_in_ref[0, h],
                                       dec_ref[h], qd_ref[h], kd_ref[h], cd_ref[h])
        s_out_ref[0, h] = new_state
        o_ref[0, :, vs] = _group_norm_gate(o, gn_ref[:, vs], g_ref[0, :, vs]).astype(o_ref.dtype)


def _retention_sample(ret, state, tables, gn):
    b, p, _ = ret.shape
    state_spec = pl.BlockSpec((1, RET_HEADS, RET_DK, RET_DV), lambda bi: (bi, 0, 0, 0))
    return pl.pallas_call(
        _retention_sample_kernel,
        grid=(b,),
        in_specs=_retention_specs(p, lambda bi: (bi, 0)) + [state_spec] + _table_specs(tables + (gn,), 1),
        out_specs=[pl.BlockSpec((1, p, RET_V_W), lambda bi: (bi, 0, 0)), state_spec],
        out_shape=[
            jax.ShapeDtypeStruct((b, p, RET_V_W), ret.dtype),
            jax.ShapeDtypeStruct(state.shape, F32),
        ],
        scratch_shapes=[pltpu.VMEM((RET_CHUNK, RET_QK_W), F32), pltpu.VMEM((RET_CHUNK, RET_V_W), F32)],
        compiler_params=pltpu.CompilerParams(
            dimension_semantics=("arbitrary",), vmem_limit_bytes=VMEM_LIMIT),
        name="retention_sample",
    )(ret, ret, ret, ret, state, *tables, gn)


def _attn_prompt_blocks(q_ref, kp_ref, kc_ref, vp_ref, vc_ref, bias_ref, o_ref, lse_ref, first):
    n_blocks = q_ref.shape[2] // ATT_BLOCK
    pair_w = 2 * ATT_DH
    low_q = lax.broadcasted_iota(jnp.int32, (ATT_BLOCK, pair_w), 1) < ATT_DH
    rows0, rows1 = slice(0, ATT_BLOCK), slice(ATT_BLOCK, 2 * ATT_BLOCK)

    def keys(prev_ref, cur_ref, t, ps):
        if t > 0:
            return cur_ref[0, 0, (t - 1) * ATT_BLOCK:(t + 1) * ATT_BLOCK, ps]
        if first:
            return cur_ref[0, 0, 0:ATT_BLOCK, ps]
        return jnp.concatenate([prev_ref[0, 0, :, ps], cur_ref[0, 0, 0:ATT_BLOCK, ps]], axis=0)

    staged = []
    for t in range(n_blocks):
        for p in range(ATT_HEADS // 2):
            ps = slice(p * pair_w, (p + 1) * pair_w)
            qp = q_ref[0, 0, t * ATT_BLOCK:(t + 1) * ATT_BLOCK, ps]
            kcat = keys(kp_ref, kc_ref, t, ps)
            q2 = jnp.concatenate([jnp.where(low_q, qp, 0.0), jnp.where(low_q, 0.0, qp)], axis=0)
            s = _nt_dot(q2, kcat) + bias_ref[p, :, 2 * ATT_BLOCK - kcat.shape[0]:]
            m = jnp.max(s, axis=-1, keepdims=True)
            staged.append((t, p, m, jnp.exp(s - m).astype(BF16)))
    for t, p, m, e in staged:
        ps = slice(p * pair_w, (p + 1) * pair_w)
        vcat = keys(vp_ref, vc_ref, t, ps)
        low_k = lax.broadcasted_iota(jnp.int32, vcat.shape, 1) < ATT_DH
        p0 = _dot(e[rows0], jnp.where(low_k, vcat, 1.0))
        p1 = _dot(e[rows1], jnp.where(low_k, 1.0, vcat))
        den = pltpu.roll(jnp.where(low_q, p1, p0), ATT_DH, axis=1)
        ts = slice(t * ATT_BLOCK, (t + 1) * ATT_BLOCK)
        o_ref[0, 0, ts, ps] = jnp.where(low_q, p0, p1) / den
        lse_ref[0, 0, ts, ps] = jnp.where(low_q, m[rows0], m[rows1]) + jnp.log(den)


def _attn_prompt_kernel(q_ref, kp_ref, kc_ref, vp_ref, vc_ref, bias_ref, o_ref, lse_ref):
    first = pl.program_id(2) == 0

    @pl.when(first)
    def _():
        _attn_prompt_blocks(q_ref, kp_ref, kc_ref, vp_ref, vc_ref, bias_ref, o_ref, lse_ref, True)

    @pl.when(jnp.logical_not(first))
    def _():
        _attn_prompt_blocks(q_ref, kp_ref, kc_ref, vp_ref, vc_ref, bias_ref, o_ref, lse_ref, False)


def _attn_prompt_group(qkv, bias, n_blocks):
    b, dil, tr, _ = qkv.shape
    rows = n_blocks * ATT_BLOCK
    cur = lambda c: pl.BlockSpec((1, 1, rows, ATT_OUT_W), lambda bi, r, j: (bi, r, j, c))
    prev = lambda c: pl.BlockSpec((1, 1, ATT_BLOCK, ATT_OUT_W),
                                  lambda bi, r, j: (bi, r, jnp.maximum(j * n_blocks - 1, 0), c))
    res_shape = jax.ShapeDtypeStruct((b, dil, tr, ATT_OUT_W), F32)
    return pl.pallas_call(
        _attn_prompt_kernel,
        grid=(b, dil, tr // rows),
        in_specs=[cur(0), prev(1), cur(1), prev(2), cur(2),
                  pl.BlockSpec(bias.shape, lambda bi, r, j: (0, 0, 0))],
        out_specs=[cur(0), cur(0)],
        out_shape=[res_shape, res_shape],
        compiler_params=pltpu.CompilerParams(
            dimension_semantics=("parallel", "parallel", "arbitrary"), vmem_limit_bytes=VMEM_LIMIT),
        name=f"attn_prompt_d{dil}",
    )(qkv, qkv, qkv, qkv, qkv, bias)


def _attn_sample_kernel(a0_ref, a1_ref, a2_ref, c0_ref, c1_ref, c2_ref, b0_ref, b1_ref, b2_ref, bn_ref,
                        o_ref, kn_ref, vn_ref):
    qkv_refs = (a0_ref, a1_ref, a2_ref)
    cache_refs = (c0_ref, c1_ref, c2_ref)
    bias_refs = (b0_ref, b1_ref, b2_ref)

    @pl.when(pl.program_id(0) == 0)
    def _():
        kn_ref[...] = jnp.zeros_like(kn_ref)
        vn_ref[...] = jnp.zeros_like(vn_ref)

    for g in range(N_GROUPS):
        gs = slice(g * ATT_OUT_W, (g + 1) * ATT_OUT_W)
        kn_ref[0:SAMPLE_PAD, gs] = qkv_refs[g][0, :, ATT_OUT_W:2 * ATT_OUT_W]
        vn_ref[0:SAMPLE_PAD, gs] = qkv_refs[g][0, :, 2 * ATT_OUT_W:3 * ATT_OUT_W]

    for h in range(ATT_HEADS):
        logits, values = [], []
        for g in range(N_GROUPS):
            hs = slice(g * ATT_OUT_W + h * ATT_DH, g * ATT_OUT_W + (h + 1) * ATT_DH)
            qh = qkv_refs[g][0, :, h * ATT_DH:(h + 1) * ATT_DH].astype(BF16)
            logits.append(_dot(qh, cache_refs[g][0, 0, h].astype(BF16)) + bias_refs[g][h])
            values.append((cache_refs[g][0, 1, h].astype(BF16), True))
            logits.append(_nt_dot(qh, kn_ref[:, hs].astype(BF16)) + bn_ref[g, h])
            values.append((vn_ref[:, hs].astype(BF16), False))
        m = functools.reduce(jnp.maximum, [jnp.max(x, axis=-1, keepdims=True) for x in logits])
        es = [jnp.exp(x - m) for x in logits]
        l = functools.reduce(jnp.add, [jnp.sum(e, axis=-1, keepdims=True) for e in es])
        acc = jnp.zeros((SAMPLE_PAD, ATT_DH), F32)
        for e, (v, transposed) in zip(es, values):
            eb = e.astype(BF16)
            acc = acc + (_nt_dot(eb, v) if transposed else _dot(eb, v))
        o_ref[0, :, h * ATT_DH:(h + 1) * ATT_DH] = acc / l


def _attn_sample(qkvs, caches_t, biases, bias_new):
    b, p, _ = qkvs[0].shape
    return pl.pallas_call(
        _attn_sample_kernel,
        grid=(b,),
        in_specs=[pl.BlockSpec((1, p, QKV_W), lambda bi: (bi, 0, 0)) for _ in qkvs]
        + [pl.BlockSpec((1,) + c.shape[1:], lambda bi: (bi, 0, 0, 0, 0)) for c in caches_t]
        + [pl.BlockSpec(x.shape, lambda bi: (0, 0, 0)) for x in biases]
        + [pl.BlockSpec(bias_new.shape, lambda bi: (0, 0, 0, 0))],
        out_specs=pl.BlockSpec((1, p, ATT_OUT_W), lambda bi: (bi, 0, 0)),
        out_shape=jax.ShapeDtypeStruct((b, p, ATT_OUT_W), F32),
        scratch_shapes=[pltpu.VMEM((ATT_BLOCK, ATT_W), F32), pltpu.VMEM((ATT_BLOCK, ATT_W), F32)],
        compiler_params=pltpu.CompilerParams(
            dimension_semantics=("arbitrary",), vmem_limit_bytes=VMEM_LIMIT),
        name="attn_sample",
    )(*qkvs, *caches_t, *biases, bias_new)


def _out_proj_kernel(*refs, dils):
    x_ref, ret_ref, ga_ref, gb_ref, ag_ref, wr_ref, wa_ref, wo_ref = refs[:8]
    if dils is None:
        att_ref, o_ref = refs[8:]
        att = att_ref[0]
    else:
        group_refs = refs[8:8 + 2 * N_GROUPS]
        o_ref = refs[8 + 2 * N_GROUPS]
        scratch = refs[9 + 2 * N_GROUPS:]
        tm = x_ref.shape[1]
        os, lses = [], []
        for g, dil in enumerate(dils):
            og_ref, lg_ref = group_refs[2 * g], group_refs[2 * g + 1]
            if dil == 1:
                os.append(og_ref[0, 0]); lses.append(lg_ref[0, 0])
                continue
            so_ref, sl_ref = scratch[2 * (g - 1)], scratch[2 * (g - 1) + 1]
            n_chunks = ATT_OUT_W // LANES
            for c in range(n_chunks):
                cs = slice(c * LANES, (c + 1) * LANES)
                for r in range(dil):
                    so_ref[c, pl.ds(r, tm // dil, stride=dil), :] = og_ref[0, r, :, cs]
                    sl_ref[c, pl.ds(r, tm // dil, stride=dil), :] = lg_ref[0, r, :, cs]
            os.append(jnp.concatenate([so_ref[c] for c in range(n_chunks)], axis=1))
            lses.append(jnp.concatenate([sl_ref[c] for c in range(n_chunks)], axis=1))
        mx = functools.reduce(jnp.maximum, lses)
        ws = [jnp.exp(l - mx) for l in lses]
        att = functools.reduce(jnp.add, [w * o for w, o in zip(ws, os)]) / functools.reduce(jnp.add, ws)
    ag = ag_ref[0].astype(F32)
    u = (ag * jax.nn.sigmoid(ag) * att).astype(BF16)
    o_b = _dot(u, wa_ref[...])
    o_a = _dot(ret_ref[0].astype(BF16), wr_ref[...])
    merged = jax.nn.sigmoid(ga_ref[0].astype(F32)) * o_a + jax.nn.sigmoid(gb_ref[0].astype(F32)) * o_b
    o_ref[0] = x_ref[0] + _dot(merged.astype(BF16), wo_ref[...])


def _out_proj(x, ret, gates, wr, wa, wo, tm, att=None, groups=None):
    b, t, _ = x.shape
    row = lambda w, c: pl.BlockSpec((1, tm, w), lambda bi, i: (bi, i, c))
    full = lambda a: pl.BlockSpec(a.shape, lambda bi, i: (0, 0))
    in_specs = [row(D_MODEL, 0), row(RET_V_W, 0), row(D_MODEL, 0), row(D_MODEL, 1),
                row(ATT_OUT_W, 2 * D_MODEL // ATT_OUT_W), full(wr), full(wa), full(wo)]
    args = [x, ret, gates, gates, gates, wr, wa, wo]
    scratch = []
    if groups is None:
        dils = None
        in_specs.append(row(ATT_OUT_W, 0))
        args.append(att)
    else:
        dils = tuple(o.shape[1] for o, _ in groups)
        for (o, lse), d in zip(groups, dils):
            spec = pl.BlockSpec((1, d, tm // d, ATT_OUT_W), lambda bi, i: (bi, 0, i, 0))
            in_specs += [spec, spec]
            args += [o, lse]
            if d > 1:
                scratch += [pltpu.VMEM((ATT_OUT_W // LANES, tm, LANES), F32)] * 2
    return pl.pallas_call(
        functools.partial(_out_proj_kernel, dils=dils),
        grid=(b, t // tm),
        in_specs=in_specs,
        out_specs=row(D_MODEL, 0),
        out_shape=jax.ShapeDtypeStruct((b, t, D_MODEL), F32),
        scratch_shapes=scratch,
        compiler_params=pltpu.CompilerParams(
            dimension_semantics=("parallel", "parallel"), vmem_limit_bytes=VMEM_LIMIT),
        name="out_proj",
    )(*args)


def _rotary_tables(pos):
    half = RET_DK // 2
    inv = ROPE_BASE ** (-jnp.arange(half, dtype=F32) / half)
    ang = pos[:, None] * inv[None, :]
    cos, sin = jnp.cos(ang), jnp.sin(ang)
    return jnp.concatenate([cos, cos], axis=1), jnp.concatenate([-sin, sin], axis=1)


def _retention_tables(c, rows):
    log_g = jnp.log1p(-(2.0 ** (-5.0 - jnp.arange(RET_HEADS, dtype=F32))))
    i = jnp.arange(c, dtype=F32)
    diff = i[:, None] - i[None, :]
    decay = jnp.where(diff[None] >= 0, jnp.exp(jnp.maximum(diff, 0.0)[None] * log_g[:, None, None]), 0.0)
    q_decay = jnp.exp((i + 1.0)[None, :] * log_g[:, None])
    k_decay = jnp.exp((c - 1.0 - i)[None, :] * log_g[:, None])
    chunk_decay = jnp.exp(c * log_g)
    dec = jnp.zeros((RET_HEADS, rows, RET_CHUNK), F32).at[:, :c, :c].set(decay)
    qd = jnp.zeros((RET_HEADS, rows, 1), F32).at[:, :c, 0].set(q_decay)
    kd = jnp.zeros((RET_HEADS, RET_CHUNK, 1), F32).at[:, :c, 0].set(k_decay)
    cd = jnp.broadcast_to(chunk_decay[:, None, None], (RET_HEADS, 1, RET_DV))
    return dec, qd, kd, cd


def _t5_bucket(dist):
    max_exact = REL_BUCKETS // 2
    d = jnp.maximum(dist.astype(F32), 1.0)
    large = max_exact + (jnp.log(d / max_exact) / math.log(REL_MAX_DIST / max_exact)
                         * (REL_BUCKETS - max_exact)).astype(jnp.int32)
    large = jnp.minimum(large, REL_BUCKETS - 1)
    return jnp.where(dist < max_exact, dist, large)


def _group_bias(rel_bias, g, dil):
    dist = dil * jnp.arange(N_KEYS, dtype=jnp.int32)
    return rel_bias[_t5_bucket(dist)][:, g * ATT_HEADS:(g + 1) * ATT_HEADS].astype(F32).T


def _toeplitz(f, n_rows, n_cols):
    heads, period = f.shape
    assert period == n_rows + n_cols - 1
    g = jnp.roll(f, -(n_rows - 1), axis=1)
    flat = jnp.tile(g, (1, n_rows))[:, :n_rows * (period - 1)]
    return flat.reshape(heads, n_rows, period - 1)[:, :, :n_cols]


def _neg(heads, n):
    return jnp.full((heads, n), NEG, F32)


def _dilate(v, dil):
    heads, n = v.shape
    return jnp.stack([v] + [_neg(heads, n)] * (dil - 1), axis=-1).reshape(heads, n * dil)


def _prompt_bias(tb):
    heads = tb.shape[0]
    f = jnp.concatenate([_neg(heads, ATT_BLOCK - 1), tb[:, ::-1], _neg(heads, ATT_BLOCK - 1)], axis=1)
    return _toeplitz(f, ATT_BLOCK, 2 * ATT_BLOCK).reshape(heads // 2, 2 * ATT_BLOCK, 2 * ATT_BLOCK)


def _sample_bias(tb, win, dil):
    heads = tb.shape[0]
    p = SAMPLE_PAD
    f_c = jnp.concatenate([_neg(heads, p - 1), _dilate(tb[:, :0:-1], dil)], axis=1)
    bias_c = _toeplitz(f_c, p, win)
    slots = _dilate(tb[:, :p], dil)[:, :p]
    f_n = jnp.concatenate([slots[:, ::-1], _neg(heads, p - 1)], axis=1)
    bias_n = _toeplitz(f_n, p, p)
    return bias_c, jnp.pad(bias_n, ((0, 0), (0, 0), (0, ATT_BLOCK - p)), constant_values=NEG)


def _kv_rows(tail, g, n_rows):
    b, rows, _ = tail.shape
    kv = tail[:, rows - n_rows:, 2 * g * ATT_OUT_W:(2 * g + 2) * ATT_OUT_W]
    return kv.reshape(b, n_rows, 2, ATT_HEADS, ATT_DH)


def kernel(x_prompt, x_sample, cache_kv_w128, cache_kv_w512, cache_kv_w2048, state_retention,
           w_norm, w_in, q_norm, k_norm, rel_bias, ret_norm, w_proj_ret, w_proj_att, w_out):
    assert w_in.shape[0] == 1
    bp, t, _ = x_prompt.shape
    bs, ts, _ = x_sample.shape
    dils = tuple(d for _, d in ATT_GROUPS)
    assert t % (ATT_BLOCK * max(dils)) == 0 and ts <= SAMPLE_PAD
    caches = (cache_kv_w128[0], cache_kv_w512[0], cache_kv_w2048[0])
    for cch, (win, _) in zip(caches, ATT_GROUPS):
        assert cch.shape[1] == win and win <= PAST_LEN

    wn = w_norm[0].reshape(1, D_MODEL)
    w_in_bf = w_in[0].astype(BF16)
    qg = jnp.tile(q_norm[0] * (ATT_DH ** -0.5), TILE // ATT_DH).reshape(1, TILE)
    kg = jnp.tile(k_norm[0], TILE // ATT_DH).reshape(1, TILE)
    gn = ret_norm[0].reshape(1, RET_V_W)
    wr = w_proj_ret[0].astype(BF16)
    wa = w_proj_att[0].astype(BF16)
    wo = w_out[0].astype(BF16)
    hid = jnp.arange(TILE // 2) // ATT_DH
    seg_mean = jnp.where(hid[:, None] == hid[None, :], 1.0 / ATT_DH, 0.0).astype(BF16)
    group_bias = [_group_bias(rel_bias, g, d) for g, d in enumerate(dils)]

    cos_p, sin_p = _rotary_tables(jnp.arange(t, dtype=F32))
    tail_p_rows = min(max(w for w, _ in ATT_GROUPS), t)
    ret_in, *qkv_p, gates_p, tail_p = _in_proj(x_prompt.reshape(bp * t, D_MODEL), t, dils, tail_p_rows, BF16,
                                               wn, w_in_bf, cos_p, sin_p, qg, kg, seg_mean, tm=256)
    ret_p, state_p = _retention_prompt(ret_in.reshape(bp, t, RET_OUT_W),
                                       _retention_tables(RET_CHUNK, RET_CHUNK), gn)
    groups = [_attn_prompt_group(qkv_p[g], _prompt_bias(group_bias[g]), n_blocks=2) for g in range(N_GROUPS)]
    y_p = _out_proj(x_prompt, ret_p, gates_p.reshape(bp, t, GATE_W), wr, wa, wo, tm=256, groups=groups)

    pad = SAMPLE_PAD
    ns = bs * pad
    xs = jnp.pad(x_sample, ((0, 0), (0, pad - ts), (0, 0))).reshape(ns, D_MODEL)
    cos_s, sin_s = _rotary_tables(jnp.tile(PAST_LEN + jnp.arange(pad, dtype=F32), bs))
    ret_in_s, *qkv_s, gates_s, tail_s = _in_proj(xs, ns, (1,) * N_GROUPS, ns, F32, wn, w_in_bf, cos_s, sin_s,
                                                 qg, kg, seg_mean, tm=ns)
    ret_s, state_s = _retention_sample(ret_in_s.reshape(bs, pad, RET_OUT_W), state_retention[0],
                                       _retention_tables(ts, pad), gn)
    qkv_s = [a.reshape(bs, pad, QKV_W) for a in qkv_s]
    caches_t = [jnp.transpose(c, (0, 2, 3, 4, 1)) for c in caches]
    sb = [_sample_bias(group_bias[g], win, d) for g, (win, d) in enumerate(ATT_GROUPS)]
    att_s = _attn_sample(qkv_s, caches_t, [c for c, _ in sb], jnp.stack([n for _, n in sb]))
    y_s = _out_proj(xs.reshape(1, ns, D_MODEL), ret_s.reshape(1, ns, RET_V_W), gates_s.reshape(1, ns, GATE_W),
                    wr, wa, wo, tm=ns, att=att_s.reshape(1, ns, ATT_OUT_W))
    y_s = y_s.reshape(bs, pad, D_MODEL)[:, :ts]

    kv_p = [_kv_rows(tail_p, g, min(win, t))[None] for g, (win, _) in enumerate(ATT_GROUPS)]
    tail_s = tail_s.reshape(bs, pad, 2 * ATT_W)[:, :ts]
    kv_s = [_kv_rows(tail_s, g, ts)[None] for g in range(N_GROUPS)]
    return (y_p, y_s, state_p[None], state_s[None], kv_p[0], kv_p[1], kv_p[2], kv_s[0], kv_s[1], kv_s[2])
```

```python
import functools
import math

import jax
import jax.numpy as jnp
from jax import lax
from jax.experimental import pallas as pl
from jax.experimental.pallas import tpu as pltpu

D_MODEL = 1024
PAST_LEN = 16384
RET_HEADS = 4
RET_DK = 128
RET_DV = 256
RET_CHUNK = 128
ROPE_BASE = 10000.0
ATT_GROUPS = ((128, 1), (512, 4), (2048, 16))
N_GROUPS = 3
ATT_HEADS = 8
ATT_DH = 64
REL_BUCKETS = 32
REL_MAX_DIST = 2048
EPS = 1e-6

RET_QK_W = RET_HEADS * RET_DK
RET_V_W = RET_HEADS * RET_DV
ATT_W = N_GROUPS * ATT_HEADS * ATT_DH
ATT_OUT_W = ATT_HEADS * ATT_DH
IN_W = 2 * RET_QK_W + 2 * RET_V_W + 3 * ATT_W + ATT_OUT_W + 2 * D_MODEL

LANES = 128
TILE = 512
RET_OUT_W = 2 * RET_QK_W + 2 * RET_V_W
QKV_W = 3 * ATT_OUT_W
GATE_W = 2 * D_MODEL + ATT_OUT_W
ATT_BLOCK = 128
N_KEYS = ATT_BLOCK + 1
SAMPLE_PAD = 8
NEG = -1e30
VMEM_LIMIT = 48 * 1024 * 1024

F32 = jnp.float32
BF16 = jnp.bfloat16


def _nt_dot(a, b):
    return lax.dot_general(a, b, (((1,), (1,)), ((), ())), preferred_element_type=F32)


def _dot(a, b):
    return jnp.dot(a, b, preferred_element_type=F32)


def _proj_schedule():
    sched = [(0, 0, 0, "rot_q"), (RET_QK_W, 0, RET_QK_W, "rot_k")]
    for k in range(2 * RET_V_W // TILE):
        sched.append((2 * RET_QK_W + k * TILE, 0, 2 * RET_QK_W + k * TILE, "plain"))
    att0 = 2 * RET_QK_W + 2 * RET_V_W
    for g in range(N_GROUPS):
        for kind, epi in enumerate(("norm_q", "norm_k", "plain")):
            sched.append((att0 + kind * ATT_W + g * ATT_OUT_W, 1 + g, kind * ATT_OUT_W, epi))
    gate0 = att0 + 3 * ATT_W
    for k in range(2 * D_MODEL // TILE):
        sched.append((gate0 + ATT_OUT_W + k * TILE, 4, k * TILE, "plain"))
    sched.append((gate0, 4, 2 * D_MODEL, "plain"))
    return sched


_SCHEDULE = _proj_schedule()


def _in_proj_kernel(*refs, dils, per_seq, fuse_retention):
    (x_ref, wn_ref, w_ref, ca_ref, sa_ref, cb_ref, sb_ref, cbs_ref, sbs_ref,
     qg_ref, kg_ref, seg_ref) = refs[:12]
    refs = refs[12:]
    if fuse_retention:
        dec_ref, qd_ref, kd_ref, cd_ref, gn_ref, ret_ref, s_ref = refs[:7]
        refs = refs[7:]
    else:
        ret_ref = refs[0]
        refs = refs[1:]
    a0_ref, a1_ref, a2_ref, gate_ref, tail_ref, h_ref, y_ref = refs[:7]
    r_ref = refs[7] if fuse_retention else ret_ref
    out_refs = (r_ref, a0_ref, a1_ref, a2_ref, gate_ref)

    if fuse_retention:
        @pl.when(pl.program_id(0) % per_seq == 0)
        def _():
            s_ref[...] = jnp.zeros_like(s_ref)

    x = x_ref[...]
    tm = x.shape[0]
    ms = jnp.mean(x * x, axis=-1, keepdims=True)
    h_ref[...] = (x * lax.rsqrt(ms + EPS) * wn_ref[...]).astype(BF16)

    ca, sa = ca_ref[0], sa_ref[0]
    cos = ca * cb_ref[...] - sa * sb_ref[...]
    sin = sa * cbs_ref[...] + ca * sbs_ref[...]

    def rotary(y, scale):
        parts = []
        for hh in range(TILE // RET_DK):
            yh = y[:, hh * RET_DK:(hh + 1) * RET_DK]
            parts.append((yh * cos + pltpu.roll(yh, RET_DK // 2, axis=1) * sin) * scale)
        return jnp.concatenate(parts, axis=1)

    def head_rms(y, gain):
        y2 = (y * y).astype(BF16)
        half = TILE // 2
        ms = jnp.concatenate([_dot(y2[:, :half], seg_ref[...]), _dot(y2[:, half:], seg_ref[...])], axis=1)
        return y * lax.rsqrt(ms + EPS) * gain

    epilogues = {
        "plain": lambda y: y,
        "rot_q": lambda y: rotary(y, 1.0),
        "rot_k": lambda y: rotary(y, RET_DK ** -0.5),
        "norm_q": lambda y: head_rms(y, qg_ref[...]),
        "norm_k": lambda y: head_rms(y, kg_ref[...]),
    }

    def retention():
        for c in range(tm // RET_CHUNK):
            rows = slice(c * RET_CHUNK, (c + 1) * RET_CHUNK)
            for h in range(RET_HEADS):
                ks = slice(h * RET_DK, (h + 1) * RET_DK)
                ks2 = slice(RET_QK_W + h * RET_DK, RET_QK_W + (h + 1) * RET_DK)
                vs = slice(h * RET_DV, (h + 1) * RET_DV)
                vs_in = slice(2 * RET_QK_W + h * RET_DV, 2 * RET_QK_W + (h + 1) * RET_DV)
                gs_in = slice(2 * RET_QK_W + RET_V_W + h * RET_DV, 2 * RET_QK_W + RET_V_W + (h + 1) * RET_DV)
                o, new_state = _retention_head(r_ref[rows, ks], r_ref[rows, ks2], r_ref[rows, vs_in], s_ref[0, h],
                                               dec_ref[h], qd_ref[h], kd_ref[h], cd_ref[h])
                s_ref[0, h] = new_state
                ret_ref[rows, vs] = _group_norm_gate(o, gn_ref[:, vs], r_ref[rows, gs_in]).astype(ret_ref.dtype)

    n_ret_tiles = sum(1 for t in _SCHEDULE if t[1] == 0)
    for step, (w_col, out_idx, out_col, epi) in enumerate(_SCHEDULE):
        if fuse_retention and step == n_ret_tiles:
            retention()
        val = epilogues[epi](_dot(h_ref[...], w_ref[:, w_col:w_col + TILE]))
        o_ref = out_refs[out_idx]
        ocs = slice(out_col, out_col + TILE)
        if not 1 <= out_idx <= N_GROUPS:
            o_ref[:, ocs] = val.astype(o_ref.dtype)
            continue
        g, kind = out_idx - 1, out_col // ATT_OUT_W
        if kind > 0:
            tcol = (2 * g + kind - 1) * ATT_OUT_W
            tail_ref[0, :, tcol:tcol + ATT_OUT_W] = val
        dil = dils[g]
        if dil == 1:
            o_ref[0, 0, :, ocs] = val.astype(o_ref.dtype)
            continue
        for c in range(TILE // LANES):
            y_ref[c] = val[:, c * LANES:(c + 1) * LANES]
            for r in range(dil):
                o_ref[0, r, :, out_col + c * LANES:out_col + (c + 1) * LANES] = (
                    y_ref[c, pl.ds(r, tm // dil, stride=dil), :].astype(o_ref.dtype))


def _in_proj(x2d, seq_len, dils, tail_rows, out_dtype, wn, w_bf, rot, qg, kg, seg, tm, retention=None):
    n = x2d.shape[0]
    batch = n // seq_len
    per_seq = seq_len // tm
    first_tail = (seq_len - tail_rows) // tm
    fuse = retention is not None
    const = lambda a: pl.BlockSpec(a.shape, lambda i, nd=a.ndim: (0,) * nd)
    base_spec = pl.BlockSpec((1, 1, RET_DK), lambda i: (i % per_seq, 0, 0))
    qkv_spec = lambda d: pl.BlockSpec((1, d, tm // d, QKV_W), lambda i: (i // per_seq, 0, i % per_seq, 0))
    tail_spec = pl.BlockSpec((1, tm, 2 * ATT_W),
                             lambda i: (i // per_seq, jnp.maximum(i % per_seq - first_tail, 0), 0))
    ca, sa, *offset_tables = rot
    args = [x2d, wn, w_bf, ca, sa, *offset_tables, qg, kg, seg]
    in_specs = [
        pl.BlockSpec((tm, D_MODEL), lambda i: (i, 0)),
        const(wn),
        pl.BlockSpec(w_bf.shape, lambda i: (0, 0), pipeline_mode=pl.Buffered(1)),
        base_spec, base_spec, *[const(t) for t in offset_tables],
        const(qg), const(kg), const(seg),
    ]
    scratch = [pltpu.VMEM((tm, D_MODEL), BF16), pltpu.VMEM((TILE // LANES, tm, LANES), F32)]
    if fuse:
        tables, gn = retention
        args += [*tables, gn]
        in_specs += [const(t) for t in (*tables, gn)]
        ret_specs = [pl.BlockSpec((tm, RET_V_W), lambda i: (i, 0)),
                     pl.BlockSpec((1, RET_HEADS, RET_DK, RET_DV), lambda i: (i // per_seq, 0, 0, 0))]
        ret_shapes = [jax.ShapeDtypeStruct((n, RET_V_W), out_dtype),
                      jax.ShapeDtypeStruct((batch, RET_HEADS, RET_DK, RET_DV), F32)]
        scratch.append(pltpu.VMEM((tm, RET_OUT_W), F32))
    else:
        ret_specs = [pl.BlockSpec((tm, RET_OUT_W), lambda i: (i, 0))]
        ret_shapes = [jax.ShapeDtypeStruct((n, RET_OUT_W), out_dtype)]
    return pl.pallas_call(
        functools.partial(_in_proj_kernel, dils=dils, per_seq=per_seq, fuse_retention=fuse),
        grid=(n // tm,),
        in_specs=in_specs,
        out_specs=ret_specs + [qkv_spec(d) for d in dils]
        + [pl.BlockSpec((tm, GATE_W), lambda i: (i, 0)), tail_spec],
        out_shape=ret_shapes
        + [jax.ShapeDtypeStruct((batch, d, seq_len // d, QKV_W), out_dtype) for d in dils]
        + [jax.ShapeDtypeStruct((n, GATE_W), out_dtype),
           jax.ShapeDtypeStruct((batch, tail_rows, 2 * ATT_W), F32)],
        scratch_shapes=scratch,
        compiler_params=pltpu.CompilerParams(
            dimension_semantics=("arbitrary",), vmem_limit_bytes=VMEM_LIMIT),
        name="in_proj",
    )(*args)


def _group_norm_gate(o, gain, gate):
    gate = gate.astype(F32)
    mu = jnp.mean(o, axis=-1, keepdims=True)
    d = o - mu
    var = jnp.mean(d * d, axis=-1, keepdims=True)
    return gate * jax.nn.sigmoid(gate) * (d * lax.rsqrt(var + EPS) * gain)


def _retention_head(q, k_keys, v_keys, state, dec, qd, kd, cd):
    qb = q.astype(BF16)
    vb = v_keys.astype(BF16)
    scores = _nt_dot(qb, k_keys.astype(BF16)) * dec
    o = _dot(scores.astype(BF16), vb) + _dot(qb, state.astype(BF16)) * qd
    kt = jnp.transpose(k_keys * kd).astype(BF16)
    new_state = state * cd + _dot(kt, vb)
    return o, new_state


def _retention_specs(rows, idx):
    return [
        pl.BlockSpec((1, rows, RET_QK_W), lambda *a: (*idx(*a), 0)),
        pl.BlockSpec((1, rows, RET_QK_W), lambda *a: (*idx(*a), 1)),
        pl.BlockSpec((1, rows, RET_V_W), lambda *a: (*idx(*a), 1)),
        pl.BlockSpec((1, rows, RET_V_W), lambda *a: (*idx(*a), 2)),
    ]


def _table_specs(tables, ndim_grid):
    return [pl.BlockSpec(t.shape, lambda *a, nd=t.ndim: (0,) * nd) for t in tables]


def _retention_sample_kernel(q_ref, k_ref, v_ref, g_ref, s_in_ref, dec_ref, qd_ref, kd_ref, cd_ref,
                             gn_ref, o_ref, s_out_ref, kpad_ref, vpad_ref):
    @pl.when(pl.program_id(0) == 0)
    def _():
        kpad_ref[...] = jnp.zeros_like(kpad_ref)
        vpad_ref[...] = jnp.zeros_like(vpad_ref)

    kpad_ref[0:SAMPLE_PAD, :] = k_ref[0]
    vpad_ref[0:SAMPLE_PAD, :] = v_ref[0]
    for h in range(RET_HEADS):
        ks = slice(h * RET_DK, (h + 1) * RET_DK)
        vs = slice(h * RET_DV, (h + 1) * RET_DV)
        o, new_state = _retention_head(q_ref[0, :, ks], kpad_ref[:, ks], vpad_ref[:, vs], s_in_ref[0, h],
                                       dec_ref[h], qd_ref[h], kd_ref[h], cd_ref[h])
        s_out_ref[0, h] = new_state
        o_ref[0, :, vs] = _group_norm_gate(o, gn_ref[:, vs], g_ref[0, :, vs]).astype(o_ref.dtype)


def _retention_sample(ret, state, tables, gn):
    b, p, _ = ret.shape
    state_spec = pl.BlockSpec((1, RET_HEADS, RET_DK, RET_DV), lambda bi: (bi, 0, 0, 0))
    return pl.pallas_call(
        _retention_sample_kernel,
        grid=(b,),
        in_specs=_retention_specs(p, lambda bi: (bi, 0)) + [state_spec] + _table_specs(tables + (gn,), 1),
        out_specs=[pl.BlockSpec((1, p, RET_V_W), lambda bi: (bi, 0, 0)), state_spec],
        out_shape=[
            jax.ShapeDtypeStruct((b, p, RET_V_W), ret.dtype),
            jax.ShapeDtypeStruct(state.shape, F32),
        ],
        scratch_shapes=[pltpu.VMEM((RET_CHUNK, RET_QK_W), F32), pltpu.VMEM((RET_CHUNK, RET_V_W), F32)],
        compiler_params=pltpu.CompilerParams(
            dimension_semantics=("arbitrary",), vmem_limit_bytes=VMEM_LIMIT),
        name="retention_sample",
    )(ret, ret, ret, ret, state, *tables, gn)


def _attn_prompt_blocks(q_ref, kp_ref, kc_ref, vp_ref, vc_ref, bias_ref, o_ref, lse_ref, first):
    n_blocks = q_ref.shape[2] // ATT_BLOCK
    pair_w = 2 * ATT_DH
    low_q = lax.broadcasted_iota(jnp.int32, (ATT_BLOCK, pair_w), 1) < ATT_DH
    rows0, rows1 = slice(0, ATT_BLOCK), slice(ATT_BLOCK, 2 * ATT_BLOCK)

    def keys(prev_ref, cur_ref, t, ps):
        if t > 0:
            return cur_ref[0, 0, (t - 1) * ATT_BLOCK:(t + 1) * ATT_BLOCK, ps]
        if first:
            return cur_ref[0, 0, 0:ATT_BLOCK, ps]
        return jnp.concatenate([prev_ref[0, 0, :, ps], cur_ref[0, 0, 0:ATT_BLOCK, ps]], axis=0)

    staged = []
    for t in range(n_blocks):
        for p in range(ATT_HEADS // 2):
            ps = slice(p * pair_w, (p + 1) * pair_w)
            qp = q_ref[0, 0, t * ATT_BLOCK:(t + 1) * ATT_BLOCK, ps]
            kcat = keys(kp_ref, kc_ref, t, ps)
            q2 = jnp.concatenate([jnp.where(low_q, qp, 0.0), jnp.where(low_q, 0.0, qp)], axis=0)
            s = _nt_dot(q2, kcat) + bias_ref[p, :, 2 * ATT_BLOCK - kcat.shape[0]:]
            m = jnp.max(s, axis=-1, keepdims=True)
            staged.append((t, p, m, jnp.exp(s - m).astype(BF16)))
    for t, p, m, e in staged:
        ps = slice(p * pair_w, (p + 1) * pair_w)
        vcat = keys(vp_ref, vc_ref, t, ps)
        low_k = lax.broadcasted_iota(jnp.int32, vcat.shape, 1) < ATT_DH
        p0 = _dot(e[rows0], jnp.where(low_k, vcat, 1.0))
        p1 = _dot(e[rows1], jnp.where(low_k, 1.0, vcat))
        den = pltpu.roll(jnp.where(low_q, p1, p0), ATT_DH, axis=1)
        ts = slice(t * ATT_BLOCK, (t + 1) * ATT_BLOCK)
        o_ref[0, 0, ts, ps] = jnp.where(low_q, p0, p1) / den
        lse_ref[0, 0, ts, ps] = jnp.where(low_q, m[rows0], m[rows1]) + jnp.log(den)


def _attn_prompt_kernel(q_ref, kp_ref, kc_ref, vp_ref, vc_ref, bias_ref, o_ref, lse_ref):
    first = pl.program_id(2) == 0

    @pl.when(first)
    def _():
        _attn_prompt_blocks(q_ref, kp_ref, kc_ref, vp_ref, vc_ref, bias_ref, o_ref, lse_ref, True)

    @pl.when(jnp.logical_not(first))
    def _():
        _attn_prompt_blocks(q_ref, kp_ref, kc_ref, vp_ref, vc_ref, bias_ref, o_ref, lse_ref, False)


def _attn_prompt_group(qkv, bias, n_blocks):
    b, dil, tr, _ = qkv.shape
    rows = n_blocks * ATT_BLOCK
    cur = lambda c: pl.BlockSpec((1, 1, rows, ATT_OUT_W), lambda bi, r, j: (bi, r, j, c))
    prev = lambda c: pl.BlockSpec((1, 1, ATT_BLOCK, ATT_OUT_W),
                                  lambda bi, r, j: (bi, r, jnp.maximum(j * n_blocks - 1, 0), c))
    res_shape = jax.ShapeDtypeStruct((b, dil, tr, ATT_OUT_W), F32)
    return pl.pallas_call(
        _attn_prompt_kernel,
        grid=(b, dil, tr // rows),
        in_specs=[cur(0), prev(1), cur(1), prev(2), cur(2),
                  pl.BlockSpec(bias.shape, lambda bi, r, j: (0, 0, 0))],
        out_specs=[cur(0), cur(0)],
        out_shape=[res_shape, res_shape],
        compiler_params=pltpu.CompilerParams(
            dimension_semantics=("parallel", "parallel", "arbitrary"), vmem_limit_bytes=VMEM_LIMIT),
        name=f"attn_prompt_d{dil}",
    )(qkv, qkv, qkv, qkv, qkv, bias)


def _attn_sample_kernel(a0_ref, a1_ref, a2_ref, c0_ref, c1_ref, c2_ref, b0_ref, b1_ref, b2_ref, bn_ref,
                        o_ref, kn_ref, vn_ref):
    qkv_refs = (a0_ref, a1_ref, a2_ref)
    cache_refs = (c0_ref, c1_ref, c2_ref)
    bias_refs = (b0_ref, b1_ref, b2_ref)

    @pl.when(pl.program_id(0) == 0)
    def _():
        kn_ref[...] = jnp.zeros_like(kn_ref)
        vn_ref[...] = jnp.zeros_like(vn_ref)

    for g in range(N_GROUPS):
        gs = slice(g * ATT_OUT_W, (g + 1) * ATT_OUT_W)
        kn_ref[0:SAMPLE_PAD, gs] = qkv_refs[g][0, :, ATT_OUT_W:2 * ATT_OUT_W]
        vn_ref[0:SAMPLE_PAD, gs] = qkv_refs[g][0, :, 2 * ATT_OUT_W:3 * ATT_OUT_W]

    for h in range(ATT_HEADS):
        logits, values = [], []
        for g in range(N_GROUPS):
            hs = slice(g * ATT_OUT_W + h * ATT_DH, g * ATT_OUT_W + (h + 1) * ATT_DH)
            qh = qkv_refs[g][0, :, h * ATT_DH:(h + 1) * ATT_DH].astype(BF16)
            logits.append(_dot(qh, cache_refs[g][0, 0, h].astype(BF16)) + bias_refs[g][h])
            values.append((cache_refs[g][0, 1, h].astype(BF16), True))
            logits.append(_nt_dot(qh, kn_ref[:, hs].astype(BF16)) + bn_ref[g, h])
            values.append((vn_ref[:, hs].astype(BF16), False))
        m = functools.reduce(jnp.maximum, [jnp.max(x, axis=-1, keepdims=True) for x in logits])
        es = [jnp.exp(x - m) for x in logits]
        l = functools.reduce(jnp.add, [jnp.sum(e, axis=-1, keepdims=True) for e in es])
        acc = jnp.zeros((SAMPLE_PAD, ATT_DH), F32)
        for e, (v, transposed) in zip(es, values):
            eb = e.astype(BF16)
            acc = acc + (_nt_dot(eb, v) if transposed else _dot(eb, v))
        o_ref[0, :, h * ATT_DH:(h + 1) * ATT_DH] = acc / l


def _attn_sample(qkvs, caches_t, biases, bias_new):
    b, p, _ = qkvs[0].shape
    return pl.pallas_call(
        _attn_sample_kernel,
        grid=(b,),
        in_specs=[pl.BlockSpec((1, p, QKV_W), lambda bi: (bi, 0, 0)) for _ in qkvs]
        + [pl.BlockSpec((1,) + c.shape[1:], lambda bi: (bi, 0, 0, 0, 0)) for c in caches_t]
        + [pl.BlockSpec(x.shape, lambda bi: (0, 0, 0)) for x in biases]
        + [pl.BlockSpec(bias_new.shape, lambda bi: (0, 0, 0, 0))],
        out_specs=pl.BlockSpec((1, p, ATT_OUT_W), lambda bi: (bi, 0, 0)),
        out_shape=jax.ShapeDtypeStruct((b, p, ATT_OUT_W), F32),
        scratch_shapes=[pltpu.VMEM((ATT_BLOCK, ATT_W), F32), pltpu.VMEM((ATT_BLOCK, ATT_W), F32)],
        compiler_params=pltpu.CompilerParams(
            dimension_semantics=("arbitrary",), vmem_limit_bytes=VMEM_LIMIT),
        name="attn_sample",
    )(*qkvs, *caches_t, *biases, bias_new)


def _out_proj_kernel(*refs, dils):
    x_ref, ret_ref, ga_ref, gb_ref, ag_ref, wr_ref, wa_ref, wo_ref = refs[:8]
    if dils is None:
        att_ref, o_ref = refs[8:]
        att = att_ref[0]
    else:
        group_refs = refs[8:8 + 2 * N_GROUPS]
        o_ref = refs[8 + 2 * N_GROUPS]
        scratch = refs[9 + 2 * N_GROUPS:]
        tm = x_ref.shape[1]
        os, lses = [], []
        for g, dil in enumerate(dils):
            og_ref, lg_ref = group_refs[2 * g], group_refs[2 * g + 1]
            if dil == 1:
                os.append(og_ref[0, 0]); lses.append(lg_ref[0, 0])
                continue
            so_ref, sl_ref = scratch[2 * (g - 1)], scratch[2 * (g - 1) + 1]
            n_chunks = ATT_OUT_W // LANES
            for c in range(n_chunks):
                cs = slice(c * LANES, (c + 1) * LANES)
                for r in range(dil):
                    so_ref[c, pl.ds(r, tm // dil, stride=dil), :] = og_ref[0, r, :, cs]
                    sl_ref[c, pl.ds(r, tm // dil, stride=dil), :] = lg_ref[0, r, :, cs]
            os.append(jnp.concatenate([so_ref[c] for c in range(n_chunks)], axis=1))
            lses.append(jnp.concatenate([sl_ref[c] for c in range(n_chunks)], axis=1))
        mx = functools.reduce(jnp.maximum, lses)
        ws = [jnp.exp(l - mx) for l in lses]
        att = functools.reduce(jnp.add, [w * o for w, o in zip(ws, os)]) / functools.reduce(jnp.add, ws)
    ag = ag_ref[0].astype(F32)
    u = (ag * jax.nn.sigmoid(ag) * att).astype(BF16)
    o_b = _dot(u, wa_ref[...])
    o_a = _dot(ret_ref[0].astype(BF16), wr_ref[...])
    merged = jax.nn.sigmoid(ga_ref[0].astype(F32)) * o_a + jax.nn.sigmoid(gb_ref[0].astype(F32)) * o_b
    o_ref[0] = x_ref[0] + _dot(merged.astype(BF16), wo_ref[...])


def _out_proj(x, ret, gates, wr, wa, wo, tm, att=None, groups=None):
    b, t, _ = x.shape
    row = lambda w, c: pl.BlockSpec((1, tm, w), lambda bi, i: (bi, i, c))
    full = lambda a: pl.BlockSpec(a.shape, lambda bi, i: (0, 0))
    in_specs = [row(D_MODEL, 0), row(RET_V_W, 0), row(D_MODEL, 0), row(D_MODEL, 1),
                row(ATT_OUT_W, 2 * D_MODEL // ATT_OUT_W), full(wr), full(wa), full(wo)]
    args = [x, ret, gates, gates, gates, wr, wa, wo]
    scratch = []
    if groups is None:
        dils = None
        in_specs.append(row(ATT_OUT_W, 0))
        args.append(att)
    else:
        dils = tuple(o.shape[1] for o, _ in groups)
        for (o, lse), d in zip(groups, dils):
            spec = pl.BlockSpec((1, d, tm // d, ATT_OUT_W), lambda bi, i: (bi, 0, i, 0))
            in_specs += [spec, spec]
            args += [o, lse]
            if d > 1:
                scratch += [pltpu.VMEM((ATT_OUT_W // LANES, tm, LANES), F32)] * 2
    return pl.pallas_call(
        functools.partial(_out_proj_kernel, dils=dils),
        grid=(b, t // tm),
        in_specs=in_specs,
        out_specs=row(D_MODEL, 0),
        out_shape=jax.ShapeDtypeStruct((b, t, D_MODEL), F32),
        scratch_shapes=scratch,
        compiler_params=pltpu.CompilerParams(
            dimension_semantics=("parallel", "parallel"), vmem_limit_bytes=VMEM_LIMIT),
        name="out_proj",
    )(*args)


def _rotary_tables(bases, offsets):
    half = RET_DK // 2
    inv = ROPE_BASE ** (-jnp.arange(half, dtype=F32) / half)
    inv2 = jnp.concatenate([inv, inv])
    sign = jnp.concatenate([-jnp.ones((half,), F32), jnp.ones((half,), F32)])
    a = bases.astype(F32)[:, None, None] * inv2
    b = offsets.astype(F32)[:, None] * inv2
    cb, sb = jnp.cos(b), jnp.sin(b)
    return jnp.cos(a), jnp.sin(a), cb, sb, sign * cb, sign * sb


def _retention_tables(c, rows):
    log_g = jnp.log1p(-(2.0 ** (-5.0 - jnp.arange(RET_HEADS, dtype=F32))))
    i = jnp.arange(c, dtype=F32)
    diff = i[:, None] - i[None, :]
    decay = jnp.where(diff[None] >= 0, jnp.exp(jnp.maximum(diff, 0.0)[None] * log_g[:, None, None]), 0.0)
    q_decay = jnp.exp((i + 1.0)[None, :] * log_g[:, None])
    k_decay = jnp.exp((c - 1.0 - i)[None, :] * log_g[:, None])
    chunk_decay = jnp.exp(c * log_g)
    dec = jnp.zeros((RET_HEADS, rows, RET_CHUNK), F32).at[:, :c, :c].set(decay)
    qd = jnp.zeros((RET_HEADS, rows, 1), F32).at[:, :c, 0].set(q_decay)
    kd = jnp.zeros((RET_HEADS, RET_CHUNK, 1), F32).at[:, :c, 0].set(k_decay)
    cd = jnp.broadcast_to(chunk_decay[:, None, None], (RET_HEADS, 1, RET_DV))
    return dec, qd, kd, cd


def _t5_bucket(dist):
    max_exact = REL_BUCKETS // 2
    d = jnp.maximum(dist.astype(F32), 1.0)
    large = max_exact + (jnp.log(d / max_exact) / math.log(REL_MAX_DIST / max_exact)
                         * (REL_BUCKETS - max_exact)).astype(jnp.int32)
    large = jnp.minimum(large, REL_BUCKETS - 1)
    return jnp.where(dist < max_exact, dist, large)


def _group_bias(rel_bias, g, dil, slots):
    dist = dil * jnp.asarray(slots, dtype=jnp.int32)
    return rel_bias[_t5_bucket(dist)][:, g * ATT_HEADS:(g + 1) * ATT_HEADS].astype(F32).T


def _toeplitz(f, n_rows, n_cols):
    heads, period = f.shape
    assert period == n_rows + n_cols - 1
    g = jnp.roll(f, -(n_rows - 1), axis=1)
    flat = jnp.tile(g, (1, n_rows))[:, :n_rows * (period - 1)]
    return flat.reshape(heads, n_rows, period - 1)[:, :, :n_cols]


def _neg(heads, n):
    return jnp.full((heads, n), NEG, F32)


def _dilate(v, dil):
    heads, n = v.shape
    return jnp.stack([v] + [_neg(heads, n)] * (dil - 1), axis=-1).reshape(heads, n * dil)


def _prompt_bias(tb_rev):
    heads = tb_rev.shape[0]
    f = jnp.concatenate([_neg(heads, ATT_BLOCK - 1), tb_rev, _neg(heads, ATT_BLOCK - 1)], axis=1)
    return _toeplitz(f, ATT_BLOCK, 2 * ATT_BLOCK).reshape(heads // 2, 2 * ATT_BLOCK, 2 * ATT_BLOCK)


def _sample_bias(tb, tb_rev, win, dil):
    heads = tb.shape[0]
    p = SAMPLE_PAD
    f_c = jnp.concatenate([_neg(heads, p - 1), _dilate(tb_rev[:, :N_KEYS - 1], dil)], axis=1)
    bias_c = _toeplitz(f_c, p, win)
    cols = [tb[:, k // dil:k // dil + 1] if k % dil == 0 else _neg(heads, 1) for k in range(p - 1, -1, -1)]
    f_n = jnp.concatenate(cols + [_neg(heads, p - 1)], axis=1)
    bias_n = _toeplitz(f_n, p, p)
    return bias_c, jnp.pad(bias_n, ((0, 0), (0, 0), (0, ATT_BLOCK - p)), constant_values=NEG)


def _kv_rows(tail, g, n_rows):
    b, rows, _ = tail.shape
    kv = tail[:, rows - n_rows:, 2 * g * ATT_OUT_W:(2 * g + 2) * ATT_OUT_W]
    return kv.reshape(b, n_rows, 2, ATT_HEADS, ATT_DH)


def kernel(x_prompt, x_sample, cache_kv_w128, cache_kv_w512, cache_kv_w2048, state_retention,
           w_norm, w_in, q_norm, k_norm, rel_bias, ret_norm, w_proj_ret, w_proj_att, w_out):
    assert w_in.shape[0] == 1
    bp, t, _ = x_prompt.shape
    bs, ts, _ = x_sample.shape
    dils = tuple(d for _, d in ATT_GROUPS)
    assert t % (ATT_BLOCK * max(dils)) == 0 and ts <= SAMPLE_PAD
    caches = (cache_kv_w128[0], cache_kv_w512[0], cache_kv_w2048[0])
    for cch, (win, _) in zip(caches, ATT_GROUPS):
        assert cch.shape[1] == win and win <= PAST_LEN

    wn = w_norm[0].reshape(1, D_MODEL)
    w_in_bf = w_in[0].astype(BF16)
    qg = jnp.tile(q_norm[0] * (ATT_DH ** -0.5), TILE // ATT_DH).reshape(1, TILE)
    kg = jnp.tile(k_norm[0], TILE // ATT_DH).reshape(1, TILE)
    gn = ret_norm[0].reshape(1, RET_V_W)
    wr = w_proj_ret[0].astype(BF16)
    wa = w_proj_att[0].astype(BF16)
    wo = w_out[0].astype(BF16)
    hid = jnp.arange(TILE // 2) // ATT_DH
    seg_mean = jnp.where(hid[:, None] == hid[None, :], 1.0 / ATT_DH, 0.0).astype(BF16)
    asc, desc = tuple(range(N_KEYS)), tuple(range(N_KEYS - 1, -1, -1))
    group_bias = [(_group_bias(rel_bias, g, d, asc), _group_bias(rel_bias, g, d, desc)) for g, d in enumerate(dils)]

    tm_p = 256
    rot_p = _rotary_tables(jnp.arange(0, t, tm_p), jnp.arange(tm_p))
    tail_p_rows = min(max(w for w, _ in ATT_GROUPS), t)
    ret_p, state_p, *qkv_p, gates_p, tail_p = _in_proj(
        x_prompt.reshape(bp * t, D_MODEL), t, dils, tail_p_rows, BF16, wn, w_in_bf, rot_p, qg, kg, seg_mean,
        tm=tm_p, retention=(_retention_tables(RET_CHUNK, RET_CHUNK), gn))
    groups = [_attn_prompt_group(qkv_p[g], _prompt_bias(group_bias[g][1]), n_blocks=2) for g in range(N_GROUPS)]
    y_p = _out_proj(x_prompt, ret_p.reshape(bp, t, RET_V_W), gates_p.reshape(bp, t, GATE_W), wr, wa, wo,
                    tm=256, groups=groups)

    pad = SAMPLE_PAD
    ns = bs * pad
    xs = jnp.pad(x_sample, ((0, 0), (0, pad - ts), (0, 0))).reshape(ns, D_MODEL)
    rot_s = _rotary_tables(jnp.full((1,), PAST_LEN), jnp.tile(jnp.arange(pad), bs))
    ret_in_s, *qkv_s, gates_s, tail_s = _in_proj(xs, ns, (1,) * N_GROUPS, ns, F32, wn, w_in_bf, rot_s,
                                                 qg, kg, seg_mean, tm=ns)
    ret_s, state_s = _retention_sample(ret_in_s.reshape(bs, pad, RET_OUT_W), state_retention[0],
                                       _retention_tables(ts, pad), gn)
    qkv_s = [a.reshape(bs, pad, QKV_W) for a in qkv_s]
    caches_t = [jnp.transpose(c, (0, 2, 3, 4, 1)) for c in caches]
    sb = [_sample_bias(*group_bias[g], win, d) for g, (win, d) in enumerate(ATT_GROUPS)]
    att_s = _attn_sample(qkv_s, caches_t, [c for c, _ in sb], jnp.stack([n for _, n in sb]))
    y_s = _out_proj(xs.reshape(1, ns, D_MODEL), ret_s.reshape(1, ns, RET_V_W), gates_s.reshape(1, ns, GATE_W),
                    wr, wa, wo, tm=ns, att=att_s.reshape(1, ns, ATT_OUT_W))
    y_s = y_s.reshape(bs, pad, D_MODEL)[:, :ts]

    kv_p = [_kv_rows(tail_p, g, min(win, t))[None] for g, (win, _) in enumerate(ATT_GROUPS)]
    tail_s = tail_s.reshape(bs, pad, 2 * ATT_W)[:, :ts]
    kv_s = [_kv_rows(tail_s, g, ts)[None] for g in range(N_GROUPS)]
    return (y_p, y_s, state_p[None], state_s[None], kv_p[0], kv_p[1], kv_p[2], kv_s[0], kv_s[1], kv_s[2])
```

```python
import functools
import math

import jax
import jax.numpy as jnp
from jax import lax
from jax.experimental import pallas as pl
from jax.experimental.pallas import tpu as pltpu

D_MODEL = 1024
PAST_LEN = 16384
RET_HEADS = 4
RET_DK = 128
RET_DV = 256
RET_CHUNK = 128
ROPE_BASE = 10000.0
ATT_GROUPS = ((128, 1), (512, 4), (2048, 16))
N_GROUPS = 3
ATT_HEADS = 8
ATT_DH = 64
REL_BUCKETS = 32
REL_MAX_DIST = 2048
EPS = 1e-6

RET_QK_W = RET_HEADS * RET_DK
RET_V_W = RET_HEADS * RET_DV
ATT_W = N_GROUPS * ATT_HEADS * ATT_DH
ATT_OUT_W = ATT_HEADS * ATT_DH
IN_W = 2 * RET_QK_W + 2 * RET_V_W + 3 * ATT_W + ATT_OUT_W + 2 * D_MODEL

LANES = 128
TILE = 512
RET_OUT_W = 2 * RET_QK_W + 2 * RET_V_W
QKV_W = 3 * ATT_OUT_W
GATE_W = 2 * D_MODEL + ATT_OUT_W
ATT_BLOCK = 128
N_KEYS = ATT_BLOCK + 1
SAMPLE_PAD = 8
NEG = -1e30
VMEM_LIMIT = 48 * 1024 * 1024

F32 = jnp.float32
BF16 = jnp.bfloat16


def _nt_dot(a, b):
    return lax.dot_general(a, b, (((1,), (1,)), ((), ())), preferred_element_type=F32)


def _dot(a, b):
    return jnp.dot(a, b, preferred_element_type=F32)


def _proj_schedule():
    sched = [(0, 0, 0, "rot_q"), (RET_QK_W, 0, RET_QK_W, "rot_k")]
    for k in range(2 * RET_V_W // TILE):
        sched.append((2 * RET_QK_W + k * TILE, 0, 2 * RET_QK_W + k * TILE, "plain"))
    att0 = 2 * RET_QK_W + 2 * RET_V_W
    for g in range(N_GROUPS):
        for kind, epi in enumerate(("norm_q", "norm_k", "plain")):
            sched.append((att0 + kind * ATT_W + g * ATT_OUT_W, 1 + g, kind * ATT_OUT_W, epi))
    gate0 = att0 + 3 * ATT_W
    for k in range(2 * D_MODEL // TILE):
        sched.append((gate0 + ATT_OUT_W + k * TILE, 4, k * TILE, "plain"))
    sched.append((gate0, 4, 2 * D_MODEL, "plain"))
    return sched


_SCHEDULE = _proj_schedule()


def _in_proj_kernel(*refs, dils, per_seq, fuse_retention):
    (x_ref, wn_ref, w_ref, ca_ref, sa_ref, cb_ref, sb_ref, cbs_ref, sbs_ref,
     qg_ref, kg_ref, seg_ref) = refs[:12]
    refs = refs[12:]
    if fuse_retention:
        dec_ref, qd_ref, kd_ref, cd_ref, gn_ref, ret_ref, s_ref = refs[:7]
        refs = refs[7:]
    else:
        ret_ref = refs[0]
        refs = refs[1:]
    a0_ref, a1_ref, a2_ref, gate_ref, tail_ref, h_ref, y_ref = refs[:7]
    r_ref = refs[7] if fuse_retention else ret_ref
    out_refs = (r_ref, a0_ref, a1_ref, a2_ref, gate_ref)

    if fuse_retention:
        @pl.when(pl.program_id(0) % per_seq == 0)
        def _():
            s_ref[...] = jnp.zeros_like(s_ref)

    x = x_ref[...]
    tm = x.shape[0]
    ms = jnp.mean(x * x, axis=-1, keepdims=True)
    h_ref[...] = (x * lax.rsqrt(ms + EPS) * wn_ref[...]).astype(BF16)

    ca, sa = ca_ref[0], sa_ref[0]
    cos = ca * cb_ref[...] - sa * sb_ref[...]
    sin = sa * cbs_ref[...] + ca * sbs_ref[...]

    def rotary(y, scale):
        parts = []
        for hh in range(TILE // RET_DK):
            yh = y[:, hh * RET_DK:(hh + 1) * RET_DK]
            parts.append((yh * cos + pltpu.roll(yh, RET_DK // 2, axis=1) * sin) * scale)
        return jnp.concatenate(parts, axis=1)

    def head_rms(y, gain):
        y2 = (y * y).astype(BF16)
        half = TILE // 2
        ms = jnp.concatenate([_dot(y2[:, :half], seg_ref[...]), _dot(y2[:, half:], seg_ref[...])], axis=1)
        return y * lax.rsqrt(ms + EPS) * gain

    epilogues = {
        "plain": lambda y: y,
        "rot_q": lambda y: rotary(y, 1.0),
        "rot_k": lambda y: rotary(y, RET_DK ** -0.5),
        "norm_q": lambda y: head_rms(y, qg_ref[...]),
        "norm_k": lambda y: head_rms(y, kg_ref[...]),
    }

    def retention():
        for c in range(tm // RET_CHUNK):
            rows = slice(c * RET_CHUNK, (c + 1) * RET_CHUNK)
            for h in range(RET_HEADS):
                ks = slice(h * RET_DK, (h + 1) * RET_DK)
                ks2 = slice(RET_QK_W + h * RET_DK, RET_QK_W + (h + 1) * RET_DK)
                vs = slice(h * RET_DV, (h + 1) * RET_DV)
                vs_in = slice(2 * RET_QK_W + h * RET_DV, 2 * RET_QK_W + (h + 1) * RET_DV)
                gs_in = slice(2 * RET_QK_W + RET_V_W + h * RET_DV, 2 * RET_QK_W + RET_V_W + (h + 1) * RET_DV)
                o, new_state = _retention_head(r_ref[rows, ks], r_ref[rows, ks2], r_ref[rows, vs_in], s_ref[0, h],
                                               dec_ref[h], qd_ref[h], kd_ref[h], cd_ref[h])
                s_ref[0, h] = new_state
                ret_ref[rows, vs] = _group_norm_gate(o, gn_ref[:, vs], r_ref[rows, gs_in]).astype(ret_ref.dtype)

    n_ret_tiles = sum(1 for t in _SCHEDULE if t[1] == 0)
    for step, (w_col, out_idx, out_col, epi) in enumerate(_SCHEDULE):
        if fuse_retention and step == n_ret_tiles:
            retention()
        val = epilogues[epi](_dot(h_ref[...], w_ref[:, w_col:w_col + TILE]))
        o_ref = out_refs[out_idx]
        ocs = slice(out_col, out_col + TILE)
        if not 1 <= out_idx <= N_GROUPS:
            o_ref[:, ocs] = val.astype(o_ref.dtype)
            continue
        g, kind = out_idx - 1, out_col // ATT_OUT_W
        if kind > 0:
            tcol = (2 * g + kind - 1) * ATT_OUT_W
            tail_ref[0, :, tcol:tcol + ATT_OUT_W] = val
        dil = dils[g]
        if dil == 1:
            o_ref[0, 0, :, ocs] = val.astype(o_ref.dtype)
            continue
        for c in range(TILE // LANES):
            y_ref[c] = val[:, c * LANES:(c + 1) * LANES]
            for r in range(dil):
                o_ref[0, r, :, out_col + c * LANES:out_col + (c + 1) * LANES] = (
                    y_ref[c, pl.ds(r, tm // dil, stride=dil), :].astype(o_ref.dtype))


def _in_proj(x2d, seq_len, dils, tail_rows, out_dtype, wn, w_bf, rot, qg, kg, seg, tm, retention=None):
    n = x2d.shape[0]
    batch = n // seq_len
    per_seq = seq_len // tm
    first_tail = (seq_len - tail_rows) // tm
    fuse = retention is not None
    const = lambda a: pl.BlockSpec(a.shape, lambda i, nd=a.ndim: (0,) * nd)
    base_spec = pl.BlockSpec((1, 1, RET_DK), lambda i: (i % per_seq, 0, 0))
    qkv_spec = lambda d: pl.BlockSpec((1, d, tm // d, QKV_W), lambda i: (i // per_seq, 0, i % per_seq, 0))
    tail_spec = pl.BlockSpec((1, tm, 2 * ATT_W),
                             lambda i: (i // per_seq, jnp.maximum(i % per_seq - first_tail, 0), 0))
    ca, sa, *offset_tables = rot
    args = [x2d, wn, w_bf, ca, sa, *offset_tables, qg, kg, seg]
    in_specs = [
        pl.BlockSpec((tm, D_MODEL), lambda i: (i, 0)),
        const(wn),
        pl.BlockSpec(w_bf.shape, lambda i: (0, 0), pipeline_mode=pl.Buffered(1)),
        base_spec, base_spec, *[const(t) for t in offset_tables],
        const(qg), const(kg), const(seg),
    ]
    scratch = [pltpu.VMEM((tm, D_MODEL), BF16), pltpu.VMEM((TILE // LANES, tm, LANES), F32)]
    if fuse:
        tables, gn = retention
        args += [*tables, gn]
        in_specs += [const(t) for t in (*tables, gn)]
        ret_specs = [pl.BlockSpec((tm, RET_V_W), lambda i: (i, 0)),
                     pl.BlockSpec((1, RET_HEADS, RET_DK, RET_DV), lambda i: (i // per_seq, 0, 0, 0))]
        ret_shapes = [jax.ShapeDtypeStruct((n, RET_V_W), out_dtype),
                      jax.ShapeDtypeStruct((batch, RET_HEADS, RET_DK, RET_DV), F32)]
        scratch.append(pltpu.VMEM((tm, RET_OUT_W), F32))
    else:
        ret_specs = [pl.BlockSpec((tm, RET_OUT_W), lambda i: (i, 0))]
        ret_shapes = [jax.ShapeDtypeStruct((n, RET_OUT_W), out_dtype)]
    return pl.pallas_call(
        functools.partial(_in_proj_kernel, dils=dils, per_seq=per_seq, fuse_retention=fuse),
        grid=(n // tm,),
        in_specs=in_specs,
        out_specs=ret_specs + [qkv_spec(d) for d in dils]
        + [pl.BlockSpec((tm, GATE_W), lambda i: (i, 0)), tail_spec],
        out_shape=ret_shapes
        + [jax.ShapeDtypeStruct((batch, d, seq_len // d, QKV_W), out_dtype) for d in dils]
        + [jax.ShapeDtypeStruct((n, GATE_W), out_dtype),
           jax.ShapeDtypeStruct((batch, tail_rows, 2 * ATT_W), F32)],
        scratch_shapes=scratch,
        compiler_params=pltpu.CompilerParams(
            dimension_semantics=("arbitrary",), vmem_limit_bytes=VMEM_LIMIT),
        name="in_proj",
    )(*args)


def _group_norm_gate(o, gain, gate):
    gate = gate.astype(F32)
    mu = jnp.mean(o, axis=-1, keepdims=True)
    d = o - mu
    var = jnp.mean(d * d, axis=-1, keepdims=True)
    return gate * jax.nn.sigmoid(gate) * (d * lax.rsqrt(var + EPS) * gain)


def _retention_head(q, k_keys, v_keys, state, dec, qd, kd, cd):
    qb = q.astype(BF16)
    vb = v_keys.astype(BF16)
    scores = _nt_dot(qb, k_keys.astype(BF16)) * dec
    o = _dot(scores.astype(BF16), vb) + _dot(qb, state.astype(BF16)) * qd
    kt = jnp.transpose(k_keys * kd).astype(BF16)
    new_state = state * cd + _dot(kt, vb)
    return o, new_state


def _retention_specs(rows, idx):
    return [
        pl.BlockSpec((1, rows, RET_QK_W), lambda *a: (*idx(*a), 0)),
        pl.BlockSpec((1, rows, RET_QK_W), lambda *a: (*idx(*a), 1)),
        pl.BlockSpec((1, rows, RET_V_W), lambda *a: (*idx(*a), 1)),
        pl.BlockSpec((1, rows, RET_V_W), lambda *a: (*idx(*a), 2)),
    ]


def _table_specs(tables, ndim_grid):
    return [pl.BlockSpec(t.shape, lambda *a, nd=t.ndim: (0,) * nd) for t in tables]


def _retention_sample_kernel(q_ref, k_ref, v_ref, g_ref, s_in_ref, dec_ref, qd_ref, kd_ref, cd_ref,
                             gn_ref, o_ref, s_out_ref, kpad_ref, vpad_ref):
    @pl.when(pl.program_id(0) == 0)
    def _():
        kpad_ref[...] = jnp.zeros_like(kpad_ref)
        vpad_ref[...] = jnp.zeros_like(vpad_ref)

    kpad_ref[0:SAMPLE_PAD, :] = k_ref[0]
    vpad_ref[0:SAMPLE_PAD, :] = v_ref[0]
    for h in range(RET_HEADS):
        ks = slice(h * RET_DK, (h + 1) * RET_DK)
        vs = slice(h * RET_DV, (h + 1) * RET_DV)
        o, new_state = _retention_head(q_ref[0, :, ks], kpad_ref[:, ks], vpad_ref[:, vs], s_in_ref[0, h],
                                       dec_ref[h], qd_ref[h], kd_ref[h], cd_ref[h])
        s_out_ref[0, h] = new_state
        o_ref[0, :, vs] = _group_norm_gate(o, gn_ref[:, vs], g_ref[0, :, vs]).astype(o_ref.dtype)


def _retention_sample(ret, state, tables, gn):
    b, p, _ = ret.shape
    state_spec = pl.BlockSpec((1, RET_HEADS, RET_DK, RET_DV), lambda bi: (bi, 0, 0, 0))
    return pl.pallas_call(
        _retention_sample_kernel,
        grid=(b,),
        in_specs=_retention_specs(p, lambda bi: (bi, 0)) + [state_spec] + _table_specs(tables + (gn,), 1),
        out_specs=[pl.BlockSpec((1, p, RET_V_W), lambda bi: (bi, 0, 0)), state_spec],
        out_shape=[
            jax.ShapeDtypeStruct((b, p, RET_V_W), ret.dtype),
            jax.ShapeDtypeStruct(state.shape, F32),
        ],
        scratch_shapes=[pltpu.VMEM((RET_CHUNK, RET_QK_W), F32), pltpu.VMEM((RET_CHUNK, RET_V_W), F32)],
        compiler_params=pltpu.CompilerParams(
            dimension_semantics=("arbitrary",), vmem_limit_bytes=VMEM_LIMIT),
        name="retention_sample",
    )(ret, ret, ret, ret, state, *tables, gn)


def _attn_prompt_blocks(q_ref, kp_ref, kc_ref, vp_ref, vc_ref, bias_ref, o_ref, lse_ref, first):
    n_blocks = q_ref.shape[2] // ATT_BLOCK
    pair_w = 2 * ATT_DH
    low_q = lax.broadcasted_iota(jnp.int32, (ATT_BLOCK, pair_w), 1) < ATT_DH
    rows0, rows1 = slice(0, ATT_BLOCK), slice(ATT_BLOCK, 2 * ATT_BLOCK)

    def keys(prev_ref, cur_ref, t, ps):
        if t > 0:
            return cur_ref[0, 0, (t - 1) * ATT_BLOCK:(t + 1) * ATT_BLOCK, ps]
        if first:
            return cur_ref[0, 0, 0:ATT_BLOCK, ps]
        return jnp.concatenate([prev_ref[0, 0, :, ps], cur_ref[0, 0, 0:ATT_BLOCK, ps]], axis=0)

    staged = []
    for t in range(n_blocks):
        for p in range(ATT_HEADS // 2):
            ps = slice(p * pair_w, (p + 1) * pair_w)
            qp = q_ref[0, 0, t * ATT_BLOCK:(t + 1) * ATT_BLOCK, ps]
            kcat = keys(kp_ref, kc_ref, t, ps)
            q2 = jnp.concatenate([jnp.where(low_q, qp, 0.0), jnp.where(low_q, 0.0, qp)], axis=0)
            s = _nt_dot(q2, kcat) + bias_ref[p, :, 2 * ATT_BLOCK - kcat.shape[0]:]
            m = jnp.max(s, axis=-1, keepdims=True)
            staged.append((t, p, m, jnp.exp(s - m).astype(BF16)))
    for t, p, m, e in staged:
        ps = slice(p * pair_w, (p + 1) * pair_w)
        vcat = keys(vp_ref, vc_ref, t, ps)
        low_k = lax.broadcasted_iota(jnp.int32, vcat.shape, 1) < ATT_DH
        p0 = _dot(e[rows0], jnp.where(low_k, vcat, 1.0))
        p1 = _dot(e[rows1], jnp.where(low_k, 1.0, vcat))
        den = pltpu.roll(jnp.where(low_q, p1, p0), ATT_DH, axis=1)
        ts = slice(t * ATT_BLOCK, (t + 1) * ATT_BLOCK)
        o_ref[0, 0, ts, ps] = jnp.where(low_q, p0, p1) / den
        lse_ref[0, 0, ts, ps] = jnp.where(low_q, m[rows0], m[rows1]) + jnp.log(den)


def _attn_prompt_kernel(q_ref, kp_ref, kc_ref, vp_ref, vc_ref, bias_ref, o_ref, lse_ref):
    first = pl.program_id(2) == 0

    @pl.when(first)
    def _():
        _attn_prompt_blocks(q_ref, kp_ref, kc_ref, vp_ref, vc_ref, bias_ref, o_ref, lse_ref, True)

    @pl.when(jnp.logical_not(first))
    def _():
        _attn_prompt_blocks(q_ref, kp_ref, kc_ref, vp_ref, vc_ref, bias_ref, o_ref, lse_ref, False)


def _attn_prompt_group(qkv, bias, n_blocks):
    b, dil, tr, _ = qkv.shape
    rows = n_blocks * ATT_BLOCK
    cur = lambda c: pl.BlockSpec((1, 1, rows, ATT_OUT_W), lambda bi, r, j: (bi, r, j, c))
    prev = lambda c: pl.BlockSpec((1, 1, ATT_BLOCK, ATT_OUT_W),
                                  lambda bi, r, j: (bi, r, jnp.maximum(j * n_blocks - 1, 0), c))
    res_shape = jax.ShapeDtypeStruct((b, dil, tr, ATT_OUT_W), F32)
    return pl.pallas_call(
        _attn_prompt_kernel,
        grid=(b, dil, tr // rows),
        in_specs=[cur(0), prev(1), cur(1), prev(2), cur(2),
                  pl.BlockSpec(bias.shape, lambda bi, r, j: (0, 0, 0))],
        out_specs=[cur(0), cur(0)],
        out_shape=[res_shape, res_shape],
        compiler_params=pltpu.CompilerParams(
            dimension_semantics=("parallel", "parallel", "arbitrary"), vmem_limit_bytes=VMEM_LIMIT),
        name=f"attn_prompt_d{dil}",
    )(qkv, qkv, qkv, qkv, qkv, bias)


def _attn_sample_kernel(a0_ref, a1_ref, a2_ref, c0_ref, c1_ref, c2_ref, b0_ref, b1_ref, b2_ref, bn_ref,
                        o_ref, kn_ref, vn_ref):
    qkv_refs = (a0_ref, a1_ref, a2_ref)
    cache_refs = (c0_ref, c1_ref, c2_ref)
    bias_refs = (b0_ref, b1_ref, b2_ref)

    @pl.when(pl.program_id(0) == 0)
    def _():
        kn_ref[...] = jnp.zeros_like(kn_ref)
        vn_ref[...] = jnp.zeros_like(vn_ref)

    for g in range(N_GROUPS):
        gs = slice(g * ATT_OUT_W, (g + 1) * ATT_OUT_W)
        kn_ref[0:SAMPLE_PAD, gs] = qkv_refs[g][0, :, ATT_OUT_W:2 * ATT_OUT_W]
        vn_ref[0:SAMPLE_PAD, gs] = qkv_refs[g][0, :, 2 * ATT_OUT_W:3 * ATT_OUT_W]

    staged = []
    for h in range(ATT_HEADS):
        logits = []
        for g in range(N_GROUPS):
            hs = slice(g * ATT_OUT_W + h * ATT_DH, g * ATT_OUT_W + (h + 1) * ATT_DH)
            qh = qkv_refs[g][0, :, h * ATT_DH:(h + 1) * ATT_DH].astype(BF16)
            logits.append(_dot(qh, cache_refs[g][0, 0, h].astype(BF16)) + bias_refs[g][h])
            logits.append(_nt_dot(qh, kn_ref[:, hs].astype(BF16)) + bn_ref[g, h])
        m = functools.reduce(jnp.maximum, [jnp.max(x, axis=-1, keepdims=True) for x in logits])
        es = [jnp.exp(x - m) for x in logits]
        l = functools.reduce(jnp.add, [jnp.sum(e, axis=-1, keepdims=True) for e in es])
        staged.append(([e.astype(BF16) for e in es], l))
    for h, (es, l) in enumerate(staged):
        acc = jnp.zeros((SAMPLE_PAD, ATT_DH), F32)
        for g in range(N_GROUPS):
            hs = slice(g * ATT_OUT_W + h * ATT_DH, g * ATT_OUT_W + (h + 1) * ATT_DH)
            acc = acc + _nt_dot(es[2 * g], cache_refs[g][0, 1, h].astype(BF16))
            acc = acc + _dot(es[2 * g + 1], vn_ref[:, hs].astype(BF16))
        o_ref[0, :, h * ATT_DH:(h + 1) * ATT_DH] = acc / l


def _attn_sample(qkvs, caches_t, biases, bias_new):
    b, p, _ = qkvs[0].shape
    return pl.pallas_call(
        _attn_sample_kernel,
        grid=(b,),
        in_specs=[pl.BlockSpec((1, p, QKV_W), lambda bi: (bi, 0, 0)) for _ in qkvs]
        + [pl.BlockSpec((1,) + c.shape[1:], lambda bi: (bi, 0, 0, 0, 0)) for c in caches_t]
        + [pl.BlockSpec(x.shape, lambda bi: (0, 0, 0)) for x in biases]
        + [pl.BlockSpec(bias_new.shape, lambda bi: (0, 0, 0, 0))],
        out_specs=pl.BlockSpec((1, p, ATT_OUT_W), lambda bi: (bi, 0, 0)),
        out_shape=jax.ShapeDtypeStruct((b, p, ATT_OUT_W), F32),
        scratch_shapes=[pltpu.VMEM((ATT_BLOCK, ATT_W), F32), pltpu.VMEM((ATT_BLOCK, ATT_W), F32)],
        compiler_params=pltpu.CompilerParams(
            dimension_semantics=("arbitrary",), vmem_limit_bytes=VMEM_LIMIT),
        name="attn_sample",
    )(*qkvs, *caches_t, *biases, bias_new)


def _out_proj_kernel(*refs, dils):
    x_ref, ret_ref, ga_ref, gb_ref, ag_ref, wr_ref, wa_ref, wo_ref = refs[:8]
    if dils is None:
        att_ref, o_ref = refs[8:]
        att = att_ref[0]
    else:
        group_refs = refs[8:8 + 2 * N_GROUPS]
        o_ref = refs[8 + 2 * N_GROUPS]
        scratch = refs[9 + 2 * N_GROUPS:]
        tm = x_ref.shape[1]
        os, lses = [], []
        for g, dil in enumerate(dils):
            og_ref, lg_ref = group_refs[2 * g], group_refs[2 * g + 1]
            if dil == 1:
                os.append(og_ref[0, 0]); lses.append(lg_ref[0, 0])
                continue
            so_ref, sl_ref = scratch[2 * (g - 1)], scratch[2 * (g - 1) + 1]
            n_chunks = ATT_OUT_W // LANES
            for c in range(n_chunks):
                cs = slice(c * LANES, (c + 1) * LANES)
                for r in range(dil):
                    so_ref[c, pl.ds(r, tm // dil, stride=dil), :] = og_ref[0, r, :, cs]
                    sl_ref[c, pl.ds(r, tm // dil, stride=dil), :] = lg_ref[0, r, :, cs]
            os.append(jnp.concatenate([so_ref[c] for c in range(n_chunks)], axis=1))
            lses.append(jnp.concatenate([sl_ref[c] for c in range(n_chunks)], axis=1))
        mx = functools.reduce(jnp.maximum, lses)
        ws = [jnp.exp(l - mx) for l in lses]
        att = functools.reduce(jnp.add, [w * o for w, o in zip(ws, os)]) / functools.reduce(jnp.add, ws)
    ag = ag_ref[0].astype(F32)
    u = (ag * jax.nn.sigmoid(ag) * att).astype(BF16)
    o_b = _dot(u, wa_ref[...])
    o_a = _dot(ret_ref[0].astype(BF16), wr_ref[...])
    merged = jax.nn.sigmoid(ga_ref[0].astype(F32)) * o_a + jax.nn.sigmoid(gb_ref[0].astype(F32)) * o_b
    o_ref[0] = x_ref[0] + _dot(merged.astype(BF16), wo_ref[...])


def _out_proj(x, ret, gates, wr, wa, wo, tm, att=None, groups=None):
    b, t, _ = x.shape
    row = lambda w, c: pl.BlockSpec((1, tm, w), lambda bi, i: (bi, i, c))
    full = lambda a: pl.BlockSpec(a.shape, lambda bi, i: (0, 0))
    in_specs = [row(D_MODEL, 0), row(RET_V_W, 0), row(D_MODEL, 0), row(D_MODEL, 1),
                row(ATT_OUT_W, 2 * D_MODEL // ATT_OUT_W), full(wr), full(wa), full(wo)]
    args = [x, ret, gates, gates, gates, wr, wa, wo]
    scratch = []
    if groups is None:
        dils = None
        in_specs.append(row(ATT_OUT_W, 0))
        args.append(att)
    else:
        dils = tuple(o.shape[1] for o, _ in groups)
        for (o, lse), d in zip(groups, dils):
            spec = pl.BlockSpec((1, d, tm // d, ATT_OUT_W), lambda bi, i: (bi, 0, i, 0))
            in_specs += [spec, spec]
            args += [o, lse]
            if d > 1:
                scratch += [pltpu.VMEM((ATT_OUT_W // LANES, tm, LANES), F32)] * 2
    return pl.pallas_call(
        functools.partial(_out_proj_kernel, dils=dils),
        grid=(b, t // tm),
        in_specs=in_specs,
        out_specs=row(D_MODEL, 0),
        out_shape=jax.ShapeDtypeStruct((b, t, D_MODEL), F32),
        scratch_shapes=scratch,
        compiler_params=pltpu.CompilerParams(
            dimension_semantics=("parallel", "parallel"), vmem_limit_bytes=VMEM_LIMIT),
        name="out_proj",
    )(*args)


def _rotary_tables(bases, offsets):
    half = RET_DK // 2
    inv = ROPE_BASE ** (-jnp.arange(half, dtype=F32) / half)
    inv2 = jnp.concatenate([inv, inv])
    sign = jnp.concatenate([-jnp.ones((half,), F32), jnp.ones((half,), F32)])
    a = bases.astype(F32)[:, None, None] * inv2
    b = offsets.astype(F32)[:, None] * inv2
    cb, sb = jnp.cos(b), jnp.sin(b)
    return jnp.cos(a), jnp.sin(a), cb, sb, sign * cb, sign * sb


def _retention_tables(c, rows):
    log_g = jnp.log1p(-(2.0 ** (-5.0 - jnp.arange(RET_HEADS, dtype=F32))))
    i = jnp.arange(c, dtype=F32)
    diff = i[:, None] - i[None, :]
    decay = jnp.where(diff[None] >= 0, jnp.exp(jnp.maximum(diff, 0.0)[None] * log_g[:, None, None]), 0.0)
    q_decay = jnp.exp((i + 1.0)[None, :] * log_g[:, None])
    k_decay = jnp.exp((c - 1.0 - i)[None, :] * log_g[:, None])
    chunk_decay = jnp.exp(c * log_g)
    dec = jnp.zeros((RET_HEADS, rows, RET_CHUNK), F32).at[:, :c, :c].set(decay)
    qd = jnp.zeros((RET_HEADS, rows, 1), F32).at[:, :c, 0].set(q_decay)
    kd = jnp.zeros((RET_HEADS, RET_CHUNK, 1), F32).at[:, :c, 0].set(k_decay)
    cd = jnp.broadcast_to(chunk_decay[:, None, None], (RET_HEADS, 1, RET_DV))
    return dec, qd, kd, cd


def _t5_bucket(dist):
    max_exact = REL_BUCKETS // 2
    d = jnp.maximum(dist.astype(F32), 1.0)
    large = max_exact + (jnp.log(d / max_exact) / math.log(REL_MAX_DIST / max_exact)
                         * (REL_BUCKETS - max_exact)).astype(jnp.int32)
    large = jnp.minimum(large, REL_BUCKETS - 1)
    return jnp.where(dist < max_exact, dist, large)


def _group_bias(rel_bias, g, dil, slots):
    dist = dil * jnp.asarray(slots, dtype=jnp.int32)
    return rel_bias[_t5_bucket(dist)][:, g * ATT_HEADS:(g + 1) * ATT_HEADS].astype(F32).T


def _toeplitz(f, n_rows, n_cols):
    heads, period = f.shape
    assert period == n_rows + n_cols - 1
    g = jnp.roll(f, -(n_rows - 1), axis=1)
    flat = jnp.tile(g, (1, n_rows))[:, :n_rows * (period - 1)]
    return flat.reshape(heads, n_rows, period - 1)[:, :, :n_cols]


def _neg(heads, n):
    return jnp.full((heads, n), NEG, F32)


def _dilate(v, dil):
    heads, n = v.shape
    return jnp.stack([v] + [_neg(heads, n)] * (dil - 1), axis=-1).reshape(heads, n * dil)


def _prompt_bias(tb_rev):
    heads = tb_rev.shape[0]
    f = jnp.concatenate([_neg(heads, ATT_BLOCK - 1), tb_rev, _neg(heads, ATT_BLOCK - 1)], axis=1)
    return _toeplitz(f, ATT_BLOCK, 2 * ATT_BLOCK).reshape(heads // 2, 2 * ATT_BLOCK, 2 * ATT_BLOCK)


def _sample_bias(tb, tb_rev, win, dil):
    heads = tb.shape[0]
    p = SAMPLE_PAD
    f_c = jnp.concatenate([_neg(heads, p - 1), _dilate(tb_rev[:, :N_KEYS - 1], dil)], axis=1)
    bias_c = _toeplitz(f_c, p, win)
    cols = [tb[:, k // dil:k // dil + 1] if k % dil == 0 else _neg(heads, 1) for k in range(p - 1, -1, -1)]
    f_n = jnp.concatenate(cols + [_neg(heads, p - 1)], axis=1)
    bias_n = _toeplitz(f_n, p, p)
    return bias_c, jnp.pad(bias_n, ((0, 0), (0, 0), (0, ATT_BLOCK - p)), constant_values=NEG)


def _kv_rows(tail, g, n_rows):
    b, rows, _ = tail.shape
    kv = tail[:, rows - n_rows:, 2 * g * ATT_OUT_W:(2 * g + 2) * ATT_OUT_W]
    return kv.reshape(b, n_rows, 2, ATT_HEADS, ATT_DH)


def kernel(x_prompt, x_sample, cache_kv_w128, cache_kv_w512, cache_kv_w2048, state_retention,
           w_norm, w_in, q_norm, k_norm, rel_bias, ret_norm, w_proj_ret, w_proj_att, w_out):
    assert w_in.shape[0] == 1
    bp, t, _ = x_prompt.shape
    bs, ts, _ = x_sample.shape
    dils = tuple(d for _, d in ATT_GROUPS)
    assert t % (ATT_BLOCK * max(dils)) == 0 and ts <= SAMPLE_PAD
    caches = (cache_kv_w128[0], cache_kv_w512[0], cache_kv_w2048[0])
    for cch, (win, _) in zip(caches, ATT_GROUPS):
        assert cch.shape[1] == win and win <= PAST_LEN

    wn = w_norm[0].reshape(1, D_MODEL)
    w_in_bf = w_in[0].astype(BF16)
    qg = jnp.tile(q_norm[0] * (ATT_DH ** -0.5), TILE // ATT_DH).reshape(1, TILE)
    kg = jnp.tile(k_norm[0], TILE // ATT_DH).reshape(1, TILE)
    gn = ret_norm[0].reshape(1, RET_V_W)
    wr = w_proj_ret[0].astype(BF16)
    wa = w_proj_att[0].astype(BF16)
    wo = w_out[0].astype(BF16)
    hid = jnp.arange(TILE // 2) // ATT_DH
    seg_mean = jnp.where(hid[:, None] == hid[None, :], 1.0 / ATT_DH, 0.0).astype(BF16)
    asc, desc = tuple(range(N_KEYS)), tuple(range(N_KEYS - 1, -1, -1))
    group_bias = [(_group_bias(rel_bias, g, d, asc), _group_bias(rel_bias, g, d, desc)) for g, d in enumerate(dils)]

    tm_p = 256
    rot_p = _rotary_tables(jnp.arange(0, t, tm_p), jnp.arange(tm_p))
    tail_p_rows = min(max(w for w, _ in ATT_GROUPS), t)
    ret_p, state_p, *qkv_p, gates_p, tail_p = _in_proj(
        x_prompt.reshape(bp * t, D_MODEL), t, dils, tail_p_rows, BF16, wn, w_in_bf, rot_p, qg, kg, seg_mean,
        tm=tm_p, retention=(_retention_tables(RET_CHUNK, RET_CHUNK), gn))
    att_blocks = tuple(min(4, max(1, t // (d * ATT_BLOCK * 2))) for d in dils)
    groups = [_attn_prompt_group(qkv_p[g], _prompt_bias(group_bias[g][1]), n_blocks=att_blocks[g])
              for g in range(N_GROUPS)]
    y_p = _out_proj(x_prompt, ret_p.reshape(bp, t, RET_V_W), gates_p.reshape(bp, t, GATE_W), wr, wa, wo,
                    tm=512, groups=groups)

    pad = SAMPLE_PAD
    ns = bs * pad
    xs = jnp.pad(x_sample, ((0, 0), (0, pad - ts), (0, 0))).reshape(ns, D_MODEL)
    rot_s = _rotary_tables(jnp.full((1,), PAST_LEN), jnp.tile(jnp.arange(pad), bs))
    ret_in_s, *qkv_s, gates_s, tail_s = _in_proj(xs, ns, (1,) * N_GROUPS, ns, F32, wn, w_in_bf, rot_s,
                                                 qg, kg, seg_mean, tm=ns)
    ret_s, state_s = _retention_sample(ret_in_s.reshape(bs, pad, RET_OUT_W), state_retention[0],
                                       _retention_tables(ts, pad), gn)
    qkv_s = [a.reshape(bs, pad, QKV_W) for a in qkv_s]
    caches_t = [jnp.transpose(c, (0, 2, 3, 4, 1)) for c in caches]
    sb = [_sample_bias(*group_bias[g], win, d) for g, (win, d) in enumerate(ATT_GROUPS)]
    att_s = _attn_sample(qkv_s, caches_t, [c for c, _ in sb], jnp.stack([n for _, n in sb]))
    y_s = _out_proj(xs.reshape(1, ns, D_MODEL), ret_s.reshape(1, ns, RET_V_W), gates_s.reshape(1, ns, GATE_W),
                    wr, wa, wo, tm=ns, att=att_s.reshape(1, ns, ATT_OUT_W))
    y_s = y_s.reshape(bs, pad, D_MODEL)[:, :ts]

    kv_p = [_kv_rows(tail_p, g, min(win, t))[None] for g, (win, _) in enumerate(ATT_GROUPS)]
    tail_s = tail_s.reshape(bs, pad, 2 * ATT_W)[:, :ts]
    kv_s = [_kv_rows(tail_s, g, ts)[None] for g in range(N_GROUPS)]
    return (y_p, y_s, state_p[None], state_s[None], kv_p[0], kv_p[1], kv_p[2], kv_s[0], kv_s[1], kv_s[2])
```

```python
import functools
import math

import jax
import jax.numpy as jnp
from jax import lax
from jax.experimental import pallas as pl
from jax.experimental.pallas import tpu as pltpu

D_MODEL = 1024
PAST_LEN = 16384
RET_HEADS = 4
RET_DK = 128
RET_DV = 256
RET_CHUNK = 128
ROPE_BASE = 10000.0
ATT_GROUPS = ((128, 1), (512, 4), (2048, 16))
N_GROUPS = 3
ATT_HEADS = 8
ATT_DH = 64
REL_BUCKETS = 32
REL_MAX_DIST = 2048
EPS = 1e-6

RET_QK_W = RET_HEADS * RET_DK
RET_V_W = RET_HEADS * RET_DV
ATT_W = N_GROUPS * ATT_HEADS * ATT_DH
ATT_OUT_W = ATT_HEADS * ATT_DH
IN_W = 2 * RET_QK_W + 2 * RET_V_W + 3 * ATT_W + ATT_OUT_W + 2 * D_MODEL

LANES = 128
TILE = 512
RET_OUT_W = 2 * RET_QK_W + 2 * RET_V_W
QKV_W = 3 * ATT_OUT_W
GATE_W = 2 * D_MODEL + ATT_OUT_W
ATT_BLOCK = 128
N_KEYS = ATT_BLOCK + 1
SAMPLE_PAD = 8
NEG = -1e30
VMEM_LIMIT = 48 * 1024 * 1024

F32 = jnp.float32
BF16 = jnp.bfloat16


def _nt_dot(a, b):
    return lax.dot_general(a, b, (((1,), (1,)), ((), ())), preferred_element_type=F32)


def _dot(a, b):
    return jnp.dot(a, b, preferred_element_type=F32)


def _proj_schedule():
    sched = [(0, 0, 0, "rot_q"), (RET_QK_W, 0, RET_QK_W, "rot_k")]
    for k in range(2 * RET_V_W // TILE):
        sched.append((2 * RET_QK_W + k * TILE, 0, 2 * RET_QK_W + k * TILE, "plain"))
    att0 = 2 * RET_QK_W + 2 * RET_V_W
    for g in range(N_GROUPS):
        for kind, epi in enumerate(("norm_q", "norm_k", "plain")):
            sched.append((att0 + kind * ATT_W + g * ATT_OUT_W, 1 + g, kind * ATT_OUT_W, epi))
    gate0 = att0 + 3 * ATT_W
    for k in range(2 * D_MODEL // TILE):
        sched.append((gate0 + ATT_OUT_W + k * TILE, 4, k * TILE, "plain"))
    sched.append((gate0, 4, 2 * D_MODEL, "plain"))
    return sched


_SCHEDULE = _proj_schedule()


def _in_proj_kernel(*refs, dils, per_seq, fuse_retention):
    (x_ref, wn_ref, w_ref, ca_ref, sa_ref, cb_ref, sb_ref, cbs_ref, sbs_ref,
     qg_ref, kg_ref, seg_ref) = refs[:12]
    refs = refs[12:]
    if fuse_retention:
        dec_ref, qd_ref, kd_ref, cd_ref, gn_ref, ret_ref, s_ref = refs[:7]
        refs = refs[7:]
    else:
        ret_ref = refs[0]
        refs = refs[1:]
    a0_ref, a1_ref, a2_ref, gate_ref, tail_ref, h_ref, y_ref = refs[:7]
    r_ref = refs[7] if fuse_retention else ret_ref
    kt_ref = refs[8] if fuse_retention else None
    out_refs = (r_ref, a0_ref, a1_ref, a2_ref, gate_ref)

    if fuse_retention:
        @pl.when(pl.program_id(0) % per_seq == 0)
        def _():
            s_ref[...] = jnp.zeros_like(s_ref)

    x = x_ref[...]
    tm = x.shape[0]
    ms = jnp.mean(x * x, axis=-1, keepdims=True)
    h_ref[...] = (x * lax.rsqrt(ms + EPS) * wn_ref[...]).astype(BF16)

    ca, sa = ca_ref[0], sa_ref[0]
    cos = ca * cb_ref[...] - sa * sb_ref[...]
    sin = sa * cbs_ref[...] + ca * sbs_ref[...]

    def rotary(y, scale):
        parts = []
        for hh in range(TILE // RET_DK):
            yh = y[:, hh * RET_DK:(hh + 1) * RET_DK]
            parts.append((yh * cos + pltpu.roll(yh, RET_DK // 2, axis=1) * sin) * scale)
        return jnp.concatenate(parts, axis=1)

    def head_rms(y, gain):
        y2 = (y * y).astype(BF16)
        half = TILE // 2
        ms = jnp.concatenate([_dot(y2[:, :half], seg_ref[...]), _dot(y2[:, half:], seg_ref[...])], axis=1)
        return y * lax.rsqrt(ms + EPS) * gain

    epilogues = {
        "plain": lambda y: y,
        "rot_q": lambda y: rotary(y, 1.0),
        "rot_k": lambda y: rotary(y, RET_DK ** -0.5),
        "norm_q": lambda y: head_rms(y, qg_ref[...]),
        "norm_k": lambda y: head_rms(y, kg_ref[...]),
    }

    kt_base = jnp.minimum(pl.program_id(0), 0)

    def retention_issue(c, h):
        rows = slice(c * RET_CHUNK, (c + 1) * RET_CHUNK)
        ks = slice(h * RET_DK, (h + 1) * RET_DK)
        ks2 = slice(RET_QK_W + h * RET_DK, RET_QK_W + (h + 1) * RET_DK)
        vs_in = slice(2 * RET_QK_W + h * RET_DV, 2 * RET_QK_W + (h + 1) * RET_DV)
        qb = r_ref[rows, ks].astype(BF16)
        vb = r_ref[rows, vs_in].astype(BF16)
        state = s_ref[0, h]
        scores = (_nt_dot(qb, r_ref[rows, ks2].astype(BF16)) * dec_ref[h]).astype(BF16)
        carried = _dot(qb, state.astype(BF16)) * qd_ref[h]
        s_ref[0, h] = state * cd_ref[h] + _dot(kt_ref[kt_base + c, ks, :], vb)
        return rows, h, scores, vb, carried

    def retention_finish(rows, h, scores, vb, carried):
        vs = slice(h * RET_DV, (h + 1) * RET_DV)
        gs_in = slice(2 * RET_QK_W + RET_V_W + h * RET_DV, 2 * RET_QK_W + RET_V_W + (h + 1) * RET_DV)
        o = _dot(scores, vb) + carried
        ret_ref[rows, vs] = _group_norm_gate(o, gn_ref[:, vs], r_ref[rows, gs_in]).astype(ret_ref.dtype)

    def store_tile(val, out_idx, out_col):
        o_ref = out_refs[out_idx]
        ocs = slice(out_col, out_col + TILE)
        if not 1 <= out_idx <= N_GROUPS:
            o_ref[:, ocs] = val.astype(o_ref.dtype)
            return
        g, kind = out_idx - 1, out_col // ATT_OUT_W
        if kind > 0:
            tcol = (2 * g + kind - 1) * ATT_OUT_W
            tail_ref[0, :, tcol:tcol + ATT_OUT_W] = val
        dil = dils[g]
        if dil == 1:
            o_ref[0, 0, :, ocs] = val.astype(o_ref.dtype)
            return
        for c in range(TILE // LANES):
            y_ref[c] = val[:, c * LANES:(c + 1) * LANES]
            for r in range(dil):
                o_ref[0, r, :, out_col + c * LANES:out_col + (c + 1) * LANES] = (
                    y_ref[c, pl.ds(r, tm // dil, stride=dil), :].astype(o_ref.dtype))

    n_ret_tiles = sum(1 for t in _SCHEDULE if t[1] == 0)
    units = [(c, h) for c in range(tm // RET_CHUNK) for h in range(RET_HEADS)] if fuse_retention else []
    assert len(units) <= len(_SCHEDULE) - n_ret_tiles
    for step, (w_col, out_idx, out_col, epi) in enumerate(_SCHEDULE):
        if fuse_retention and step == 2:
            for c, h in units:
                rows = slice(c * RET_CHUNK, (c + 1) * RET_CHUNK)
                kt_ref[c, h * RET_DK:(h + 1) * RET_DK, :] = _decayed_keys_t(
                    r_ref[rows, RET_QK_W + h * RET_DK:RET_QK_W + (h + 1) * RET_DK], kd_ref[h])
        unit = units[step - n_ret_tiles] if 0 <= step - n_ret_tiles < len(units) else None
        issued = retention_issue(*unit) if unit else None
        store_tile(epilogues[epi](_dot(h_ref[...], w_ref[:, w_col:w_col + TILE])), out_idx, out_col)
        if unit:
            retention_finish(*issued)


def _in_proj(x2d, seq_len, dils, tail_rows, out_dtype, wn, w_bf, rot, qg, kg, seg, tm, retention=None):
    n = x2d.shape[0]
    batch = n // seq_len
    per_seq = seq_len // tm
    first_tail = (seq_len - tail_rows) // tm
    fuse = retention is not None
    const = lambda a: pl.BlockSpec(a.shape, lambda i, nd=a.ndim: (0,) * nd)
    base_spec = pl.BlockSpec((1, 1, RET_DK), lambda i: (i % per_seq, 0, 0))
    qkv_spec = lambda d: pl.BlockSpec((1, d, tm // d, QKV_W), lambda i: (i // per_seq, 0, i % per_seq, 0))
    tail_spec = pl.BlockSpec((1, tm, 2 * ATT_W),
                             lambda i: (i // per_seq, jnp.maximum(i % per_seq - first_tail, 0), 0))
    ca, sa, *offset_tables = rot
    args = [x2d, wn, w_bf, ca, sa, *offset_tables, qg, kg, seg]
    in_specs = [
        pl.BlockSpec((tm, D_MODEL), lambda i: (i, 0)),
        const(wn),
        pl.BlockSpec(w_bf.shape, lambda i: (0, 0), pipeline_mode=pl.Buffered(1)),
        base_spec, base_spec, *[const(t) for t in offset_tables],
        const(qg), const(kg), const(seg),
    ]
    scratch = [pltpu.VMEM((tm, D_MODEL), BF16), pltpu.VMEM((TILE // LANES, tm, LANES), F32)]
    if fuse:
        tables, gn = retention
        args += [*tables, gn]
        in_specs += [const(t) for t in (*tables, gn)]
        ret_specs = [pl.BlockSpec((tm, RET_V_W), lambda i: (i, 0)),
                     pl.BlockSpec((1, RET_HEADS, RET_DK, RET_DV), lambda i: (i // per_seq, 0, 0, 0))]
        ret_shapes = [jax.ShapeDtypeStruct((n, RET_V_W), out_dtype),
                      jax.ShapeDtypeStruct((batch, RET_HEADS, RET_DK, RET_DV), F32)]
        scratch += [pltpu.VMEM((tm, RET_OUT_W), F32), pltpu.VMEM((tm // RET_CHUNK, RET_QK_W, RET_CHUNK), BF16)]
    else:
        ret_specs = [pl.BlockSpec((tm, RET_OUT_W), lambda i: (i, 0))]
        ret_shapes = [jax.ShapeDtypeStruct((n, RET_OUT_W), out_dtype)]
    return pl.pallas_call(
        functools.partial(_in_proj_kernel, dils=dils, per_seq=per_seq, fuse_retention=fuse),
        grid=(n // tm,),
        in_specs=in_specs,
        out_specs=ret_specs + [qkv_spec(d) for d in dils]
        + [pl.BlockSpec((tm, GATE_W), lambda i: (i, 0)), tail_spec],
        out_shape=ret_shapes
        + [jax.ShapeDtypeStruct((batch, d, seq_len // d, QKV_W), out_dtype) for d in dils]
        + [jax.ShapeDtypeStruct((n, GATE_W), out_dtype),
           jax.ShapeDtypeStruct((batch, tail_rows, 2 * ATT_W), F32)],
        scratch_shapes=scratch,
        compiler_params=pltpu.CompilerParams(
            dimension_semantics=("arbitrary",), vmem_limit_bytes=VMEM_LIMIT),
        name="in_proj",
    )(*args)


def _group_norm_gate(o, gain, gate):
    gate = gate.astype(F32)
    mu = jnp.mean(o, axis=-1, keepdims=True)
    d = o - mu
    var = jnp.mean(d * d, axis=-1, keepdims=True)
    return gate * jax.nn.sigmoid(gate) * (d * lax.rsqrt(var + EPS) * gain)


def _decayed_keys_t(k_keys, kd):
    return jnp.transpose(k_keys * kd).astype(BF16)


def _retention_specs(rows, idx):
    return [
        pl.BlockSpec((1, rows, RET_QK_W), lambda *a: (*idx(*a), 0)),
        pl.BlockSpec((1, rows, RET_QK_W), lambda *a: (*idx(*a), 1)),
        pl.BlockSpec((1, rows, RET_V_W), lambda *a: (*idx(*a), 1)),
        pl.BlockSpec((1, rows, RET_V_W), lambda *a: (*idx(*a), 2)),
    ]


def _table_specs(tables, ndim_grid):
    return [pl.BlockSpec(t.shape, lambda *a, nd=t.ndim: (0,) * nd) for t in tables]


def _retention_sample_kernel(q_ref, k_ref, v_ref, g_ref, s_in_ref, dec_ref, qd_ref, kd_ref, cd_ref,
                             gn_ref, o_ref, s_out_ref, kpad_ref, vpad_ref):
    @pl.when(pl.program_id(0) == 0)
    def _():
        kpad_ref[...] = jnp.zeros_like(kpad_ref)
        vpad_ref[...] = jnp.zeros_like(vpad_ref)

    kpad_ref[0:SAMPLE_PAD, :] = k_ref[0]
    vpad_ref[0:SAMPLE_PAD, :] = v_ref[0]
    issued = []
    for h in range(RET_HEADS):
        ks = slice(h * RET_DK, (h + 1) * RET_DK)
        vs = slice(h * RET_DV, (h + 1) * RET_DV)
        qb = q_ref[0, :, ks].astype(BF16)
        vb = vpad_ref[:, vs].astype(BF16)
        state = s_in_ref[0, h]
        scores = (_nt_dot(qb, kpad_ref[:, ks].astype(BF16)) * dec_ref[h]).astype(BF16)
        carried = _dot(qb, state.astype(BF16)) * qd_ref[h]
        s_out_ref[0, h] = state * cd_ref[h] + _dot(_decayed_keys_t(kpad_ref[:, ks], kd_ref[h]), vb)
        issued.append((scores, vb, carried))
    for h, (scores, vb, carried) in enumerate(issued):
        vs = slice(h * RET_DV, (h + 1) * RET_DV)
        o = _dot(scores, vb) + carried
        o_ref[0, :, vs] = _group_norm_gate(o, gn_ref[:, vs], g_ref[0, :, vs]).astype(o_ref.dtype)


def _retention_sample(ret, state, tables, gn):
    b, p, _ = ret.shape
    state_spec = pl.BlockSpec((1, RET_HEADS, RET_DK, RET_DV), lambda bi: (bi, 0, 0, 0))
    return pl.pallas_call(
        _retention_sample_kernel,
        grid=(b,),
        in_specs=_retention_specs(p, lambda bi: (bi, 0)) + [state_spec] + _table_specs(tables + (gn,), 1),
        out_specs=[pl.BlockSpec((1, p, RET_V_W), lambda bi: (bi, 0, 0)), state_spec],
        out_shape=[
            jax.ShapeDtypeStruct((b, p, RET_V_W), ret.dtype),
            jax.ShapeDtypeStruct(state.shape, F32),
        ],
        scratch_shapes=[pltpu.VMEM((RET_CHUNK, RET_QK_W), F32), pltpu.VMEM((RET_CHUNK, RET_V_W), F32)],
        compiler_params=pltpu.CompilerParams(
            dimension_semantics=("arbitrary",), vmem_limit_bytes=VMEM_LIMIT),
        name="retention_sample",
    )(ret, ret, ret, ret, state, *tables, gn)


def _attn_prompt_blocks(q_ref, kp_ref, kc_ref, vp_ref, vc_ref, bias_ref, o_ref, lse_ref, first):
    n_blocks = q_ref.shape[2] // ATT_BLOCK
    pair_w = 2 * ATT_DH
    low_q = lax.broadcasted_iota(jnp.int32, (ATT_BLOCK, pair_w), 1) < ATT_DH
    rows0, rows1 = slice(0, ATT_BLOCK), slice(ATT_BLOCK, 2 * ATT_BLOCK)

    def keys(prev_ref, cur_ref, t, ps):
        if t > 0:
            return cur_ref[0, 0, (t - 1) * ATT_BLOCK:(t + 1) * ATT_BLOCK, ps]
        if first:
            return cur_ref[0, 0, 0:ATT_BLOCK, ps]
        return jnp.concatenate([prev_ref[0, 0, :, ps], cur_ref[0, 0, 0:ATT_BLOCK, ps]], axis=0)

    staged = []
    for t in range(n_blocks):
        for p in range(ATT_HEADS // 2):
            ps = slice(p * pair_w, (p + 1) * pair_w)
            qp = q_ref[0, 0, t * ATT_BLOCK:(t + 1) * ATT_BLOCK, ps]
            kcat = keys(kp_ref, kc_ref, t, ps)
            q2 = jnp.concatenate([jnp.where(low_q, qp, 0.0), jnp.where(low_q, 0.0, qp)], axis=0)
            s = _nt_dot(q2, kcat) + bias_ref[p, :, 2 * ATT_BLOCK - kcat.shape[0]:]
            m = jnp.max(s, axis=-1, keepdims=True)
            staged.append((t, p, m, jnp.exp(s - m).astype(BF16)))
    for t, p, m, e in staged:
        ps = slice(p * pair_w, (p + 1) * pair_w)
        vcat = keys(vp_ref, vc_ref, t, ps)
        low_k = lax.broadcasted_iota(jnp.int32, vcat.shape, 1) < ATT_DH
        p0 = _dot(e[rows0], jnp.where(low_k, vcat, 1.0))
        p1 = _dot(e[rows1], jnp.where(low_k, 1.0, vcat))
        den = pltpu.roll(jnp.where(low_q, p1, p0), ATT_DH, axis=1)
        ts = slice(t * ATT_BLOCK, (t + 1) * ATT_BLOCK)
        o_ref[0, 0, ts, ps] = jnp.where(low_q, p0, p1) / den
        lse_ref[0, 0, ts, ps] = jnp.where(low_q, m[rows0], m[rows1]) + jnp.log(den)


def _attn_prompt_kernel(q_ref, kp_ref, kc_ref, vp_ref, vc_ref, bias_ref, o_ref, lse_ref):
    first = pl.program_id(2) == 0

    @pl.when(first)
    def _():
        _attn_prompt_blocks(q_ref, kp_ref, kc_ref, vp_ref, vc_ref, bias_ref, o_ref, lse_ref, True)

    @pl.when(jnp.logical_not(first))
    def _():
        _attn_prompt_blocks(q_ref, kp_ref, kc_ref, vp_ref, vc_ref, bias_ref, o_ref, lse_ref, False)


def _attn_prompt_group(qkv, bias, n_blocks):
    b, dil, tr, _ = qkv.shape
    rows = n_blocks * ATT_BLOCK
    cur = lambda c: pl.BlockSpec((1, 1, rows, ATT_OUT_W), lambda bi, r, j: (bi, r, j, c))
    prev = lambda c: pl.BlockSpec((1, 1, ATT_BLOCK, ATT_OUT_W),
                                  lambda bi, r, j: (bi, r, jnp.maximum(j * n_blocks - 1, 0), c))
    res_shape = jax.ShapeDtypeStruct((b, dil, tr, ATT_OUT_W), F32)
    return pl.pallas_call(
        _attn_prompt_kernel,
        grid=(b, dil, tr // rows),
        in_specs=[cur(0), prev(1), cur(1), prev(2), cur(2),
                  pl.BlockSpec(bias.shape, lambda bi, r, j: (0, 0, 0))],
        out_specs=[cur(0), cur(0)],
        out_shape=[res_shape, res_shape],
        compiler_params=pltpu.CompilerParams(
            dimension_semantics=("parallel", "parallel", "arbitrary"), vmem_limit_bytes=VMEM_LIMIT),
        name=f"attn_prompt_d{dil}",
    )(qkv, qkv, qkv, qkv, qkv, bias)


def _attn_sample_kernel(a0_ref, a1_ref, a2_ref, c0_ref, c1_ref, c2_ref, b0_ref, b1_ref, b2_ref, bn_ref,
                        o_ref, kn_ref, vn_ref):
    qkv_refs = (a0_ref, a1_ref, a2_ref)
    cache_refs = (c0_ref, c1_ref, c2_ref)
    bias_refs = (b0_ref, b1_ref, b2_ref)

    @pl.when(pl.program_id(0) == 0)
    def _():
        kn_ref[...] = jnp.zeros_like(kn_ref)
        vn_ref[...] = jnp.zeros_like(vn_ref)

    for g in range(N_GROUPS):
        gs = slice(g * ATT_OUT_W, (g + 1) * ATT_OUT_W)
        kn_ref[0:SAMPLE_PAD, gs] = qkv_refs[g][0, :, ATT_OUT_W:2 * ATT_OUT_W]
        vn_ref[0:SAMPLE_PAD, gs] = qkv_refs[g][0, :, 2 * ATT_OUT_W:3 * ATT_OUT_W]

    staged = []
    for h in range(ATT_HEADS):
        logits = []
        for g in range(N_GROUPS):
            hs = slice(g * ATT_OUT_W + h * ATT_DH, g * ATT_OUT_W + (h + 1) * ATT_DH)
            qh = qkv_refs[g][0, :, h * ATT_DH:(h + 1) * ATT_DH].astype(BF16)
            logits.append(_dot(qh, cache_refs[g][0, 0, h].astype(BF16)) + bias_refs[g][h])
            logits.append(_nt_dot(qh, kn_ref[:, hs].astype(BF16)) + bn_ref[g, h])
        m = functools.reduce(jnp.maximum, [jnp.max(x, axis=-1, keepdims=True) for x in logits])
        es = [jnp.exp(x - m) for x in logits]
        l = functools.reduce(jnp.add, [jnp.sum(e, axis=-1, keepdims=True) for e in es])
        staged.append(([e.astype(BF16) for e in es], l))
    for h, (es, l) in enumerate(staged):
        acc = jnp.zeros((SAMPLE_PAD, ATT_DH), F32)
        for g in range(N_GROUPS):
            hs = slice(g * ATT_OUT_W + h * ATT_DH, g * ATT_OUT_W + (h + 1) * ATT_DH)
            acc = acc + _nt_dot(es[2 * g], cache_refs[g][0, 1, h].astype(BF16))
            acc = acc + _dot(es[2 * g + 1], vn_ref[:, hs].astype(BF16))
        o_ref[0, :, h * ATT_DH:(h + 1) * ATT_DH] = acc / l


def _attn_sample(qkvs, caches_t, biases, bias_new):
    b, p, _ = qkvs[0].shape
    return pl.pallas_call(
        _attn_sample_kernel,
        grid=(b,),
        in_specs=[pl.BlockSpec((1, p, QKV_W), lambda bi: (bi, 0, 0)) for _ in qkvs]
        + [pl.BlockSpec((1,) + c.shape[1:], lambda bi: (bi, 0, 0, 0, 0)) for c in caches_t]
        + [pl.BlockSpec(x.shape, lambda bi: (0, 0, 0)) for x in biases]
        + [pl.BlockSpec(bias_new.shape, lambda bi: (0, 0, 0, 0))],
        out_specs=pl.BlockSpec((1, p, ATT_OUT_W), lambda bi: (bi, 0, 0)),
        out_shape=jax.ShapeDtypeStruct((b, p, ATT_OUT_W), F32),
        scratch_shapes=[pltpu.VMEM((ATT_BLOCK, ATT_W), F32), pltpu.VMEM((ATT_BLOCK, ATT_W), F32)],
        compiler_params=pltpu.CompilerParams(
            dimension_semantics=("arbitrary",), vmem_limit_bytes=VMEM_LIMIT),
        name="attn_sample",
    )(*qkvs, *caches_t, *biases, bias_new)


def _out_proj_kernel(*refs, dils, sub):
    x_ref, ret_ref, ga_ref, gb_ref, ag_ref, wr_ref, wa_ref, wo_ref = refs[:8]
    tm = x_ref.shape[1]
    if dils is None:
        att_ref, o_ref = refs[8:]
        group_rows = None
    else:
        group_refs = refs[8:8 + 2 * N_GROUPS]
        o_ref = refs[8 + 2 * N_GROUPS]
        scratch = refs[9 + 2 * N_GROUPS:]
        group_rows = []
        for g, dil in enumerate(dils):
            og_ref, lg_ref = group_refs[2 * g], group_refs[2 * g + 1]
            if dil == 1:
                group_rows.append(lambda rs, og_ref=og_ref, lg_ref=lg_ref: (og_ref[0, 0, rs], lg_ref[0, 0, rs]))
                continue
            so_ref, sl_ref = scratch[2 * (g - 1)], scratch[2 * (g - 1) + 1]
            n_chunks = ATT_OUT_W // LANES
            for c in range(n_chunks):
                cs = slice(c * LANES, (c + 1) * LANES)
                for r in range(dil):
                    so_ref[c, pl.ds(r, tm // dil, stride=dil), :] = og_ref[0, r, :, cs]
                    sl_ref[c, pl.ds(r, tm // dil, stride=dil), :] = lg_ref[0, r, :, cs]
            group_rows.append(lambda rs, so_ref=so_ref, sl_ref=sl_ref, n_chunks=n_chunks: (
                jnp.concatenate([so_ref[c, rs] for c in range(n_chunks)], axis=1),
                jnp.concatenate([sl_ref[c, rs] for c in range(n_chunks)], axis=1)))

    def branches(rs):
        if group_rows is None:
            att = att_ref[0, rs]
        else:
            os, lses = zip(*[f(rs) for f in group_rows])
            mx = functools.reduce(jnp.maximum, lses)
            ws = [jnp.exp(l - mx) for l in lses]
            att = functools.reduce(jnp.add, [w * o for w, o in zip(ws, os)]) / functools.reduce(jnp.add, ws)
        ag = ag_ref[0, rs].astype(F32)
        u = (ag * jax.nn.sigmoid(ag) * att).astype(BF16)
        o_b = _dot(u, wa_ref[...])
        return _dot(ret_ref[0, rs].astype(BF16), wr_ref[...]), o_b

    def finish(rs, o_a, o_b):
        merged = jax.nn.sigmoid(ga_ref[0, rs].astype(F32)) * o_a + jax.nn.sigmoid(gb_ref[0, rs].astype(F32)) * o_b
        o_ref[0, rs] = x_ref[0, rs] + _dot(merged.astype(BF16), wo_ref[...])

    blocks = [slice(i, i + sub) for i in range(0, tm, sub)]
    pending = None
    for rs in blocks:
        cur = (rs, *branches(rs))
        if pending is not None:
            finish(*pending)
        pending = cur
    finish(*pending)


def _out_proj(x, ret, gates, wr, wa, wo, tm, sub, att=None, groups=None):
    b, t, _ = x.shape
    row = lambda w, c: pl.BlockSpec((1, tm, w), lambda bi, i: (bi, i, c))
    full = lambda a: pl.BlockSpec(a.shape, lambda bi, i: (0, 0))
    in_specs = [row(D_MODEL, 0), row(RET_V_W, 0), row(D_MODEL, 0), row(D_MODEL, 1),
                row(ATT_OUT_W, 2 * D_MODEL // ATT_OUT_W), full(wr), full(wa), full(wo)]
    args = [x, ret, gates, gates, gates, wr, wa, wo]
    scratch = []
    if groups is None:
        dils = None
        in_specs.append(row(ATT_OUT_W, 0))
        args.append(att)
    else:
        dils = tuple(o.shape[1] for o, _ in groups)
        for (o, lse), d in zip(groups, dils):
            spec = pl.BlockSpec((1, d, tm // d, ATT_OUT_W), lambda bi, i: (bi, 0, i, 0))
            in_specs += [spec, spec]
            args += [o, lse]
            if d > 1:
                scratch += [pltpu.VMEM((ATT_OUT_W // LANES, tm, LANES), F32)] * 2
    return pl.pallas_call(
        functools.partial(_out_proj_kernel, dils=dils, sub=sub),
        grid=(b, t // tm),
        in_specs=in_specs,
        out_specs=row(D_MODEL, 0),
        out_shape=jax.ShapeDtypeStruct((b, t, D_MODEL), F32),
        scratch_shapes=scratch,
        compiler_params=pltpu.CompilerParams(
            dimension_semantics=("parallel", "parallel"), vmem_limit_bytes=VMEM_LIMIT),
        name="out_proj",
    )(*args)


def _rotary_tables(bases, offsets):
    half = RET_DK // 2
    inv = ROPE_BASE ** (-jnp.arange(half, dtype=F32) / half)
    inv2 = jnp.concatenate([inv, inv])
    sign = jnp.concatenate([-jnp.ones((half,), F32), jnp.ones((half,), F32)])
    a = bases.astype(F32)[:, None, None] * inv2
    b = offsets.astype(F32)[:, None] * inv2
    cb, sb = jnp.cos(b), jnp.sin(b)
    return jnp.cos(a), jnp.sin(a), cb, sb, sign * cb, sign * sb


def _retention_tables(c, rows):
    log_g = jnp.log1p(-(2.0 ** (-5.0 - jnp.arange(RET_HEADS, dtype=F32))))
    i = jnp.arange(c, dtype=F32)
    diff = i[:, None] - i[None, :]
    decay = jnp.where(diff[None] >= 0, jnp.exp(jnp.maximum(diff, 0.0)[None] * log_g[:, None, None]), 0.0)
    q_decay = jnp.exp((i + 1.0)[None, :] * log_g[:, None])
    k_decay = jnp.exp((c - 1.0 - i)[None, :] * log_g[:, None])
    chunk_decay = jnp.exp(c * log_g)
    dec = jnp.zeros((RET_HEADS, rows, RET_CHUNK), F32).at[:, :c, :c].set(decay)
    qd = jnp.zeros((RET_HEADS, rows, 1), F32).at[:, :c, 0].set(q_decay)
    kd = jnp.zeros((RET_HEADS, RET_CHUNK, 1), F32).at[:, :c, 0].set(k_decay)
    cd = jnp.broadcast_to(chunk_decay[:, None, None], (RET_HEADS, 1, RET_DV))
    return dec, qd, kd, cd


def _t5_bucket(dist):
    max_exact = REL_BUCKETS // 2
    d = jnp.maximum(dist.astype(F32), 1.0)
    large = max_exact + (jnp.log(d / max_exact) / math.log(REL_MAX_DIST / max_exact)
                         * (REL_BUCKETS - max_exact)).astype(jnp.int32)
    large = jnp.minimum(large, REL_BUCKETS - 1)
    return jnp.where(dist < max_exact, dist, large)


def _group_bias(rel_bias, g, dil, slots):
    dist = dil * jnp.asarray(slots, dtype=jnp.int32)
    return rel_bias[_t5_bucket(dist)][:, g * ATT_HEADS:(g + 1) * ATT_HEADS].astype(F32).T


def _toeplitz(f, n_rows, n_cols):
    heads, period = f.shape
    assert period == n_rows + n_cols - 1
    g = jnp.roll(f, -(n_rows - 1), axis=1)
    flat = jnp.tile(g, (1, n_rows))[:, :n_rows * (period - 1)]
    return flat.reshape(heads, n_rows, period - 1)[:, :, :n_cols]


def _neg(heads, n):
    return jnp.full((heads, n), NEG, F32)


def _dilate(v, dil):
    heads, n = v.shape
    return jnp.stack([v] + [_neg(heads, n)] * (dil - 1), axis=-1).reshape(heads, n * dil)


def _prompt_bias(tb_rev):
    heads = tb_rev.shape[0]
    f = jnp.concatenate([_neg(heads, ATT_BLOCK - 1), tb_rev, _neg(heads, ATT_BLOCK - 1)], axis=1)
    return _toeplitz(f, ATT_BLOCK, 2 * ATT_BLOCK).reshape(heads // 2, 2 * ATT_BLOCK, 2 * ATT_BLOCK)


def _sample_bias(tb, tb_rev, win, dil):
    heads = tb.shape[0]
    p = SAMPLE_PAD
    f_c = jnp.concatenate([_neg(heads, p - 1), _dilate(tb_rev[:, :N_KEYS - 1], dil)], axis=1)
    bias_c = _toeplitz(f_c, p, win)
    cols = [tb[:, k // dil:k // dil + 1] if k % dil == 0 else _neg(heads, 1) for k in range(p - 1, -1, -1)]
    f_n = jnp.concatenate(cols + [_neg(heads, p - 1)], axis=1)
    bias_n = _toeplitz(f_n, p, p)
    return bias_c, jnp.pad(bias_n, ((0, 0), (0, 0), (0, ATT_BLOCK - p)), constant_values=NEG)


def _kv_rows(tail, g, n_rows):
    b, rows, _ = tail.shape
    kv = tail[:, rows - n_rows:, 2 * g * ATT_OUT_W:(2 * g + 2) * ATT_OUT_W]
    return kv.reshape(b, n_rows, 2, ATT_HEADS, ATT_DH)


def kernel(x_prompt, x_sample, cache_kv_w128, cache_kv_w512, cache_kv_w2048, state_retention,
           w_norm, w_in, q_norm, k_norm, rel_bias, ret_norm, w_proj_ret, w_proj_att, w_out):
    assert w_in.shape[0] == 1
    bp, t, _ = x_prompt.shape
    bs, ts, _ = x_sample.shape
    dils = tuple(d for _, d in ATT_GROUPS)
    assert t % (ATT_BLOCK * max(dils)) == 0 and ts <= SAMPLE_PAD
    caches = (cache_kv_w128[0], cache_kv_w512[0], cache_kv_w2048[0])
    for cch, (win, _) in zip(caches, ATT_GROUPS):
        assert cch.shape[1] == win and win <= PAST_LEN

    wn = w_norm[0].reshape(1, D_MODEL)
    w_in_bf = w_in[0].astype(BF16)
    qg = jnp.tile(q_norm[0] * (ATT_DH ** -0.5), TILE // ATT_DH).reshape(1, TILE)
    kg = jnp.tile(k_norm[0], TILE // ATT_DH).reshape(1, TILE)
    gn = ret_norm[0].reshape(1, RET_V_W)
    wr = w_proj_ret[0].astype(BF16)
    wa = w_proj_att[0].astype(BF16)
    wo = w_out[0].astype(BF16)
    hid = jnp.arange(TILE // 2) // ATT_DH
    seg_mean = jnp.where(hid[:, None] == hid[None, :], 1.0 / ATT_DH, 0.0).astype(BF16)
    asc, desc = tuple(range(N_KEYS)), tuple(range(N_KEYS - 1, -1, -1))
    group_bias = [(_group_bias(rel_bias, g, d, asc), _group_bias(rel_bias, g, d, desc)) for g, d in enumerate(dils)]

    tm_p = 256
    rot_p = _rotary_tables(jnp.arange(0, t, tm_p), jnp.arange(tm_p))
    tail_p_rows = min(max(w for w, _ in ATT_GROUPS), t)
    ret_p, state_p, *qkv_p, gates_p, tail_p = _in_proj(
        x_prompt.reshape(bp * t, D_MODEL), t, dils, tail_p_rows, BF16, wn, w_in_bf, rot_p, qg, kg, seg_mean,
        tm=tm_p, retention=(_retention_tables(RET_CHUNK, RET_CHUNK), gn))
    att_blocks = tuple(min(4, max(1, t // (d * ATT_BLOCK * 2))) for d in dils)
    groups = [_attn_prompt_group(qkv_p[g], _prompt_bias(group_bias[g][1]), n_blocks=att_blocks[g])
              for g in range(N_GROUPS)]
    y_p = _out_proj(x_prompt, ret_p.reshape(bp, t, RET_V_W), gates_p.reshape(bp, t, GATE_W), wr, wa, wo,
                    tm=512, sub=512, groups=groups)

    pad = SAMPLE_PAD
    ns = bs * pad
    xs = jnp.pad(x_sample, ((0, 0), (0, pad - ts), (0, 0))).reshape(ns, D_MODEL)
    rot_s = _rotary_tables(jnp.full((1,), PAST_LEN), jnp.tile(jnp.arange(pad), bs))
    ret_in_s, *qkv_s, gates_s, tail_s = _in_proj(xs, ns, (1,) * N_GROUPS, ns, F32, wn, w_in_bf, rot_s,
                                                 qg, kg, seg_mean, tm=ns)
    ret_s, state_s = _retention_sample(ret_in_s.reshape(bs, pad, RET_OUT_W), state_retention[0],
                                       _retention_tables(ts, pad), gn)
    qkv_s = [a.reshape(bs, pad, QKV_W) for a in qkv_s]
    caches_t = [jnp.transpose(c, (0, 2, 3, 4, 1)) for c in caches]
    sb = [_sample_bias(*group_bias[g], win, d) for g, (win, d) in enumerate(ATT_GROUPS)]
    att_s = _attn_sample(qkv_s, caches_t, [c for c, _ in sb], jnp.stack([n for _, n in sb]))
    y_s = _out_proj(xs.reshape(1, ns, D_MODEL), ret_s.reshape(1, ns, RET_V_W), gates_s.reshape(1, ns, GATE_W),
                    wr, wa, wo, tm=ns, sub=ns, att=att_s.reshape(1, ns, ATT_OUT_W))
    y_s = y_s.reshape(bs, pad, D_MODEL)[:, :ts]

    kv_p = [_kv_rows(tail_p, g, min(win, t))[None] for g, (win, _) in enumerate(ATT_GROUPS)]
    tail_s = tail_s.reshape(bs, pad, 2 * ATT_W)[:, :ts]
    kv_s = [_kv_rows(tail_s, g, ts)[None] for g in range(N_GROUPS)]
    return (y_p, y_s, state_p[None], state_s[None], kv_p[0], kv_p[1], kv_p[2], kv_s[0], kv_s[1], kv_s[2])
```

```python
import functools
import math

import jax
import jax.numpy as jnp
from jax import lax
from jax.experimental import pallas as pl
from jax.experimental.pallas import tpu as pltpu

D_MODEL = 1024
PAST_LEN = 16384
RET_HEADS = 4
RET_DK = 128
RET_DV = 256
RET_CHUNK = 128
ROPE_BASE = 10000.0
ATT_GROUPS = ((128, 1), (512, 4), (2048, 16))
N_GROUPS = 3
ATT_HEADS = 8
ATT_DH = 64
REL_BUCKETS = 32
REL_MAX_DIST = 2048
EPS = 1e-6

RET_QK_W = RET_HEADS * RET_DK
RET_V_W = RET_HEADS * RET_DV
ATT_W = N_GROUPS * ATT_HEADS * ATT_DH
ATT_OUT_W = ATT_HEADS * ATT_DH
IN_W = 2 * RET_QK_W + 2 * RET_V_W + 3 * ATT_W + ATT_OUT_W + 2 * D_MODEL

LANES = 128
TILE = 512
RET_OUT_W = 2 * RET_QK_W + 2 * RET_V_W
QKV_W = 3 * ATT_OUT_W
GATE_W = 2 * D_MODEL + ATT_OUT_W
ATT_BLOCK = 128
N_KEYS = ATT_BLOCK + 1
SAMPLE_PAD = 8
NEG = -1e30
VMEM_LIMIT = 48 * 1024 * 1024

F32 = jnp.float32
BF16 = jnp.bfloat16


def _nt_dot(a, b):
    return lax.dot_general(a, b, (((1,), (1,)), ((), ())), preferred_element_type=F32)


def _dot(a, b):
    return jnp.dot(a, b, preferred_element_type=F32)


def _proj_schedule():
    sched = [(0, 0, 0, "rot_q"), (RET_QK_W, 0, RET_QK_W, "rot_k")]
    for k in range(2 * RET_V_W // TILE):
        sched.append((2 * RET_QK_W + k * TILE, 0, 2 * RET_QK_W + k * TILE, "plain"))
    att0 = 2 * RET_QK_W + 2 * RET_V_W
    for g in range(N_GROUPS):
        for kind, epi in enumerate(("norm_q", "norm_k", "plain")):
            sched.append((att0 + kind * ATT_W + g * ATT_OUT_W, 1 + g, kind * ATT_OUT_W, epi))
    gate0 = att0 + 3 * ATT_W
    for k in range(2 * D_MODEL // TILE):
        sched.append((gate0 + ATT_OUT_W + k * TILE, 4, k * TILE, "plain"))
    sched.append((gate0, 4, 2 * D_MODEL, "plain"))
    return sched


_SCHEDULE = _proj_schedule()


def _in_proj_kernel(*refs, dils, per_seq, fuse_retention, n_tails):
    (x_ref, wn_ref, w_ref, ca_ref, sa_ref, cb_ref, sb_ref, cbs_ref, sbs_ref,
     qg_ref, kg_ref, seg_ref) = refs[:12]
    refs = refs[12:]
    if fuse_retention:
        dec_ref, qd_ref, kd_ref, cd_ref, gn_ref, ret_ref, s_ref = refs[:7]
        refs = refs[7:]
    else:
        ret_ref = refs[0]
        refs = refs[1:]
    a0_ref, a1_ref, a2_ref, gate_ref = refs[:4]
    tail_refs = refs[4:4 + n_tails]
    h_ref, y_ref = refs[4 + n_tails:6 + n_tails]
    refs = refs[6 + n_tails:]
    r_ref = refs[0] if fuse_retention else ret_ref
    kt_ref = refs[1] if fuse_retention else None
    out_refs = (r_ref, a0_ref, a1_ref, a2_ref, gate_ref)

    if fuse_retention:
        @pl.when(pl.program_id(0) % per_seq == 0)
        def _():
            s_ref[...] = jnp.zeros_like(s_ref)

    x = x_ref[...]
    tm = x.shape[0]
    ms = jnp.mean(x * x, axis=-1, keepdims=True)
    h_ref[...] = (x * lax.rsqrt(ms + EPS) * wn_ref[...]).astype(BF16)

    ca, sa = ca_ref[0], sa_ref[0]
    cos = ca * cb_ref[...] - sa * sb_ref[...]
    sin = sa * cbs_ref[...] + ca * sbs_ref[...]

    def rotary(y, scale):
        parts = []
        for hh in range(TILE // RET_DK):
            yh = y[:, hh * RET_DK:(hh + 1) * RET_DK]
            parts.append((yh * cos + pltpu.roll(yh, RET_DK // 2, axis=1) * sin) * scale)
        return jnp.concatenate(parts, axis=1)

    def head_rms(y, gain):
        y2 = (y * y).astype(BF16)
        half = TILE // 2
        ms = jnp.concatenate([_dot(y2[:, :half], seg_ref[...]), _dot(y2[:, half:], seg_ref[...])], axis=1)
        return y * lax.rsqrt(ms + EPS) * gain

    epilogues = {
        "plain": lambda y: y,
        "rot_q": lambda y: rotary(y, 1.0),
        "rot_k": lambda y: rotary(y, RET_DK ** -0.5),
        "norm_q": lambda y: head_rms(y, qg_ref[...]),
        "norm_k": lambda y: head_rms(y, kg_ref[...]),
    }

    kt_base = jnp.minimum(pl.program_id(0), 0)

    def retention_issue(c, h):
        rows = slice(c * RET_CHUNK, (c + 1) * RET_CHUNK)
        ks = slice(h * RET_DK, (h + 1) * RET_DK)
        ks2 = slice(RET_QK_W + h * RET_DK, RET_QK_W + (h + 1) * RET_DK)
        vs_in = slice(2 * RET_QK_W + h * RET_DV, 2 * RET_QK_W + (h + 1) * RET_DV)
        qb = r_ref[rows, ks].astype(BF16)
        vb = r_ref[rows, vs_in].astype(BF16)
        state = s_ref[0, h]
        scores = (_nt_dot(qb, r_ref[rows, ks2].astype(BF16)) * dec_ref[h]).astype(BF16)
        carried = _dot(qb, state.astype(BF16)) * qd_ref[h]
        s_ref[0, h] = state * cd_ref[h] + _dot(kt_ref[kt_base + c, ks, :], vb)
        return rows, h, scores, vb, carried

    def retention_finish(rows, h, scores, vb, carried):
        vs = slice(h * RET_DV, (h + 1) * RET_DV)
        gs_in = slice(2 * RET_QK_W + RET_V_W + h * RET_DV, 2 * RET_QK_W + RET_V_W + (h + 1) * RET_DV)
        o = _dot(scores, vb) + carried
        ret_ref[rows, vs] = _group_norm_gate(o, gn_ref[:, vs], r_ref[rows, gs_in]).astype(ret_ref.dtype)

    def store_tile(val, out_idx, out_col):
        o_ref = out_refs[out_idx]
        ocs = slice(out_col, out_col + TILE)
        if not 1 <= out_idx <= N_GROUPS:
            o_ref[:, ocs] = val.astype(o_ref.dtype)
            return
        g, kind = out_idx - 1, out_col // ATT_OUT_W
        if kind > 0 and tail_refs:
            t_ref = tail_refs[g]
            t_ref[0, kind - 1] = jnp.transpose(val)[:, tm - t_ref.shape[3]:]
        dil = dils[g]
        if dil == 1:
            o_ref[0, 0, :, ocs] = val.astype(o_ref.dtype)
            return
        for c in range(TILE // LANES):
            y_ref[c] = val[:, c * LANES:(c + 1) * LANES]
            for r in range(dil):
                o_ref[0, r, :, out_col + c * LANES:out_col + (c + 1) * LANES] = (
                    y_ref[c, pl.ds(r, tm // dil, stride=dil), :].astype(o_ref.dtype))

    n_ret_tiles = sum(1 for t in _SCHEDULE if t[1] == 0)
    units = [(c, h) for c in range(tm // RET_CHUNK) for h in range(RET_HEADS)] if fuse_retention else []
    assert len(units) <= len(_SCHEDULE) - n_ret_tiles
    for step, (w_col, out_idx, out_col, epi) in enumerate(_SCHEDULE):
        if fuse_retention and step == 2:
            for c, h in units:
                rows = slice(c * RET_CHUNK, (c + 1) * RET_CHUNK)
                kt_ref[c, h * RET_DK:(h + 1) * RET_DK, :] = _decayed_keys_t(
                    r_ref[rows, RET_QK_W + h * RET_DK:RET_QK_W + (h + 1) * RET_DK], kd_ref[h])
        unit = units[step - n_ret_tiles] if 0 <= step - n_ret_tiles < len(units) else None
        issued = retention_issue(*unit) if unit else None
        store_tile(epilogues[epi](_dot(h_ref[...], w_ref[:, w_col:w_col + TILE])), out_idx, out_col)
        if unit:
            retention_finish(*issued)


def _in_proj(x2d, seq_len, dils, tail_rows, out_dtype, wn, w_bf, rot, qg, kg, seg, tm, retention=None):
    n = x2d.shape[0]
    batch = n // seq_len
    per_seq = seq_len // tm
    fuse = retention is not None
    const = lambda a: pl.BlockSpec(a.shape, lambda i, nd=a.ndim: (0,) * nd)
    base_spec = pl.BlockSpec((1, 1, RET_DK), lambda i: (i % per_seq, 0, 0))
    qkv_spec = lambda d: pl.BlockSpec((1, d, tm // d, QKV_W), lambda i: (i // per_seq, 0, i % per_seq, 0))
    tail_specs, tail_shapes = [], []
    for rows in tail_rows:
        width = min(tm, rows)
        first = (seq_len - rows) // tm if rows >= tm else per_seq
        tail_specs.append(pl.BlockSpec(
            (1, 2, ATT_OUT_W, width),
            lambda i, first=first: (i // per_seq, 0, 0, jnp.maximum(i % per_seq - first, 0))))
        tail_shapes.append(jax.ShapeDtypeStruct((batch, 2, ATT_OUT_W, rows), F32))
    ca, sa, *offset_tables = rot
    args = [x2d, wn, w_bf, ca, sa, *offset_tables, qg, kg, seg]
    in_specs = [
        pl.BlockSpec((tm, D_MODEL), lambda i: (i, 0)),
        const(wn),
        pl.BlockSpec(w_bf.shape, lambda i: (0, 0), pipeline_mode=pl.Buffered(1)),
        base_spec, base_spec, *[const(t) for t in offset_tables],
        const(qg), const(kg), const(seg),
    ]
    scratch = [pltpu.VMEM((tm, D_MODEL), BF16), pltpu.VMEM((TILE // LANES, tm, LANES), F32)]
    if fuse:
        tables, gn = retention
        args += [*tables, gn]
        in_specs += [const(t) for t in (*tables, gn)]
        ret_specs = [pl.BlockSpec((tm, RET_V_W), lambda i: (i, 0)),
                     pl.BlockSpec((1, RET_HEADS, RET_DK, RET_DV), lambda i: (i // per_seq, 0, 0, 0))]
        ret_shapes = [jax.ShapeDtypeStruct((n, RET_V_W), out_dtype),
                      jax.ShapeDtypeStruct((batch, RET_HEADS, RET_DK, RET_DV), F32)]
        scratch += [pltpu.VMEM((tm, RET_OUT_W), F32), pltpu.VMEM((tm // RET_CHUNK, RET_QK_W, RET_CHUNK), BF16)]
    else:
        ret_specs = [pl.BlockSpec((tm, RET_OUT_W), lambda i: (i, 0))]
        ret_shapes = [jax.ShapeDtypeStruct((n, RET_OUT_W), out_dtype)]
    return pl.pallas_call(
        functools.partial(_in_proj_kernel, dils=dils, per_seq=per_seq, fuse_retention=fuse,
                          n_tails=len(tail_rows)),
        grid=(n // tm,),
        in_specs=in_specs,
        out_specs=ret_specs + [qkv_spec(d) for d in dils]
        + [pl.BlockSpec((tm, GATE_W), lambda i: (i, 0))] + tail_specs,
        out_shape=ret_shapes
        + [jax.ShapeDtypeStruct((batch, d, seq_len // d, QKV_W), out_dtype) for d in dils]
        + [jax.ShapeDtypeStruct((n, GATE_W), out_dtype)] + tail_shapes,
        scratch_shapes=scratch,
        compiler_params=pltpu.CompilerParams(
            dimension_semantics=("arbitrary",), vmem_limit_bytes=VMEM_LIMIT),
        name="in_proj",
    )(*args)


def _group_norm_gate(o, gain, gate):
    gate = gate.astype(F32)
    mu = jnp.mean(o, axis=-1, keepdims=True)
    d = o - mu
    var = jnp.mean(d * d, axis=-1, keepdims=True)
    return gate * jax.nn.sigmoid(gate) * (d * lax.rsqrt(var + EPS) * gain)


def _decayed_keys_t(k_keys, kd):
    return jnp.transpose(k_keys * kd).astype(BF16)


def _retention_specs(rows, idx):
    return [
        pl.BlockSpec((1, rows, RET_QK_W), lambda *a: (*idx(*a), 0)),
        pl.BlockSpec((1, rows, RET_QK_W), lambda *a: (*idx(*a), 1)),
        pl.BlockSpec((1, rows, RET_V_W), lambda *a: (*idx(*a), 1)),
        pl.BlockSpec((1, rows, RET_V_W), lambda *a: (*idx(*a), 2)),
    ]


def _table_specs(tables, ndim_grid):
    return [pl.BlockSpec(t.shape, lambda *a, nd=t.ndim: (0,) * nd) for t in tables]


def _retention_sample_kernel(q_ref, k_ref, v_ref, g_ref, s_in_ref, dec_ref, qd_ref, kd_ref, cd_ref,
                             gn_ref, o_ref, s_out_ref, kpad_ref, vpad_ref):
    @pl.when(pl.program_id(0) == 0)
    def _():
        kpad_ref[...] = jnp.zeros_like(kpad_ref)
        vpad_ref[...] = jnp.zeros_like(vpad_ref)

    kpad_ref[0:SAMPLE_PAD, :] = k_ref[0]
    vpad_ref[0:SAMPLE_PAD, :] = v_ref[0]
    issued = []
    for h in range(RET_HEADS):
        ks = slice(h * RET_DK, (h + 1) * RET_DK)
        vs = slice(h * RET_DV, (h + 1) * RET_DV)
        qb = q_ref[0, :, ks].astype(BF16)
        vb = vpad_ref[:, vs].astype(BF16)
        state = s_in_ref[0, h]
        scores = (_nt_dot(qb, kpad_ref[:, ks].astype(BF16)) * dec_ref[h]).astype(BF16)
        carried = _dot(qb, state.astype(BF16)) * qd_ref[h]
        s_out_ref[0, h] = state * cd_ref[h] + _dot(_decayed_keys_t(kpad_ref[:, ks], kd_ref[h]), vb)
        issued.append((scores, vb, carried))
    for h, (scores, vb, carried) in enumerate(issued):
        vs = slice(h * RET_DV, (h + 1) * RET_DV)
        o = _dot(scores, vb) + carried
        o_ref[0, :, vs] = _group_norm_gate(o, gn_ref[:, vs], g_ref[0, :, vs]).astype(o_ref.dtype)


def _retention_sample(ret, state, tables, gn):
    b, p, _ = ret.shape
    state_spec = pl.BlockSpec((1, RET_HEADS, RET_DK, RET_DV), lambda bi: (bi, 0, 0, 0))
    return pl.pallas_call(
        _retention_sample_kernel,
        grid=(b,),
        in_specs=_retention_specs(p, lambda bi: (bi, 0)) + [state_spec] + _table_specs(tables + (gn,), 1),
        out_specs=[pl.BlockSpec((1, p, RET_V_W), lambda bi: (bi, 0, 0)), state_spec],
        out_shape=[
            jax.ShapeDtypeStruct((b, p, RET_V_W), ret.dtype),
            jax.ShapeDtypeStruct(state.shape, F32),
        ],
        scratch_shapes=[pltpu.VMEM((RET_CHUNK, RET_QK_W), F32), pltpu.VMEM((RET_CHUNK, RET_V_W), F32)],
        compiler_params=pltpu.CompilerParams(
            dimension_semantics=("arbitrary",), vmem_limit_bytes=VMEM_LIMIT),
        name="retention_sample",
    )(ret, ret, ret, ret, state, *tables, gn)


def _attn_prompt_blocks(q_ref, kp_ref, kc_ref, vp_ref, vc_ref, bias_ref, o_ref, lse_ref, first):
    n_blocks = q_ref.shape[2] // ATT_BLOCK
    pair_w = 2 * ATT_DH
    low_q = lax.broadcasted_iota(jnp.int32, (ATT_BLOCK, pair_w), 1) < ATT_DH
    rows0, rows1 = slice(0, ATT_BLOCK), slice(ATT_BLOCK, 2 * ATT_BLOCK)

    def keys(prev_ref, cur_ref, t, ps):
        if t > 0:
            return cur_ref[0, 0, (t - 1) * ATT_BLOCK:(t + 1) * ATT_BLOCK, ps]
        if first:
            return cur_ref[0, 0, 0:ATT_BLOCK, ps]
        return jnp.concatenate([prev_ref[0, 0, :, ps], cur_ref[0, 0, 0:ATT_BLOCK, ps]], axis=0)

    staged = []
    for t in range(n_blocks):
        for p in range(ATT_HEADS // 2):
            ps = slice(p * pair_w, (p + 1) * pair_w)
            qp = q_ref[0, 0, t * ATT_BLOCK:(t + 1) * ATT_BLOCK, ps]
            kcat = keys(kp_ref, kc_ref, t, ps)
            q2 = jnp.concatenate([jnp.where(low_q, qp, 0.0), jnp.where(low_q, 0.0, qp)], axis=0)
            s = _nt_dot(q2, kcat) + bias_ref[p, :, 2 * ATT_BLOCK - kcat.shape[0]:]
            m = jnp.max(s, axis=-1, keepdims=True)
            staged.append((t, p, m, jnp.exp(s - m).astype(BF16)))
    for t, p, m, e in staged:
        ps = slice(p * pair_w, (p + 1) * pair_w)
        vcat = keys(vp_ref, vc_ref, t, ps)
        low_k = lax.broadcasted_iota(jnp.int32, vcat.shape, 1) < ATT_DH
        p0 = _dot(e[rows0], jnp.where(low_k, vcat, 1.0))
        p1 = _dot(e[rows1], jnp.where(low_k, 1.0, vcat))
        den = pltpu.roll(jnp.where(low_q, p1, p0), ATT_DH, axis=1)
        ts = slice(t * ATT_BLOCK, (t + 1) * ATT_BLOCK)
        o_ref[0, 0, ts, ps] = jnp.where(low_q, p0, p1) / den
        lse_ref[0, 0, ts, ps] = jnp.where(low_q, m[rows0], m[rows1]) + jnp.log(den)


def _attn_prompt_kernel(q_ref, kp_ref, kc_ref, vp_ref, vc_ref, slot_ref, o_ref, lse_ref, bias_ref):
    first = pl.program_id(2) == 0

    @pl.when(jnp.logical_and(first, jnp.logical_and(pl.program_id(0) == 0, pl.program_id(1) == 0)))
    def _():
        for h in range(ATT_HEADS):
            slots = jnp.broadcast_to(slot_ref[h:h + 1, :], (ATT_BLOCK, slot_ref.shape[1]))
            rows = pltpu.roll(slots, 0, 1, stride=1, stride_axis=0)
            bias_ref[h // 2, (h % 2) * ATT_BLOCK:(h % 2 + 1) * ATT_BLOCK, :] = rows[:, :2 * ATT_BLOCK]

    @pl.when(first)
    def _():
        _attn_prompt_blocks(q_ref, kp_ref, kc_ref, vp_ref, vc_ref, bias_ref, o_ref, lse_ref, True)

    @pl.when(jnp.logical_not(first))
    def _():
        _attn_prompt_blocks(q_ref, kp_ref, kc_ref, vp_ref, vc_ref, bias_ref, o_ref, lse_ref, False)


def _attn_prompt_group(qkv, slot_bias, n_blocks):
    b, dil, tr, _ = qkv.shape
    rows = n_blocks * ATT_BLOCK
    cur = lambda c: pl.BlockSpec((1, 1, rows, ATT_OUT_W), lambda bi, r, j: (bi, r, j, c))
    prev = lambda c: pl.BlockSpec((1, 1, ATT_BLOCK, ATT_OUT_W),
                                  lambda bi, r, j: (bi, r, jnp.maximum(j * n_blocks - 1, 0), c))
    res_shape = jax.ShapeDtypeStruct((b, dil, tr, ATT_OUT_W), F32)
    return pl.pallas_call(
        _attn_prompt_kernel,
        grid=(b, dil, tr // rows),
        in_specs=[cur(0), prev(1), cur(1), prev(2), cur(2),
                  pl.BlockSpec(slot_bias.shape, lambda bi, r, j: (0, 0))],
        out_specs=[cur(0), cur(0)],
        out_shape=[res_shape, res_shape],
        scratch_shapes=[pltpu.VMEM((ATT_HEADS // 2, 2 * ATT_BLOCK, 2 * ATT_BLOCK), F32)],
        compiler_params=pltpu.CompilerParams(
            dimension_semantics=("arbitrary", "arbitrary", "arbitrary"), vmem_limit_bytes=VMEM_LIMIT),
        name=f"attn_prompt_d{dil}",
    )(qkv, qkv, qkv, qkv, qkv, slot_bias)


def _attn_sample_kernel(a0_ref, a1_ref, a2_ref, c0_ref, c1_ref, c2_ref, b0_ref, b1_ref, b2_ref, bn_ref,
                        o_ref, kn_ref, vn_ref):
    qkv_refs = (a0_ref, a1_ref, a2_ref)
    cache_refs = (c0_ref, c1_ref, c2_ref)
    bias_refs = (b0_ref, b1_ref, b2_ref)

    @pl.when(pl.program_id(0) == 0)
    def _():
        kn_ref[...] = jnp.zeros_like(kn_ref)
        vn_ref[...] = jnp.zeros_like(vn_ref)

    for g in range(N_GROUPS):
        gs = slice(g * ATT_OUT_W, (g + 1) * ATT_OUT_W)
        kn_ref[0:SAMPLE_PAD, gs] = qkv_refs[g][0, :, ATT_OUT_W:2 * ATT_OUT_W]
        vn_ref[0:SAMPLE_PAD, gs] = qkv_refs[g][0, :, 2 * ATT_OUT_W:3 * ATT_OUT_W]

    staged = []
    for h in range(ATT_HEADS):
        logits = []
        for g in range(N_GROUPS):
            hs = slice(g * ATT_OUT_W + h * ATT_DH, g * ATT_OUT_W + (h + 1) * ATT_DH)
            qh = qkv_refs[g][0, :, h * ATT_DH:(h + 1) * ATT_DH].astype(BF16)
            logits.append(_dot(qh, cache_refs[g][0, 0, h].astype(BF16)) + bias_refs[g][h])
            logits.append(_nt_dot(qh, kn_ref[:, hs].astype(BF16)) + bn_ref[g, h])
        m = functools.reduce(jnp.maximum, [jnp.max(x, axis=-1, keepdims=True) for x in logits])
        es = [jnp.exp(x - m) for x in logits]
        l = functools.reduce(jnp.add, [jnp.sum(e, axis=-1, keepdims=True) for e in es])
        staged.append(([e.astype(BF16) for e in es], l))
    for h, (es, l) in enumerate(staged):
        acc = jnp.zeros((SAMPLE_PAD, ATT_DH), F32)
        for g in range(N_GROUPS):
            hs = slice(g * ATT_OUT_W + h * ATT_DH, g * ATT_OUT_W + (h + 1) * ATT_DH)
            acc = acc + _nt_dot(es[2 * g], cache_refs[g][0, 1, h].astype(BF16))
            acc = acc + _dot(es[2 * g + 1], vn_ref[:, hs].astype(BF16))
        o_ref[0, :, h * ATT_DH:(h + 1) * ATT_DH] = acc / l


def _attn_sample(qkvs, caches_t, biases, bias_new):
    b, p, _ = qkvs[0].shape
    return pl.pallas_call(
        _attn_sample_kernel,
        grid=(b,),
        in_specs=[pl.BlockSpec((1, p, QKV_W), lambda bi: (bi, 0, 0)) for _ in qkvs]
        + [pl.BlockSpec((1,) + c.shape[1:], lambda bi: (bi, 0, 0, 0, 0)) for c in caches_t]
        + [pl.BlockSpec(x.shape, lambda bi: (0, 0, 0)) for x in biases]
        + [pl.BlockSpec(bias_new.shape, lambda bi: (0, 0, 0, 0))],
        out_specs=pl.BlockSpec((1, p, ATT_OUT_W), lambda bi: (bi, 0, 0)),
        out_shape=jax.ShapeDtypeStruct((b, p, ATT_OUT_W), F32),
        scratch_shapes=[pltpu.VMEM((ATT_BLOCK, ATT_W), F32), pltpu.VMEM((ATT_BLOCK, ATT_W), F32)],
        compiler_params=pltpu.CompilerParams(
            dimension_semantics=("arbitrary",), vmem_limit_bytes=VMEM_LIMIT),
        name="attn_sample",
    )(*qkvs, *caches_t, *biases, bias_new)


def _out_proj_kernel(*refs, dils, sub):
    x_ref, ret_ref, ga_ref, gb_ref, ag_ref, wr_ref, wa_ref, wo_ref = refs[:8]
    tm = x_ref.shape[1]
    if dils is None:
        att_ref, o_ref = refs[8:]
        group_rows = None
    else:
        group_refs = refs[8:8 + 2 * N_GROUPS]
        o_ref = refs[8 + 2 * N_GROUPS]
        scratch = refs[9 + 2 * N_GROUPS:]
        group_rows = []
        for g, dil in enumerate(dils):
            og_ref, lg_ref = group_refs[2 * g], group_refs[2 * g + 1]
            if dil == 1:
                group_rows.append(lambda rs, og_ref=og_ref, lg_ref=lg_ref: (og_ref[0, 0, rs], lg_ref[0, 0, rs]))
                continue
            so_ref, sl_ref = scratch[2 * (g - 1)], scratch[2 * (g - 1) + 1]
            n_chunks = ATT_OUT_W // LANES
            for c in range(n_chunks):
                cs = slice(c * LANES, (c + 1) * LANES)
                for r in range(dil):
                    so_ref[c, pl.ds(r, tm // dil, stride=dil), :] = og_ref[0, r, :, cs]
                    sl_ref[c, pl.ds(r, tm // dil, stride=dil), :] = lg_ref[0, r, :, cs]
            group_rows.append(lambda rs, so_ref=so_ref, sl_ref=sl_ref, n_chunks=n_chunks: (
                jnp.concatenate([so_ref[c, rs] for c in range(n_chunks)], axis=1),
                jnp.concatenate([sl_ref[c, rs] for c in range(n_chunks)], axis=1)))

    def branches(rs):
        if group_rows is None:
            att = att_ref[0, rs]
        else:
            os, lses = zip(*[f(rs) for f in group_rows])
            mx = functools.reduce(jnp.maximum, lses)
            ws = [jnp.exp(l - mx) for l in lses]
            att = functools.reduce(jnp.add, [w * o for w, o in zip(ws, os)]) / functools.reduce(jnp.add, ws)
        ag = ag_ref[0, rs].astype(F32)
        u = (ag * jax.nn.sigmoid(ag) * att).astype(BF16)
        o_b = _dot(u, wa_ref[...])
        return _dot(ret_ref[0, rs].astype(BF16), wr_ref[...]), o_b

    def finish(rs, o_a, o_b):
        merged = jax.nn.sigmoid(ga_ref[0, rs].astype(F32)) * o_a + jax.nn.sigmoid(gb_ref[0, rs].astype(F32)) * o_b
        o_ref[0, rs] = x_ref[0, rs] + _dot(merged.astype(BF16), wo_ref[...])

    blocks = [slice(i, i + sub) for i in range(0, tm, sub)]
    pending = None
    for rs in blocks:
        cur = (rs, *branches(rs))
        if pending is not None:
            finish(*pending)
        pending = cur
    finish(*pending)


def _out_proj(x, ret, gates, wr, wa, wo, tm, sub, att=None, groups=None):
    b, t, _ = x.shape
    row = lambda w, c: pl.BlockSpec((1, tm, w), lambda bi, i: (bi, i, c))
    full = lambda a: pl.BlockSpec(a.shape, lambda bi, i: (0, 0))
    in_specs = [row(D_MODEL, 0), row(RET_V_W, 0), row(D_MODEL, 0), row(D_MODEL, 1),
                row(ATT_OUT_W, 2 * D_MODEL // ATT_OUT_W), full(wr), full(wa), full(wo)]
    args = [x, ret, gates, gates, gates, wr, wa, wo]
    scratch = []
    if groups is None:
        dils = None
        in_specs.append(row(ATT_OUT_W, 0))
        args.append(att)
    else:
        dils = tuple(o.shape[1] for o, _ in groups)
        for (o, lse), d in zip(groups, dils):
            spec = pl.BlockSpec((1, d, tm // d, ATT_OUT_W), lambda bi, i: (bi, 0, i, 0))
            in_specs += [spec, spec]
            args += [o, lse]
            if d > 1:
                scratch += [pltpu.VMEM((ATT_OUT_W // LANES, tm, LANES), F32)] * 2
    return pl.pallas_call(
        functools.partial(_out_proj_kernel, dils=dils, sub=sub),
        grid=(b, t // tm),
        in_specs=in_specs,
        out_specs=row(D_MODEL, 0),
        out_shape=jax.ShapeDtypeStruct((b, t, D_MODEL), F32),
        scratch_shapes=scratch,
        compiler_params=pltpu.CompilerParams(
            dimension_semantics=("parallel", "parallel"), vmem_limit_bytes=VMEM_LIMIT),
        name="out_proj",
    )(*args)


def _rotary_tables(bases, offsets):
    half = RET_DK // 2
    inv = ROPE_BASE ** (-jnp.arange(half, dtype=F32) / half)
    inv2 = jnp.concatenate([inv, inv])
    sign = jnp.concatenate([-jnp.ones((half,), F32), jnp.ones((half,), F32)])
    a = bases.astype(F32)[:, None, None] * inv2
    b = offsets.astype(F32)[:, None] * inv2
    cb, sb = jnp.cos(b), jnp.sin(b)
    return jnp.cos(a), jnp.sin(a), cb, sb, sign * cb, sign * sb


def _retention_tables(c, rows):
    log_g = jnp.log1p(-(2.0 ** (-5.0 - jnp.arange(RET_HEADS, dtype=F32))))
    i = jnp.arange(c, dtype=F32)
    diff = i[:, None] - i[None, :]
    decay = jnp.where(diff[None] >= 0, jnp.exp(jnp.maximum(diff, 0.0)[None] * log_g[:, None, None]), 0.0)
    q_decay = jnp.exp((i + 1.0)[None, :] * log_g[:, None])
    k_decay = jnp.exp((c - 1.0 - i)[None, :] * log_g[:, None])
    chunk_decay = jnp.exp(c * log_g)
    dec = jnp.zeros((RET_HEADS, rows, RET_CHUNK), F32).at[:, :c, :c].set(decay)
    qd = jnp.zeros((RET_HEADS, rows, 1), F32).at[:, :c, 0].set(q_decay)
    kd = jnp.zeros((RET_HEADS, RET_CHUNK, 1), F32).at[:, :c, 0].set(k_decay)
    cd = jnp.broadcast_to(chunk_decay[:, None, None], (RET_HEADS, 1, RET_DV))
    return dec, qd, kd, cd


def _t5_bucket(dist):
    max_exact = REL_BUCKETS // 2
    d = jnp.maximum(dist.astype(F32), 1.0)
    large = max_exact + (jnp.log(d / max_exact) / math.log(REL_MAX_DIST / max_exact)
                         * (REL_BUCKETS - max_exact)).astype(jnp.int32)
    large = jnp.minimum(large, REL_BUCKETS - 1)
    return jnp.where(dist < max_exact, dist, large)


def _group_bias(rel_bias, g, dil, slots):
    dist = dil * jnp.asarray(slots, dtype=jnp.int32)
    return rel_bias[_t5_bucket(dist)][:, g * ATT_HEADS:(g + 1) * ATT_HEADS].astype(F32).T


def _toeplitz(f, n_rows, n_cols):
    heads, period = f.shape
    assert period == n_rows + n_cols - 1
    g = jnp.roll(f, -(n_rows - 1), axis=1)
    flat = jnp.tile(g, (1, n_rows))[:, :n_rows * (period - 1)]
    return flat.reshape(heads, n_rows, period - 1)[:, :, :n_cols]


def _neg(heads, n):
    return jnp.full((heads, n), NEG, F32)


def _dilate(v, dil):
    heads, n = v.shape
    return jnp.stack([v] + [_neg(heads, n)] * (dil - 1), axis=-1).reshape(heads, n * dil)


def _prompt_slots(tb_rev):
    heads = tb_rev.shape[0]
    return jnp.concatenate([tb_rev, _neg(heads, 4 * ATT_BLOCK - N_KEYS)], axis=1)


def _sample_bias(tb, tb_rev, win, dil):
    heads = tb.shape[0]
    p = SAMPLE_PAD
    f_c = jnp.concatenate([_neg(heads, p - 1), _dilate(tb_rev[:, :N_KEYS - 1], dil)], axis=1)
    bias_c = _toeplitz(f_c, p, win)
    cols = [tb[:, k // dil:k // dil + 1] if k % dil == 0 else _neg(heads, 1) for k in range(p - 1, -1, -1)]
    f_n = jnp.concatenate(cols + [_neg(heads, p - 1)], axis=1)
    bias_n = _toeplitz(f_n, p, p)
    return bias_c, jnp.pad(bias_n, ((0, 0), (0, 0), (0, ATT_BLOCK - p)), constant_values=NEG)


def _kv_rows_t(tail_t):
    b, _, _, rows = tail_t.shape
    return jnp.transpose(tail_t.reshape(b, 2, ATT_HEADS, ATT_DH, rows), (0, 4, 1, 2, 3))


def _kv_rows(qkv, n_rows):
    b = qkv.shape[0]
    return qkv[:, :n_rows, ATT_OUT_W:].reshape(b, n_rows, 2, ATT_HEADS, ATT_DH)


def kernel(x_prompt, x_sample, cache_kv_w128, cache_kv_w512, cache_kv_w2048, state_retention,
           w_norm, w_in, q_norm, k_norm, rel_bias, ret_norm, w_proj_ret, w_proj_att, w_out):
    assert w_in.shape[0] == 1
    bp, t, _ = x_prompt.shape
    bs, ts, _ = x_sample.shape
    dils = tuple(d for _, d in ATT_GROUPS)
    assert t % (ATT_BLOCK * max(dils)) == 0 and ts <= SAMPLE_PAD
    caches = (cache_kv_w128[0], cache_kv_w512[0], cache_kv_w2048[0])
    for cch, (win, _) in zip(caches, ATT_GROUPS):
        assert cch.shape[1] == win and win <= PAST_LEN

    wn = w_norm[0].reshape(1, D_MODEL)
    w_in_bf = w_in[0].astype(BF16)
    qg = jnp.tile(q_norm[0] * (ATT_DH ** -0.5), TILE // ATT_DH).reshape(1, TILE)
    kg = jnp.tile(k_norm[0], TILE // ATT_DH).reshape(1, TILE)
    gn = ret_norm[0].reshape(1, RET_V_W)
    wr = w_proj_ret[0].astype(BF16)
    wa = w_proj_att[0].astype(BF16)
    wo = w_out[0].astype(BF16)
    hid = jnp.arange(TILE // 2) // ATT_DH
    seg_mean = jnp.where(hid[:, None] == hid[None, :], 1.0 / ATT_DH, 0.0).astype(BF16)
    asc, desc = tuple(range(N_KEYS)), tuple(range(N_KEYS - 1, -1, -1))
    group_bias = [(_group_bias(rel_bias, g, d, asc), _group_bias(rel_bias, g, d, desc)) for g, d in enumerate(dils)]

    tm_p = 256
    rot_p = _rotary_tables(jnp.arange(0, t, tm_p), jnp.arange(tm_p))
    tail_p_rows = tuple(min(w, t) for w, _ in ATT_GROUPS)
    ret_p, state_p, *rest = _in_proj(
        x_prompt.reshape(bp * t, D_MODEL), t, dils, tail_p_rows, BF16, wn, w_in_bf, rot_p, qg, kg, seg_mean,
        tm=tm_p, retention=(_retention_tables(RET_CHUNK, RET_CHUNK), gn))
    qkv_p, gates_p, tails_p = rest[:N_GROUPS], rest[N_GROUPS], rest[N_GROUPS + 1:]
    att_blocks = tuple(min(4, t // (d * ATT_BLOCK)) for d in dils)
    groups = [_attn_prompt_group(qkv_p[g], _prompt_slots(group_bias[g][1]), n_blocks=att_blocks[g])
              for g in range(N_GROUPS)]
    y_p = _out_proj(x_prompt, ret_p.reshape(bp, t, RET_V_W), gates_p.reshape(bp, t, GATE_W), wr, wa, wo,
                    tm=512, sub=512, groups=groups)

    pad = SAMPLE_PAD
    ns = bs * pad
    xs = jnp.pad(x_sample, ((0, 0), (0, pad - ts), (0, 0))).reshape(ns, D_MODEL)
    rot_s = _rotary_tables(jnp.full((1,), PAST_LEN), jnp.tile(jnp.arange(pad), bs))
    ret_in_s, *qkv_s, gates_s = _in_proj(xs, ns, (1,) * N_GROUPS, (), F32, wn, w_in_bf, rot_s,
                                                 qg, kg, seg_mean, tm=ns)
    ret_s, state_s = _retention_sample(ret_in_s.reshape(bs, pad, RET_OUT_W), state_retention[0],
                                       _retention_tables(ts, pad), gn)
    qkv_s = [a.reshape(bs, pad, QKV_W) for a in qkv_s]
    caches_t = [jnp.transpose(c, (0, 2, 3, 4, 1)) for c in caches]
    sb = [_sample_bias(*group_bias[g], win, d) for g, (win, d) in enumerate(ATT_GROUPS)]
    att_s = _attn_sample(qkv_s, caches_t, [c for c, _ in sb], jnp.stack([n for _, n in sb]))
    y_s = _out_proj(xs.reshape(1, ns, D_MODEL), ret_s.reshape(1, ns, RET_V_W), gates_s.reshape(1, ns, GATE_W),
                    wr, wa, wo, tm=ns, sub=ns, att=att_s.reshape(1, ns, ATT_OUT_W))
    y_s = y_s.reshape(bs, pad, D_MODEL)[:, :ts]

    kv_p = [_kv_rows_t(tt)[None] for tt in tails_p]
    kv_s = [_kv_rows(a, ts)[None] for a in qkv_s]
    return (y_p, y_s, state_p[None], state_s[None], kv_p[0], kv_p[1], kv_p[2], kv_s[0], kv_s[1], kv_s[2])
```

```python
import functools
import math

import jax
import jax.numpy as jnp
from jax import lax
from jax.experimental import pallas as pl
from jax.experimental.pallas import tpu as pltpu

D_MODEL = 1024
PAST_LEN = 16384
RET_HEADS = 4
RET_DK = 128
RET_DV = 256
RET_CHUNK = 128
ROPE_BASE = 10000.0
ATT_GROUPS = ((128, 1), (512, 4), (2048, 16))
N_GROUPS = 3
ATT_HEADS = 8
ATT_DH = 64
REL_BUCKETS = 32
REL_MAX_DIST = 2048
EPS = 1e-6

RET_QK_W = RET_HEADS * RET_DK
RET_V_W = RET_HEADS * RET_DV
ATT_W = N_GROUPS * ATT_HEADS * ATT_DH
ATT_OUT_W = ATT_HEADS * ATT_DH
IN_W = 2 * RET_QK_W + 2 * RET_V_W + 3 * ATT_W + ATT_OUT_W + 2 * D_MODEL

LANES = 128
TILE = 512
RET_OUT_W = 2 * RET_QK_W + 2 * RET_V_W
QKV_W = 3 * ATT_OUT_W
GATE_W = 2 * D_MODEL + ATT_OUT_W
ATT_BLOCK = 128
N_KEYS = ATT_BLOCK + 1
SAMPLE_PAD = 8
NEG = -1e30
VMEM_LIMIT = 48 * 1024 * 1024

F32 = jnp.float32
BF16 = jnp.bfloat16


def _nt_dot(a, b):
    return lax.dot_general(a, b, (((1,), (1,)), ((), ())), preferred_element_type=F32)


def _dot(a, b):
    return jnp.dot(a, b, preferred_element_type=F32)


def _proj_schedule():
    sched = [(0, 0, 0, "rot_q"), (RET_QK_W, 0, RET_QK_W, "rot_k")]
    for k in range(2 * RET_V_W // TILE):
        sched.append((2 * RET_QK_W + k * TILE, 0, 2 * RET_QK_W + k * TILE, "plain"))
    att0 = 2 * RET_QK_W + 2 * RET_V_W
    for g in range(N_GROUPS):
        for kind, epi in enumerate(("norm_q", "norm_k", "plain")):
            sched.append((att0 + kind * ATT_W + g * ATT_OUT_W, 1 + g, kind * ATT_OUT_W, epi))
    gate0 = att0 + 3 * ATT_W
    for k in range(2 * D_MODEL // TILE):
        sched.append((gate0 + ATT_OUT_W + k * TILE, 4, k * TILE, "plain"))
    sched.append((gate0, 4, 2 * D_MODEL, "plain"))
    return sched


_SCHEDULE = _proj_schedule()


def _in_proj_kernel(*refs, dils, per_seq, fuse_retention, n_tails):
    (x_ref, wn_ref, w_ref, ca_ref, sa_ref, cb_ref, sb_ref, cbs_ref, sbs_ref,
     qg_ref, kg_ref, seg_ref) = refs[:12]
    refs = refs[12:]
    if fuse_retention:
        dec_ref, qd_ref, kd_ref, cd_ref, gn_ref, ret_ref, s_ref = refs[:7]
        refs = refs[7:]
    else:
        ret_ref = refs[0]
        refs = refs[1:]
    a0_ref, a1_ref, a2_ref, gate_ref = refs[:4]
    tail_refs = refs[4:4 + n_tails]
    h_ref, y_ref = refs[4 + n_tails:6 + n_tails]
    refs = refs[6 + n_tails:]
    r_ref = refs[0] if fuse_retention else ret_ref
    kt_ref = refs[1] if fuse_retention else None
    out_refs = (r_ref, a0_ref, a1_ref, a2_ref, gate_ref)

    if fuse_retention:
        @pl.when(pl.program_id(0) % per_seq == 0)
        def _():
            s_ref[...] = jnp.zeros_like(s_ref)

    x = x_ref[...]
    tm = x.shape[0]
    ms = jnp.mean(x * x, axis=-1, keepdims=True)
    h_ref[...] = (x * lax.rsqrt(ms + EPS) * wn_ref[...]).astype(BF16)

    ca, sa = ca_ref[0], sa_ref[0]
    cos = ca * cb_ref[...] - sa * sb_ref[...]
    sin = sa * cbs_ref[...] + ca * sbs_ref[...]

    def rotary(y, scale):
        parts = []
        for hh in range(TILE // RET_DK):
            yh = y[:, hh * RET_DK:(hh + 1) * RET_DK]
            parts.append((yh * cos + pltpu.roll(yh, RET_DK // 2, axis=1) * sin) * scale)
        return jnp.concatenate(parts, axis=1)

    def head_rms(y, gain):
        y2 = (y * y).astype(BF16)
        half = TILE // 2
        ms = jnp.concatenate([_dot(y2[:, :half], seg_ref[...]), _dot(y2[:, half:], seg_ref[...])], axis=1)
        return y * lax.rsqrt(ms + EPS) * gain

    epilogues = {
        "plain": lambda y: y,
        "rot_q": lambda y: rotary(y, 1.0),
        "rot_k": lambda y: rotary(y, RET_DK ** -0.5),
        "norm_q": lambda y: head_rms(y, qg_ref[...]),
        "norm_k": lambda y: head_rms(y, kg_ref[...]),
    }

    kt_base = jnp.minimum(pl.program_id(0), 0)

    def retention_issue(c, h):
        rows = slice(c * RET_CHUNK, (c + 1) * RET_CHUNK)
        ks = slice(h * RET_DK, (h + 1) * RET_DK)
        ks2 = slice(RET_QK_W + h * RET_DK, RET_QK_W + (h + 1) * RET_DK)
        vs_in = slice(2 * RET_QK_W + h * RET_DV, 2 * RET_QK_W + (h + 1) * RET_DV)
        qb = r_ref[rows, ks].astype(BF16)
        vb = r_ref[rows, vs_in].astype(BF16)
        state = s_ref[0, h]
        scores = (_nt_dot(qb, r_ref[rows, ks2].astype(BF16)) * dec_ref[h]).astype(BF16)
        carried = _dot(qb, state.astype(BF16)) * qd_ref[h]
        s_ref[0, h] = state * cd_ref[h] + _dot(kt_ref[kt_base + c, ks, :], vb)
        return rows, h, scores, vb, carried

    def retention_finish(rows, h, scores, vb, carried):
        vs = slice(h * RET_DV, (h + 1) * RET_DV)
        gs_in = slice(2 * RET_QK_W + RET_V_W + h * RET_DV, 2 * RET_QK_W + RET_V_W + (h + 1) * RET_DV)
        o = _dot(scores, vb) + carried
        ret_ref[rows, vs] = _group_norm_gate(o, gn_ref[:, vs], r_ref[rows, gs_in]).astype(ret_ref.dtype)

    def store_tile(val, out_idx, out_col):
        o_ref = out_refs[out_idx]
        ocs = slice(out_col, out_col + TILE)
        if not 1 <= out_idx <= N_GROUPS:
            o_ref[:, ocs] = val.astype(o_ref.dtype)
            return
        g, kind = out_idx - 1, out_col // ATT_OUT_W
        if kind > 0 and tail_refs:
            t_ref = tail_refs[g]
            t_ref[0, kind - 1] = jnp.transpose(val)[:, tm - t_ref.shape[3]:]
        dil = dils[g]
        if dil == 1:
            o_ref[0, 0, :, ocs] = val.astype(o_ref.dtype)
            return
        for c in range(TILE // LANES):
            y_ref[c] = val[:, c * LANES:(c + 1) * LANES]
            for r in range(dil):
                o_ref[0, r, :, out_col + c * LANES:out_col + (c + 1) * LANES] = (
                    y_ref[c, pl.ds(r, tm // dil, stride=dil), :].astype(o_ref.dtype))

    n_ret_tiles = sum(1 for t in _SCHEDULE if t[1] == 0)
    units = [(c, h) for c in range(tm // RET_CHUNK) for h in range(RET_HEADS)] if fuse_retention else []
    assert len(units) <= len(_SCHEDULE) - n_ret_tiles
    for step, (w_col, out_idx, out_col, epi) in enumerate(_SCHEDULE):
        if fuse_retention and step == 2:
            for c, h in units:
                rows = slice(c * RET_CHUNK, (c + 1) * RET_CHUNK)
                kt_ref[c, h * RET_DK:(h + 1) * RET_DK, :] = _decayed_keys_t(
                    r_ref[rows, RET_QK_W + h * RET_DK:RET_QK_W + (h + 1) * RET_DK], kd_ref[h])
        unit = units[step - n_ret_tiles] if 0 <= step - n_ret_tiles < len(units) else None
        issued = retention_issue(*unit) if unit else None
        store_tile(epilogues[epi](_dot(h_ref[...], w_ref[:, w_col:w_col + TILE])), out_idx, out_col)
        if unit:
            retention_finish(*issued)


def _in_proj(x2d, seq_len, dils, tail_rows, out_dtype, wn, w_bf, rot, qg, kg, seg, tm, retention=None):
    n = x2d.shape[0]
    batch = n // seq_len
    per_seq = seq_len // tm
    fuse = retention is not None
    const = lambda a: pl.BlockSpec(a.shape, lambda i, nd=a.ndim: (0,) * nd)
    base_spec = pl.BlockSpec((1, 1, RET_DK), lambda i: (i % per_seq, 0, 0))
    qkv_spec = lambda d: pl.BlockSpec((1, d, tm // d, QKV_W), lambda i: (i // per_seq, 0, i % per_seq, 0))
    tail_specs, tail_shapes = [], []
    for rows in tail_rows:
        width = min(tm, rows)
        first = (seq_len - rows) // tm if rows >= tm else per_seq
        tail_specs.append(pl.BlockSpec(
            (1, 2, ATT_OUT_W, width),
            lambda i, first=first: (i // per_seq, 0, 0, jnp.maximum(i % per_seq - first, 0))))
        tail_shapes.append(jax.ShapeDtypeStruct((batch, 2, ATT_OUT_W, rows), F32))
    ca, sa, *offset_tables = rot
    args = [x2d, wn, w_bf, ca, sa, *offset_tables, qg, kg, seg]
    in_specs = [
        pl.BlockSpec((tm, D_MODEL), lambda i: (i, 0)),
        const(wn),
        pl.BlockSpec(w_bf.shape, lambda i: (0, 0), pipeline_mode=pl.Buffered(1)),
        base_spec, base_spec, *[const(t) for t in offset_tables],
        const(qg), const(kg), const(seg),
    ]
    scratch = [pltpu.VMEM((tm, D_MODEL), BF16), pltpu.VMEM((TILE // LANES, tm, LANES), F32)]
    if fuse:
        tables, gn = retention
        args += [*tables, gn]
        in_specs += [const(t) for t in (*tables, gn)]
        ret_specs = [pl.BlockSpec((tm, RET_V_W), lambda i: (i, 0)),
                     pl.BlockSpec((1, RET_HEADS, RET_DK, RET_DV), lambda i: (i // per_seq, 0, 0, 0))]
        ret_shapes = [jax.ShapeDtypeStruct((n, RET_V_W), out_dtype),
                      jax.ShapeDtypeStruct((batch, RET_HEADS, RET_DK, RET_DV), F32)]
        scratch += [pltpu.VMEM((tm, RET_OUT_W), F32), pltpu.VMEM((tm // RET_CHUNK, RET_QK_W, RET_CHUNK), BF16)]
    else:
        ret_specs = [pl.BlockSpec((tm, RET_OUT_W), lambda i: (i, 0))]
        ret_shapes = [jax.ShapeDtypeStruct((n, RET_OUT_W), out_dtype)]
    return pl.pallas_call(
        functools.partial(_in_proj_kernel, dils=dils, per_seq=per_seq, fuse_retention=fuse,
                          n_tails=len(tail_rows)),
        grid=(n // tm,),
        in_specs=in_specs,
        out_specs=ret_specs + [qkv_spec(d) for d in dils]
        + [pl.BlockSpec((tm, GATE_W), lambda i: (i, 0))] + tail_specs,
        out_shape=ret_shapes
        + [jax.ShapeDtypeStruct((batch, d, seq_len // d, QKV_W), out_dtype) for d in dils]
        + [jax.ShapeDtypeStruct((n, GATE_W), out_dtype)] + tail_shapes,
        scratch_shapes=scratch,
        compiler_params=pltpu.CompilerParams(
            dimension_semantics=("arbitrary",), vmem_limit_bytes=VMEM_LIMIT),
        name="in_proj",
    )(*args)


def _group_norm_gate(o, gain, gate):
    gate = gate.astype(F32)
    mu = jnp.mean(o, axis=-1, keepdims=True)
    d = o - mu
    var = jnp.mean(d * d, axis=-1, keepdims=True)
    return gate * jax.nn.sigmoid(gate) * (d * lax.rsqrt(var + EPS) * gain)


def _decayed_keys_t(k_keys, kd):
    return jnp.transpose(k_keys * kd).astype(BF16)


def _retention_specs(rows, idx):
    return [
        pl.BlockSpec((1, rows, RET_QK_W), lambda *a: (*idx(*a), 0)),
        pl.BlockSpec((1, rows, RET_QK_W), lambda *a: (*idx(*a), 1)),
        pl.BlockSpec((1, rows, RET_V_W), lambda *a: (*idx(*a), 1)),
        pl.BlockSpec((1, rows, RET_V_W), lambda *a: (*idx(*a), 2)),
    ]


def _table_specs(tables, ndim_grid):
    return [pl.BlockSpec(t.shape, lambda *a, nd=t.ndim: (0,) * nd) for t in tables]


def _retention_sample_kernel(q_ref, k_ref, v_ref, g_ref, s_in_ref, dec_ref, qd_ref, kd_ref, cd_ref,
                             gn_ref, o_ref, s_out_ref, kpad_ref, vpad_ref):
    @pl.when(pl.program_id(0) == 0)
    def _():
        kpad_ref[...] = jnp.zeros_like(kpad_ref)
        vpad_ref[...] = jnp.zeros_like(vpad_ref)

    kpad_ref[0:SAMPLE_PAD, :] = k_ref[0]
    vpad_ref[0:SAMPLE_PAD, :] = v_ref[0]
    issued = []
    for h in range(RET_HEADS):
        ks = slice(h * RET_DK, (h + 1) * RET_DK)
        vs = slice(h * RET_DV, (h + 1) * RET_DV)
        qb = q_ref[0, :, ks].astype(BF16)
        vb = vpad_ref[:, vs].astype(BF16)
        state = s_in_ref[0, h]
        scores = (_nt_dot(qb, kpad_ref[:, ks].astype(BF16)) * dec_ref[h]).astype(BF16)
        carried = _dot(qb, state.astype(BF16)) * qd_ref[h]
        s_out_ref[0, h] = state * cd_ref[h] + _dot(_decayed_keys_t(kpad_ref[:, ks], kd_ref[h]), vb)
        issued.append((scores, vb, carried))
    for h, (scores, vb, carried) in enumerate(issued):
        vs = slice(h * RET_DV, (h + 1) * RET_DV)
        o = _dot(scores, vb) + carried
        o_ref[0, :, vs] = _group_norm_gate(o, gn_ref[:, vs], g_ref[0, :, vs]).astype(o_ref.dtype)


def _attn_prompt_blocks(q_ref, kp_ref, kc_ref, vp_ref, vc_ref, bias_ref, o_ref, lse_ref, first):
    n_blocks = q_ref.shape[2] // ATT_BLOCK
    pair_w = 2 * ATT_DH
    low_q = lax.broadcasted_iota(jnp.int32, (ATT_BLOCK, pair_w), 1) < ATT_DH
    rows0, rows1 = slice(0, ATT_BLOCK), slice(ATT_BLOCK, 2 * ATT_BLOCK)

    def keys(prev_ref, cur_ref, t, ps):
        if t > 0:
            return cur_ref[0, 0, (t - 1) * ATT_BLOCK:(t + 1) * ATT_BLOCK, ps]
        if first:
            return cur_ref[0, 0, 0:ATT_BLOCK, ps]
        return jnp.concatenate([prev_ref[0, 0, :, ps], cur_ref[0, 0, 0:ATT_BLOCK, ps]], axis=0)

    staged = []
    for t in range(n_blocks):
        for p in range(ATT_HEADS // 2):
            ps = slice(p * pair_w, (p + 1) * pair_w)
            qp = q_ref[0, 0, t * ATT_BLOCK:(t + 1) * ATT_BLOCK, ps]
            kcat = keys(kp_ref, kc_ref, t, ps)
            q2 = jnp.concatenate([jnp.where(low_q, qp, 0.0), jnp.where(low_q, 0.0, qp)], axis=0)
            s = _nt_dot(q2, kcat) + bias_ref[p, :, 2 * ATT_BLOCK - kcat.shape[0]:]
            s = s.astype(BF16)
            m = jnp.max(s, axis=-1, keepdims=True)
            staged.append((t, p, m.astype(F32), jnp.exp(s - m)))
    for t, p, m, e in staged:
        ps = slice(p * pair_w, (p + 1) * pair_w)
        vcat = keys(vp_ref, vc_ref, t, ps)
        low_k = lax.broadcasted_iota(jnp.int32, vcat.shape, 1) < ATT_DH
        p0 = _dot(e[rows0], jnp.where(low_k, vcat, 1.0))
        p1 = _dot(e[rows1], jnp.where(low_k, 1.0, vcat))
        den = pltpu.roll(jnp.where(low_q, p1, p0), ATT_DH, axis=1)
        ts = slice(t * ATT_BLOCK, (t + 1) * ATT_BLOCK)
        o_ref[0, 0, ts, ps] = jnp.where(low_q, p0, p1) / den
        lse_ref[0, 0, ts, ps] = jnp.where(low_q, m[rows0], m[rows1]) + jnp.log(den)


def _attn_prompt_kernel(q_ref, kp_ref, kc_ref, vp_ref, vc_ref, slot_ref, o_ref, lse_ref, bias_ref):
    first = pl.program_id(2) == 0

    @pl.when(jnp.logical_and(first, jnp.logical_and(pl.program_id(0) == 0, pl.program_id(1) == 0)))
    def _():
        for h in range(ATT_HEADS):
            slots = jnp.broadcast_to(slot_ref[h:h + 1, :], (ATT_BLOCK, slot_ref.shape[1]))
            rows = pltpu.roll(slots, 0, 1, stride=1, stride_axis=0)
            bias_ref[h // 2, (h % 2) * ATT_BLOCK:(h % 2 + 1) * ATT_BLOCK, :] = rows[:, :2 * ATT_BLOCK]

    @pl.when(first)
    def _():
        _attn_prompt_blocks(q_ref, kp_ref, kc_ref, vp_ref, vc_ref, bias_ref, o_ref, lse_ref, True)

    @pl.when(jnp.logical_not(first))
    def _():
        _attn_prompt_blocks(q_ref, kp_ref, kc_ref, vp_ref, vc_ref, bias_ref, o_ref, lse_ref, False)


def _attn_prompt_group(qkv, slot_bias, n_blocks):
    b, dil, tr, _ = qkv.shape
    rows = n_blocks * ATT_BLOCK
    cur = lambda c: pl.BlockSpec((1, 1, rows, ATT_OUT_W), lambda bi, r, j: (bi, r, j, c))
    prev = lambda c: pl.BlockSpec((1, 1, ATT_BLOCK, ATT_OUT_W),
                                  lambda bi, r, j: (bi, r, jnp.maximum(j * n_blocks - 1, 0), c))
    res_shape = jax.ShapeDtypeStruct((b, dil, tr, ATT_OUT_W), F32)
    return pl.pallas_call(
        _attn_prompt_kernel,
        grid=(b, dil, tr // rows),
        in_specs=[cur(0), prev(1), cur(1), prev(2), cur(2),
                  pl.BlockSpec(slot_bias.shape, lambda bi, r, j: (0, 0))],
        out_specs=[cur(0), cur(0)],
        out_shape=[res_shape, res_shape],
        scratch_shapes=[pltpu.VMEM((ATT_HEADS // 2, 2 * ATT_BLOCK, 2 * ATT_BLOCK), F32)],
        compiler_params=pltpu.CompilerParams(
            dimension_semantics=("arbitrary", "arbitrary", "arbitrary"), vmem_limit_bytes=VMEM_LIMIT),
        name=f"attn_prompt_d{dil}",
    )(qkv, qkv, qkv, qkv, qkv, slot_bias)


def _attn_sample_kernel(a0_ref, a1_ref, a2_ref, c0_ref, c1_ref, c2_ref, b0_ref, b1_ref, b2_ref, bn_ref,
                        o_ref, kn_ref, vn_ref):
    qkv_refs = (a0_ref, a1_ref, a2_ref)
    cache_refs = (c0_ref, c1_ref, c2_ref)
    bias_refs = (b0_ref, b1_ref, b2_ref)

    @pl.when(pl.program_id(0) == 0)
    def _():
        kn_ref[...] = jnp.zeros_like(kn_ref)
        vn_ref[...] = jnp.zeros_like(vn_ref)

    for g in range(N_GROUPS):
        gs = slice(g * ATT_OUT_W, (g + 1) * ATT_OUT_W)
        kn_ref[0:SAMPLE_PAD, gs] = qkv_refs[g][0, :, ATT_OUT_W:2 * ATT_OUT_W]
        vn_ref[0:SAMPLE_PAD, gs] = qkv_refs[g][0, :, 2 * ATT_OUT_W:3 * ATT_OUT_W]

    staged = []
    for h in range(ATT_HEADS):
        logits = []
        for g in range(N_GROUPS):
            hs = slice(g * ATT_OUT_W + h * ATT_DH, g * ATT_OUT_W + (h + 1) * ATT_DH)
            qh = qkv_refs[g][0, :, h * ATT_DH:(h + 1) * ATT_DH].astype(BF16)
            logits.append(_dot(qh, cache_refs[g][0, 0, h].astype(BF16)) + bias_refs[g][h])
            logits.append(_nt_dot(qh, kn_ref[:, hs].astype(BF16)) + bn_ref[g, h])
        m = functools.reduce(jnp.maximum, [jnp.max(x, axis=-1, keepdims=True) for x in logits])
        es = [jnp.exp(x - m) for x in logits]
        l = functools.reduce(jnp.add, [jnp.sum(e, axis=-1, keepdims=True) for e in es])
        staged.append(([e.astype(BF16) for e in es], l))
    for h, (es, l) in enumerate(staged):
        acc = jnp.zeros((SAMPLE_PAD, ATT_DH), F32)
        for g in range(N_GROUPS):
            hs = slice(g * ATT_OUT_W + h * ATT_DH, g * ATT_OUT_W + (h + 1) * ATT_DH)
            acc = acc + _nt_dot(es[2 * g], cache_refs[g][0, 1, h].astype(BF16))
            acc = acc + _dot(es[2 * g + 1], vn_ref[:, hs].astype(BF16))
        o_ref[0, :, h * ATT_DH:(h + 1) * ATT_DH] = acc / l


_N_RET_IN, _N_ATT_IN = 10, 10


def _sample_mixers_kernel(*refs):
    n_in = _N_RET_IN + _N_ATT_IN
    ret_in, att_in = refs[:_N_RET_IN], refs[_N_RET_IN:n_in]
    ret_o_ref, state_o_ref, att_o_ref = refs[n_in:n_in + 3]
    kpad_ref, vpad_ref, kn_ref, vn_ref = refs[n_in + 3:]
    _attn_sample_kernel(*att_in, att_o_ref, kn_ref, vn_ref)
    _retention_sample_kernel(*ret_in, ret_o_ref, state_o_ref, kpad_ref, vpad_ref)


def _sample_mixers(ret, state, tables, gn, qkvs, caches_t, biases, bias_new):
    b, p, _ = ret.shape
    state_spec = pl.BlockSpec((1, RET_HEADS, RET_DK, RET_DV), lambda bi: (bi, 0, 0, 0))
    ret_specs = _retention_specs(p, lambda bi: (bi, 0)) + [state_spec] + _table_specs(tables + (gn,), 1)
    att_specs = ([pl.BlockSpec((1, p, QKV_W), lambda bi: (bi, 0, 0)) for _ in qkvs]
                 + [pl.BlockSpec((1,) + c.shape[1:], lambda bi: (bi, 0, 0, 0, 0)) for c in caches_t]
                 + [pl.BlockSpec(x.shape, lambda bi: (0, 0, 0)) for x in biases]
                 + [pl.BlockSpec(bias_new.shape, lambda bi: (0, 0, 0, 0))])
    assert len(ret_specs) == _N_RET_IN and len(att_specs) == _N_ATT_IN
    return pl.pallas_call(
        _sample_mixers_kernel,
        grid=(b,),
        in_specs=ret_specs + att_specs,
        out_specs=[pl.BlockSpec((1, p, RET_V_W), lambda bi: (bi, 0, 0)), state_spec,
                   pl.BlockSpec((1, p, ATT_OUT_W), lambda bi: (bi, 0, 0))],
        out_shape=[jax.ShapeDtypeStruct((b, p, RET_V_W), ret.dtype), jax.ShapeDtypeStruct(state.shape, F32),
                   jax.ShapeDtypeStruct((b, p, ATT_OUT_W), F32)],
        scratch_shapes=[pltpu.VMEM((RET_CHUNK, RET_QK_W), F32), pltpu.VMEM((RET_CHUNK, RET_V_W), F32),
                        pltpu.VMEM((ATT_BLOCK, ATT_W), F32), pltpu.VMEM((ATT_BLOCK, ATT_W), F32)],
        compiler_params=pltpu.CompilerParams(
            dimension_semantics=("arbitrary",), vmem_limit_bytes=VMEM_LIMIT),
        name="sample_mixers",
    )(ret, ret, ret, ret, state, *tables, gn, *qkvs, *caches_t, *biases, bias_new)


def _out_proj_kernel(*refs, dils):
    x_ref, ret_ref, ga_ref, gb_ref, ag_ref, wr_ref, wa_ref, wo_ref = refs[:8]
    if dils is None:
        att_ref, o_ref = refs[8:]
        att = att_ref[0]
    else:
        group_refs = refs[8:8 + 2 * N_GROUPS]
        o_ref = refs[8 + 2 * N_GROUPS]
        scratch = refs[9 + 2 * N_GROUPS:]
        tm = x_ref.shape[1]
        os, lses = [], []
        for g, dil in enumerate(dils):
            og_ref, lg_ref = group_refs[2 * g], group_refs[2 * g + 1]
            if dil == 1:
                os.append(og_ref[0, 0]); lses.append(lg_ref[0, 0])
                continue
            so_ref, sl_ref = scratch[2 * (g - 1)], scratch[2 * (g - 1) + 1]
            n_chunks = ATT_OUT_W // LANES
            for c in range(n_chunks):
                cs = slice(c * LANES, (c + 1) * LANES)
                for r in range(dil):
                    so_ref[c, pl.ds(r, tm // dil, stride=dil), :] = og_ref[0, r, :, cs]
                    sl_ref[c, pl.ds(r, tm // dil, stride=dil), :] = lg_ref[0, r, :, cs]
            os.append(jnp.concatenate([so_ref[c] for c in range(n_chunks)], axis=1))
            lses.append(jnp.concatenate([sl_ref[c] for c in range(n_chunks)], axis=1))
        mx = functools.reduce(jnp.maximum, lses)
        ws = [jnp.exp(l - mx) for l in lses]
        att = functools.reduce(jnp.add, [w * o for w, o in zip(ws, os)]) / functools.reduce(jnp.add, ws)
    ag = ag_ref[0].astype(F32)
    u = (ag * jax.nn.sigmoid(ag) * att).astype(BF16)
    o_b = _dot(u, wa_ref[...])
    o_a = _dot(ret_ref[0].astype(BF16), wr_ref[...])
    merged = jax.nn.sigmoid(ga_ref[0].astype(F32)) * o_a + jax.nn.sigmoid(gb_ref[0].astype(F32)) * o_b
    o_ref[0] = x_ref[0] + _dot(merged.astype(BF16), wo_ref[...])


def _out_proj(x, ret, gates, wr, wa, wo, tm, att=None, groups=None):
    b, t, _ = x.shape
    row = lambda w, c: pl.BlockSpec((1, tm, w), lambda bi, i: (bi, i, c))
    full = lambda a: pl.BlockSpec(a.shape, lambda bi, i: (0, 0))
    in_specs = [row(D_MODEL, 0), row(RET_V_W, 0), row(D_MODEL, 0), row(D_MODEL, 1),
                row(ATT_OUT_W, 2 * D_MODEL // ATT_OUT_W), full(wr), full(wa), full(wo)]
    args = [x, ret, gates, gates, gates, wr, wa, wo]
    scratch = []
    if groups is None:
        dils = None
        in_specs.append(row(ATT_OUT_W, 0))
        args.append(att)
    else:
        dils = tuple(o.shape[1] for o, _ in groups)
        for (o, lse), d in zip(groups, dils):
            spec = pl.BlockSpec((1, d, tm // d, ATT_OUT_W), lambda bi, i: (bi, 0, i, 0))
            in_specs += [spec, spec]
            args += [o, lse]
            if d > 1:
                scratch += [pltpu.VMEM((ATT_OUT_W // LANES, tm, LANES), F32)] * 2
    return pl.pallas_call(
        functools.partial(_out_proj_kernel, dils=dils),
        grid=(b, t // tm),
        in_specs=in_specs,
        out_specs=row(D_MODEL, 0),
        out_shape=jax.ShapeDtypeStruct((b, t, D_MODEL), F32),
        scratch_shapes=scratch,
        compiler_params=pltpu.CompilerParams(
            dimension_semantics=("parallel", "parallel"), vmem_limit_bytes=VMEM_LIMIT),
        name="out_proj",
    )(*args)


def _rotary_tables(bases, offsets):
    half = RET_DK // 2
    inv = ROPE_BASE ** (-jnp.arange(half, dtype=F32) / half)
    inv2 = jnp.concatenate([inv, inv])
    sign = jnp.concatenate([-jnp.ones((half,), F32), jnp.ones((half,), F32)])
    a = bases.astype(F32)[:, None, None] * inv2
    b = offsets.astype(F32)[:, None] * inv2
    cb, sb = jnp.cos(b), jnp.sin(b)
    return jnp.cos(a), jnp.sin(a), cb, sb, sign * cb, sign * sb


def _retention_tables(c, rows):
    log_g = jnp.log1p(-(2.0 ** (-5.0 - jnp.arange(RET_HEADS, dtype=F32))))
    i = jnp.arange(c, dtype=F32)
    diff = i[:, None] - i[None, :]
    decay = jnp.where(diff[None] >= 0, jnp.exp(jnp.maximum(diff, 0.0)[None] * log_g[:, None, None]), 0.0)
    q_decay = jnp.exp((i + 1.0)[None, :] * log_g[:, None])
    k_decay = jnp.exp((c - 1.0 - i)[None, :] * log_g[:, None])
    chunk_decay = jnp.exp(c * log_g)
    dec = jnp.zeros((RET_HEADS, rows, RET_CHUNK), F32).at[:, :c, :c].set(decay)
    qd = jnp.zeros((RET_HEADS, rows, 1), F32).at[:, :c, 0].set(q_decay)
    kd = jnp.zeros((RET_HEADS, RET_CHUNK, 1), F32).at[:, :c, 0].set(k_decay)
    cd = jnp.broadcast_to(chunk_decay[:, None, None], (RET_HEADS, 1, RET_DV))
    return dec, qd, kd, cd


def _t5_bucket(dist):
    max_exact = REL_BUCKETS // 2
    d = jnp.maximum(dist.astype(F32), 1.0)
    large = max_exact + (jnp.log(d / max_exact) / math.log(REL_MAX_DIST / max_exact)
                         * (REL_BUCKETS - max_exact)).astype(jnp.int32)
    large = jnp.minimum(large, REL_BUCKETS - 1)
    return jnp.where(dist < max_exact, dist, large)


def _group_bias(rel_bias, g, dil, slots):
    dist = dil * jnp.asarray(slots, dtype=jnp.int32)
    return rel_bias[_t5_bucket(dist)][:, g * ATT_HEADS:(g + 1) * ATT_HEADS].astype(F32).T


def _toeplitz(f, n_rows, n_cols):
    heads, period = f.shape
    assert period == n_rows + n_cols - 1
    g = jnp.roll(f, -(n_rows - 1), axis=1)
    flat = jnp.tile(g, (1, n_rows))[:, :n_rows * (period - 1)]
    return flat.reshape(heads, n_rows, period - 1)[:, :, :n_cols]


def _neg(heads, n):
    return jnp.full((heads, n), NEG, F32)


def _dilate(v, dil):
    heads, n = v.shape
    return jnp.stack([v] + [_neg(heads, n)] * (dil - 1), axis=-1).reshape(heads, n * dil)


def _prompt_slots(tb_rev):
    heads = tb_rev.shape[0]
    return jnp.concatenate([tb_rev, _neg(heads, 4 * ATT_BLOCK - N_KEYS)], axis=1)


def _sample_bias(tb, tb_rev, win, dil):
    heads = tb.shape[0]
    p = SAMPLE_PAD
    f_c = jnp.concatenate([_neg(heads, p - 1), _dilate(tb_rev[:, :N_KEYS - 1], dil)], axis=1)
    bias_c = _toeplitz(f_c, p, win)
    cols = [tb[:, k // dil:k // dil + 1] if k % dil == 0 else _neg(heads, 1) for k in range(p - 1, -1, -1)]
    f_n = jnp.concatenate(cols + [_neg(heads, p - 1)], axis=1)
    bias_n = _toeplitz(f_n, p, p)
    return bias_c, jnp.pad(bias_n, ((0, 0), (0, 0), (0, ATT_BLOCK - p)), constant_values=NEG)


def _kv_rows_t(tail_t):
    b, _, _, rows = tail_t.shape
    return jnp.transpose(tail_t.reshape(b, 2, ATT_HEADS, ATT_DH, rows), (0, 4, 1, 2, 3))


def _kv_rows(qkv, n_rows):
    b = qkv.shape[0]
    return qkv[:, :n_rows, ATT_OUT_W:].reshape(b, n_rows, 2, ATT_HEADS, ATT_DH)


def kernel(x_prompt, x_sample, cache_kv_w128, cache_kv_w512, cache_kv_w2048, state_retention,
           w_norm, w_in, q_norm, k_norm, rel_bias, ret_norm, w_proj_ret, w_proj_att, w_out):
    assert w_in.shape[0] == 1
    bp, t, _ = x_prompt.shape
    bs, ts, _ = x_sample.shape
    dils = tuple(d for _, d in ATT_GROUPS)
    assert t % (ATT_BLOCK * max(dils)) == 0 and ts <= SAMPLE_PAD
    caches = (cache_kv_w128[0], cache_kv_w512[0], cache_kv_w2048[0])
    for cch, (win, _) in zip(caches, ATT_GROUPS):
        assert cch.shape[1] == win and win <= PAST_LEN

    wn = w_norm[0].reshape(1, D_MODEL)
    w_in_bf = w_in[0].astype(BF16)
    qg = jnp.tile(q_norm[0] * (ATT_DH ** -0.5), TILE // ATT_DH).reshape(1, TILE)
    kg = jnp.tile(k_norm[0], TILE // ATT_DH).reshape(1, TILE)
    gn = ret_norm[0].reshape(1, RET_V_W)
    wr = w_proj_ret[0].astype(BF16)
    wa = w_proj_att[0].astype(BF16)
    wo = w_out[0].astype(BF16)
    hid = jnp.arange(TILE // 2) // ATT_DH
    seg_mean = jnp.where(hid[:, None] == hid[None, :], 1.0 / ATT_DH, 0.0).astype(BF16)
    asc, desc = tuple(range(N_KEYS)), tuple(range(N_KEYS - 1, -1, -1))
    group_bias = [(_group_bias(rel_bias, g, d, asc), _group_bias(rel_bias, g, d, desc)) for g, d in enumerate(dils)]

    tm_p = 256
    rot_p = _rotary_tables(jnp.arange(0, t, tm_p), jnp.arange(tm_p))
    tail_p_rows = tuple(min(w, t) for w, _ in ATT_GROUPS)
    ret_p, state_p, *rest = _in_proj(
        x_prompt.reshape(bp * t, D_MODEL), t, dils, tail_p_rows, BF16, wn, w_in_bf, rot_p, qg, kg, seg_mean,
        tm=tm_p, retention=(_retention_tables(RET_CHUNK, RET_CHUNK), gn))
    qkv_p, gates_p, tails_p = rest[:N_GROUPS], rest[N_GROUPS], rest[N_GROUPS + 1:]
    att_blocks = tuple(min(4, t // (d * ATT_BLOCK)) for d in dils)
    groups = [_attn_prompt_group(qkv_p[g], _prompt_slots(group_bias[g][1]), n_blocks=att_blocks[g])
              for g in range(N_GROUPS)]
    y_p = _out_proj(x_prompt, ret_p.reshape(bp, t, RET_V_W), gates_p.reshape(bp, t, GATE_W), wr, wa, wo,
                    tm=512, groups=groups)

    pad = SAMPLE_PAD
    ns = bs * pad
    xs = jnp.pad(x_sample, ((0, 0), (0, pad - ts), (0, 0))).reshape(ns, D_MODEL)
    rot_s = _rotary_tables(jnp.full((1,), PAST_LEN), jnp.tile(jnp.arange(pad), bs))
    ret_in_s, *qkv_s, gates_s = _in_proj(xs, ns, (1,) * N_GROUPS, (), F32, wn, w_in_bf, rot_s,
                                                 qg, kg, seg_mean, tm=ns)
    qkv_s = [a.reshape(bs, pad, QKV_W) for a in qkv_s]
    caches_t = [jnp.transpose(c, (0, 2, 3, 4, 1)) for c in caches]
    sb = [_sample_bias(*group_bias[g], win, d) for g, (win, d) in enumerate(ATT_GROUPS)]
    ret_s, state_s, att_s = _sample_mixers(
        ret_in_s.reshape(bs, pad, RET_OUT_W), state_retention[0], _retention_tables(ts, pad), gn,
        qkv_s, caches_t, [c for c, _ in sb], jnp.stack([n for _, n in sb]))
    y_s = _out_proj(xs.reshape(1, ns, D_MODEL), ret_s.reshape(1, ns, RET_V_W), gates_s.reshape(1, ns, GATE_W),
                    wr, wa, wo, tm=ns, att=att_s.reshape(1, ns, ATT_OUT_W))
    y_s = y_s.reshape(bs, pad, D_MODEL)[:, :ts]

    kv_p = [_kv_rows_t(tt)[None] for tt in tails_p]
    kv_s = [_kv_rows(a, ts)[None] for a in qkv_s]
    return (y_p, y_s, state_p[None], state_s[None], kv_p[0], kv_p[1], kv_p[2], kv_s[0], kv_s[1], kv_s[2])
```

```python
import functools
import math

import jax
import jax.numpy as jnp
from jax import lax
from jax.experimental import pallas as pl
from jax.experimental.pallas import tpu as pltpu

D_MODEL = 1024
PAST_LEN = 16384
RET_HEADS = 4
RET_DK = 128
RET_DV = 256
RET_CHUNK = 128
ROPE_BASE = 10000.0
ATT_GROUPS = ((128, 1), (512, 4), (2048, 16))
N_GROUPS = 3
ATT_HEADS = 8
ATT_DH = 64
REL_BUCKETS = 32
REL_MAX_DIST = 2048
EPS = 1e-6

RET_QK_W = RET_HEADS * RET_DK
RET_V_W = RET_HEADS * RET_DV
ATT_W = N_GROUPS * ATT_HEADS * ATT_DH
ATT_OUT_W = ATT_HEADS * ATT_DH
IN_W = 2 * RET_QK_W + 2 * RET_V_W + 3 * ATT_W + ATT_OUT_W + 2 * D_MODEL

LANES = 128
TILE = 512
RET_OUT_W = 2 * RET_QK_W + 2 * RET_V_W
QKV_W = 3 * ATT_OUT_W
GATE_W = 2 * D_MODEL + ATT_OUT_W
ATT_BLOCK = 128
N_KEYS = ATT_BLOCK + 1
SAMPLE_PAD = 8
NEG = -1e30
VMEM_LIMIT = 48 * 1024 * 1024

F32 = jnp.float32
BF16 = jnp.bfloat16


def _nt_dot(a, b):
    return lax.dot_general(a, b, (((1,), (1,)), ((), ())), preferred_element_type=F32)


def _dot(a, b):
    return jnp.dot(a, b, preferred_element_type=F32)


def _proj_schedule():
    sched = [(0, 0, 0, "rot_q"), (RET_QK_W, 0, RET_QK_W, "rot_k")]
    for k in range(2 * RET_V_W // TILE):
        sched.append((2 * RET_QK_W + k * TILE, 0, 2 * RET_QK_W + k * TILE, "plain"))
    att0 = 2 * RET_QK_W + 2 * RET_V_W
    for g in range(N_GROUPS):
        for kind, epi in enumerate(("norm_q", "norm_k", "plain")):
            sched.append((att0 + kind * ATT_W + g * ATT_OUT_W, 1 + g, kind * ATT_OUT_W, epi))
    gate0 = att0 + 3 * ATT_W
    for k in range(2 * D_MODEL // TILE):
        sched.append((gate0 + ATT_OUT_W + k * TILE, 4, k * TILE, "plain"))
    sched.append((gate0, 4, 2 * D_MODEL, "plain"))
    return sched


_SCHEDULE = _proj_schedule()


def _in_proj_kernel(*refs, dils, per_seq, fuse_retention, n_tails):
    (x_ref, wn_ref, w_ref, ca_ref, sa_ref, cb_ref, sb_ref, cbs_ref, sbs_ref,
     qg_ref, kg_ref, seg_ref) = refs[:12]
    refs = refs[12:]
    if fuse_retention:
        dec_ref, qd_ref, kd_ref, cd_ref, gn_ref, ret_ref, s_ref = refs[:7]
        refs = refs[7:]
    else:
        ret_ref = refs[0]
        refs = refs[1:]
    a0_ref, a1_ref, a2_ref, gate_ref = refs[:4]
    tail_refs = refs[4:4 + n_tails]
    h_ref, y_ref = refs[4 + n_tails:6 + n_tails]
    refs = refs[6 + n_tails:]
    r_ref = refs[0] if fuse_retention else ret_ref
    kt_ref = refs[1] if fuse_retention else None
    out_refs = (r_ref, a0_ref, a1_ref, a2_ref, gate_ref)

    if fuse_retention:
        @pl.when(pl.program_id(0) % per_seq == 0)
        def _():
            s_ref[...] = jnp.zeros_like(s_ref)

    x = x_ref[...]
    tm = x.shape[0]
    ms = jnp.mean(x * x, axis=-1, keepdims=True)
    h_ref[...] = (x * lax.rsqrt(ms + EPS) * wn_ref[...]).astype(BF16)

    ca, sa = ca_ref[0], sa_ref[0]
    cos = ca * cb_ref[...] - sa * sb_ref[...]
    sin = sa * cbs_ref[...] + ca * sbs_ref[...]

    def rotary(y, scale):
        parts = []
        for hh in range(TILE // RET_DK):
            yh = y[:, hh * RET_DK:(hh + 1) * RET_DK]
            parts.append((yh * cos + pltpu.roll(yh, RET_DK // 2, axis=1) * sin) * scale)
        return jnp.concatenate(parts, axis=1)

    def head_rms(y, gain):
        y2 = (y * y).astype(BF16)
        half = TILE // 2
        ms = jnp.concatenate([_dot(y2[:, :half], seg_ref[...]), _dot(y2[:, half:], seg_ref[...])], axis=1)
        return y * lax.rsqrt(ms + EPS) * gain

    epilogues = {
        "plain": lambda y: y,
        "rot_q": lambda y: rotary(y, 1.0),
        "rot_k": lambda y: rotary(y, RET_DK ** -0.5),
        "norm_q": lambda y: head_rms(y, qg_ref[...]),
        "norm_k": lambda y: head_rms(y, kg_ref[...]),
    }

    kt_base = jnp.minimum(pl.program_id(0), 0)

    def retention_issue(c, h):
        rows = slice(c * RET_CHUNK, (c + 1) * RET_CHUNK)
        ks = slice(h * RET_DK, (h + 1) * RET_DK)
        ks2 = slice(RET_QK_W + h * RET_DK, RET_QK_W + (h + 1) * RET_DK)
        vs_in = slice(2 * RET_QK_W + h * RET_DV, 2 * RET_QK_W + (h + 1) * RET_DV)
        qb = r_ref[rows, ks].astype(BF16)
        vb = r_ref[rows, vs_in].astype(BF16)
        state = s_ref[0, h]
        scores = (_nt_dot(qb, r_ref[rows, ks2].astype(BF16)) * dec_ref[h]).astype(BF16)
        carried = _dot(qb, state.astype(BF16)) * qd_ref[h]
        s_ref[0, h] = state * cd_ref[h] + _dot(kt_ref[kt_base + c, ks, :], vb)
        return rows, h, scores, vb, carried

    def retention_finish(rows, h, scores, vb, carried):
        vs = slice(h * RET_DV, (h + 1) * RET_DV)
        gs_in = slice(2 * RET_QK_W + RET_V_W + h * RET_DV, 2 * RET_QK_W + RET_V_W + (h + 1) * RET_DV)
        o = _dot(scores, vb) + carried
        ret_ref[rows, vs] = _group_norm_gate(o, gn_ref[:, vs], r_ref[rows, gs_in]).astype(ret_ref.dtype)

    def store_tile(val, out_idx, out_col):
        o_ref = out_refs[out_idx]
        ocs = slice(out_col, out_col + TILE)
        if not 1 <= out_idx <= N_GROUPS:
            o_ref[:, ocs] = val.astype(o_ref.dtype)
            return
        g, kind = out_idx - 1, out_col // ATT_OUT_W
        if kind > 0 and tail_refs:
            t_ref = tail_refs[g]
            t_ref[0, kind - 1] = jnp.transpose(val)[:, tm - t_ref.shape[3]:]
        dil = dils[g]
        if dil == 1:
            o_ref[0, 0, :, ocs] = val.astype(o_ref.dtype)
            return
        for c in range(TILE // LANES):
            y_ref[c] = val[:, c * LANES:(c + 1) * LANES]
            for r in range(dil):
                o_ref[0, r, :, out_col + c * LANES:out_col + (c + 1) * LANES] = (
                    y_ref[c, pl.ds(r, tm // dil, stride=dil), :].astype(o_ref.dtype))

    n_ret_tiles = sum(1 for t in _SCHEDULE if t[1] == 0)
    units = [(c, h) for c in range(tm // RET_CHUNK) for h in range(RET_HEADS)] if fuse_retention else []
    assert len(units) <= len(_SCHEDULE) - n_ret_tiles
    for step, (w_col, out_idx, out_col, epi) in enumerate(_SCHEDULE):
        if fuse_retention and step == 2:
            for c, h in units:
                rows = slice(c * RET_CHUNK, (c + 1) * RET_CHUNK)
                kt_ref[c, h * RET_DK:(h + 1) * RET_DK, :] = _decayed_keys_t(
                    r_ref[rows, RET_QK_W + h * RET_DK:RET_QK_W + (h + 1) * RET_DK], kd_ref[h])
        unit = units[step - n_ret_tiles] if 0 <= step - n_ret_tiles < len(units) else None
        issued = retention_issue(*unit) if unit else None
        store_tile(epilogues[epi](_dot(h_ref[...], w_ref[:, w_col:w_col + TILE])), out_idx, out_col)
        if unit:
            retention_finish(*issued)


def _in_proj(x2d, seq_len, dils, tail_rows, out_dtype, wn, w_bf, rot, qg, kg, seg, tm, retention=None):
    n = x2d.shape[0]
    batch = n // seq_len
    per_seq = seq_len // tm
    fuse = retention is not None
    const = lambda a: pl.BlockSpec(a.shape, lambda i, nd=a.ndim: (0,) * nd)
    base_spec = pl.BlockSpec((1, 1, RET_DK), lambda i: (i % per_seq, 0, 0))
    qkv_spec = lambda d: pl.BlockSpec((1, d, tm // d, QKV_W), lambda i: (i // per_seq, 0, i % per_seq, 0))
    tail_specs, tail_shapes = [], []
    for rows in tail_rows:
        width = min(tm, rows)
        first = (seq_len - rows) // tm if rows >= tm else per_seq
        tail_specs.append(pl.BlockSpec(
            (1, 2, ATT_OUT_W, width),
            lambda i, first=first: (i // per_seq, 0, 0, jnp.maximum(i % per_seq - first, 0))))
        tail_shapes.append(jax.ShapeDtypeStruct((batch, 2, ATT_OUT_W, rows), F32))
    ca, sa, *offset_tables = rot
    args = [x2d, wn, w_bf, ca, sa, *offset_tables, qg, kg, seg]
    in_specs = [
        pl.BlockSpec((tm, D_MODEL), lambda i: (i, 0)),
        const(wn),
        pl.BlockSpec(w_bf.shape, lambda i: (0, 0), pipeline_mode=pl.Buffered(1)),
        base_spec, base_spec, *[const(t) for t in offset_tables],
        const(qg), const(kg), const(seg),
    ]
    scratch = [pltpu.VMEM((tm, D_MODEL), BF16), pltpu.VMEM((TILE // LANES, tm, LANES), F32)]
    if fuse:
        tables, gn = retention
        args += [*tables, gn]
        in_specs += [const(t) for t in (*tables, gn)]
        ret_specs = [pl.BlockSpec((tm, RET_V_W), lambda i: (i, 0)),
                     pl.BlockSpec((1, RET_HEADS, RET_DK, RET_DV), lambda i: (i // per_seq, 0, 0, 0))]
        ret_shapes = [jax.ShapeDtypeStruct((n, RET_V_W), out_dtype),
                      jax.ShapeDtypeStruct((batch, RET_HEADS, RET_DK, RET_DV), F32)]
        scratch += [pltpu.VMEM((tm, RET_OUT_W), F32), pltpu.VMEM((tm // RET_CHUNK, RET_QK_W, RET_CHUNK), BF16)]
    else:
        ret_specs = [pl.BlockSpec((tm, RET_OUT_W), lambda i: (i, 0))]
        ret_shapes = [jax.ShapeDtypeStruct((n, RET_OUT_W), out_dtype)]
    return pl.pallas_call(
        functools.partial(_in_proj_kernel, dils=dils, per_seq=per_seq, fuse_retention=fuse,
                          n_tails=len(tail_rows)),
        grid=(n // tm,),
        in_specs=in_specs,
        out_specs=ret_specs + [qkv_spec(d) for d in dils]
        + [pl.BlockSpec((tm, GATE_W), lambda i: (i, 0))] + tail_specs,
        out_shape=ret_shapes
        + [jax.ShapeDtypeStruct((batch, d, seq_len // d, QKV_W), out_dtype) for d in dils]
        + [jax.ShapeDtypeStruct((n, GATE_W), out_dtype)] + tail_shapes,
        scratch_shapes=scratch,
        compiler_params=pltpu.CompilerParams(
            dimension_semantics=("arbitrary",), vmem_limit_bytes=VMEM_LIMIT),
        name="in_proj",
    )(*args)


def _group_norm_gate(o, gain, gate):
    gate = gate.astype(F32)
    mu = jnp.mean(o, axis=-1, keepdims=True)
    d = o - mu
    var = jnp.mean(d * d, axis=-1, keepdims=True)
    return gate * jax.nn.sigmoid(gate) * (d * lax.rsqrt(var + EPS) * gain)


def _decayed_keys_t(k_keys, kd):
    return jnp.transpose(k_keys * kd).astype(BF16)


def _retention_specs(rows, idx):
    return [
        pl.BlockSpec((1, rows, RET_QK_W), lambda *a: (*idx(*a), 0)),
        pl.BlockSpec((1, rows, RET_QK_W), lambda *a: (*idx(*a), 1)),
        pl.BlockSpec((1, rows, RET_V_W), lambda *a: (*idx(*a), 1)),
        pl.BlockSpec((1, rows, RET_V_W), lambda *a: (*idx(*a), 2)),
    ]


def _table_specs(tables, ndim_grid):
    return [pl.BlockSpec(t.shape, lambda *a, nd=t.ndim: (0,) * nd) for t in tables]


def _retention_sample_kernel(q_ref, k_ref, v_ref, g_ref, s_in_ref, dec_ref, qd_ref, kd_ref, cd_ref,
                             gn_ref, o_ref, s_out_ref, kpad_ref, vpad_ref):
    @pl.when(pl.program_id(0) == 0)
    def _():
        kpad_ref[...] = jnp.zeros_like(kpad_ref)
        vpad_ref[...] = jnp.zeros_like(vpad_ref)

    kpad_ref[0:SAMPLE_PAD, :] = k_ref[0]
    vpad_ref[0:SAMPLE_PAD, :] = v_ref[0]
    issued = []
    for h in range(RET_HEADS):
        ks = slice(h * RET_DK, (h + 1) * RET_DK)
        vs = slice(h * RET_DV, (h + 1) * RET_DV)
        qb = q_ref[0, :, ks].astype(BF16)
        vb = vpad_ref[:, vs].astype(BF16)
        state = s_in_ref[0, h]
        scores = (_nt_dot(qb, kpad_ref[:, ks].astype(BF16)) * dec_ref[h]).astype(BF16)
        carried = _dot(qb, state.astype(BF16)) * qd_ref[h]
        s_out_ref[0, h] = state * cd_ref[h] + _dot(_decayed_keys_t(kpad_ref[:, ks], kd_ref[h]), vb)
        issued.append((scores, vb, carried))
    for h, (scores, vb, carried) in enumerate(issued):
        vs = slice(h * RET_DV, (h + 1) * RET_DV)
        o = _dot(scores, vb) + carried
        o_ref[0, :, vs] = _group_norm_gate(o, gn_ref[:, vs], g_ref[0, :, vs]).astype(o_ref.dtype)


def _attn_prompt_blocks(q_ref, kp_ref, kc_ref, vp_ref, vc_ref, bias_ref, o_ref, lse_ref, first):
    n_blocks = q_ref.shape[2] // ATT_BLOCK
    pair_w = 2 * ATT_DH
    low_q = lax.broadcasted_iota(jnp.int32, (ATT_BLOCK, pair_w), 1) < ATT_DH
    rows0, rows1 = slice(0, ATT_BLOCK), slice(ATT_BLOCK, 2 * ATT_BLOCK)

    def keys(prev_ref, cur_ref, t, ps):
        if t > 0:
            return cur_ref[0, 0, (t - 1) * ATT_BLOCK:(t + 1) * ATT_BLOCK, ps]
        if first:
            return cur_ref[0, 0, 0:ATT_BLOCK, ps]
        return jnp.concatenate([prev_ref[0, 0, :, ps], cur_ref[0, 0, 0:ATT_BLOCK, ps]], axis=0)

    staged = []
    for t in range(n_blocks):
        for p in range(ATT_HEADS // 2):
            ps = slice(p * pair_w, (p + 1) * pair_w)
            qp = q_ref[0, 0, t * ATT_BLOCK:(t + 1) * ATT_BLOCK, ps]
            kcat = keys(kp_ref, kc_ref, t, ps)
            q2 = jnp.concatenate([jnp.where(low_q, qp, 0.0), jnp.where(low_q, 0.0, qp)], axis=0)
            s = _nt_dot(q2, kcat) + bias_ref[p, :, 2 * ATT_BLOCK - kcat.shape[0]:]
            s = s.astype(BF16)
            m = jnp.max(s, axis=-1, keepdims=True)
            staged.append((t, p, m.astype(F32), jnp.exp(s - m)))
    for t, p, m, e in staged:
        ps = slice(p * pair_w, (p + 1) * pair_w)
        vcat = keys(vp_ref, vc_ref, t, ps)
        low_k = lax.broadcasted_iota(jnp.int32, vcat.shape, 1) < ATT_DH
        p0 = _dot(e[rows0], jnp.where(low_k, vcat, 1.0))
        p1 = _dot(e[rows1], jnp.where(low_k, 1.0, vcat))
        den = pltpu.roll(jnp.where(low_q, p1, p0), ATT_DH, axis=1)
        ts = slice(t * ATT_BLOCK, (t + 1) * ATT_BLOCK)
        o_ref[0, 0, ts, ps] = jnp.where(low_q, p0, p1) / den
        lse_ref[0, 0, ts, ps] = jnp.where(low_q, m[rows0], m[rows1]) + jnp.log(den)


def _attn_prompt_kernel(q_ref, kp_ref, kc_ref, vp_ref, vc_ref, slot_ref, o_ref, lse_ref, bias_ref):
    first = pl.program_id(2) == 0

    @pl.when(jnp.logical_and(first, jnp.logical_and(pl.program_id(0) == 0, pl.program_id(1) == 0)))
    def _():
        for h in range(ATT_HEADS):
            slots = jnp.broadcast_to(slot_ref[h:h + 1, :], (ATT_BLOCK, slot_ref.shape[1]))
            rows = pltpu.roll(slots, 0, 1, stride=1, stride_axis=0)
            bias_ref[h // 2, (h % 2) * ATT_BLOCK:(h % 2 + 1) * ATT_BLOCK, :] = rows[:, :2 * ATT_BLOCK]

    @pl.when(first)
    def _():
        _attn_prompt_blocks(q_ref, kp_ref, kc_ref, vp_ref, vc_ref, bias_ref, o_ref, lse_ref, True)

    @pl.when(jnp.logical_not(first))
    def _():
        _attn_prompt_blocks(q_ref, kp_ref, kc_ref, vp_ref, vc_ref, bias_ref, o_ref, lse_ref, False)


def _attn_prompt_group(qkv, slot_bias, n_blocks):
    b, dil, tr, _ = qkv.shape
    rows = n_blocks * ATT_BLOCK
    cur = lambda c: pl.BlockSpec((1, 1, rows, ATT_OUT_W), lambda bi, r, j: (bi, r, j, c))
    prev = lambda c: pl.BlockSpec((1, 1, ATT_BLOCK, ATT_OUT_W),
                                  lambda bi, r, j: (bi, r, jnp.maximum(j * n_blocks - 1, 0), c))
    res_shape = jax.ShapeDtypeStruct((b, dil, tr, ATT_OUT_W), F32)
    return pl.pallas_call(
        _attn_prompt_kernel,
        grid=(b, dil, tr // rows),
        in_specs=[cur(0), prev(1), cur(1), prev(2), cur(2),
                  pl.BlockSpec(slot_bias.shape, lambda bi, r, j: (0, 0))],
        out_specs=[cur(0), cur(0)],
        out_shape=[res_shape, res_shape],
        scratch_shapes=[pltpu.VMEM((ATT_HEADS // 2, 2 * ATT_BLOCK, 2 * ATT_BLOCK), F32)],
        compiler_params=pltpu.CompilerParams(
            dimension_semantics=("arbitrary", "arbitrary", "arbitrary"), vmem_limit_bytes=VMEM_LIMIT),
        name=f"attn_prompt_d{dil}",
    )(qkv, qkv, qkv, qkv, qkv, slot_bias)


def _attn_sample_kernel(a0_ref, a1_ref, a2_ref, c0_ref, c1_ref, c2_ref, s0_ref, s1_ref, s2_ref, sn_ref,
                        o_ref, kn_ref, vn_ref, b0_ref, b1_ref, b2_ref, bn_ref):
    qkv_refs = (a0_ref, a1_ref, a2_ref)
    cache_refs = (c0_ref, c1_ref, c2_ref)
    bias_refs = (b0_ref, b1_ref, b2_ref)

    @pl.when(pl.program_id(0) == 0)
    def _():
        kn_ref[...] = jnp.zeros_like(kn_ref)
        vn_ref[...] = jnp.zeros_like(vn_ref)
        def rotated_rows(vec, width):
            rows = jnp.broadcast_to(vec, (SAMPLE_PAD, vec.shape[1]))
            return pltpu.roll(rows, 0, 1, stride=1, stride_axis=0)[:, :width]
        for g, (s_ref, b_ref) in enumerate(zip((s0_ref, s1_ref, s2_ref), bias_refs)):
            for h in range(ATT_HEADS):
                b_ref[h] = rotated_rows(s_ref[h:h + 1, :], b_ref.shape[2])
                bn_ref[g, h] = rotated_rows(sn_ref[g, h:h + 1, :], ATT_BLOCK)

    for g in range(N_GROUPS):
        gs = slice(g * ATT_OUT_W, (g + 1) * ATT_OUT_W)
        kn_ref[0:SAMPLE_PAD, gs] = qkv_refs[g][0, :, ATT_OUT_W:2 * ATT_OUT_W]
        vn_ref[0:SAMPLE_PAD, gs] = qkv_refs[g][0, :, 2 * ATT_OUT_W:3 * ATT_OUT_W]

    staged = []
    for h in range(ATT_HEADS):
        logits = []
        for g in range(N_GROUPS):
            hs = slice(g * ATT_OUT_W + h * ATT_DH, g * ATT_OUT_W + (h + 1) * ATT_DH)
            qh = qkv_refs[g][0, :, h * ATT_DH:(h + 1) * ATT_DH].astype(BF16)
            logits.append(_dot(qh, cache_refs[g][0, 0, h].astype(BF16)) + bias_refs[g][h])
            logits.append(_nt_dot(qh, kn_ref[:, hs].astype(BF16)) + bn_ref[g, h])
        m = functools.reduce(jnp.maximum, [jnp.max(x, axis=-1, keepdims=True) for x in logits])
        es = [jnp.exp(x - m) for x in logits]
        l = functools.reduce(jnp.add, [jnp.sum(e, axis=-1, keepdims=True) for e in es])
        staged.append(([e.astype(BF16) for e in es], l))
    for h, (es, l) in enumerate(staged):
        acc = jnp.zeros((SAMPLE_PAD, ATT_DH), F32)
        for g in range(N_GROUPS):
            hs = slice(g * ATT_OUT_W + h * ATT_DH, g * ATT_OUT_W + (h + 1) * ATT_DH)
            acc = acc + _nt_dot(es[2 * g], cache_refs[g][0, 1, h].astype(BF16))
            acc = acc + _dot(es[2 * g + 1], vn_ref[:, hs].astype(BF16))
        o_ref[0, :, h * ATT_DH:(h + 1) * ATT_DH] = acc / l


_N_RET_IN, _N_ATT_IN = 10, 10


def _sample_mixers_kernel(*refs):
    n_in = _N_RET_IN + _N_ATT_IN
    ret_in, att_in = refs[:_N_RET_IN], refs[_N_RET_IN:n_in]
    ret_o_ref, state_o_ref, att_o_ref = refs[n_in:n_in + 3]
    kpad_ref, vpad_ref, kn_ref, vn_ref = refs[n_in + 3:n_in + 7]
    _attn_sample_kernel(*att_in, att_o_ref, kn_ref, vn_ref, *refs[n_in + 7:])
    _retention_sample_kernel(*ret_in, ret_o_ref, state_o_ref, kpad_ref, vpad_ref)


def _sample_mixers(ret, state, tables, gn, qkvs, caches_t, slots, slots_new):
    b, p, _ = ret.shape
    state_spec = pl.BlockSpec((1, RET_HEADS, RET_DK, RET_DV), lambda bi: (bi, 0, 0, 0))
    ret_specs = _retention_specs(p, lambda bi: (bi, 0)) + [state_spec] + _table_specs(tables + (gn,), 1)
    att_specs = ([pl.BlockSpec((1, p, QKV_W), lambda bi: (bi, 0, 0)) for _ in qkvs]
                 + [pl.BlockSpec((1,) + c.shape[1:], lambda bi: (bi, 0, 0, 0, 0)) for c in caches_t]
                 + [pl.BlockSpec(x.shape, lambda bi: (0, 0)) for x in slots]
                 + [pl.BlockSpec(slots_new.shape, lambda bi: (0, 0, 0))])
    assert len(ret_specs) == _N_RET_IN and len(att_specs) == _N_ATT_IN
    return pl.pallas_call(
        _sample_mixers_kernel,
        grid=(b,),
        in_specs=ret_specs + att_specs,
        out_specs=[pl.BlockSpec((1, p, RET_V_W), lambda bi: (bi, 0, 0)), state_spec,
                   pl.BlockSpec((1, p, ATT_OUT_W), lambda bi: (bi, 0, 0))],
        out_shape=[jax.ShapeDtypeStruct((b, p, RET_V_W), ret.dtype), jax.ShapeDtypeStruct(state.shape, F32),
                   jax.ShapeDtypeStruct((b, p, ATT_OUT_W), F32)],
        scratch_shapes=[pltpu.VMEM((RET_CHUNK, RET_QK_W), F32), pltpu.VMEM((RET_CHUNK, RET_V_W), F32),
                        pltpu.VMEM((ATT_BLOCK, ATT_W), F32), pltpu.VMEM((ATT_BLOCK, ATT_W), F32)]
        + [pltpu.VMEM((ATT_HEADS, p, c.shape[-1]), F32) for c in caches_t]
        + [pltpu.VMEM((N_GROUPS, ATT_HEADS, p, ATT_BLOCK), F32)],
        compiler_params=pltpu.CompilerParams(
            dimension_semantics=("arbitrary",), vmem_limit_bytes=VMEM_LIMIT),
        name="sample_mixers",
    )(ret, ret, ret, ret, state, *tables, gn, *qkvs, *caches_t, *slots, slots_new)


def _out_proj_kernel(*refs, dils):
    x_ref, ret_ref, ga_ref, gb_ref, ag_ref, wr_ref, wa_ref, wo_ref = refs[:8]
    if dils is None:
        att_ref, o_ref = refs[8:]
        att = att_ref[0]
    else:
        group_refs = refs[8:8 + 2 * N_GROUPS]
        o_ref = refs[8 + 2 * N_GROUPS]
        scratch = refs[9 + 2 * N_GROUPS:]
        tm = x_ref.shape[1]
        os, lses = [], []
        for g, dil in enumerate(dils):
            og_ref, lg_ref = group_refs[2 * g], group_refs[2 * g + 1]
            if dil == 1:
                os.append(og_ref[0, 0]); lses.append(lg_ref[0, 0])
                continue
            so_ref, sl_ref = scratch[2 * (g - 1)], scratch[2 * (g - 1) + 1]
            n_chunks = ATT_OUT_W // LANES
            for c in range(n_chunks):
                cs = slice(c * LANES, (c + 1) * LANES)
                for r in range(dil):
                    so_ref[c, pl.ds(r, tm // dil, stride=dil), :] = og_ref[0, r, :, cs]
                    sl_ref[c, pl.ds(r, tm // dil, stride=dil), :] = lg_ref[0, r, :, cs]
            os.append(jnp.concatenate([so_ref[c] for c in range(n_chunks)], axis=1))
            lses.append(jnp.concatenate([sl_ref[c] for c in range(n_chunks)], axis=1))
        mx = functools.reduce(jnp.maximum, lses)
        ws = [jnp.exp(l - mx) for l in lses]
        att = functools.reduce(jnp.add, [w * o for w, o in zip(ws, os)]) / functools.reduce(jnp.add, ws)
    ag = ag_ref[0].astype(F32)
    u = (ag * jax.nn.sigmoid(ag) * att).astype(BF16)
    o_b = _dot(u, wa_ref[...])
    o_a = _dot(ret_ref[0].astype(BF16), wr_ref[...])
    merged = jax.nn.sigmoid(ga_ref[0].astype(F32)) * o_a + jax.nn.sigmoid(gb_ref[0].astype(F32)) * o_b
    o_ref[0] = x_ref[0] + _dot(merged.astype(BF16), wo_ref[...])


def _out_proj(x, ret, gates, wr, wa, wo, tm, att=None, groups=None):
    b, t, _ = x.shape
    row = lambda w, c: pl.BlockSpec((1, tm, w), lambda bi, i: (bi, i, c))
    full = lambda a: pl.BlockSpec(a.shape, lambda bi, i: (0, 0))
    in_specs = [row(D_MODEL, 0), row(RET_V_W, 0), row(D_MODEL, 0), row(D_MODEL, 1),
                row(ATT_OUT_W, 2 * D_MODEL // ATT_OUT_W), full(wr), full(wa), full(wo)]
    args = [x, ret, gates, gates, gates, wr, wa, wo]
    scratch = []
    if groups is None:
        dils = None
        in_specs.append(row(ATT_OUT_W, 0))
        args.append(att)
    else:
        dils = tuple(o.shape[1] for o, _ in groups)
        for (o, lse), d in zip(groups, dils):
            spec = pl.BlockSpec((1, d, tm // d, ATT_OUT_W), lambda bi, i: (bi, 0, i, 0))
            in_specs += [spec, spec]
            args += [o, lse]
            if d > 1:
                scratch += [pltpu.VMEM((ATT_OUT_W // LANES, tm, LANES), F32)] * 2
    return pl.pallas_call(
        functools.partial(_out_proj_kernel, dils=dils),
        grid=(b, t // tm),
        in_specs=in_specs,
        out_specs=row(D_MODEL, 0),
        out_shape=jax.ShapeDtypeStruct((b, t, D_MODEL), F32),
        scratch_shapes=scratch,
        compiler_params=pltpu.CompilerParams(
            dimension_semantics=("parallel", "parallel"), vmem_limit_bytes=VMEM_LIMIT),
        name="out_proj",
    )(*args)


def _rotary_tables(bases, offsets):
    half = RET_DK // 2
    inv = ROPE_BASE ** (-jnp.arange(half, dtype=F32) / half)
    inv2 = jnp.concatenate([inv, inv])
    sign = jnp.concatenate([-jnp.ones((half,), F32), jnp.ones((half,), F32)])
    a = bases.astype(F32)[:, None, None] * inv2
    b = offsets.astype(F32)[:, None] * inv2
    cb, sb = jnp.cos(b), jnp.sin(b)
    return jnp.cos(a), jnp.sin(a), cb, sb, sign * cb, sign * sb


def _retention_tables(c, rows):
    log_g = jnp.log1p(-(2.0 ** (-5.0 - jnp.arange(RET_HEADS, dtype=F32))))
    i = jnp.arange(c, dtype=F32)
    diff = i[:, None] - i[None, :]
    decay = jnp.where(diff[None] >= 0, jnp.exp(jnp.maximum(diff, 0.0)[None] * log_g[:, None, None]), 0.0)
    q_decay = jnp.exp((i + 1.0)[None, :] * log_g[:, None])
    k_decay = jnp.exp((c - 1.0 - i)[None, :] * log_g[:, None])
    chunk_decay = jnp.exp(c * log_g)
    dec = jnp.zeros((RET_HEADS, rows, RET_CHUNK), F32).at[:, :c, :c].set(decay)
    qd = jnp.zeros((RET_HEADS, rows, 1), F32).at[:, :c, 0].set(q_decay)
    kd = jnp.zeros((RET_HEADS, RET_CHUNK, 1), F32).at[:, :c, 0].set(k_decay)
    cd = jnp.broadcast_to(chunk_decay[:, None, None], (RET_HEADS, 1, RET_DV))
    return dec, qd, kd, cd


def _t5_bucket(dist):
    max_exact = REL_BUCKETS // 2
    d = jnp.maximum(dist.astype(F32), 1.0)
    large = max_exact + (jnp.log(d / max_exact) / math.log(REL_MAX_DIST / max_exact)
                         * (REL_BUCKETS - max_exact)).astype(jnp.int32)
    large = jnp.minimum(large, REL_BUCKETS - 1)
    return jnp.where(dist < max_exact, dist, large)


def _group_bias(rel_bias, g, dil, slots):
    dist = dil * jnp.asarray(slots, dtype=jnp.int32)
    return rel_bias[_t5_bucket(dist)][:, g * ATT_HEADS:(g + 1) * ATT_HEADS].astype(F32).T


def _neg(heads, n):
    return jnp.full((heads, n), NEG, F32)


def _dilate(v, dil):
    heads, n = v.shape
    return jnp.stack([v] + [_neg(heads, n)] * (dil - 1), axis=-1).reshape(heads, n * dil)


def _prompt_slots(tb_rev):
    heads = tb_rev.shape[0]
    return jnp.concatenate([tb_rev, _neg(heads, 4 * ATT_BLOCK - N_KEYS)], axis=1)


def _sample_slots(tb, tb_rev, win, dil):
    heads = tb.shape[0]
    s_c = jnp.concatenate([_dilate(tb_rev[:, :N_KEYS - 1], dil), _neg(heads, ATT_BLOCK)], axis=1)
    back = [tb[:, k // dil:k // dil + 1] if k % dil == 0 else _neg(heads, 1) for k in range(SAMPLE_PAD - 1, 0, -1)]
    s_n = jnp.concatenate([tb[:, 0:1], _neg(heads, 2 * ATT_BLOCK - SAMPLE_PAD)] + back, axis=1)
    return s_c, s_n


def _kv_rows_t(tail_t):
    b, _, _, rows = tail_t.shape
    return jnp.transpose(tail_t.reshape(b, 2, ATT_HEADS, ATT_DH, rows), (0, 4, 1, 2, 3))


def _kv_rows(qkv, n_rows):
    b = qkv.shape[0]
    return qkv[:, :n_rows, ATT_OUT_W:].reshape(b, n_rows, 2, ATT_HEADS, ATT_DH)


def kernel(x_prompt, x_sample, cache_kv_w128, cache_kv_w512, cache_kv_w2048, state_retention,
           w_norm, w_in, q_norm, k_norm, rel_bias, ret_norm, w_proj_ret, w_proj_att, w_out):
    assert w_in.shape[0] == 1
    bp, t, _ = x_prompt.shape
    bs, ts, _ = x_sample.shape
    dils = tuple(d for _, d in ATT_GROUPS)
    assert t % (ATT_BLOCK * max(dils)) == 0 and ts <= SAMPLE_PAD
    caches = (cache_kv_w128[0], cache_kv_w512[0], cache_kv_w2048[0])
    for cch, (win, _) in zip(caches, ATT_GROUPS):
        assert cch.shape[1] == win and win <= PAST_LEN

    wn = w_norm[0].reshape(1, D_MODEL)
    w_in_bf = w_in[0].astype(BF16)
    qg = jnp.tile(q_norm[0] * (ATT_DH ** -0.5), TILE // ATT_DH).reshape(1, TILE)
    kg = jnp.tile(k_norm[0], TILE // ATT_DH).reshape(1, TILE)
    gn = ret_norm[0].reshape(1, RET_V_W)
    wr = w_proj_ret[0].astype(BF16)
    wa = w_proj_att[0].astype(BF16)
    wo = w_out[0].astype(BF16)
    hid = jnp.arange(TILE // 2) // ATT_DH
    seg_mean = jnp.where(hid[:, None] == hid[None, :], 1.0 / ATT_DH, 0.0).astype(BF16)
    asc, desc = tuple(range(N_KEYS)), tuple(range(N_KEYS - 1, -1, -1))
    group_bias = [(_group_bias(rel_bias, g, d, asc), _group_bias(rel_bias, g, d, desc)) for g, d in enumerate(dils)]

    tm_p = 256
    rot_p = _rotary_tables(jnp.arange(0, t, tm_p), jnp.arange(tm_p))
    tail_p_rows = tuple(min(w, t) for w, _ in ATT_GROUPS)
    ret_p, state_p, *rest = _in_proj(
        x_prompt.reshape(bp * t, D_MODEL), t, dils, tail_p_rows, BF16, wn, w_in_bf, rot_p, qg, kg, seg_mean,
        tm=tm_p, retention=(_retention_tables(RET_CHUNK, RET_CHUNK), gn))
    qkv_p, gates_p, tails_p = rest[:N_GROUPS], rest[N_GROUPS], rest[N_GROUPS + 1:]
    att_blocks = tuple(min(4, t // (d * ATT_BLOCK)) for d in dils)
    groups = [_attn_prompt_group(qkv_p[g], _prompt_slots(group_bias[g][1]), n_blocks=att_blocks[g])
              for g in range(N_GROUPS)]
    y_p = _out_proj(x_prompt, ret_p.reshape(bp, t, RET_V_W), gates_p.reshape(bp, t, GATE_W), wr, wa, wo,
                    tm=512, groups=groups)

    pad = SAMPLE_PAD
    ns = bs * pad
    xs = jnp.pad(x_sample, ((0, 0), (0, pad - ts), (0, 0))).reshape(ns, D_MODEL)
    rot_s = _rotary_tables(jnp.full((1,), PAST_LEN), jnp.tile(jnp.arange(pad), bs))
    ret_in_s, *qkv_s, gates_s = _in_proj(xs, ns, (1,) * N_GROUPS, (), F32, wn, w_in_bf, rot_s,
                                                 qg, kg, seg_mean, tm=ns)
    qkv_s = [a.reshape(bs, pad, QKV_W) for a in qkv_s]
    caches_t = [jnp.transpose(c, (0, 2, 3, 4, 1)) for c in caches]
    ss = [_sample_slots(*group_bias[g], win, d) for g, (win, d) in enumerate(ATT_GROUPS)]
    ret_s, state_s, att_s = _sample_mixers(
        ret_in_s.reshape(bs, pad, RET_OUT_W), state_retention[0], _retention_tables(ts, pad), gn,
        qkv_s, caches_t, [c for c, _ in ss], jnp.stack([n for _, n in ss]))
    y_s = _out_proj(xs.reshape(1, ns, D_MODEL), ret_s.reshape(1, ns, RET_V_W), gates_s.reshape(1, ns, GATE_W),
                    wr, wa, wo, tm=ns, att=att_s.reshape(1, ns, ATT_OUT_W))
    y_s = y_s.reshape(bs, pad, D_MODEL)[:, :ts]

    kv_p = [_kv_rows_t(tt)[None] for tt in tails_p]
    kv_s = [_kv_rows(a, ts)[None] for a in qkv_s]
    return (y_p, y_s, state_p[None], state_s[None], kv_p[0], kv_p[1], kv_p[2], kv_s[0], kv_s[1], kv_s[2])
```

```python
import functools
import math

import jax
import jax.numpy as jnp
from jax import lax
from jax.experimental import pallas as pl
from jax.experimental.pallas import tpu as pltpu

D_MODEL = 1024
PAST_LEN = 16384
RET_HEADS = 4
RET_DK = 128
RET_DV = 256
RET_CHUNK = 128
ROPE_BASE = 10000.0
ATT_GROUPS = ((128, 1), (512, 4), (2048, 16))
N_GROUPS = 3
ATT_HEADS = 8
ATT_DH = 64
REL_BUCKETS = 32
REL_MAX_DIST = 2048
EPS = 1e-6

RET_QK_W = RET_HEADS * RET_DK
RET_V_W = RET_HEADS * RET_DV
ATT_W = N_GROUPS * ATT_HEADS * ATT_DH
ATT_OUT_W = ATT_HEADS * ATT_DH
IN_W = 2 * RET_QK_W + 2 * RET_V_W + 3 * ATT_W + ATT_OUT_W + 2 * D_MODEL

LANES = 128
TILE = 512
RET_OUT_W = 2 * RET_QK_W + 2 * RET_V_W
QKV_W = 3 * ATT_OUT_W
GATE_W = 2 * D_MODEL + ATT_OUT_W
ATT_BLOCK = 128
N_KEYS = ATT_BLOCK + 1
SAMPLE_PAD = 8
NEG = -1e30
VMEM_LIMIT = 48 * 1024 * 1024

F32 = jnp.float32
BF16 = jnp.bfloat16


def _nt_dot(a, b):
    return lax.dot_general(a, b, (((1,), (1,)), ((), ())), preferred_element_type=F32)


def _dot(a, b):
    return jnp.dot(a, b, preferred_element_type=F32)


def _proj_schedule():
    sched = [(0, 0, 0, "rot_q"), (RET_QK_W, 0, RET_QK_W, "rot_k")]
    for k in range(2 * RET_V_W // TILE):
        sched.append((2 * RET_QK_W + k * TILE, 0, 2 * RET_QK_W + k * TILE, "plain"))
    att0 = 2 * RET_QK_W + 2 * RET_V_W
    for g in range(N_GROUPS):
        for kind, epi in enumerate(("norm_q", "norm_k", "plain")):
            sched.append((att0 + kind * ATT_W + g * ATT_OUT_W, 1 + g, kind * ATT_OUT_W, epi))
    gate0 = att0 + 3 * ATT_W
    for k in range(2 * D_MODEL // TILE):
        sched.append((gate0 + ATT_OUT_W + k * TILE, 4, k * TILE, "plain"))
    sched.append((gate0, 4, 2 * D_MODEL, "plain"))
    return sched


_SCHEDULE = _proj_schedule()


def _in_proj_kernel(*refs, dils, per_seq, fuse_retention, n_tails):
    (x_ref, wn_ref, w_ref, ca_ref, sa_ref, cb_ref, sb_ref, cbs_ref, sbs_ref,
     qg_ref, kg_ref, seg_ref) = refs[:12]
    refs = refs[12:]
    if fuse_retention:
        dec_ref, qd_ref, kd_ref, cd_ref, gn_ref, ret_ref, s_ref = refs[:7]
        refs = refs[7:]
    else:
        ret_ref = refs[0]
        refs = refs[1:]
    a0_ref, a1_ref, a2_ref, gate_ref = refs[:4]
    tail_refs = refs[4:4 + n_tails]
    h_ref, y_ref = refs[4 + n_tails:6 + n_tails]
    refs = refs[6 + n_tails:]
    r_ref = refs[0] if fuse_retention else ret_ref
    kt_ref = refs[1] if fuse_retention else None
    out_refs = (r_ref, a0_ref, a1_ref, a2_ref, gate_ref)

    if fuse_retention:
        @pl.when(pl.program_id(0) % per_seq == 0)
        def _():
            s_ref[...] = jnp.zeros_like(s_ref)

    x = x_ref[...]
    tm = x.shape[0]
    ms = jnp.mean(x * x, axis=-1, keepdims=True)
    h_ref[...] = (x * lax.rsqrt(ms + EPS) * wn_ref[...]).astype(BF16)

    ca, sa = ca_ref[0], sa_ref[0]
    cos = ca * cb_ref[...] - sa * sb_ref[...]
    sin = sa * cbs_ref[...] + ca * sbs_ref[...]

    def rotary(y, scale):
        parts = []
        for hh in range(TILE // RET_DK):
            yh = y[:, hh * RET_DK:(hh + 1) * RET_DK]
            parts.append((yh * cos + pltpu.roll(yh, RET_DK // 2, axis=1) * sin) * scale)
        return jnp.concatenate(parts, axis=1)

    def head_rms(y, gain):
        y2 = (y * y).astype(BF16)
        half = TILE // 2
        ms = jnp.concatenate([_dot(y2[:, :half], seg_ref[...]), _dot(y2[:, half:], seg_ref[...])], axis=1)
        return y * lax.rsqrt(ms + EPS) * gain

    epilogues = {
        "plain": lambda y: y,
        "rot_q": lambda y: rotary(y, 1.0),
        "rot_k": lambda y: rotary(y, RET_DK ** -0.5),
        "norm_q": lambda y: head_rms(y, qg_ref[...]),
        "norm_k": lambda y: head_rms(y, kg_ref[...]),
    }

    kt_base = jnp.minimum(pl.program_id(0), 0)

    def retention_issue(c, h):
        rows = slice(c * RET_CHUNK, (c + 1) * RET_CHUNK)
        ks = slice(h * RET_DK, (h + 1) * RET_DK)
        ks2 = slice(RET_QK_W + h * RET_DK, RET_QK_W + (h + 1) * RET_DK)
        vs_in = slice(2 * RET_QK_W + h * RET_DV, 2 * RET_QK_W + (h + 1) * RET_DV)
        qb = r_ref[rows, ks].astype(BF16)
        vb = r_ref[rows, vs_in].astype(BF16)
        state = s_ref[0, h]
        scores = (_nt_dot(qb, r_ref[rows, ks2].astype(BF16)) * dec_ref[h]).astype(BF16)
        carried = _dot(qb, state.astype(BF16)) * qd_ref[h]
        s_ref[0, h] = state * cd_ref[h] + _dot(kt_ref[kt_base + c, ks, :], vb)
        return rows, h, scores, vb, carried

    def retention_finish(rows, h, scores, vb, carried):
        vs = slice(h * RET_DV, (h + 1) * RET_DV)
        gs_in = slice(2 * RET_QK_W + RET_V_W + h * RET_DV, 2 * RET_QK_W + RET_V_W + (h + 1) * RET_DV)
        o = _dot(scores, vb) + carried
        ret_ref[rows, vs] = _group_norm_gate(o, gn_ref[:, vs], r_ref[rows, gs_in]).astype(ret_ref.dtype)

    def store_tile(val, out_idx, out_col):
        o_ref = out_refs[out_idx]
        ocs = slice(out_col, out_col + TILE)
        if not 1 <= out_idx <= N_GROUPS:
            o_ref[:, ocs] = val.astype(o_ref.dtype)
            return
        g, kind = out_idx - 1, out_col // ATT_OUT_W
        if kind > 0 and tail_refs:
            t_ref = tail_refs[g]
            t_ref[0, kind - 1] = jnp.transpose(val)[:, tm - t_ref.shape[3]:]
        dil = dils[g]
        if dil == 1:
            o_ref[0, 0, :, ocs] = val.astype(o_ref.dtype)
            return
        for c in range(TILE // LANES):
            y_ref[c] = val[:, c * LANES:(c + 1) * LANES]
            for r in range(dil):
                o_ref[0, r, :, out_col + c * LANES:out_col + (c + 1) * LANES] = (
                    y_ref[c, pl.ds(r, tm // dil, stride=dil), :].astype(o_ref.dtype))

    n_ret_tiles = sum(1 for t in _SCHEDULE if t[1] == 0)
    units = [(c, h) for c in range(tm // RET_CHUNK) for h in range(RET_HEADS)] if fuse_retention else []
    assert len(units) <= len(_SCHEDULE) - n_ret_tiles
    for step, (w_col, out_idx, out_col, epi) in enumerate(_SCHEDULE):
        if fuse_retention and step == 2:
            for c, h in units:
                rows = slice(c * RET_CHUNK, (c + 1) * RET_CHUNK)
                kt_ref[c, h * RET_DK:(h + 1) * RET_DK, :] = _decayed_keys_t(
                    r_ref[rows, RET_QK_W + h * RET_DK:RET_QK_W + (h + 1) * RET_DK], kd_ref[h])
        unit = units[step - n_ret_tiles] if 0 <= step - n_ret_tiles < len(units) else None
        issued = retention_issue(*unit) if unit else None
        store_tile(epilogues[epi](_dot(h_ref[...], w_ref[:, w_col:w_col + TILE])), out_idx, out_col)
        if unit:
            retention_finish(*issued)


def _in_proj(x2d, seq_len, dils, tail_rows, out_dtype, wn, w_bf, rot, qg, kg, seg, tm, retention=None):
    n = x2d.shape[0]
    batch = n // seq_len
    per_seq = seq_len // tm
    fuse = retention is not None
    const = lambda a: pl.BlockSpec(a.shape, lambda i, nd=a.ndim: (0,) * nd)
    base_spec = pl.BlockSpec((1, 1, RET_DK), lambda i: (i % per_seq, 0, 0))
    qkv_spec = lambda d: pl.BlockSpec((1, d, tm // d, QKV_W), lambda i: (i // per_seq, 0, i % per_seq, 0))
    tail_specs, tail_shapes = [], []
    for rows in tail_rows:
        width = min(tm, rows)
        first = (seq_len - rows) // tm if rows >= tm else per_seq
        tail_specs.append(pl.BlockSpec(
            (1, 2, ATT_OUT_W, width),
            lambda i, first=first: (i // per_seq, 0, 0, jnp.maximum(i % per_seq - first, 0))))
        tail_shapes.append(jax.ShapeDtypeStruct((batch, 2, ATT_OUT_W, rows), F32))
    ca, sa, *offset_tables = rot
    args = [x2d, wn, w_bf, ca, sa, *offset_tables, qg, kg, seg]
    in_specs = [
        pl.BlockSpec((tm, D_MODEL), lambda i: (i, 0)),
        const(wn),
        pl.BlockSpec(w_bf.shape, lambda i: (0, 0), pipeline_mode=pl.Buffered(1)),
        base_spec, base_spec, *[const(t) for t in offset_tables],
        const(qg), const(kg), const(seg),
    ]
    scratch = [pltpu.VMEM((tm, D_MODEL), BF16), pltpu.VMEM((TILE // LANES, tm, LANES), F32)]
    if fuse:
        tables, gn = retention
        args += [*tables, gn]
        in_specs += [const(t) for t in (*tables, gn)]
        ret_specs = [pl.BlockSpec((tm, RET_V_W), lambda i: (i, 0)),
                     pl.BlockSpec((1, RET_HEADS, RET_DK, RET_DV), lambda i: (i // per_seq, 0, 0, 0))]
        ret_shapes = [jax.ShapeDtypeStruct((n, RET_V_W), out_dtype),
                      jax.ShapeDtypeStruct((batch, RET_HEADS, RET_DK, RET_DV), F32)]
        scratch += [pltpu.VMEM((tm, RET_OUT_W), F32), pltpu.VMEM((tm // RET_CHUNK, RET_QK_W, RET_CHUNK), BF16)]
    else:
        ret_specs = [pl.BlockSpec((tm, RET_OUT_W), lambda i: (i, 0))]
        ret_shapes = [jax.ShapeDtypeStruct((n, RET_OUT_W), out_dtype)]
    return pl.pallas_call(
        functools.partial(_in_proj_kernel, dils=dils, per_seq=per_seq, fuse_retention=fuse,
                          n_tails=len(tail_rows)),
        grid=(n // tm,),
        in_specs=in_specs,
        out_specs=ret_specs + [qkv_spec(d) for d in dils]
        + [pl.BlockSpec((tm, GATE_W), lambda i: (i, 0))] + tail_specs,
        out_shape=ret_shapes
        + [jax.ShapeDtypeStruct((batch, d, seq_len // d, QKV_W), out_dtype) for d in dils]
        + [jax.ShapeDtypeStruct((n, GATE_W), out_dtype)] + tail_shapes,
        scratch_shapes=scratch,
        compiler_params=pltpu.CompilerParams(
            dimension_semantics=("arbitrary",), vmem_limit_bytes=VMEM_LIMIT),
        name="in_proj",
    )(*args)


def _group_norm_gate(o, gain, gate):
    gate = gate.astype(F32)
    mu = jnp.mean(o, axis=-1, keepdims=True)
    d = o - mu
    var = jnp.mean(d * d, axis=-1, keepdims=True)
    return gate * jax.nn.sigmoid(gate) * (d * lax.rsqrt(var + EPS) * gain)


def _decayed_keys_t(k_keys, kd):
    return jnp.transpose(k_keys * kd).astype(BF16)


def _retention_specs(rows, idx):
    return [
        pl.BlockSpec((1, rows, RET_QK_W), lambda *a: (*idx(*a), 0)),
        pl.BlockSpec((1, rows, RET_QK_W), lambda *a: (*idx(*a), 1)),
        pl.BlockSpec((1, rows, RET_V_W), lambda *a: (*idx(*a), 1)),
        pl.BlockSpec((1, rows, RET_V_W), lambda *a: (*idx(*a), 2)),
    ]


def _table_specs(tables, ndim_grid):
    return [pl.BlockSpec(t.shape, lambda *a, nd=t.ndim: (0,) * nd) for t in tables]


def _retention_sample_kernel(q_ref, k_ref, v_ref, g_ref, s_in_ref, dec_ref, qd_ref, kd_ref, cd_ref,
                             gn_ref, o_ref, s_out_ref, kpad_ref, vpad_ref):
    @pl.when(pl.program_id(0) == 0)
    def _():
        kpad_ref[...] = jnp.zeros_like(kpad_ref)
        vpad_ref[...] = jnp.zeros_like(vpad_ref)

    kpad_ref[0:SAMPLE_PAD, :] = k_ref[0]
    vpad_ref[0:SAMPLE_PAD, :] = v_ref[0]
    issued = []
    for h in range(RET_HEADS):
        ks = slice(h * RET_DK, (h + 1) * RET_DK)
        vs = slice(h * RET_DV, (h + 1) * RET_DV)
        qb = q_ref[0, :, ks].astype(BF16)
        vb = vpad_ref[:, vs].astype(BF16)
        state = s_in_ref[0, h]
        scores = (_nt_dot(qb, kpad_ref[:, ks].astype(BF16)) * dec_ref[h]).astype(BF16)
        carried = _dot(qb, state.astype(BF16)) * qd_ref[h]
        s_out_ref[0, h] = state * cd_ref[h] + _dot(_decayed_keys_t(kpad_ref[:, ks], kd_ref[h]), vb)
        issued.append((scores, vb, carried))
    for h, (scores, vb, carried) in enumerate(issued):
        vs = slice(h * RET_DV, (h + 1) * RET_DV)
        o = _dot(scores, vb) + carried
        o_ref[0, :, vs] = _group_norm_gate(o, gn_ref[:, vs], g_ref[0, :, vs]).astype(o_ref.dtype)


def _attn_prompt_blocks(q_ref, kp_ref, kc_ref, vp_ref, vc_ref, bias_ref, o_ref, lse_ref, first):
    n_blocks = q_ref.shape[2] // ATT_BLOCK
    pair_w = 2 * ATT_DH
    low_q = lax.broadcasted_iota(jnp.int32, (ATT_BLOCK, pair_w), 1) < ATT_DH
    rows0, rows1 = slice(0, ATT_BLOCK), slice(ATT_BLOCK, 2 * ATT_BLOCK)

    def keys(prev_ref, cur_ref, t, ps):
        if t > 0:
            return cur_ref[0, 0, (t - 1) * ATT_BLOCK:(t + 1) * ATT_BLOCK, ps]
        if first:
            return cur_ref[0, 0, 0:ATT_BLOCK, ps]
        return jnp.concatenate([prev_ref[0, 0, :, ps], cur_ref[0, 0, 0:ATT_BLOCK, ps]], axis=0)

    staged = []
    for t in range(n_blocks):
        for p in range(ATT_HEADS // 2):
            ps = slice(p * pair_w, (p + 1) * pair_w)
            qp = q_ref[0, 0, t * ATT_BLOCK:(t + 1) * ATT_BLOCK, ps]
            kcat = keys(kp_ref, kc_ref, t, ps)
            q2 = jnp.concatenate([jnp.where(low_q, qp, 0.0), jnp.where(low_q, 0.0, qp)], axis=0)
            s = _nt_dot(q2, kcat) + bias_ref[p, :, 2 * ATT_BLOCK - kcat.shape[0]:]
            s = s.astype(BF16)
            m = jnp.max(s, axis=-1, keepdims=True)
            staged.append((t, p, m.astype(F32), jnp.exp(s - m)))
    for t, p, m, e in staged:
        ps = slice(p * pair_w, (p + 1) * pair_w)
        vcat = keys(vp_ref, vc_ref, t, ps)
        low_k = lax.broadcasted_iota(jnp.int32, vcat.shape, 1) < ATT_DH
        p0 = _dot(e[rows0], jnp.where(low_k, vcat, 1.0))
        p1 = _dot(e[rows1], jnp.where(low_k, 1.0, vcat))
        den = pltpu.roll(jnp.where(low_q, p1, p0), ATT_DH, axis=1)
        ts = slice(t * ATT_BLOCK, (t + 1) * ATT_BLOCK)
        o_ref[0, 0, ts, ps] = jnp.where(low_q, p0, p1) / den
        lse_ref[0, 0, ts, ps] = jnp.where(low_q, m[rows0], m[rows1]) + jnp.log(den)


def _attn_prompt_kernel(q_ref, kp_ref, kc_ref, vp_ref, vc_ref, slot_ref, o_ref, lse_ref, bias_ref):
    first = pl.program_id(2) == 0

    @pl.when(jnp.logical_and(first, jnp.logical_and(pl.program_id(0) == 0, pl.program_id(1) == 0)))
    def _():
        for h in range(ATT_HEADS):
            slots = jnp.broadcast_to(slot_ref[h:h + 1, :], (ATT_BLOCK, slot_ref.shape[1]))
            rows = pltpu.roll(slots, 0, 1, stride=1, stride_axis=0)
            bias_ref[h // 2, (h % 2) * ATT_BLOCK:(h % 2 + 1) * ATT_BLOCK, :] = rows[:, :2 * ATT_BLOCK]

    @pl.when(first)
    def _():
        _attn_prompt_blocks(q_ref, kp_ref, kc_ref, vp_ref, vc_ref, bias_ref, o_ref, lse_ref, True)

    @pl.when(jnp.logical_not(first))
    def _():
        _attn_prompt_blocks(q_ref, kp_ref, kc_ref, vp_ref, vc_ref, bias_ref, o_ref, lse_ref, False)


def _attn_prompt_group(qkv, slot_bias, n_blocks):
    b, dil, tr, _ = qkv.shape
    rows = n_blocks * ATT_BLOCK
    cur = lambda c: pl.BlockSpec((1, 1, rows, ATT_OUT_W), lambda bi, r, j: (bi, r, j, c))
    prev = lambda c: pl.BlockSpec((1, 1, ATT_BLOCK, ATT_OUT_W),
                                  lambda bi, r, j: (bi, r, jnp.maximum(j * n_blocks - 1, 0), c))
    res_shape = jax.ShapeDtypeStruct((b, dil, tr, ATT_OUT_W), F32)
    return pl.pallas_call(
        _attn_prompt_kernel,
        grid=(b, dil, tr // rows),
        in_specs=[cur(0), prev(1), cur(1), prev(2), cur(2),
                  pl.BlockSpec(slot_bias.shape, lambda bi, r, j: (0, 0))],
        out_specs=[cur(0), cur(0)],
        out_shape=[res_shape, res_shape],
        scratch_shapes=[pltpu.VMEM((ATT_HEADS // 2, 2 * ATT_BLOCK, 2 * ATT_BLOCK), F32)],
        compiler_params=pltpu.CompilerParams(
            dimension_semantics=("arbitrary", "arbitrary", "arbitrary"), vmem_limit_bytes=VMEM_LIMIT),
        name=f"attn_prompt_d{dil}",
    )(qkv, qkv, qkv, qkv, qkv, slot_bias)


def _attn_sample_kernel(a0_ref, a1_ref, a2_ref, c0_ref, c1_ref, c2_ref, s0_ref, s1_ref, s2_ref, sn_ref,
                        o_ref, kn_ref, vn_ref, b0_ref, b1_ref, b2_ref, bn_ref):
    qkv_refs = (a0_ref, a1_ref, a2_ref)
    cache_refs = (c0_ref, c1_ref, c2_ref)
    bias_refs = (b0_ref, b1_ref, b2_ref)

    @pl.when(pl.program_id(0) == 0)
    def _():
        kn_ref[...] = jnp.zeros_like(kn_ref)
        vn_ref[...] = jnp.zeros_like(vn_ref)
        def rotated_rows(vec, width):
            rows = jnp.broadcast_to(vec, (SAMPLE_PAD, vec.shape[1]))
            return pltpu.roll(rows, 0, 1, stride=1, stride_axis=0)[:, :width]
        for g, (s_ref, b_ref) in enumerate(zip((s0_ref, s1_ref, s2_ref), bias_refs)):
            for h in range(ATT_HEADS):
                b_ref[h] = rotated_rows(s_ref[h:h + 1, :], b_ref.shape[2])
                bn_ref[g, h] = rotated_rows(sn_ref[g, h:h + 1, :], ATT_BLOCK)

    for g in range(N_GROUPS):
        gs = slice(g * ATT_OUT_W, (g + 1) * ATT_OUT_W)
        kn_ref[0:SAMPLE_PAD, gs] = qkv_refs[g][0, :, ATT_OUT_W:2 * ATT_OUT_W]
        vn_ref[0:SAMPLE_PAD, gs] = qkv_refs[g][0, :, 2 * ATT_OUT_W:3 * ATT_OUT_W]

    staged = []
    for h in range(ATT_HEADS):
        logits = []
        for g in range(N_GROUPS):
            hs = slice(g * ATT_OUT_W + h * ATT_DH, g * ATT_OUT_W + (h + 1) * ATT_DH)
            qh = qkv_refs[g][0, :, h * ATT_DH:(h + 1) * ATT_DH].astype(BF16)
            logits.append(_dot(qh, cache_refs[g][0, 0, h].astype(BF16)) + bias_refs[g][h])
            logits.append(_nt_dot(qh, kn_ref[:, hs].astype(BF16)) + bn_ref[g, h])
        m = functools.reduce(jnp.maximum, [jnp.max(x, axis=-1, keepdims=True) for x in logits])
        es = [jnp.exp(x - m) for x in logits]
        l = functools.reduce(jnp.add, [jnp.sum(e, axis=-1, keepdims=True) for e in es])
        staged.append(([e.astype(BF16) for e in es], l))
    for h, (es, l) in enumerate(staged):
        acc = jnp.zeros((SAMPLE_PAD, ATT_DH), F32)
        for g in range(N_GROUPS):
            hs = slice(g * ATT_OUT_W + h * ATT_DH, g * ATT_OUT_W + (h + 1) * ATT_DH)
            acc = acc + _nt_dot(es[2 * g], cache_refs[g][0, 1, h].astype(BF16))
            acc = acc + _dot(es[2 * g + 1], vn_ref[:, hs].astype(BF16))
        o_ref[0, :, h * ATT_DH:(h + 1) * ATT_DH] = acc / l


_N_RET_IN, _N_ATT_IN = 10, 10


def _sample_mixers_kernel(*refs):
    n_in = _N_RET_IN + _N_ATT_IN
    ret_in, att_in = refs[:_N_RET_IN], refs[_N_RET_IN:n_in]
    ret_o_ref, state_o_ref, att_o_ref = refs[n_in:n_in + 3]
    kv_o_refs = refs[n_in + 3:n_in + 3 + N_GROUPS]
    kpad_ref, vpad_ref, kn_ref, vn_ref = refs[n_in + 3 + N_GROUPS:n_in + 7 + N_GROUPS]
    _attn_sample_kernel(*att_in, att_o_ref, kn_ref, vn_ref, *refs[n_in + 7 + N_GROUPS:])
    _retention_sample_kernel(*ret_in, ret_o_ref, state_o_ref, kpad_ref, vpad_ref)
    for qkv_ref, kv_ref in zip(att_in[:N_GROUPS], kv_o_refs):
        for tok in range(kv_ref.shape[1]):
            for kv in range(2):
                for h in range(ATT_HEADS):
                    col = (1 + kv) * ATT_OUT_W + h * ATT_DH
                    kv_ref[0, tok, kv, h:h + 1, :] = qkv_ref[0, tok:tok + 1, col:col + ATT_DH]


def _sample_mixers(ret, state, tables, gn, qkvs, caches_t, slots, slots_new, n_tokens):
    b, p, _ = ret.shape
    state_spec = pl.BlockSpec((1, RET_HEADS, RET_DK, RET_DV), lambda bi: (bi, 0, 0, 0))
    ret_specs = _retention_specs(p, lambda bi: (bi, 0)) + [state_spec] + _table_specs(tables + (gn,), 1)
    att_specs = ([pl.BlockSpec((1, p, QKV_W), lambda bi: (bi, 0, 0)) for _ in qkvs]
                 + [pl.BlockSpec((1,) + c.shape[1:], lambda bi: (bi, 0, 0, 0, 0)) for c in caches_t]
                 + [pl.BlockSpec(x.shape, lambda bi: (0, 0)) for x in slots]
                 + [pl.BlockSpec(slots_new.shape, lambda bi: (0, 0, 0))])
    assert len(ret_specs) == _N_RET_IN and len(att_specs) == _N_ATT_IN
    kv_shape = (b, n_tokens, 2, ATT_HEADS, ATT_DH)
    return pl.pallas_call(
        _sample_mixers_kernel,
        grid=(b,),
        in_specs=ret_specs + att_specs,
        out_specs=[pl.BlockSpec((1, p, RET_V_W), lambda bi: (bi, 0, 0)), state_spec,
                   pl.BlockSpec((1, p, ATT_OUT_W), lambda bi: (bi, 0, 0))]
        + [pl.BlockSpec((1,) + kv_shape[1:], lambda bi: (bi, 0, 0, 0, 0))] * N_GROUPS,
        out_shape=[jax.ShapeDtypeStruct((b, p, RET_V_W), ret.dtype), jax.ShapeDtypeStruct(state.shape, F32),
                   jax.ShapeDtypeStruct((b, p, ATT_OUT_W), F32)]
        + [jax.ShapeDtypeStruct(kv_shape, F32)] * N_GROUPS,
        scratch_shapes=[pltpu.VMEM((RET_CHUNK, RET_QK_W), F32), pltpu.VMEM((RET_CHUNK, RET_V_W), F32),
                        pltpu.VMEM((ATT_BLOCK, ATT_W), F32), pltpu.VMEM((ATT_BLOCK, ATT_W), F32)]
        + [pltpu.VMEM((ATT_HEADS, p, c.shape[-1]), F32) for c in caches_t]
        + [pltpu.VMEM((N_GROUPS, ATT_HEADS, p, ATT_BLOCK), F32)],
        compiler_params=pltpu.CompilerParams(
            dimension_semantics=("arbitrary",), vmem_limit_bytes=VMEM_LIMIT),
        name="sample_mixers",
    )(ret, ret, ret, ret, state, *tables, gn, *qkvs, *caches_t, *slots, slots_new)


def _out_proj_kernel(*refs, dils):
    x_ref, ret_ref, ga_ref, gb_ref, ag_ref, wr_ref, wa_ref, wo_ref = refs[:8]
    if dils is None:
        att_ref, o_ref = refs[8:]
        att = att_ref[0]
    else:
        group_refs = refs[8:8 + 2 * N_GROUPS]
        o_ref = refs[8 + 2 * N_GROUPS]
        scratch = refs[9 + 2 * N_GROUPS:]
        tm = x_ref.shape[1]
        os, lses = [], []
        for g, dil in enumerate(dils):
            og_ref, lg_ref = group_refs[2 * g], group_refs[2 * g + 1]
            if dil == 1:
                os.append(og_ref[0, 0]); lses.append(lg_ref[0, 0])
                continue
            so_ref, sl_ref = scratch[2 * (g - 1)], scratch[2 * (g - 1) + 1]
            n_chunks = ATT_OUT_W // LANES
            for c in range(n_chunks):
                cs = slice(c * LANES, (c + 1) * LANES)
                for r in range(dil):
                    so_ref[c, pl.ds(r, tm // dil, stride=dil), :] = og_ref[0, r, :, cs]
                    sl_ref[c, pl.ds(r, tm // dil, stride=dil), :] = lg_ref[0, r, :, cs]
            os.append(jnp.concatenate([so_ref[c] for c in range(n_chunks)], axis=1))
            lses.append(jnp.concatenate([sl_ref[c] for c in range(n_chunks)], axis=1))
        mx = functools.reduce(jnp.maximum, lses)
        ws = [jnp.exp(l - mx) for l in lses]
        att = functools.reduce(jnp.add, [w * o for w, o in zip(ws, os)]) / functools.reduce(jnp.add, ws)
    ag = ag_ref[0].astype(F32)
    u = (ag * jax.nn.sigmoid(ag) * att).astype(BF16)
    o_b = _dot(u, wa_ref[...])
    o_a = _dot(ret_ref[0].astype(BF16), wr_ref[...])
    merged = jax.nn.sigmoid(ga_ref[0].astype(F32)) * o_a + jax.nn.sigmoid(gb_ref[0].astype(F32)) * o_b
    o_ref[0] = x_ref[0] + _dot(merged.astype(BF16), wo_ref[...])


def _out_proj(x, ret, gates, wr, wa, wo, tm, att=None, groups=None):
    b, t, _ = x.shape
    row = lambda w, c: pl.BlockSpec((1, tm, w), lambda bi, i: (bi, i, c))
    full = lambda a: pl.BlockSpec(a.shape, lambda bi, i: (0, 0))
    in_specs = [row(D_MODEL, 0), row(RET_V_W, 0), row(D_MODEL, 0), row(D_MODEL, 1),
                row(ATT_OUT_W, 2 * D_MODEL // ATT_OUT_W), full(wr), full(wa), full(wo)]
    args = [x, ret, gates, gates, gates, wr, wa, wo]
    scratch = []
    if groups is None:
        dils = None
        in_specs.append(row(ATT_OUT_W, 0))
        args.append(att)
    else:
        dils = tuple(o.shape[1] for o, _ in groups)
        for (o, lse), d in zip(groups, dils):
            spec = pl.BlockSpec((1, d, tm // d, ATT_OUT_W), lambda bi, i: (bi, 0, i, 0))
            in_specs += [spec, spec]
            args += [o, lse]
            if d > 1:
                scratch += [pltpu.VMEM((ATT_OUT_W // LANES, tm, LANES), F32)] * 2
    return pl.pallas_call(
        functools.partial(_out_proj_kernel, dils=dils),
        grid=(b, t // tm),
        in_specs=in_specs,
        out_specs=row(D_MODEL, 0),
        out_shape=jax.ShapeDtypeStruct((b, t, D_MODEL), F32),
        scratch_shapes=scratch,
        compiler_params=pltpu.CompilerParams(
            dimension_semantics=("parallel", "parallel"), vmem_limit_bytes=VMEM_LIMIT),
        name="out_proj",
    )(*args)


def _rotary_tables(bases, offsets):
    half = RET_DK // 2
    inv = ROPE_BASE ** (-jnp.arange(half, dtype=F32) / half)
    inv2 = jnp.concatenate([inv, inv])
    sign = jnp.concatenate([-jnp.ones((half,), F32), jnp.ones((half,), F32)])
    a = bases.astype(F32)[:, None, None] * inv2
    b = offsets.astype(F32)[:, None] * inv2
    cb, sb = jnp.cos(b), jnp.sin(b)
    return jnp.cos(a), jnp.sin(a), cb, sb, sign * cb, sign * sb


def _retention_tables(c, rows):
    log_g = jnp.log1p(-(2.0 ** (-5.0 - jnp.arange(RET_HEADS, dtype=F32))))
    i = jnp.arange(c, dtype=F32)
    diff = i[:, None] - i[None, :]
    decay = jnp.where(diff[None] >= 0, jnp.exp(jnp.maximum(diff, 0.0)[None] * log_g[:, None, None]), 0.0)
    q_decay = jnp.exp((i + 1.0)[None, :] * log_g[:, None])
    k_decay = jnp.exp((c - 1.0 - i)[None, :] * log_g[:, None])
    chunk_decay = jnp.exp(c * log_g)
    dec = jnp.zeros((RET_HEADS, rows, RET_CHUNK), F32).at[:, :c, :c].set(decay)
    qd = jnp.zeros((RET_HEADS, rows, 1), F32).at[:, :c, 0].set(q_decay)
    kd = jnp.zeros((RET_HEADS, RET_CHUNK, 1), F32).at[:, :c, 0].set(k_decay)
    cd = jnp.broadcast_to(chunk_decay[:, None, None], (RET_HEADS, 1, RET_DV))
    return dec, qd, kd, cd


def _t5_bucket(dist):
    max_exact = REL_BUCKETS // 2
    d = jnp.maximum(dist.astype(F32), 1.0)
    large = max_exact + (jnp.log(d / max_exact) / math.log(REL_MAX_DIST / max_exact)
                         * (REL_BUCKETS - max_exact)).astype(jnp.int32)
    large = jnp.minimum(large, REL_BUCKETS - 1)
    return jnp.where(dist < max_exact, dist, large)


def _group_bias(rel_bias, g, dil, slots):
    dist = dil * jnp.asarray(slots, dtype=jnp.int32)
    return rel_bias[_t5_bucket(dist)][:, g * ATT_HEADS:(g + 1) * ATT_HEADS].astype(F32).T


def _neg(heads, n):
    return jnp.full((heads, n), NEG, F32)


def _dilate(v, dil):
    heads, n = v.shape
    return jnp.stack([v] + [_neg(heads, n)] * (dil - 1), axis=-1).reshape(heads, n * dil)


def _prompt_slots(tb_rev):
    heads = tb_rev.shape[0]
    return jnp.concatenate([tb_rev, _neg(heads, 4 * ATT_BLOCK - N_KEYS)], axis=1)


def _sample_slots(tb, tb_rev, win, dil):
    heads = tb.shape[0]
    s_c = jnp.concatenate([_dilate(tb_rev[:, :N_KEYS - 1], dil), _neg(heads, ATT_BLOCK)], axis=1)
    back = [tb[:, k // dil:k // dil + 1] if k % dil == 0 else _neg(heads, 1) for k in range(SAMPLE_PAD - 1, 0, -1)]
    s_n = jnp.concatenate([tb[:, 0:1], _neg(heads, 2 * ATT_BLOCK - SAMPLE_PAD)] + back, axis=1)
    return s_c, s_n


def _kv_rows_t(tail_t):
    b, _, _, rows = tail_t.shape
    return jnp.transpose(tail_t.reshape(b, 2, ATT_HEADS, ATT_DH, rows), (0, 4, 1, 2, 3))


def kernel(x_prompt, x_sample, cache_kv_w128, cache_kv_w512, cache_kv_w2048, state_retention,
           w_norm, w_in, q_norm, k_norm, rel_bias, ret_norm, w_proj_ret, w_proj_att, w_out):
    assert w_in.shape[0] == 1
    bp, t, _ = x_prompt.shape
    bs, ts, _ = x_sample.shape
    dils = tuple(d for _, d in ATT_GROUPS)
    assert t % (ATT_BLOCK * max(dils)) == 0 and ts <= SAMPLE_PAD
    caches = (cache_kv_w128[0], cache_kv_w512[0], cache_kv_w2048[0])
    for cch, (win, _) in zip(caches, ATT_GROUPS):
        assert cch.shape[1] == win and win <= PAST_LEN

    wn = w_norm[0].reshape(1, D_MODEL)
    w_in_bf = w_in[0].astype(BF16)
    qg = jnp.tile(q_norm[0] * (ATT_DH ** -0.5), TILE // ATT_DH).reshape(1, TILE)
    kg = jnp.tile(k_norm[0], TILE // ATT_DH).reshape(1, TILE)
    gn = ret_norm[0].reshape(1, RET_V_W)
    wr = w_proj_ret[0].astype(BF16)
    wa = w_proj_att[0].astype(BF16)
    wo = w_out[0].astype(BF16)
    hid = jnp.arange(TILE // 2) // ATT_DH
    seg_mean = jnp.where(hid[:, None] == hid[None, :], 1.0 / ATT_DH, 0.0).astype(BF16)
    asc, desc = tuple(range(N_KEYS)), tuple(range(N_KEYS - 1, -1, -1))
    group_bias = [(_group_bias(rel_bias, g, d, asc), _group_bias(rel_bias, g, d, desc)) for g, d in enumerate(dils)]

    tm_p = 256
    rot_p = _rotary_tables(jnp.arange(0, t, tm_p), jnp.arange(tm_p))
    tail_p_rows = tuple(min(w, t) for w, _ in ATT_GROUPS)
    ret_p, state_p, *rest = _in_proj(
        x_prompt.reshape(bp * t, D_MODEL), t, dils, tail_p_rows, BF16, wn, w_in_bf, rot_p, qg, kg, seg_mean,
        tm=tm_p, retention=(_retention_tables(RET_CHUNK, RET_CHUNK), gn))
    qkv_p, gates_p, tails_p = rest[:N_GROUPS], rest[N_GROUPS], rest[N_GROUPS + 1:]
    att_blocks = tuple(min(8, t // (d * ATT_BLOCK)) for d in dils)
    groups = [_attn_prompt_group(qkv_p[g], _prompt_slots(group_bias[g][1]), n_blocks=att_blocks[g])
              for g in range(N_GROUPS)]
    y_p = _out_proj(x_prompt, ret_p.reshape(bp, t, RET_V_W), gates_p.reshape(bp, t, GATE_W), wr, wa, wo,
                    tm=512, groups=groups)

    pad = SAMPLE_PAD
    ns = bs * pad
    xs = jnp.pad(x_sample, ((0, 0), (0, pad - ts), (0, 0))).reshape(ns, D_MODEL)
    rot_s = _rotary_tables(jnp.full((1,), PAST_LEN), jnp.tile(jnp.arange(pad), bs))
    ret_in_s, *qkv_s, gates_s = _in_proj(xs, ns, (1,) * N_GROUPS, (), F32, wn, w_in_bf, rot_s,
                                                 qg, kg, seg_mean, tm=ns)
    qkv_s = [a.reshape(bs, pad, QKV_W) for a in qkv_s]
    caches_t = [jnp.transpose(c, (0, 2, 3, 4, 1)) for c in caches]
    ss = [_sample_slots(*group_bias[g], win, d) for g, (win, d) in enumerate(ATT_GROUPS)]
    ret_s, state_s, att_s, *kv_s = _sample_mixers(
        ret_in_s.reshape(bs, pad, RET_OUT_W), state_retention[0], _retention_tables(ts, pad), gn,
        qkv_s, caches_t, [c for c, _ in ss], jnp.stack([n for _, n in ss]), ts)
    y_s = _out_proj(xs.reshape(1, ns, D_MODEL), ret_s.reshape(1, ns, RET_V_W), gates_s.reshape(1, ns, GATE_W),
                    wr, wa, wo, tm=ns, att=att_s.reshape(1, ns, ATT_OUT_W))
    y_s = y_s.reshape(bs, pad, D_MODEL)[:, :ts]

    kv_p = [_kv_rows_t(tt)[None] for tt in tails_p]
    return (y_p, y_s, state_p[None], state_s[None], kv_p[0], kv_p[1], kv_p[2],
            kv_s[0][None], kv_s[1][None], kv_s[2][None])
```

```python
import functools
import math

import jax
import jax.numpy as jnp
from jax import lax
from jax.experimental import pallas as pl
from jax.experimental.pallas import tpu as pltpu

D_MODEL = 1024
PAST_LEN = 16384
RET_HEADS = 4
RET_DK = 128
RET_DV = 256
RET_CHUNK = 128
ROPE_BASE = 10000.0
ATT_GROUPS = ((128, 1), (512, 4), (2048, 16))
N_GROUPS = 3
ATT_HEADS = 8
ATT_DH = 64
REL_BUCKETS = 32
REL_MAX_DIST = 2048
EPS = 1e-6

RET_QK_W = RET_HEADS * RET_DK
RET_V_W = RET_HEADS * RET_DV
ATT_W = N_GROUPS * ATT_HEADS * ATT_DH
ATT_OUT_W = ATT_HEADS * ATT_DH
IN_W = 2 * RET_QK_W + 2 * RET_V_W + 3 * ATT_W + ATT_OUT_W + 2 * D_MODEL

LANES = 128
TILE = 512
RET_OUT_W = 2 * RET_QK_W + 2 * RET_V_W
QKV_W = 3 * ATT_OUT_W
GATE_W = 2 * D_MODEL + ATT_OUT_W
ATT_BLOCK = 128
N_KEYS = ATT_BLOCK + 1
SAMPLE_PAD = 8
NEG = -1e30
VMEM_LIMIT = 48 * 1024 * 1024

F32 = jnp.float32
BF16 = jnp.bfloat16


def _nt_dot(a, b):
    return lax.dot_general(a, b, (((1,), (1,)), ((), ())), preferred_element_type=F32)


def _dot(a, b):
    return jnp.dot(a, b, preferred_element_type=F32)


def _proj_schedule():
    sched = [(0, 0, 0, "rot_q"), (RET_QK_W, 0, RET_QK_W, "rot_k")]
    for k in range(2 * RET_V_W // TILE):
        sched.append((2 * RET_QK_W + k * TILE, 0, 2 * RET_QK_W + k * TILE, "plain"))
    att0 = 2 * RET_QK_W + 2 * RET_V_W
    for g in range(N_GROUPS):
        for kind, epi in enumerate(("norm_q", "norm_k", "plain")):
            sched.append((att0 + kind * ATT_W + g * ATT_OUT_W, 1 + g, kind * ATT_OUT_W, epi))
    gate0 = att0 + 3 * ATT_W
    for k in range(2 * D_MODEL // TILE):
        sched.append((gate0 + ATT_OUT_W + k * TILE, 4, k * TILE, "plain"))
    sched.append((gate0, 4, 2 * D_MODEL, "plain"))
    return sched


_SCHEDULE = _proj_schedule()


def _in_proj_kernel(*refs, dils, per_seq, fuse_retention, n_tails):
    (x_ref, wn_ref, w_ref, ca_ref, sa_ref, cb_ref, sb_ref, cbs_ref, sbs_ref,
     qg_ref, kg_ref, seg_ref) = refs[:12]
    refs = refs[12:]
    if fuse_retention:
        dec_ref, qd_ref, kd_ref, cd_ref, gn_ref, ret_ref, s_ref = refs[:7]
        refs = refs[7:]
    else:
        ret_ref = refs[0]
        refs = refs[1:]
    a0_ref, a1_ref, a2_ref, gate_ref = refs[:4]
    tail_refs = refs[4:4 + n_tails]
    h_ref, y_ref = refs[4 + n_tails:6 + n_tails]
    refs = refs[6 + n_tails:]
    r_ref = refs[0] if fuse_retention else ret_ref
    kt_ref = refs[1] if fuse_retention else None
    out_refs = (r_ref, a0_ref, a1_ref, a2_ref, gate_ref)

    if fuse_retention:
        @pl.when(pl.program_id(0) % per_seq == 0)
        def _():
            s_ref[...] = jnp.zeros_like(s_ref)

    x = x_ref[...]
    tm = x.shape[0]
    ms = jnp.mean(x * x, axis=-1, keepdims=True)
    h_ref[...] = (x * lax.rsqrt(ms + EPS) * wn_ref[...]).astype(BF16)

    ca, sa = ca_ref[0], sa_ref[0]
    cos = ca * cb_ref[...] - sa * sb_ref[...]
    sin = sa * cbs_ref[...] + ca * sbs_ref[...]

    def rotary(y, scale):
        parts = []
        for hh in range(TILE // RET_DK):
            yh = y[:, hh * RET_DK:(hh + 1) * RET_DK]
            parts.append((yh * cos + pltpu.roll(yh, RET_DK // 2, axis=1) * sin) * scale)
        return jnp.concatenate(parts, axis=1)

    def head_rms(y, gain):
        y2 = (y * y).astype(BF16)
        half = TILE // 2
        ms = jnp.concatenate([_dot(y2[:, :half], seg_ref[...]), _dot(y2[:, half:], seg_ref[...])], axis=1)
        return y * lax.rsqrt(ms + EPS) * gain

    epilogues = {
        "plain": lambda y: y,
        "rot_q": lambda y: rotary(y, 1.0),
        "rot_k": lambda y: rotary(y, RET_DK ** -0.5),
        "norm_q": lambda y: head_rms(y, qg_ref[...]),
        "norm_k": lambda y: head_rms(y, kg_ref[...]),
    }

    kt_base = jnp.minimum(pl.program_id(0), 0)

    def retention_issue(c, h):
        rows = slice(c * RET_CHUNK, (c + 1) * RET_CHUNK)
        ks = slice(h * RET_DK, (h + 1) * RET_DK)
        ks2 = slice(RET_QK_W + h * RET_DK, RET_QK_W + (h + 1) * RET_DK)
        vs_in = slice(2 * RET_QK_W + h * RET_DV, 2 * RET_QK_W + (h + 1) * RET_DV)
        qb = r_ref[rows, ks].astype(BF16)
        vb = r_ref[rows, vs_in].astype(BF16)
        state = s_ref[0, h]
        scores = (_nt_dot(qb, r_ref[rows, ks2].astype(BF16)) * dec_ref[h]).astype(BF16)
        carried = _dot(qb, state.astype(BF16)) * qd_ref[h]
        s_ref[0, h] = state * cd_ref[h] + _dot(kt_ref[kt_base + c, ks, :], vb)
        return rows, h, scores, vb, carried

    def retention_finish(rows, h, scores, vb, carried):
        vs = slice(h * RET_DV, (h + 1) * RET_DV)
        gs_in = slice(2 * RET_QK_W + RET_V_W + h * RET_DV, 2 * RET_QK_W + RET_V_W + (h + 1) * RET_DV)
        o = _dot(scores, vb) + carried
        ret_ref[rows, vs] = _group_norm_gate(o, gn_ref[:, vs], r_ref[rows, gs_in]).astype(ret_ref.dtype)

    def store_tile(val, out_idx, out_col):
        o_ref = out_refs[out_idx]
        ocs = slice(out_col, out_col + TILE)
        if not 1 <= out_idx <= N_GROUPS:
            o_ref[:, ocs] = val.astype(o_ref.dtype)
            return
        g, kind = out_idx - 1, out_col // ATT_OUT_W
        if kind > 0 and tail_refs:
            t_ref = tail_refs[g]
            t_ref[0, kind - 1] = jnp.transpose(val)[:, tm - t_ref.shape[3]:]
        dil = dils[g]
        if dil == 1:
            o_ref[0, 0, :, ocs] = val.astype(o_ref.dtype)
            return
        for c in range(TILE // LANES):
            y_ref[c] = val[:, c * LANES:(c + 1) * LANES]
            for r in range(dil):
                o_ref[0, r, :, out_col + c * LANES:out_col + (c + 1) * LANES] = (
                    y_ref[c, pl.ds(r, tm // dil, stride=dil), :].astype(o_ref.dtype))

    n_ret_tiles = sum(1 for t in _SCHEDULE if t[1] == 0)
    units = [(c, h) for c in range(tm // RET_CHUNK) for h in range(RET_HEADS)] if fuse_retention else []
    assert len(units) <= len(_SCHEDULE) - n_ret_tiles
    for step, (w_col, out_idx, out_col, epi) in enumerate(_SCHEDULE):
        if fuse_retention and step == 2:
            for c, h in units:
                rows = slice(c * RET_CHUNK, (c + 1) * RET_CHUNK)
                kt_ref[c, h * RET_DK:(h + 1) * RET_DK, :] = _decayed_keys_t(
                    r_ref[rows, RET_QK_W + h * RET_DK:RET_QK_W + (h + 1) * RET_DK], kd_ref[h])
        unit = units[step - n_ret_tiles] if 0 <= step - n_ret_tiles < len(units) else None
        issued = retention_issue(*unit) if unit else None
        store_tile(epilogues[epi](_dot(h_ref[...], w_ref[:, w_col:w_col + TILE])), out_idx, out_col)
        if unit:
            retention_finish(*issued)


def _in_proj(x2d, seq_len, dils, tail_rows, out_dtype, wn, w_bf, rot, qg, kg, seg, tm, retention=None):
    n = x2d.shape[0]
    batch = n // seq_len
    per_seq = seq_len // tm
    fuse = retention is not None
    const = lambda a: pl.BlockSpec(a.shape, lambda i, nd=a.ndim: (0,) * nd)
    base_spec = pl.BlockSpec((1, 1, RET_DK), lambda i: (i % per_seq, 0, 0))
    qkv_spec = lambda d: pl.BlockSpec((1, d, tm // d, QKV_W), lambda i: (i // per_seq, 0, i % per_seq, 0))
    tail_specs, tail_shapes = [], []
    for rows in tail_rows:
        width = min(tm, rows)
        first = (seq_len - rows) // tm if rows >= tm else per_seq
        tail_specs.append(pl.BlockSpec(
            (1, 2, ATT_OUT_W, width),
            lambda i, first=first: (i // per_seq, 0, 0, jnp.maximum(i % per_seq - first, 0))))
        tail_shapes.append(jax.ShapeDtypeStruct((batch, 2, ATT_OUT_W, rows), F32))
    ca, sa, *offset_tables = rot
    args = [x2d, wn, w_bf, ca, sa, *offset_tables, qg, kg, seg]
    in_specs = [
        pl.BlockSpec((tm, D_MODEL), lambda i: (i, 0)),
        const(wn),
        pl.BlockSpec(w_bf.shape, lambda i: (0, 0), pipeline_mode=pl.Buffered(1)),
        base_spec, base_spec, *[const(t) for t in offset_tables],
        const(qg), const(kg), const(seg),
    ]
    scratch = [pltpu.VMEM((tm, D_MODEL), BF16), pltpu.VMEM((TILE // LANES, tm, LANES), F32)]
    if fuse:
        tables, gn = retention
        args += [*tables, gn]
        in_specs += [const(t) for t in (*tables, gn)]
        ret_specs = [pl.BlockSpec((tm, RET_V_W), lambda i: (i, 0)),
                     pl.BlockSpec((1, RET_HEADS, RET_DK, RET_DV), lambda i: (i // per_seq, 0, 0, 0))]
        ret_shapes = [jax.ShapeDtypeStruct((n, RET_V_W), out_dtype),
                      jax.ShapeDtypeStruct((batch, RET_HEADS, RET_DK, RET_DV), F32)]
        scratch += [pltpu.VMEM((tm, RET_OUT_W), F32), pltpu.VMEM((tm // RET_CHUNK, RET_QK_W, RET_CHUNK), BF16)]
    else:
        ret_specs = [pl.BlockSpec((tm, RET_OUT_W), lambda i: (i, 0))]
        ret_shapes = [jax.ShapeDtypeStruct((n, RET_OUT_W), out_dtype)]
    return pl.pallas_call(
        functools.partial(_in_proj_kernel, dils=dils, per_seq=per_seq, fuse_retention=fuse,
                          n_tails=len(tail_rows)),
        grid=(n // tm,),
        in_specs=in_specs,
        out_specs=ret_specs + [qkv_spec(d) for d in dils]
        + [pl.BlockSpec((tm, GATE_W), lambda i: (i, 0))] + tail_specs,
        out_shape=ret_shapes
        + [jax.ShapeDtypeStruct((batch, d, seq_len // d, QKV_W), out_dtype) for d in dils]
        + [jax.ShapeDtypeStruct((n, GATE_W), out_dtype)] + tail_shapes,
        scratch_shapes=scratch,
        compiler_params=pltpu.CompilerParams(
            dimension_semantics=("arbitrary",), vmem_limit_bytes=VMEM_LIMIT),
        name="in_proj",
    )(*args)


def _group_norm_gate(o, gain, gate):
    gate = gate.astype(F32)
    mu = jnp.mean(o, axis=-1, keepdims=True)
    d = o - mu
    var = jnp.mean(d * d, axis=-1, keepdims=True)
    return gate * jax.nn.sigmoid(gate) * (d * lax.rsqrt(var + EPS) * gain)


def _decayed_keys_t(k_keys, kd):
    return jnp.transpose(k_keys * kd).astype(BF16)


def _retention_specs(rows, idx):
    return [
        pl.BlockSpec((1, rows, RET_QK_W), lambda *a: (*idx(*a), 0)),
        pl.BlockSpec((1, rows, RET_QK_W), lambda *a: (*idx(*a), 1)),
        pl.BlockSpec((1, rows, RET_V_W), lambda *a: (*idx(*a), 1)),
        pl.BlockSpec((1, rows, RET_V_W), lambda *a: (*idx(*a), 2)),
    ]


def _table_specs(tables, ndim_grid):
    return [pl.BlockSpec(t.shape, lambda *a, nd=t.ndim: (0,) * nd) for t in tables]


def _retention_sample_kernel(q_ref, k_ref, v_ref, g_ref, s_in_ref, dec_ref, qd_ref, kd_ref, cd_ref,
                             gn_ref, o_ref, s_out_ref, kpad_ref, vpad_ref):
    @pl.when(pl.program_id(0) == 0)
    def _():
        kpad_ref[...] = jnp.zeros_like(kpad_ref)
        vpad_ref[...] = jnp.zeros_like(vpad_ref)

    kpad_ref[0:SAMPLE_PAD, :] = k_ref[0]
    vpad_ref[0:SAMPLE_PAD, :] = v_ref[0]
    issued = []
    for h in range(RET_HEADS):
        ks = slice(h * RET_DK, (h + 1) * RET_DK)
        vs = slice(h * RET_DV, (h + 1) * RET_DV)
        qb = q_ref[0, :, ks].astype(BF16)
        vb = vpad_ref[:, vs].astype(BF16)
        state = s_in_ref[0, h]
        scores = (_nt_dot(qb, kpad_ref[:, ks].astype(BF16)) * dec_ref[h]).astype(BF16)
        carried = _dot(qb, state.astype(BF16)) * qd_ref[h]
        s_out_ref[0, h] = state * cd_ref[h] + _dot(_decayed_keys_t(kpad_ref[:, ks], kd_ref[h]), vb)
        issued.append((scores, vb, carried))
    for h, (scores, vb, carried) in enumerate(issued):
        vs = slice(h * RET_DV, (h + 1) * RET_DV)
        o = _dot(scores, vb) + carried
        o_ref[0, :, vs] = _group_norm_gate(o, gn_ref[:, vs], g_ref[0, :, vs]).astype(o_ref.dtype)


def _attn_prompt_blocks(q_ref, kp_ref, kc_ref, vp_ref, vc_ref, bias_ref, o_ref, lse_ref, first):
    n_seq, n_blocks = q_ref.shape[1], q_ref.shape[2] // ATT_BLOCK
    assert first or n_seq == 1
    pair_w = 2 * ATT_DH
    low_q = lax.broadcasted_iota(jnp.int32, (ATT_BLOCK, pair_w), 1) < ATT_DH
    rows0, rows1 = slice(0, ATT_BLOCK), slice(ATT_BLOCK, 2 * ATT_BLOCK)

    def keys(prev_ref, cur_ref, sq, t, ps):
        if t > 0:
            return cur_ref[0, sq, (t - 1) * ATT_BLOCK:(t + 1) * ATT_BLOCK, ps]
        if first:
            return cur_ref[0, sq, 0:ATT_BLOCK, ps]
        return jnp.concatenate([prev_ref[0, sq, :, ps], cur_ref[0, sq, 0:ATT_BLOCK, ps]], axis=0)

    staged = []
    for sq, t in [(sq, t) for sq in range(n_seq) for t in range(n_blocks)]:
        for p in range(ATT_HEADS // 2):
            ps = slice(p * pair_w, (p + 1) * pair_w)
            qp = q_ref[0, sq, t * ATT_BLOCK:(t + 1) * ATT_BLOCK, ps]
            kcat = keys(kp_ref, kc_ref, sq, t, ps)
            q2 = jnp.concatenate([jnp.where(low_q, qp, 0.0), jnp.where(low_q, 0.0, qp)], axis=0)
            s = _nt_dot(q2, kcat) + bias_ref[p, :, 2 * ATT_BLOCK - kcat.shape[0]:]
            s = s.astype(BF16)
            m = jnp.max(s, axis=-1, keepdims=True)
            staged.append((sq, t, p, m.astype(F32), jnp.exp(s - m)))
    for sq, t, p, m, e in staged:
        ps = slice(p * pair_w, (p + 1) * pair_w)
        vcat = keys(vp_ref, vc_ref, sq, t, ps)
        low_k = lax.broadcasted_iota(jnp.int32, vcat.shape, 1) < ATT_DH
        p0 = _dot(e[rows0], jnp.where(low_k, vcat, 1.0))
        p1 = _dot(e[rows1], jnp.where(low_k, 1.0, vcat))
        den = pltpu.roll(jnp.where(low_q, p1, p0), ATT_DH, axis=1)
        ts = slice(t * ATT_BLOCK, (t + 1) * ATT_BLOCK)
        o_ref[0, sq, ts, ps] = jnp.where(low_q, p0, p1) / den
        lse_ref[0, sq, ts, ps] = jnp.where(low_q, m[rows0], m[rows1]) + jnp.log(den)


def _attn_prompt_kernel(q_ref, kp_ref, kc_ref, vp_ref, vc_ref, slot_ref, o_ref, lse_ref, bias_ref,
                        *, whole_sequences):
    first = pl.program_id(2) == 0

    @pl.when(jnp.logical_and(first, jnp.logical_and(pl.program_id(0) == 0, pl.program_id(1) == 0)))
    def _():
        for h in range(ATT_HEADS):
            slots = jnp.broadcast_to(slot_ref[h:h + 1, :], (ATT_BLOCK, slot_ref.shape[1]))
            rows = pltpu.roll(slots, 0, 1, stride=1, stride_axis=0)
            bias_ref[h // 2, (h % 2) * ATT_BLOCK:(h % 2 + 1) * ATT_BLOCK, :] = rows[:, :2 * ATT_BLOCK]

    if whole_sequences:
        _attn_prompt_blocks(q_ref, kp_ref, kc_ref, vp_ref, vc_ref, bias_ref, o_ref, lse_ref, True)
        return

    @pl.when(first)
    def _():
        _attn_prompt_blocks(q_ref, kp_ref, kc_ref, vp_ref, vc_ref, bias_ref, o_ref, lse_ref, True)

    @pl.when(jnp.logical_not(first))
    def _():
        _attn_prompt_blocks(q_ref, kp_ref, kc_ref, vp_ref, vc_ref, bias_ref, o_ref, lse_ref, False)


def _attn_prompt_group(qkv, slot_bias, step_blocks):
    b, dil, tr, _ = qkv.shape
    n_blocks = min(step_blocks, tr // ATT_BLOCK)
    n_seq = min(step_blocks // n_blocks, dil)
    rows = n_blocks * ATT_BLOCK
    cur = lambda c: pl.BlockSpec((1, n_seq, rows, ATT_OUT_W), lambda bi, r, j: (bi, r, j, c))
    prev = lambda c: pl.BlockSpec((1, n_seq, ATT_BLOCK, ATT_OUT_W),
                                  lambda bi, r, j: (bi, r, jnp.maximum(j * n_blocks - 1, 0), c))
    res_shape = jax.ShapeDtypeStruct((b, dil, tr, ATT_OUT_W), F32)
    return pl.pallas_call(
        functools.partial(_attn_prompt_kernel, whole_sequences=tr == rows),
        grid=(b, dil // n_seq, tr // rows),
        in_specs=[cur(0), prev(1), cur(1), prev(2), cur(2),
                  pl.BlockSpec(slot_bias.shape, lambda bi, r, j: (0, 0))],
        out_specs=[cur(0), cur(0)],
        out_shape=[res_shape, res_shape],
        scratch_shapes=[pltpu.VMEM((ATT_HEADS // 2, 2 * ATT_BLOCK, 2 * ATT_BLOCK), F32)],
        compiler_params=pltpu.CompilerParams(
            dimension_semantics=("arbitrary", "arbitrary", "arbitrary"), vmem_limit_bytes=VMEM_LIMIT),
        name=f"attn_prompt_d{dil}",
    )(qkv, qkv, qkv, qkv, qkv, slot_bias)


def _attn_sample_kernel(a0_ref, a1_ref, a2_ref, c0_ref, c1_ref, c2_ref, s0_ref, s1_ref, s2_ref, sn_ref,
                        o_ref, kn_ref, vn_ref, b0_ref, b1_ref, b2_ref, bn_ref):
    qkv_refs = (a0_ref, a1_ref, a2_ref)
    cache_refs = (c0_ref, c1_ref, c2_ref)
    bias_refs = (b0_ref, b1_ref, b2_ref)

    @pl.when(pl.program_id(0) == 0)
    def _():
        kn_ref[...] = jnp.zeros_like(kn_ref)
        vn_ref[...] = jnp.zeros_like(vn_ref)
        def rotated_rows(vec, width):
            rows = jnp.broadcast_to(vec, (SAMPLE_PAD, vec.shape[1]))
            return pltpu.roll(rows, 0, 1, stride=1, stride_axis=0)[:, :width]
        for g, (s_ref, b_ref) in enumerate(zip((s0_ref, s1_ref, s2_ref), bias_refs)):
            for h in range(ATT_HEADS):
                b_ref[h] = rotated_rows(s_ref[h:h + 1, :], b_ref.shape[2])
                bn_ref[g, h] = rotated_rows(sn_ref[g, h:h + 1, :], ATT_BLOCK)

    for g in range(N_GROUPS):
        gs = slice(g * ATT_OUT_W, (g + 1) * ATT_OUT_W)
        kn_ref[0:SAMPLE_PAD, gs] = qkv_refs[g][0, :, ATT_OUT_W:2 * ATT_OUT_W]
        vn_ref[0:SAMPLE_PAD, gs] = qkv_refs[g][0, :, 2 * ATT_OUT_W:3 * ATT_OUT_W]

    staged = []
    for h in range(ATT_HEADS):
        logits = []
        for g in range(N_GROUPS):
            hs = slice(g * ATT_OUT_W + h * ATT_DH, g * ATT_OUT_W + (h + 1) * ATT_DH)
            qh = qkv_refs[g][0, :, h * ATT_DH:(h + 1) * ATT_DH].astype(BF16)
            logits.append(_dot(qh, cache_refs[g][0, 0, h].astype(BF16)) + bias_refs[g][h])
            logits.append(_nt_dot(qh, kn_ref[:, hs].astype(BF16)) + bn_ref[g, h])
        m = functools.reduce(jnp.maximum, [jnp.max(x, axis=-1, keepdims=True) for x in logits])
        es = [jnp.exp(x - m) for x in logits]
        l = functools.reduce(jnp.add, [jnp.sum(e, axis=-1, keepdims=True) for e in es])
        staged.append(([e.astype(BF16) for e in es], l))
    for h, (es, l) in enumerate(staged):
        acc = jnp.zeros((SAMPLE_PAD, ATT_DH), F32)
        for g in range(N_GROUPS):
            hs = slice(g * ATT_OUT_W + h * ATT_DH, g * ATT_OUT_W + (h + 1) * ATT_DH)
            acc = acc + _nt_dot(es[2 * g], cache_refs[g][0, 1, h].astype(BF16))
            acc = acc + _dot(es[2 * g + 1], vn_ref[:, hs].astype(BF16))
        o_ref[0, :, h * ATT_DH:(h + 1) * ATT_DH] = acc / l


_N_RET_IN, _N_ATT_IN = 10, 10


def _sample_mixers_kernel(*refs):
    n_in = _N_RET_IN + _N_ATT_IN
    ret_in, att_in = refs[:_N_RET_IN], refs[_N_RET_IN:n_in]
    ret_o_ref, state_o_ref, att_o_ref = refs[n_in:n_in + 3]
    kv_o_refs = refs[n_in + 3:n_in + 3 + N_GROUPS]
    kpad_ref, vpad_ref, kn_ref, vn_ref = refs[n_in + 3 + N_GROUPS:n_in + 7 + N_GROUPS]
    _attn_sample_kernel(*att_in, att_o_ref, kn_ref, vn_ref, *refs[n_in + 7 + N_GROUPS:])
    _retention_sample_kernel(*ret_in, ret_o_ref, state_o_ref, kpad_ref, vpad_ref)
    for qkv_ref, kv_ref in zip(att_in[:N_GROUPS], kv_o_refs):
        for tok in range(kv_ref.shape[1]):
            for kv in range(2):
                for h in range(ATT_HEADS):
                    col = (1 + kv) * ATT_OUT_W + h * ATT_DH
                    kv_ref[0, tok, kv, h:h + 1, :] = qkv_ref[0, tok:tok + 1, col:col + ATT_DH]


def _sample_mixers(ret, state, tables, gn, qkvs, caches_t, slots, slots_new, n_tokens):
    b, p, _ = ret.shape
    state_spec = pl.BlockSpec((1, RET_HEADS, RET_DK, RET_DV), lambda bi: (bi, 0, 0, 0))
    ret_specs = _retention_specs(p, lambda bi: (bi, 0)) + [state_spec] + _table_specs(tables + (gn,), 1)
    att_specs = ([pl.BlockSpec((1, p, QKV_W), lambda bi: (bi, 0, 0)) for _ in qkvs]
                 + [pl.BlockSpec((1,) + c.shape[1:], lambda bi: (bi, 0, 0, 0, 0)) for c in caches_t]
                 + [pl.BlockSpec(x.shape, lambda bi: (0, 0)) for x in slots]
                 + [pl.BlockSpec(slots_new.shape, lambda bi: (0, 0, 0))])
    assert len(ret_specs) == _N_RET_IN and len(att_specs) == _N_ATT_IN
    kv_shape = (b, n_tokens, 2, ATT_HEADS, ATT_DH)
    return pl.pallas_call(
        _sample_mixers_kernel,
        grid=(b,),
        in_specs=ret_specs + att_specs,
        out_specs=[pl.BlockSpec((1, p, RET_V_W), lambda bi: (bi, 0, 0)), state_spec,
                   pl.BlockSpec((1, p, ATT_OUT_W), lambda bi: (bi, 0, 0))]
        + [pl.BlockSpec((1,) + kv_shape[1:], lambda bi: (bi, 0, 0, 0, 0))] * N_GROUPS,
        out_shape=[jax.ShapeDtypeStruct((b, p, RET_V_W), ret.dtype), jax.ShapeDtypeStruct(state.shape, F32),
                   jax.ShapeDtypeStruct((b, p, ATT_OUT_W), F32)]
        + [jax.ShapeDtypeStruct(kv_shape, F32)] * N_GROUPS,
        scratch_shapes=[pltpu.VMEM((RET_CHUNK, RET_QK_W), F32), pltpu.VMEM((RET_CHUNK, RET_V_W), F32),
                        pltpu.VMEM((ATT_BLOCK, ATT_W), F32), pltpu.VMEM((ATT_BLOCK, ATT_W), F32)]
        + [pltpu.VMEM((ATT_HEADS, p, c.shape[-1]), F32) for c in caches_t]
        + [pltpu.VMEM((N_GROUPS, ATT_HEADS, p, ATT_BLOCK), F32)],
        compiler_params=pltpu.CompilerParams(
            dimension_semantics=("arbitrary",), vmem_limit_bytes=VMEM_LIMIT),
        name="sample_mixers",
    )(ret, ret, ret, ret, state, *tables, gn, *qkvs, *caches_t, *slots, slots_new)


def _out_proj_kernel(*refs, dils):
    x_ref, ret_ref, ga_ref, gb_ref, ag_ref, wr_ref, wa_ref, wo_ref = refs[:8]
    if dils is None:
        att_ref, o_ref = refs[8:]
        att = att_ref[0]
    else:
        group_refs = refs[8:8 + 2 * N_GROUPS]
        o_ref = refs[8 + 2 * N_GROUPS]
        scratch = refs[9 + 2 * N_GROUPS:]
        tm = x_ref.shape[1]
        os, lses = [], []
        for g, dil in enumerate(dils):
            og_ref, lg_ref = group_refs[2 * g], group_refs[2 * g + 1]
            if dil == 1:
                os.append(og_ref[0, 0]); lses.append(lg_ref[0, 0])
                continue
            so_ref, sl_ref = scratch[2 * (g - 1)], scratch[2 * (g - 1) + 1]
            n_chunks = ATT_OUT_W // LANES
            for c in range(n_chunks):
                cs = slice(c * LANES, (c + 1) * LANES)
                for r in range(dil):
                    so_ref[c, pl.ds(r, tm // dil, stride=dil), :] = og_ref[0, r, :, cs]
                    sl_ref[c, pl.ds(r, tm // dil, stride=dil), :] = lg_ref[0, r, :, cs]
            os.append(jnp.concatenate([so_ref[c] for c in range(n_chunks)], axis=1))
            lses.append(jnp.concatenate([sl_ref[c] for c in range(n_chunks)], axis=1))
        mx = functools.reduce(jnp.maximum, lses)
        ws = [jnp.exp(l - mx) for l in lses]
        att = functools.reduce(jnp.add, [w * o for w, o in zip(ws, os)]) / functools.reduce(jnp.add, ws)
    ag = ag_ref[0].astype(F32)
    u = (ag * jax.nn.sigmoid(ag) * att).astype(BF16)
    o_b = _dot(u, wa_ref[...])
    o_a = _dot(ret_ref[0].astype(BF16), wr_ref[...])
    merged = jax.nn.sigmoid(ga_ref[0].astype(F32)) * o_a + jax.nn.sigmoid(gb_ref[0].astype(F32)) * o_b
    o_ref[0] = x_ref[0] + _dot(merged.astype(BF16), wo_ref[...])


def _out_proj(x, ret, gates, wr, wa, wo, tm, att=None, groups=None):
    b, t, _ = x.shape
    row = lambda w, c: pl.BlockSpec((1, tm, w), lambda bi, i: (bi, i, c))
    full = lambda a: pl.BlockSpec(a.shape, lambda bi, i: (0, 0))
    in_specs = [row(D_MODEL, 0), row(RET_V_W, 0), row(D_MODEL, 0), row(D_MODEL, 1),
                row(ATT_OUT_W, 2 * D_MODEL // ATT_OUT_W), full(wr), full(wa), full(wo)]
    args = [x, ret, gates, gates, gates, wr, wa, wo]
    scratch = []
    if groups is None:
        dils = None
        in_specs.append(row(ATT_OUT_W, 0))
        args.append(att)
    else:
        dils = tuple(o.shape[1] for o, _ in groups)
        for (o, lse), d in zip(groups, dils):
            spec = pl.BlockSpec((1, d, tm // d, ATT_OUT_W), lambda bi, i: (bi, 0, i, 0))
            in_specs += [spec, spec]
            args += [o, lse]
            if d > 1:
                scratch += [pltpu.VMEM((ATT_OUT_W // LANES, tm, LANES), F32)] * 2
    return pl.pallas_call(
        functools.partial(_out_proj_kernel, dils=dils),
        grid=(b, t // tm),
        in_specs=in_specs,
        out_specs=row(D_MODEL, 0),
        out_shape=jax.ShapeDtypeStruct((b, t, D_MODEL), F32),
        scratch_shapes=scratch,
        compiler_params=pltpu.CompilerParams(
            dimension_semantics=("parallel", "parallel"), vmem_limit_bytes=VMEM_LIMIT),
        name="out_proj",
    )(*args)


def _rotary_tables(bases, offsets):
    half = RET_DK // 2
    inv = ROPE_BASE ** (-jnp.arange(half, dtype=F32) / half)
    inv2 = jnp.concatenate([inv, inv])
    sign = jnp.concatenate([-jnp.ones((half,), F32), jnp.ones((half,), F32)])
    a = bases.astype(F32)[:, None, None] * inv2
    b = offsets.astype(F32)[:, None] * inv2
    cb, sb = jnp.cos(b), jnp.sin(b)
    return jnp.cos(a), jnp.sin(a), cb, sb, sign * cb, sign * sb


def _retention_tables(c, rows):
    log_g = jnp.log1p(-(2.0 ** (-5.0 - jnp.arange(RET_HEADS, dtype=F32))))
    i = jnp.arange(c, dtype=F32)
    diff = i[:, None] - i[None, :]
    decay = jnp.where(diff[None] >= 0, jnp.exp(jnp.maximum(diff, 0.0)[None] * log_g[:, None, None]), 0.0)
    q_decay = jnp.exp((i + 1.0)[None, :] * log_g[:, None])
    k_decay = jnp.exp((c - 1.0 - i)[None, :] * log_g[:, None])
    chunk_decay = jnp.exp(c * log_g)
    dec = jnp.zeros((RET_HEADS, rows, RET_CHUNK), F32).at[:, :c, :c].set(decay)
    qd = jnp.zeros((RET_HEADS, rows, 1), F32).at[:, :c, 0].set(q_decay)
    kd = jnp.zeros((RET_HEADS, RET_CHUNK, 1), F32).at[:, :c, 0].set(k_decay)
    cd = jnp.broadcast_to(chunk_decay[:, None, None], (RET_HEADS, 1, RET_DV))
    return dec, qd, kd, cd


def _t5_bucket(dist):
    max_exact = REL_BUCKETS // 2
    d = jnp.maximum(dist.astype(F32), 1.0)
    large = max_exact + (jnp.log(d / max_exact) / math.log(REL_MAX_DIST / max_exact)
                         * (REL_BUCKETS - max_exact)).astype(jnp.int32)
    large = jnp.minimum(large, REL_BUCKETS - 1)
    return jnp.where(dist < max_exact, dist, large)


def _group_bias(rel_bias, g, dil, slots):
    dist = dil * jnp.asarray(slots, dtype=jnp.int32)
    return rel_bias[_t5_bucket(dist)][:, g * ATT_HEADS:(g + 1) * ATT_HEADS].astype(F32).T


def _neg(heads, n):
    return jnp.full((heads, n), NEG, F32)


def _dilate(v, dil):
    heads, n = v.shape
    return jnp.stack([v] + [_neg(heads, n)] * (dil - 1), axis=-1).reshape(heads, n * dil)


def _prompt_slots(tb_rev):
    heads = tb_rev.shape[0]
    return jnp.concatenate([tb_rev, _neg(heads, 4 * ATT_BLOCK - N_KEYS)], axis=1)


def _sample_slots(tb, tb_rev, win, dil):
    heads = tb.shape[0]
    s_c = jnp.concatenate([_dilate(tb_rev[:, :N_KEYS - 1], dil), _neg(heads, ATT_BLOCK)], axis=1)
    back = [tb[:, k // dil:k // dil + 1] if k % dil == 0 else _neg(heads, 1) for k in range(SAMPLE_PAD - 1, 0, -1)]
    s_n = jnp.concatenate([tb[:, 0:1], _neg(heads, 2 * ATT_BLOCK - SAMPLE_PAD)] + back, axis=1)
    return s_c, s_n


def _kv_rows_t(tail_t):
    b, _, _, rows = tail_t.shape
    return jnp.transpose(tail_t.reshape(b, 2, ATT_HEADS, ATT_DH, rows), (0, 4, 1, 2, 3))


def kernel(x_prompt, x_sample, cache_kv_w128, cache_kv_w512, cache_kv_w2048, state_retention,
           w_norm, w_in, q_norm, k_norm, rel_bias, ret_norm, w_proj_ret, w_proj_att, w_out):
    assert w_in.shape[0] == 1
    bp, t, _ = x_prompt.shape
    bs, ts, _ = x_sample.shape
    dils = tuple(d for _, d in ATT_GROUPS)
    assert t % (ATT_BLOCK * max(dils)) == 0 and ts <= SAMPLE_PAD
    caches = (cache_kv_w128[0], cache_kv_w512[0], cache_kv_w2048[0])
    for cch, (win, _) in zip(caches, ATT_GROUPS):
        assert cch.shape[1] == win and win <= PAST_LEN

    wn = w_norm[0].reshape(1, D_MODEL)
    w_in_bf = w_in[0].astype(BF16)
    qg = jnp.tile(q_norm[0] * (ATT_DH ** -0.5), TILE // ATT_DH).reshape(1, TILE)
    kg = jnp.tile(k_norm[0], TILE // ATT_DH).reshape(1, TILE)
    gn = ret_norm[0].reshape(1, RET_V_W)
    wr = w_proj_ret[0].astype(BF16)
    wa = w_proj_att[0].astype(BF16)
    wo = w_out[0].astype(BF16)
    hid = jnp.arange(TILE // 2) // ATT_DH
    seg_mean = jnp.where(hid[:, None] == hid[None, :], 1.0 / ATT_DH, 0.0).astype(BF16)
    asc, desc = tuple(range(N_KEYS)), tuple(range(N_KEYS - 1, -1, -1))
    group_bias = [(_group_bias(rel_bias, g, d, asc), _group_bias(rel_bias, g, d, desc)) for g, d in enumerate(dils)]

    tm_p = 256
    rot_p = _rotary_tables(jnp.arange(0, t, tm_p), jnp.arange(tm_p))
    tail_p_rows = tuple(min(w, t) for w, _ in ATT_GROUPS)
    ret_p, state_p, *rest = _in_proj(
        x_prompt.reshape(bp * t, D_MODEL), t, dils, tail_p_rows, BF16, wn, w_in_bf, rot_p, qg, kg, seg_mean,
        tm=tm_p, retention=(_retention_tables(RET_CHUNK, RET_CHUNK), gn))
    qkv_p, gates_p, tails_p = rest[:N_GROUPS], rest[N_GROUPS], rest[N_GROUPS + 1:]
    groups = [_attn_prompt_group(qkv_p[g], _prompt_slots(group_bias[g][1]), step_blocks=8)
              for g in range(N_GROUPS)]
    y_p = _out_proj(x_prompt, ret_p.reshape(bp, t, RET_V_W), gates_p.reshape(bp, t, GATE_W), wr, wa, wo,
                    tm=512, groups=groups)

    pad = SAMPLE_PAD
    ns = bs * pad
    xs = jnp.pad(x_sample, ((0, 0), (0, pad - ts), (0, 0))).reshape(ns, D_MODEL)
    rot_s = _rotary_tables(jnp.full((1,), PAST_LEN), jnp.tile(jnp.arange(pad), bs))
    ret_in_s, *qkv_s, gates_s = _in_proj(xs, ns, (1,) * N_GROUPS, (), F32, wn, w_in_bf, rot_s,
                                                 qg, kg, seg_mean, tm=ns)
    qkv_s = [a.reshape(bs, pad, QKV_W) for a in qkv_s]
    caches_t = [jnp.transpose(c, (0, 2, 3, 4, 1)) for c in caches]
    ss = [_sample_slots(*group_bias[g], win, d) for g, (win, d) in enumerate(ATT_GROUPS)]
    ret_s, state_s, att_s, *kv_s = _sample_mixers(
        ret_in_s.reshape(bs, pad, RET_OUT_W), state_retention[0], _retention_tables(ts, pad), gn,
        qkv_s, caches_t, [c for c, _ in ss], jnp.stack([n for _, n in ss]), ts)
    y_s = _out_proj(xs.reshape(1, ns, D_MODEL), ret_s.reshape(1, ns, RET_V_W), gates_s.reshape(1, ns, GATE_W),
                    wr, wa, wo, tm=ns, att=att_s.reshape(1, ns, ATT_OUT_W))
    y_s = y_s.reshape(bs, pad, D_MODEL)[:, :ts]

    kv_p = [_kv_rows_t(tt)[None] for tt in tails_p]
    return (y_p, y_s, state_p[None], state_s[None], kv_p[0], kv_p[1], kv_p[2],
            kv_s[0][None], kv_s[1][None], kv_s[2][None])
```

```python
import functools
import math

import jax
import jax.numpy as jnp
from jax import lax
from jax.experimental import pallas as pl
from jax.experimental.pallas import tpu as pltpu

D_MODEL = 1024
PAST_LEN = 16384
RET_HEADS = 4
RET_DK = 128
RET_DV = 256
RET_CHUNK = 128
ROPE_BASE = 10000.0
ATT_GROUPS = ((128, 1), (512, 4), (2048, 16))
N_GROUPS = 3
ATT_HEADS = 8
ATT_DH = 64
REL_BUCKETS = 32
REL_MAX_DIST = 2048
EPS = 1e-6

RET_QK_W = RET_HEADS * RET_DK
RET_V_W = RET_HEADS * RET_DV
ATT_W = N_GROUPS * ATT_HEADS * ATT_DH
ATT_OUT_W = ATT_HEADS * ATT_DH
IN_W = 2 * RET_QK_W + 2 * RET_V_W + 3 * ATT_W + ATT_OUT_W + 2 * D_MODEL

LANES = 128
TILE = 512
RET_OUT_W = 2 * RET_QK_W + 2 * RET_V_W
QKV_W = 3 * ATT_OUT_W
GATE_W = 2 * D_MODEL + ATT_OUT_W
ATT_BLOCK = 128
N_KEYS = ATT_BLOCK + 1
SAMPLE_PAD = 8
NEG = -1e30
VMEM_LIMIT = 48 * 1024 * 1024

F32 = jnp.float32
BF16 = jnp.bfloat16


def _nt_dot(a, b):
    return lax.dot_general(a, b, (((1,), (1,)), ((), ())), preferred_element_type=F32)


def _dot(a, b):
    return jnp.dot(a, b, preferred_element_type=F32)


def _proj_schedule():
    sched = [(0, 0, 0, "rot_q"), (RET_QK_W, 0, RET_QK_W, "rot_k")]
    for k in range(2 * RET_V_W // TILE):
        sched.append((2 * RET_QK_W + k * TILE, 0, 2 * RET_QK_W + k * TILE, "plain"))
    att0 = 2 * RET_QK_W + 2 * RET_V_W
    for g in range(N_GROUPS):
        for kind, epi in enumerate(("norm_q", "norm_k", "plain")):
            sched.append((att0 + kind * ATT_W + g * ATT_OUT_W, 1 + g, kind * ATT_OUT_W, epi))
    gate0 = att0 + 3 * ATT_W
    for k in range(2 * D_MODEL // TILE):
        sched.append((gate0 + ATT_OUT_W + k * TILE, 4, k * TILE, "plain"))
    sched.append((gate0, 4, 2 * D_MODEL, "plain"))
    return sched


_SCHEDULE = _proj_schedule()


def _in_proj_kernel(*refs, dils, per_seq, fuse_retention, n_tails, cast_weights):
    (x_ref, wn_ref, w_ref, ca_ref, sa_ref, cb_ref, sb_ref, cbs_ref, sbs_ref,
     qg_ref, kg_ref, seg_ref) = refs[:12]
    refs = refs[12:]
    if cast_weights:
        *refs, wf_ref, wb_ref, sem_in, sem_out = refs
        wbf_out_ref = refs.pop(-(4 if fuse_retention else 2) - 1)

        def tile_load(step):
            return pltpu.make_async_copy(w_ref.at[:, pl.ds(_SCHEDULE[step][0], TILE)],
                                         wf_ref.at[step % 2], sem_in.at[step % 2])

        def tile_store(step):
            return pltpu.make_async_copy(wb_ref.at[step % 2],
                                         wbf_out_ref.at[:, pl.ds(_SCHEDULE[step][0], TILE)], sem_out.at[step % 2])

        tile_load(0).start()
    if fuse_retention:
        dec_ref, qd_ref, kd_ref, cd_ref, gn_ref, ret_ref, s_ref = refs[:7]
        refs = refs[7:]
    else:
        ret_ref = refs[0]
        refs = refs[1:]
    a0_ref, a1_ref, a2_ref, gate_ref = refs[:4]
    tail_refs = refs[4:4 + n_tails]
    h_ref, y_ref = refs[4 + n_tails:6 + n_tails]
    refs = refs[6 + n_tails:]
    r_ref = refs[0] if fuse_retention else ret_ref
    kt_ref = refs[1] if fuse_retention else None
    out_refs = (r_ref, a0_ref, a1_ref, a2_ref, gate_ref)

    if fuse_retention:
        @pl.when(pl.program_id(0) % per_seq == 0)
        def _():
            s_ref[...] = jnp.zeros_like(s_ref)

    x = x_ref[...]
    tm = x.shape[0]
    ms = jnp.mean(x * x, axis=-1, keepdims=True)
    h_ref[...] = (x * lax.rsqrt(ms + EPS) * wn_ref[...]).astype(BF16)

    ca, sa = ca_ref[0], sa_ref[0]
    cos = ca * cb_ref[...] - sa * sb_ref[...]
    sin = sa * cbs_ref[...] + ca * sbs_ref[...]

    def rotary(y, scale):
        parts = []
        for hh in range(TILE // RET_DK):
            yh = y[:, hh * RET_DK:(hh + 1) * RET_DK]
            parts.append((yh * cos + pltpu.roll(yh, RET_DK // 2, axis=1) * sin) * scale)
        return jnp.concatenate(parts, axis=1)

    def head_rms(y, gain):
        y2 = (y * y).astype(BF16)
        half = TILE // 2
        ms = jnp.concatenate([_dot(y2[:, :half], seg_ref[...]), _dot(y2[:, half:], seg_ref[...])], axis=1)
        return y * lax.rsqrt(ms + EPS) * gain

    epilogues = {
        "plain": lambda y: y,
        "rot_q": lambda y: rotary(y, 1.0),
        "rot_k": lambda y: rotary(y, RET_DK ** -0.5),
        "norm_q": lambda y: head_rms(y, qg_ref[...]),
        "norm_k": lambda y: head_rms(y, kg_ref[...]),
    }

    kt_base = jnp.minimum(pl.program_id(0), 0)

    def retention_issue(c, h):
        rows = slice(c * RET_CHUNK, (c + 1) * RET_CHUNK)
        ks = slice(h * RET_DK, (h + 1) * RET_DK)
        ks2 = slice(RET_QK_W + h * RET_DK, RET_QK_W + (h + 1) * RET_DK)
        vs_in = slice(2 * RET_QK_W + h * RET_DV, 2 * RET_QK_W + (h + 1) * RET_DV)
        qb = r_ref[rows, ks].astype(BF16)
        vb = r_ref[rows, vs_in].astype(BF16)
        state = s_ref[0, h]
        scores = (_nt_dot(qb, r_ref[rows, ks2].astype(BF16)) * dec_ref[h]).astype(BF16)
        carried = _dot(qb, state.astype(BF16)) * qd_ref[h]
        s_ref[0, h] = state * cd_ref[h] + _dot(kt_ref[kt_base + c, ks, :], vb)
        return rows, h, scores, vb, carried

    def retention_finish(rows, h, scores, vb, carried):
        vs = slice(h * RET_DV, (h + 1) * RET_DV)
        gs_in = slice(2 * RET_QK_W + RET_V_W + h * RET_DV, 2 * RET_QK_W + RET_V_W + (h + 1) * RET_DV)
        o = _dot(scores, vb) + carried
        ret_ref[rows, vs] = _group_norm_gate(o, gn_ref[:, vs], r_ref[rows, gs_in]).astype(ret_ref.dtype)

    def store_tile(val, out_idx, out_col):
        o_ref = out_refs[out_idx]
        ocs = slice(out_col, out_col + TILE)
        if not 1 <= out_idx <= N_GROUPS:
            o_ref[:, ocs] = val.astype(o_ref.dtype)
            return
        g, kind = out_idx - 1, out_col // ATT_OUT_W
        if kind > 0 and tail_refs:
            t_ref = tail_refs[g]
            t_ref[0, kind - 1] = jnp.transpose(val)[:, tm - t_ref.shape[3]:]
        dil = dils[g]
        if dil == 1:
            o_ref[0, 0, :, ocs] = val.astype(o_ref.dtype)
            return
        for c in range(TILE // LANES):
            y_ref[c] = val[:, c * LANES:(c + 1) * LANES]
            for r in range(dil):
                o_ref[0, r, :, out_col + c * LANES:out_col + (c + 1) * LANES] = (
                    y_ref[c, pl.ds(r, tm // dil, stride=dil), :].astype(o_ref.dtype))

    n_ret_tiles = sum(1 for t in _SCHEDULE if t[1] == 0)
    units = [(c, h) for c in range(tm // RET_CHUNK) for h in range(RET_HEADS)] if fuse_retention else []
    assert len(units) <= len(_SCHEDULE) - n_ret_tiles
    for step, (w_col, out_idx, out_col, epi) in enumerate(_SCHEDULE):
        if fuse_retention and step == 2:
            for c, h in units:
                rows = slice(c * RET_CHUNK, (c + 1) * RET_CHUNK)
                kt_ref[c, h * RET_DK:(h + 1) * RET_DK, :] = _decayed_keys_t(
                    r_ref[rows, RET_QK_W + h * RET_DK:RET_QK_W + (h + 1) * RET_DK], kd_ref[h])
        unit = units[step - n_ret_tiles] if 0 <= step - n_ret_tiles < len(units) else None
        issued = retention_issue(*unit) if unit else None
        if cast_weights:
            if step + 1 < len(_SCHEDULE):
                tile_load(step + 1).start()
            tile_load(step).wait()
            if step >= 2:
                tile_store(step - 2).wait()
            wb_ref[step % 2] = wf_ref[step % 2].astype(BF16)
            tile_store(step).start()
            w_tile = wb_ref[step % 2]
        else:
            w_tile = w_ref[:, w_col:w_col + TILE]
        store_tile(epilogues[epi](_dot(h_ref[...], w_tile)), out_idx, out_col)
        if unit:
            retention_finish(*issued)
    if cast_weights:
        tile_store(len(_SCHEDULE) - 2).wait()
        tile_store(len(_SCHEDULE) - 1).wait()


def _in_proj(x2d, seq_len, dils, tail_rows, out_dtype, wn, w_bf, rot, qg, kg, seg, tm, retention=None):
    n = x2d.shape[0]
    batch = n // seq_len
    per_seq = seq_len // tm
    fuse = retention is not None
    cast_weights = w_bf.dtype == F32
    assert not cast_weights or n == tm
    const = lambda a: pl.BlockSpec(a.shape, lambda i, nd=a.ndim: (0,) * nd)
    base_spec = pl.BlockSpec((1, 1, RET_DK), lambda i: (i % per_seq, 0, 0))
    qkv_spec = lambda d: pl.BlockSpec((1, d, tm // d, QKV_W), lambda i: (i // per_seq, 0, i % per_seq, 0))
    tail_specs, tail_shapes = [], []
    for rows in tail_rows:
        width = min(tm, rows)
        first = (seq_len - rows) // tm if rows >= tm else per_seq
        tail_specs.append(pl.BlockSpec(
            (1, 2, ATT_OUT_W, width),
            lambda i, first=first: (i // per_seq, 0, 0, jnp.maximum(i % per_seq - first, 0))))
        tail_shapes.append(jax.ShapeDtypeStruct((batch, 2, ATT_OUT_W, rows), F32))
    ca, sa, *offset_tables = rot
    args = [x2d, wn, w_bf, ca, sa, *offset_tables, qg, kg, seg]
    in_specs = [
        pl.BlockSpec((tm, D_MODEL), lambda i: (i, 0)),
        const(wn),
        pl.BlockSpec(memory_space=pl.ANY) if cast_weights
        else pl.BlockSpec(w_bf.shape, lambda i: (0, 0), pipeline_mode=pl.Buffered(1)),
        base_spec, base_spec, *[const(t) for t in offset_tables],
        const(qg), const(kg), const(seg),
    ]
    scratch = [pltpu.VMEM((tm, D_MODEL), BF16), pltpu.VMEM((TILE // LANES, tm, LANES), F32)]
    if fuse:
        tables, gn = retention
        args += [*tables, gn]
        in_specs += [const(t) for t in (*tables, gn)]
        ret_specs = [pl.BlockSpec((tm, RET_V_W), lambda i: (i, 0)),
                     pl.BlockSpec((1, RET_HEADS, RET_DK, RET_DV), lambda i: (i // per_seq, 0, 0, 0))]
        ret_shapes = [jax.ShapeDtypeStruct((n, RET_V_W), out_dtype),
                      jax.ShapeDtypeStruct((batch, RET_HEADS, RET_DK, RET_DV), F32)]
        scratch += [pltpu.VMEM((tm, RET_OUT_W), F32), pltpu.VMEM((tm // RET_CHUNK, RET_QK_W, RET_CHUNK), BF16)]
    else:
        ret_specs = [pl.BlockSpec((tm, RET_OUT_W), lambda i: (i, 0))]
        ret_shapes = [jax.ShapeDtypeStruct((n, RET_OUT_W), out_dtype)]
    cast_specs, cast_shapes = [], []
    if cast_weights:
        cast_specs = [pl.BlockSpec(memory_space=pl.ANY)]
        cast_shapes = [jax.ShapeDtypeStruct(w_bf.shape, BF16)]
        scratch += [pltpu.VMEM((2, D_MODEL, TILE), F32), pltpu.VMEM((2, D_MODEL, TILE), BF16),
                    pltpu.SemaphoreType.DMA((2,)), pltpu.SemaphoreType.DMA((2,))]
    return pl.pallas_call(
        functools.partial(_in_proj_kernel, dils=dils, per_seq=per_seq, fuse_retention=fuse,
                          n_tails=len(tail_rows), cast_weights=cast_weights),
        grid=(n // tm,),
        in_specs=in_specs,
        out_specs=ret_specs + [qkv_spec(d) for d in dils]
        + [pl.BlockSpec((tm, GATE_W), lambda i: (i, 0))] + tail_specs + cast_specs,
        out_shape=ret_shapes
        + [jax.ShapeDtypeStruct((batch, d, seq_len // d, QKV_W), out_dtype) for d in dils]
        + [jax.ShapeDtypeStruct((n, GATE_W), out_dtype)] + tail_shapes + cast_shapes,
        scratch_shapes=scratch,
        compiler_params=pltpu.CompilerParams(
            dimension_semantics=("arbitrary",), vmem_limit_bytes=VMEM_LIMIT),
        name="in_proj",
    )(*args)


def _group_norm_gate(o, gain, gate):
    gate = gate.astype(F32)
    mu = jnp.mean(o, axis=-1, keepdims=True)
    d = o - mu
    var = jnp.mean(d * d, axis=-1, keepdims=True)
    return gate * jax.nn.sigmoid(gate) * (d * lax.rsqrt(var + EPS) * gain)


def _decayed_keys_t(k_keys, kd):
    return jnp.transpose(k_keys * kd).astype(BF16)


def _retention_specs(rows, idx):
    return [
        pl.BlockSpec((1, rows, RET_QK_W), lambda *a: (*idx(*a), 0)),
        pl.BlockSpec((1, rows, RET_QK_W), lambda *a: (*idx(*a), 1)),
        pl.BlockSpec((1, rows, RET_V_W), lambda *a: (*idx(*a), 1)),
        pl.BlockSpec((1, rows, RET_V_W), lambda *a: (*idx(*a), 2)),
    ]


def _table_specs(tables, ndim_grid):
    return [pl.BlockSpec(t.shape, lambda *a, nd=t.ndim: (0,) * nd) for t in tables]


def _retention_sample_kernel(q_ref, k_ref, v_ref, g_ref, s_in_ref, dec_ref, qd_ref, kd_ref, cd_ref,
                             gn_ref, o_ref, s_out_ref, kpad_ref, vpad_ref):
    @pl.when(pl.program_id(0) == 0)
    def _():
        kpad_ref[...] = jnp.zeros_like(kpad_ref)
        vpad_ref[...] = jnp.zeros_like(vpad_ref)

    kpad_ref[0:SAMPLE_PAD, :] = k_ref[0]
    vpad_ref[0:SAMPLE_PAD, :] = v_ref[0]
    issued = []
    for h in range(RET_HEADS):
        ks = slice(h * RET_DK, (h + 1) * RET_DK)
        vs = slice(h * RET_DV, (h + 1) * RET_DV)
        qb = q_ref[0, :, ks].astype(BF16)
        vb = vpad_ref[:, vs].astype(BF16)
        state = s_in_ref[0, h]
        scores = (_nt_dot(qb, kpad_ref[:, ks].astype(BF16)) * dec_ref[h]).astype(BF16)
        carried = _dot(qb, state.astype(BF16)) * qd_ref[h]
        s_out_ref[0, h] = state * cd_ref[h] + _dot(_decayed_keys_t(kpad_ref[:, ks], kd_ref[h]), vb)
        issued.append((scores, vb, carried))
    for h, (scores, vb, carried) in enumerate(issued):
        vs = slice(h * RET_DV, (h + 1) * RET_DV)
        o = _dot(scores, vb) + carried
        o_ref[0, :, vs] = _group_norm_gate(o, gn_ref[:, vs], g_ref[0, :, vs]).astype(o_ref.dtype)


def _attn_prompt_blocks(q_ref, kp_ref, kc_ref, vp_ref, vc_ref, bias_ref, o_ref, lse_ref, first):
    n_seq, n_blocks = q_ref.shape[1], q_ref.shape[2] // ATT_BLOCK
    assert first or n_seq == 1
    pair_w = 2 * ATT_DH
    low_q = lax.broadcasted_iota(jnp.int32, (ATT_BLOCK, pair_w), 1) < ATT_DH
    rows0, rows1 = slice(0, ATT_BLOCK), slice(ATT_BLOCK, 2 * ATT_BLOCK)

    def keys(prev_ref, cur_ref, sq, t, ps):
        if t > 0:
            return cur_ref[0, sq, (t - 1) * ATT_BLOCK:(t + 1) * ATT_BLOCK, ps]
        if first:
            return cur_ref[0, sq, 0:ATT_BLOCK, ps]
        return jnp.concatenate([prev_ref[0, sq, :, ps], cur_ref[0, sq, 0:ATT_BLOCK, ps]], axis=0)

    staged = []
    for sq, t in [(sq, t) for sq in range(n_seq) for t in range(n_blocks)]:
        for p in range(ATT_HEADS // 2):
            ps = slice(p * pair_w, (p + 1) * pair_w)
            qp = q_ref[0, sq, t * ATT_BLOCK:(t + 1) * ATT_BLOCK, ps]
            kcat = keys(kp_ref, kc_ref, sq, t, ps)
            q2 = jnp.concatenate([jnp.where(low_q, qp, 0.0), jnp.where(low_q, 0.0, qp)], axis=0)
            s = _nt_dot(q2, kcat) + bias_ref[p, :, 2 * ATT_BLOCK - kcat.shape[0]:]
            s = s.astype(BF16)
            m = jnp.max(s, axis=-1, keepdims=True)
            staged.append((sq, t, p, m.astype(F32), jnp.exp(s - m)))
    for sq, t, p, m, e in staged:
        ps = slice(p * pair_w, (p + 1) * pair_w)
        vcat = keys(vp_ref, vc_ref, sq, t, ps)
        low_k = lax.broadcasted_iota(jnp.int32, vcat.shape, 1) < ATT_DH
        p0 = _dot(e[rows0], jnp.where(low_k, vcat, 1.0))
        p1 = _dot(e[rows1], jnp.where(low_k, 1.0, vcat))
        den = pltpu.roll(jnp.where(low_q, p1, p0), ATT_DH, axis=1)
        ts = slice(t * ATT_BLOCK, (t + 1) * ATT_BLOCK)
        o_ref[0, sq, ts, ps] = jnp.where(low_q, p0, p1) / den
        lse_ref[0, sq, ts, ps] = jnp.where(low_q, m[rows0], m[rows1]) + jnp.log(den)


def _attn_prompt_kernel(q_ref, kp_ref, kc_ref, vp_ref, vc_ref, slot_ref, o_ref, lse_ref, bias_ref,
                        *, whole_sequences):
    first = pl.program_id(2) == 0

    @pl.when(jnp.logical_and(first, jnp.logical_and(pl.program_id(0) == 0, pl.program_id(1) == 0)))
    def _():
        for h in range(ATT_HEADS):
            slots = jnp.broadcast_to(slot_ref[h:h + 1, :], (ATT_BLOCK, slot_ref.shape[1]))
            rows = pltpu.roll(slots, 0, 1, stride=1, stride_axis=0)
            bias_ref[h // 2, (h % 2) * ATT_BLOCK:(h % 2 + 1) * ATT_BLOCK, :] = rows[:, :2 * ATT_BLOCK]

    if whole_sequences:
        _attn_prompt_blocks(q_ref, kp_ref, kc_ref, vp_ref, vc_ref, bias_ref, o_ref, lse_ref, True)
        return

    @pl.when(first)
    def _():
        _attn_prompt_blocks(q_ref, kp_ref, kc_ref, vp_ref, vc_ref, bias_ref, o_ref, lse_ref, True)

    @pl.when(jnp.logical_not(first))
    def _():
        _attn_prompt_blocks(q_ref, kp_ref, kc_ref, vp_ref, vc_ref, bias_ref, o_ref, lse_ref, False)


def _attn_prompt_group(qkv, slot_bias, step_blocks):
    b, dil, tr, _ = qkv.shape
    n_blocks = min(step_blocks, tr // ATT_BLOCK)
    n_seq = min(step_blocks // n_blocks, dil)
    rows = n_blocks * ATT_BLOCK
    cur = lambda c: pl.BlockSpec((1, n_seq, rows, ATT_OUT_W), lambda bi, r, j: (bi, r, j, c))
    prev = lambda c: pl.BlockSpec((1, n_seq, ATT_BLOCK, ATT_OUT_W),
                                  lambda bi, r, j: (bi, r, jnp.maximum(j * n_blocks - 1, 0), c))
    res_shape = jax.ShapeDtypeStruct((b, dil, tr, ATT_OUT_W), F32)
    return pl.pallas_call(
        functools.partial(_attn_prompt_kernel, whole_sequences=tr == rows),
        grid=(b, dil // n_seq, tr // rows),
        in_specs=[cur(0), prev(1), cur(1), prev(2), cur(2),
                  pl.BlockSpec(slot_bias.shape, lambda bi, r, j: (0, 0))],
        out_specs=[cur(0), cur(0)],
        out_shape=[res_shape, res_shape],
        scratch_shapes=[pltpu.VMEM((ATT_HEADS // 2, 2 * ATT_BLOCK, 2 * ATT_BLOCK), F32)],
        compiler_params=pltpu.CompilerParams(
            dimension_semantics=("arbitrary", "arbitrary", "arbitrary"), vmem_limit_bytes=VMEM_LIMIT),
        name=f"attn_prompt_d{dil}",
    )(qkv, qkv, qkv, qkv, qkv, slot_bias)


def _attn_sample_kernel(a0_ref, a1_ref, a2_ref, c0_ref, c1_ref, c2_ref, s0_ref, s1_ref, s2_ref, sn_ref,
                        o_ref, kn_ref, vn_ref, b0_ref, b1_ref, b2_ref, bn_ref):
    qkv_refs = (a0_ref, a1_ref, a2_ref)
    cache_refs = (c0_ref, c1_ref, c2_ref)
    bias_refs = (b0_ref, b1_ref, b2_ref)

    @pl.when(pl.program_id(0) == 0)
    def _():
        kn_ref[...] = jnp.zeros_like(kn_ref)
        vn_ref[...] = jnp.zeros_like(vn_ref)
        def rotated_rows(vec, width):
            rows = jnp.broadcast_to(vec, (SAMPLE_PAD, vec.shape[1]))
            return pltpu.roll(rows, 0, 1, stride=1, stride_axis=0)[:, :width]
        for g, (s_ref, b_ref) in enumerate(zip((s0_ref, s1_ref, s2_ref), bias_refs)):
            for h in range(ATT_HEADS):
                b_ref[h] = rotated_rows(s_ref[h:h + 1, :], b_ref.shape[2])
                bn_ref[g, h] = rotated_rows(sn_ref[g, h:h + 1, :], ATT_BLOCK)

    for g in range(N_GROUPS):
        gs = slice(g * ATT_OUT_W, (g + 1) * ATT_OUT_W)
        kn_ref[0:SAMPLE_PAD, gs] = qkv_refs[g][0, :, ATT_OUT_W:2 * ATT_OUT_W]
        vn_ref[0:SAMPLE_PAD, gs] = qkv_refs[g][0, :, 2 * ATT_OUT_W:3 * ATT_OUT_W]

    staged = []
    for h in range(ATT_HEADS):
        logits = []
        for g in range(N_GROUPS):
            hs = slice(g * ATT_OUT_W + h * ATT_DH, g * ATT_OUT_W + (h + 1) * ATT_DH)
            qh = qkv_refs[g][0, :, h * ATT_DH:(h + 1) * ATT_DH].astype(BF16)
            logits.append(_dot(qh, cache_refs[g][0, 0, h].astype(BF16)) + bias_refs[g][h])
            logits.append(_nt_dot(qh, kn_ref[:, hs].astype(BF16)) + bn_ref[g, h])
        m = functools.reduce(jnp.maximum, [jnp.max(x, axis=-1, keepdims=True) for x in logits])
        es = [jnp.exp(x - m) for x in logits]
        l = functools.reduce(jnp.add, [jnp.sum(e, axis=-1, keepdims=True) for e in es])
        staged.append(([e.astype(BF16) for e in es], l))
    for h, (es, l) in enumerate(staged):
        acc = jnp.zeros((SAMPLE_PAD, ATT_DH), F32)
        for g in range(N_GROUPS):
            hs = slice(g * ATT_OUT_W + h * ATT_DH, g * ATT_OUT_W + (h + 1) * ATT_DH)
            acc = acc + _nt_dot(es[2 * g], cache_refs[g][0, 1, h].astype(BF16))
            acc = acc + _dot(es[2 * g + 1], vn_ref[:, hs].astype(BF16))
        o_ref[0, :, h * ATT_DH:(h + 1) * ATT_DH] = acc / l


_N_RET_IN, _N_ATT_IN = 10, 10


def _sample_mixers_kernel(*refs):
    n_in = _N_RET_IN + _N_ATT_IN
    ret_in, att_in = refs[:_N_RET_IN], refs[_N_RET_IN:n_in]
    ret_o_ref, state_o_ref, att_o_ref = refs[n_in:n_in + 3]
    kv_o_refs = refs[n_in + 3:n_in + 3 + N_GROUPS]
    kpad_ref, vpad_ref, kn_ref, vn_ref = refs[n_in + 3 + N_GROUPS:n_in + 7 + N_GROUPS]
    _attn_sample_kernel(*att_in, att_o_ref, kn_ref, vn_ref, *refs[n_in + 7 + N_GROUPS:])
    _retention_sample_kernel(*ret_in, ret_o_ref, state_o_ref, kpad_ref, vpad_ref)
    for qkv_ref, kv_ref in zip(att_in[:N_GROUPS], kv_o_refs):
        for tok in range(kv_ref.shape[1]):
            for kv in range(2):
                for h in range(ATT_HEADS):
                    col = (1 + kv) * ATT_OUT_W + h * ATT_DH
                    kv_ref[0, tok, kv, h:h + 1, :] = qkv_ref[0, tok:tok + 1, col:col + ATT_DH]


def _sample_mixers(ret, state, tables, gn, qkvs, caches_t, slots, slots_new, n_tokens):
    b, p, _ = ret.shape
    state_spec = pl.BlockSpec((1, RET_HEADS, RET_DK, RET_DV), lambda bi: (bi, 0, 0, 0))
    ret_specs = _retention_specs(p, lambda bi: (bi, 0)) + [state_spec] + _table_specs(tables + (gn,), 1)
    att_specs = ([pl.BlockSpec((1, p, QKV_W), lambda bi: (bi, 0, 0)) for _ in qkvs]
                 + [pl.BlockSpec((1,) + c.shape[1:], lambda bi: (bi, 0, 0, 0, 0)) for c in caches_t]
                 + [pl.BlockSpec(x.shape, lambda bi: (0, 0)) for x in slots]
                 + [pl.BlockSpec(slots_new.shape, lambda bi: (0, 0, 0))])
    assert len(ret_specs) == _N_RET_IN and len(att_specs) == _N_ATT_IN
    kv_shape = (b, n_tokens, 2, ATT_HEADS, ATT_DH)
    return pl.pallas_call(
        _sample_mixers_kernel,
        grid=(b,),
        in_specs=ret_specs + att_specs,
        out_specs=[pl.BlockSpec((1, p, RET_V_W), lambda bi: (bi, 0, 0)), state_spec,
                   pl.BlockSpec((1, p, ATT_OUT_W), lambda bi: (bi, 0, 0))]
        + [pl.BlockSpec((1,) + kv_shape[1:], lambda bi: (bi, 0, 0, 0, 0))] * N_GROUPS,
        out_shape=[jax.ShapeDtypeStruct((b, p, RET_V_W), ret.dtype), jax.ShapeDtypeStruct(state.shape, F32),
                   jax.ShapeDtypeStruct((b, p, ATT_OUT_W), F32)]
        + [jax.ShapeDtypeStruct(kv_shape, F32)] * N_GROUPS,
        scratch_shapes=[pltpu.VMEM((RET_CHUNK, RET_QK_W), F32), pltpu.VMEM((RET_CHUNK, RET_V_W), F32),
                        pltpu.VMEM((ATT_BLOCK, ATT_W), F32), pltpu.VMEM((ATT_BLOCK, ATT_W), F32)]
        + [pltpu.VMEM((ATT_HEADS, p, c.shape[-1]), F32) for c in caches_t]
        + [pltpu.VMEM((N_GROUPS, ATT_HEADS, p, ATT_BLOCK), F32)],
        compiler_params=pltpu.CompilerParams(
            dimension_semantics=("arbitrary",), vmem_limit_bytes=VMEM_LIMIT),
        name="sample_mixers",
    )(ret, ret, ret, ret, state, *tables, gn, *qkvs, *caches_t, *slots, slots_new)


def _out_proj_kernel(*refs, dils):
    x_ref, ret_ref, ga_ref, gb_ref, ag_ref, wr_ref, wa_ref, wo_ref = refs[:8]
    if dils is None:
        att_ref, o_ref = refs[8:]
        att = att_ref[0]
    else:
        group_refs = refs[8:8 + 2 * N_GROUPS]
        o_ref = refs[8 + 2 * N_GROUPS]
        scratch = refs[9 + 2 * N_GROUPS:]
        tm = x_ref.shape[1]
        os, lses = [], []
        for g, dil in enumerate(dils):
            og_ref, lg_ref = group_refs[2 * g], group_refs[2 * g + 1]
            if dil == 1:
                os.append(og_ref[0, 0]); lses.append(lg_ref[0, 0])
                continue
            so_ref, sl_ref = scratch[2 * (g - 1)], scratch[2 * (g - 1) + 1]
            n_chunks = ATT_OUT_W // LANES
            for c in range(n_chunks):
                cs = slice(c * LANES, (c + 1) * LANES)
                for r in range(dil):
                    so_ref[c, pl.ds(r, tm // dil, stride=dil), :] = og_ref[0, r, :, cs]
                    sl_ref[c, pl.ds(r, tm // dil, stride=dil), :] = lg_ref[0, r, :, cs]
            os.append(jnp.concatenate([so_ref[c] for c in range(n_chunks)], axis=1))
            lses.append(jnp.concatenate([sl_ref[c] for c in range(n_chunks)], axis=1))
        mx = functools.reduce(jnp.maximum, lses)
        ws = [jnp.exp(l - mx) for l in lses]
        att = functools.reduce(jnp.add, [w * o for w, o in zip(ws, os)]) / functools.reduce(jnp.add, ws)
    ag = ag_ref[0].astype(F32)
    u = (ag * jax.nn.sigmoid(ag) * att).astype(BF16)
    o_b = _dot(u, wa_ref[...])
    o_a = _dot(ret_ref[0].astype(BF16), wr_ref[...])
    merged = jax.nn.sigmoid(ga_ref[0].astype(F32)) * o_a + jax.nn.sigmoid(gb_ref[0].astype(F32)) * o_b
    o_ref[0] = x_ref[0] + _dot(merged.astype(BF16), wo_ref[...])


def _out_proj(x, ret, gates, wr, wa, wo, tm, att=None, groups=None):
    b, t, _ = x.shape
    row = lambda w, c: pl.BlockSpec((1, tm, w), lambda bi, i: (bi, i, c))
    full = lambda a: pl.BlockSpec(a.shape, lambda bi, i: (0, 0))
    in_specs = [row(D_MODEL, 0), row(RET_V_W, 0), row(D_MODEL, 0), row(D_MODEL, 1),
                row(ATT_OUT_W, 2 * D_MODEL // ATT_OUT_W), full(wr), full(wa), full(wo)]
    args = [x, ret, gates, gates, gates, wr, wa, wo]
    scratch = []
    if groups is None:
        dils = None
        in_specs.append(row(ATT_OUT_W, 0))
        args.append(att)
    else:
        dils = tuple(o.shape[1] for o, _ in groups)
        for (o, lse), d in zip(groups, dils):
            spec = pl.BlockSpec((1, d, tm // d, ATT_OUT_W), lambda bi, i: (bi, 0, i, 0))
            in_specs += [spec, spec]
            args += [o, lse]
            if d > 1:
                scratch += [pltpu.VMEM((ATT_OUT_W // LANES, tm, LANES), F32)] * 2
    return pl.pallas_call(
        functools.partial(_out_proj_kernel, dils=dils),
        grid=(b, t // tm),
        in_specs=in_specs,
        out_specs=row(D_MODEL, 0),
        out_shape=jax.ShapeDtypeStruct((b, t, D_MODEL), F32),
        scratch_shapes=scratch,
        compiler_params=pltpu.CompilerParams(
            dimension_semantics=("parallel", "parallel"), vmem_limit_bytes=VMEM_LIMIT),
        name="out_proj",
    )(*args)


def _rotary_tables(bases, offsets):
    half = RET_DK // 2
    inv = ROPE_BASE ** (-jnp.arange(half, dtype=F32) / half)
    inv2 = jnp.concatenate([inv, inv])
    sign = jnp.concatenate([-jnp.ones((half,), F32), jnp.ones((half,), F32)])
    a = bases.astype(F32)[:, None, None] * inv2
    b = offsets.astype(F32)[:, None] * inv2
    cb, sb = jnp.cos(b), jnp.sin(b)
    return jnp.cos(a), jnp.sin(a), cb, sb, sign * cb, sign * sb


def _retention_tables(c, rows):
    log_g = jnp.log1p(-(2.0 ** (-5.0 - jnp.arange(RET_HEADS, dtype=F32))))
    i = jnp.arange(c, dtype=F32)
    diff = i[:, None] - i[None, :]
    decay = jnp.where(diff[None] >= 0, jnp.exp(jnp.maximum(diff, 0.0)[None] * log_g[:, None, None]), 0.0)
    q_decay = jnp.exp((i + 1.0)[None, :] * log_g[:, None])
    k_decay = jnp.exp((c - 1.0 - i)[None, :] * log_g[:, None])
    chunk_decay = jnp.exp(c * log_g)
    dec = jnp.zeros((RET_HEADS, rows, RET_CHUNK), F32).at[:, :c, :c].set(decay)
    qd = jnp.zeros((RET_HEADS, rows, 1), F32).at[:, :c, 0].set(q_decay)
    kd = jnp.zeros((RET_HEADS, RET_CHUNK, 1), F32).at[:, :c, 0].set(k_decay)
    cd = jnp.broadcast_to(chunk_decay[:, None, None], (RET_HEADS, 1, RET_DV))
    return dec, qd, kd, cd


def _t5_bucket(dist):
    max_exact = REL_BUCKETS // 2
    d = jnp.maximum(dist.astype(F32), 1.0)
    large = max_exact + (jnp.log(d / max_exact) / math.log(REL_MAX_DIST / max_exact)
                         * (REL_BUCKETS - max_exact)).astype(jnp.int32)
    large = jnp.minimum(large, REL_BUCKETS - 1)
    return jnp.where(dist < max_exact, dist, large)


def _group_bias(rel_bias, g, dil, slots):
    dist = dil * jnp.asarray(slots, dtype=jnp.int32)
    return rel_bias[_t5_bucket(dist)][:, g * ATT_HEADS:(g + 1) * ATT_HEADS].astype(F32).T


def _neg(heads, n):
    return jnp.full((heads, n), NEG, F32)


def _dilate(v, dil):
    heads, n = v.shape
    return jnp.stack([v] + [_neg(heads, n)] * (dil - 1), axis=-1).reshape(heads, n * dil)


def _prompt_slots(tb_rev):
    heads = tb_rev.shape[0]
    return jnp.concatenate([tb_rev, _neg(heads, 4 * ATT_BLOCK - N_KEYS)], axis=1)


def _sample_slots(tb, tb_rev, win, dil):
    heads = tb.shape[0]
    s_c = jnp.concatenate([_dilate(tb_rev[:, :N_KEYS - 1], dil), _neg(heads, ATT_BLOCK)], axis=1)
    back = [tb[:, k // dil:k // dil + 1] if k % dil == 0 else _neg(heads, 1) for k in range(SAMPLE_PAD - 1, 0, -1)]
    s_n = jnp.concatenate([tb[:, 0:1], _neg(heads, 2 * ATT_BLOCK - SAMPLE_PAD)] + back, axis=1)
    return s_c, s_n


def _kv_rows_t(tail_t):
    b, _, _, rows = tail_t.shape
    return jnp.transpose(tail_t.reshape(b, 2, ATT_HEADS, ATT_DH, rows), (0, 4, 1, 2, 3))


def kernel(x_prompt, x_sample, cache_kv_w128, cache_kv_w512, cache_kv_w2048, state_retention,
           w_norm, w_in, q_norm, k_norm, rel_bias, ret_norm, w_proj_ret, w_proj_att, w_out):
    assert w_in.shape[0] == 1
    bp, t, _ = x_prompt.shape
    bs, ts, _ = x_sample.shape
    dils = tuple(d for _, d in ATT_GROUPS)
    assert t % (ATT_BLOCK * max(dils)) == 0 and ts <= SAMPLE_PAD
    caches = (cache_kv_w128[0], cache_kv_w512[0], cache_kv_w2048[0])
    for cch, (win, _) in zip(caches, ATT_GROUPS):
        assert cch.shape[1] == win and win <= PAST_LEN

    wn = w_norm[0].reshape(1, D_MODEL)
    qg =jnp.tile(q_norm[0] * (ATT_DH ** -0.5), TILE // ATT_DH).reshape(1, TILE)
    kg = jnp.tile(k_norm[0], TILE // ATT_DH).reshape(1, TILE)
    gn = ret_norm[0].reshape(1, RET_V_W)
    wr = w_proj_ret[0].astype(BF16)
    wa = w_proj_att[0].astype(BF16)
    wo = w_out[0].astype(BF16)
    hid = jnp.arange(TILE // 2) // ATT_DH
    seg_mean = jnp.where(hid[:, None] == hid[None, :], 1.0 / ATT_DH, 0.0).astype(BF16)
    asc, desc = tuple(range(N_KEYS)), tuple(range(N_KEYS - 1, -1, -1))
    group_bias = [(_group_bias(rel_bias, g, d, asc), _group_bias(rel_bias, g, d, desc)) for g, d in enumerate(dils)]

    pad = SAMPLE_PAD
    ns = bs * pad
    xs = jnp.pad(x_sample, ((0, 0), (0, pad - ts), (0, 0))).reshape(ns, D_MODEL)
    rot_s = _rotary_tables(jnp.full((1,), PAST_LEN), jnp.tile(jnp.arange(pad), bs))
    ret_in_s, *qkv_s, gates_s, w_in_bf = _in_proj(xs, ns, (1,) * N_GROUPS, (), F32, wn, w_in[0], rot_s,
                                                  qg, kg, seg_mean, tm=ns)

    tm_p = 256
    rot_p = _rotary_tables(jnp.arange(0, t, tm_p), jnp.arange(tm_p))
    tail_p_rows = tuple(min(w, t) for w, _ in ATT_GROUPS)
    ret_p, state_p, *rest = _in_proj(
        x_prompt.reshape(bp * t, D_MODEL), t, dils, tail_p_rows, BF16, wn, w_in_bf, rot_p, qg, kg, seg_mean,
        tm=tm_p, retention=(_retention_tables(RET_CHUNK, RET_CHUNK), gn))
    qkv_p, gates_p, tails_p = rest[:N_GROUPS], rest[N_GROUPS], rest[N_GROUPS + 1:]
    groups = [_attn_prompt_group(qkv_p[g], _prompt_slots(group_bias[g][1]), step_blocks=8)
              for g in range(N_GROUPS)]
    y_p = _out_proj(x_prompt, ret_p.reshape(bp, t, RET_V_W), gates_p.reshape(bp, t, GATE_W), wr, wa, wo,
                    tm=512, groups=groups)

    qkv_s =[a.reshape(bs, pad, QKV_W) for a in qkv_s]
    caches_t = [jnp.transpose(c, (0, 2, 3, 4, 1)) for c in caches]
    ss = [_sample_slots(*group_bias[g], win, d) for g, (win, d) in enumerate(ATT_GROUPS)]
    ret_s, state_s, att_s, *kv_s = _sample_mixers(
        ret_in_s.reshape(bs, pad, RET_OUT_W), state_retention[0], _retention_tables(ts, pad), gn,
        qkv_s, caches_t, [c for c, _ in ss], jnp.stack([n for _, n in ss]), ts)
    y_s = _out_proj(xs.reshape(1, ns, D_MODEL), ret_s.reshape(1, ns, RET_V_W), gates_s.reshape(1, ns, GATE_W),
                    wr, wa, wo, tm=ns, att=att_s.reshape(1, ns, ATT_OUT_W))
    y_s = y_s.reshape(bs, pad, D_MODEL)[:, :ts]

    kv_p = [_kv_rows_t(tt)[None] for tt in tails_p]
    return (y_p, y_s, state_p[None], state_s[None], kv_p[0], kv_p[1], kv_p[2],
            kv_s[0][None], kv_s[1][None], kv_s[2][None])
```

```python
import functools
import math

import jax
import jax.numpy as jnp
from jax import lax
from jax.experimental import pallas as pl
from jax.experimental.pallas import tpu as pltpu

D_MODEL = 1024
PAST_LEN = 16384
RET_HEADS = 4
RET_DK = 128
RET_DV = 256
RET_CHUNK = 128
ROPE_BASE = 10000.0
ATT_GROUPS = ((128, 1), (512, 4), (2048, 16))
N_GROUPS = 3
ATT_HEADS = 8
ATT_DH = 64
REL_BUCKETS = 32
REL_MAX_DIST = 2048
EPS = 1e-6

RET_QK_W = RET_HEADS * RET_DK
RET_V_W = RET_HEADS * RET_DV
ATT_W = N_GROUPS * ATT_HEADS * ATT_DH
ATT_OUT_W = ATT_HEADS * ATT_DH
IN_W = 2 * RET_QK_W + 2 * RET_V_W + 3 * ATT_W + ATT_OUT_W + 2 * D_MODEL

LANES = 128
TILE = 512
RET_OUT_W = 2 * RET_QK_W + 2 * RET_V_W
QKV_W = 3 * ATT_OUT_W
GATE_W = 2 * D_MODEL + ATT_OUT_W
ATT_BLOCK = 128
N_KEYS = ATT_BLOCK + 1
SAMPLE_PAD = 8
NEG = -1e30
VMEM_LIMIT = 48 * 1024 * 1024
W_LOAD_SLOTS = 4

F32 = jnp.float32
BF16 = jnp.bfloat16


def _nt_dot(a, b):
    return lax.dot_general(a, b, (((1,), (1,)), ((), ())), preferred_element_type=F32)


def _dot(a, b):
    return jnp.dot(a, b, preferred_element_type=F32)


def _proj_schedule():
    sched = [(0, 0, 0, "rot_q"), (RET_QK_W, 0, RET_QK_W, "rot_k")]
    for k in range(2 * RET_V_W // TILE):
        sched.append((2 * RET_QK_W + k * TILE, 0, 2 * RET_QK_W + k * TILE, "plain"))
    att0 = 2 * RET_QK_W + 2 * RET_V_W
    for g in range(N_GROUPS):
        for kind, epi in enumerate(("norm_q", "norm_k", "plain")):
            sched.append((att0 + kind * ATT_W + g * ATT_OUT_W, 1 + g, kind * ATT_OUT_W, epi))
    gate0 = att0 + 3 * ATT_W
    for k in range(2 * D_MODEL // TILE):
        sched.append((gate0 + ATT_OUT_W + k * TILE, 4, k * TILE, "plain"))
    sched.append((gate0, 4, 2 * D_MODEL, "plain"))
    return sched


_SCHEDULE = _proj_schedule()


def _in_proj_kernel(*refs, dils, per_seq, fuse_retention, n_tails, cast_weights):
    (x_ref, wn_ref, w_ref, ca_ref, sa_ref, cb_ref, sb_ref, cbs_ref, sbs_ref,
     qg_ref, kg_ref, seg_ref) = refs[:12]
    refs = refs[12:]
    if cast_weights:
        *refs, wf_ref, wb_ref, sem_in, sem_out = refs
        wbf_out_ref = refs.pop(-(4 if fuse_retention else 2) - 1)

        load_slots = wf_ref.shape[0]

        def tile_load(step):
            return pltpu.make_async_copy(w_ref.at[:, pl.ds(_SCHEDULE[step][0], TILE)],
                                         wf_ref.at[step % load_slots], sem_in.at[step % load_slots])

        def tile_store(step):
            return pltpu.make_async_copy(wb_ref.at[step % 2],
                                         wbf_out_ref.at[:, pl.ds(_SCHEDULE[step][0], TILE)], sem_out.at[step % 2])

        for ahead in range(load_slots - 1):
            tile_load(ahead).start()
    if fuse_retention:
        dec_ref, qd_ref, kd_ref, cd_ref, gn_ref, ret_ref, s_ref = refs[:7]
        refs = refs[7:]
    else:
        ret_ref = refs[0]
        refs = refs[1:]
    a0_ref, a1_ref, a2_ref, gate_ref = refs[:4]
    tail_refs = refs[4:4 + n_tails]
    h_ref, y_ref = refs[4 + n_tails:6 + n_tails]
    refs = refs[6 + n_tails:]
    r_ref = refs[0] if fuse_retention else ret_ref
    kt_ref = refs[1] if fuse_retention else None
    out_refs = (r_ref, a0_ref, a1_ref, a2_ref, gate_ref)

    if fuse_retention:
        @pl.when(pl.program_id(0) % per_seq == 0)
        def _():
            s_ref[...] = jnp.zeros_like(s_ref)

    x = x_ref[...]
    tm = x.shape[0]
    ms = jnp.mean(x * x, axis=-1, keepdims=True)
    h_ref[...] = (x * lax.rsqrt(ms + EPS) * wn_ref[...]).astype(BF16)

    ca, sa = ca_ref[0], sa_ref[0]
    cos = ca * cb_ref[...] - sa * sb_ref[...]
    sin = sa * cbs_ref[...] + ca * sbs_ref[...]

    def rotary(y, scale):
        parts = []
        for hh in range(TILE // RET_DK):
            yh = y[:, hh * RET_DK:(hh + 1) * RET_DK]
            parts.append((yh * cos + pltpu.roll(yh, RET_DK // 2, axis=1) * sin) * scale)
        return jnp.concatenate(parts, axis=1)

    def head_rms(y, gain):
        y2 = (y * y).astype(BF16)
        half = TILE // 2
        ms = jnp.concatenate([_dot(y2[:, :half], seg_ref[...]), _dot(y2[:, half:], seg_ref[...])], axis=1)
        return y * lax.rsqrt(ms + EPS) * gain

    epilogues = {
        "plain": lambda y: y,
        "rot_q": lambda y: rotary(y, 1.0),
        "rot_k": lambda y: rotary(y, RET_DK ** -0.5),
        "norm_q": lambda y: head_rms(y, qg_ref[...]),
        "norm_k": lambda y: head_rms(y, kg_ref[...]),
    }

    kt_base = jnp.minimum(pl.program_id(0), 0)

    def retention_issue(c, h):
        rows = slice(c * RET_CHUNK, (c + 1) * RET_CHUNK)
        ks = slice(h * RET_DK, (h + 1) * RET_DK)
        ks2 = slice(RET_QK_W + h * RET_DK, RET_QK_W + (h + 1) * RET_DK)
        vs_in = slice(2 * RET_QK_W + h * RET_DV, 2 * RET_QK_W + (h + 1) * RET_DV)
        qb = r_ref[rows, ks].astype(BF16)
        vb = r_ref[rows, vs_in].astype(BF16)
        state = s_ref[0, h]
        scores = (_nt_dot(qb, r_ref[rows, ks2].astype(BF16)) * dec_ref[h]).astype(BF16)
        carried = _dot(qb, state.astype(BF16)) * qd_ref[h]
        s_ref[0, h] = state * cd_ref[h] + _dot(kt_ref[kt_base + c, ks, :], vb)
        return rows, h, scores, vb, carried

    def retention_finish(rows, h, scores, vb, carried):
        vs = slice(h * RET_DV, (h + 1) * RET_DV)
        gs_in = slice(2 * RET_QK_W + RET_V_W + h * RET_DV, 2 * RET_QK_W + RET_V_W + (h + 1) * RET_DV)
        o = _dot(scores, vb) + carried
        ret_ref[rows, vs] = _group_norm_gate(o, gn_ref[:, vs], r_ref[rows, gs_in]).astype(ret_ref.dtype)

    def store_tile(val, out_idx, out_col):
        o_ref = out_refs[out_idx]
        ocs = slice(out_col, out_col + TILE)
        if not 1 <= out_idx <= N_GROUPS:
            o_ref[:, ocs] = val.astype(o_ref.dtype)
            return
        g, kind = out_idx - 1, out_col // ATT_OUT_W
        if kind > 0 and tail_refs:
            t_ref = tail_refs[g]
            t_ref[0, kind - 1] = jnp.transpose(val)[:, tm - t_ref.shape[3]:]
        dil = dils[g]
        if dil == 1:
            o_ref[0, 0, :, ocs] = val.astype(o_ref.dtype)
            return
        for c in range(TILE // LANES):
            y_ref[c] = val[:, c * LANES:(c + 1) * LANES]
            for r in range(dil):
                o_ref[0, r, :, out_col + c * LANES:out_col + (c + 1) * LANES] = (
                    y_ref[c, pl.ds(r, tm // dil, stride=dil), :].astype(o_ref.dtype))

    n_ret_tiles = sum(1 for t in _SCHEDULE if t[1] == 0)
    units = [(c, h) for c in range(tm // RET_CHUNK) for h in range(RET_HEADS)] if fuse_retention else []
    assert len(units) <= len(_SCHEDULE) - n_ret_tiles
    for step, (w_col, out_idx, out_col, epi) in enumerate(_SCHEDULE):
        if fuse_retention and step == 2:
            for c, h in units:
                rows = slice(c * RET_CHUNK, (c + 1) * RET_CHUNK)
                kt_ref[c, h * RET_DK:(h + 1) * RET_DK, :] = _decayed_keys_t(
                    r_ref[rows, RET_QK_W + h * RET_DK:RET_QK_W + (h + 1) * RET_DK], kd_ref[h])
        unit = units[step - n_ret_tiles] if 0 <= step - n_ret_tiles < len(units) else None
        issued = retention_issue(*unit) if unit else None
        if cast_weights:
            if step + load_slots - 1 < len(_SCHEDULE):
                tile_load(step + load_slots - 1).start()
            tile_load(step).wait()
            if step >= 2:
                tile_store(step - 2).wait()
            wb_ref[step % 2] = wf_ref[step % load_slots].astype(BF16)
            tile_store(step).start()
            w_tile = wb_ref[step % 2]
        else:
            w_tile = w_ref[:, w_col:w_col + TILE]
        store_tile(epilogues[epi](_dot(h_ref[...], w_tile)), out_idx, out_col)
        if unit:
            retention_finish(*issued)
    if cast_weights:
        tile_store(len(_SCHEDULE) - 2).wait()
        tile_store(len(_SCHEDULE) - 1).wait()


def _in_proj(x2d, seq_len, dils, tail_rows, out_dtype, wn, w_bf, rot, qg, kg, seg, tm, retention=None):
    n = x2d.shape[0]
    batch = n // seq_len
    per_seq = seq_len // tm
    fuse = retention is not None
    cast_weights = w_bf.dtype == F32
    assert not cast_weights or n == tm
    const = lambda a: pl.BlockSpec(a.shape, lambda i, nd=a.ndim: (0,) * nd)
    base_spec = pl.BlockSpec((1, 1, RET_DK), lambda i: (i % per_seq, 0, 0))
    qkv_spec = lambda d: pl.BlockSpec((1, d, tm // d, QKV_W), lambda i: (i // per_seq, 0, i % per_seq, 0))
    tail_specs, tail_shapes = [], []
    for rows in tail_rows:
        width = min(tm, rows)
        first = (seq_len - rows) // tm if rows >= tm else per_seq
        tail_specs.append(pl.BlockSpec(
            (1, 2, ATT_OUT_W, width),
            lambda i, first=first: (i // per_seq, 0, 0, jnp.maximum(i % per_seq - first, 0))))
        tail_shapes.append(jax.ShapeDtypeStruct((batch, 2, ATT_OUT_W, rows), F32))
    ca, sa, *offset_tables = rot
    args = [x2d, wn, w_bf, ca, sa, *offset_tables, qg, kg, seg]
    in_specs = [
        pl.BlockSpec((tm, D_MODEL), lambda i: (i, 0)),
        const(wn),
        pl.BlockSpec(memory_space=pl.ANY) if cast_weights
        else pl.BlockSpec(w_bf.shape, lambda i: (0, 0), pipeline_mode=pl.Buffered(1)),
        base_spec, base_spec, *[const(t) for t in offset_tables],
        const(qg), const(kg), const(seg),
    ]
    scratch = [pltpu.VMEM((tm, D_MODEL), BF16), pltpu.VMEM((TILE // LANES, tm, LANES), F32)]
    if fuse:
        tables, gn = retention
        args += [*tables, gn]
        in_specs += [const(t) for t in (*tables, gn)]
        ret_specs = [pl.BlockSpec((tm, RET_V_W), lambda i: (i, 0)),
                     pl.BlockSpec((1, RET_HEADS, RET_DK, RET_DV), lambda i: (i // per_seq, 0, 0, 0))]
        ret_shapes = [jax.ShapeDtypeStruct((n, RET_V_W), out_dtype),
                      jax.ShapeDtypeStruct((batch, RET_HEADS, RET_DK, RET_DV), F32)]
        scratch += [pltpu.VMEM((tm, RET_OUT_W), F32), pltpu.VMEM((tm // RET_CHUNK, RET_QK_W, RET_CHUNK), BF16)]
    else:
        ret_specs = [pl.BlockSpec((tm, RET_OUT_W), lambda i: (i, 0))]
        ret_shapes = [jax.ShapeDtypeStruct((n, RET_OUT_W), out_dtype)]
    cast_specs, cast_shapes = [], []
    if cast_weights:
        cast_specs = [pl.BlockSpec(memory_space=pl.ANY)]
        cast_shapes = [jax.ShapeDtypeStruct(w_bf.shape, BF16)]
        scratch += [pltpu.VMEM((W_LOAD_SLOTS, D_MODEL, TILE), F32), pltpu.VMEM((2, D_MODEL, TILE), BF16),
                    pltpu.SemaphoreType.DMA((W_LOAD_SLOTS,)), pltpu.SemaphoreType.DMA((2,))]
    return pl.pallas_call(
        functools.partial(_in_proj_kernel, dils=dils, per_seq=per_seq, fuse_retention=fuse,
                          n_tails=len(tail_rows), cast_weights=cast_weights),
        grid=(n // tm,),
        in_specs=in_specs,
        out_specs=ret_specs + [qkv_spec(d) for d in dils]
        + [pl.BlockSpec((tm, GATE_W), lambda i: (i, 0))] + tail_specs + cast_specs,
        out_shape=ret_shapes
        + [jax.ShapeDtypeStruct((batch, d, seq_len // d, QKV_W), out_dtype) for d in dils]
        + [jax.ShapeDtypeStruct((n, GATE_W), out_dtype)] + tail_shapes + cast_shapes,
        scratch_shapes=scratch,
        compiler_params=pltpu.CompilerParams(
            dimension_semantics=("arbitrary",), vmem_limit_bytes=VMEM_LIMIT),
        name="in_proj",
    )(*args)


def _group_norm_gate(o, gain, gate):
    gate = gate.astype(F32)
    mu = jnp.mean(o, axis=-1, keepdims=True)
    d = o - mu
    var = jnp.mean(d * d, axis=-1, keepdims=True)
    return gate * jax.nn.sigmoid(gate) * (d * lax.rsqrt(var + EPS) * gain)


def _decayed_keys_t(k_keys, kd):
    return jnp.transpose(k_keys * kd).astype(BF16)


def _retention_specs(rows, idx):
    return [
        pl.BlockSpec((1, rows, RET_QK_W), lambda *a: (*idx(*a), 0)),
        pl.BlockSpec((1, rows, RET_QK_W), lambda *a: (*idx(*a), 1)),
        pl.BlockSpec((1, rows, RET_V_W), lambda *a: (*idx(*a), 1)),
        pl.BlockSpec((1, rows, RET_V_W), lambda *a: (*idx(*a), 2)),
    ]


def _table_specs(tables, ndim_grid):
    return [pl.BlockSpec(t.shape, lambda *a, nd=t.ndim: (0,) * nd) for t in tables]


def _retention_sample_kernel(q_ref, k_ref, v_ref, g_ref, s_in_ref, dec_ref, qd_ref, kd_ref, cd_ref,
                             gn_ref, o_ref, s_out_ref, kpad_ref, vpad_ref):
    @pl.when(pl.program_id(0) == 0)
    def _():
        kpad_ref[...] = jnp.zeros_like(kpad_ref)
        vpad_ref[...] = jnp.zeros_like(vpad_ref)

    kpad_ref[0:SAMPLE_PAD, :] = k_ref[0]
    vpad_ref[0:SAMPLE_PAD, :] = v_ref[0]
    issued = []
    for h in range(RET_HEADS):
        ks = slice(h * RET_DK, (h + 1) * RET_DK)
        vs = slice(h * RET_DV, (h + 1) * RET_DV)
        qb = q_ref[0, :, ks].astype(BF16)
        vb = vpad_ref[:, vs].astype(BF16)
        state = s_in_ref[0, h]
        scores = (_nt_dot(qb, kpad_ref[:, ks].astype(BF16)) * dec_ref[h]).astype(BF16)
        carried = _dot(qb, state.astype(BF16)) * qd_ref[h]
        s_out_ref[0, h] = state * cd_ref[h] + _dot(_decayed_keys_t(kpad_ref[:, ks], kd_ref[h]), vb)
        issued.append((scores, vb, carried))
    for h, (scores, vb, carried) in enumerate(issued):
        vs = slice(h * RET_DV, (h + 1) * RET_DV)
        o = _dot(scores, vb) + carried
        o_ref[0, :, vs] = _group_norm_gate(o, gn_ref[:, vs], g_ref[0, :, vs]).astype(o_ref.dtype)


def _attn_prompt_blocks(q_ref, kp_ref, kc_ref, vp_ref, vc_ref, bias_ref, o_ref, lse_ref, first):
    n_seq, n_blocks = q_ref.shape[1], q_ref.shape[2] // ATT_BLOCK
    assert first or n_seq == 1
    pair_w = 2 * ATT_DH
    low_q = lax.broadcasted_iota(jnp.int32, (ATT_BLOCK, pair_w), 1) < ATT_DH
    rows0, rows1 = slice(0, ATT_BLOCK), slice(ATT_BLOCK, 2 * ATT_BLOCK)

    def keys(prev_ref, cur_ref, sq, t, ps):
        if t > 0:
            return cur_ref[0, sq, (t - 1) * ATT_BLOCK:(t + 1) * ATT_BLOCK, ps]
        if first:
            return cur_ref[0, sq, 0:ATT_BLOCK, ps]
        return jnp.concatenate([prev_ref[0, sq, :, ps], cur_ref[0, sq, 0:ATT_BLOCK, ps]], axis=0)

    staged = []
    for sq, t in [(sq, t) for sq in range(n_seq) for t in range(n_blocks)]:
        for p in range(ATT_HEADS // 2):
            ps = slice(p * pair_w, (p + 1) * pair_w)
            qp = q_ref[0, sq, t * ATT_BLOCK:(t + 1) * ATT_BLOCK, ps]
            kcat = keys(kp_ref, kc_ref, sq, t, ps)
            q2 = jnp.concatenate([jnp.where(low_q, qp, 0.0), jnp.where(low_q, 0.0, qp)], axis=0)
            s = _nt_dot(q2, kcat) + bias_ref[p, :, 2 * ATT_BLOCK - kcat.shape[0]:]
            s = s.astype(BF16)
            m = jnp.max(s, axis=-1, keepdims=True)
            staged.append((sq, t, p, m.astype(F32), jnp.exp(s - m)))
    for sq, t, p, m, e in staged:
        ps = slice(p * pair_w, (p + 1) * pair_w)
        vcat = keys(vp_ref, vc_ref, sq, t, ps)
        low_k = lax.broadcasted_iota(jnp.int32, vcat.shape, 1) < ATT_DH
        p0 = _dot(e[rows0], jnp.where(low_k, vcat, 1.0))
        p1 = _dot(e[rows1], jnp.where(low_k, 1.0, vcat))
        den = pltpu.roll(jnp.where(low_q, p1, p0), ATT_DH, axis=1)
        ts = slice(t * ATT_BLOCK, (t + 1) * ATT_BLOCK)
        o_ref[0, sq, ts, ps] = jnp.where(low_q, p0, p1) / den
        lse_ref[0, sq, ts, ps] = jnp.where(low_q, m[rows0], m[rows1]) + jnp.log(den)


def _attn_prompt_kernel(q_ref, kp_ref, kc_ref, vp_ref, vc_ref, slot_ref, o_ref, lse_ref, bias_ref,
                        *, whole_sequences):
    first = pl.program_id(2) == 0

    @pl.when(jnp.logical_and(first, jnp.logical_and(pl.program_id(0) == 0, pl.program_id(1) == 0)))
    def _():
        for h in range(ATT_HEADS):
            slots = jnp.broadcast_to(slot_ref[h:h + 1, :], (ATT_BLOCK, slot_ref.shape[1]))
            rows = pltpu.roll(slots, 0, 1, stride=1, stride_axis=0)
            bias_ref[h // 2, (h % 2) * ATT_BLOCK:(h % 2 + 1) * ATT_BLOCK, :] = rows[:, :2 * ATT_BLOCK]

    if whole_sequences:
        _attn_prompt_blocks(q_ref, kp_ref, kc_ref, vp_ref, vc_ref, bias_ref, o_ref, lse_ref, True)
        return

    @pl.when(first)
    def _():
        _attn_prompt_blocks(q_ref, kp_ref, kc_ref, vp_ref, vc_ref, bias_ref, o_ref, lse_ref, True)

    @pl.when(jnp.logical_not(first))
    def _():
        _attn_prompt_blocks(q_ref, kp_ref, kc_ref, vp_ref, vc_ref, bias_ref, o_ref, lse_ref, False)


def _attn_prompt_group(qkv, slot_bias, step_blocks):
    b, dil, tr, _ = qkv.shape
    n_blocks = min(step_blocks, tr // ATT_BLOCK)
    n_seq = min(step_blocks // n_blocks, dil)
    rows = n_blocks * ATT_BLOCK
    cur = lambda c: pl.BlockSpec((1, n_seq, rows, ATT_OUT_W), lambda bi, r, j: (bi, r, j, c))
    prev = lambda c: pl.BlockSpec((1, n_seq, ATT_BLOCK, ATT_OUT_W),
                                  lambda bi, r, j: (bi, r, jnp.maximum(j * n_blocks - 1, 0), c))
    res_shape = jax.ShapeDtypeStruct((b, dil, tr, ATT_OUT_W), F32)
    return pl.pallas_call(
        functools.partial(_attn_prompt_kernel, whole_sequences=tr == rows),
        grid=(b, dil // n_seq, tr // rows),
        in_specs=[cur(0), prev(1), cur(1), prev(2), cur(2),
                  pl.BlockSpec(slot_bias.shape, lambda bi, r, j: (0, 0))],
        out_specs=[cur(0), cur(0)],
        out_shape=[res_shape, res_shape],
        scratch_shapes=[pltpu.VMEM((ATT_HEADS // 2, 2 * ATT_BLOCK, 2 * ATT_BLOCK), F32)],
        compiler_params=pltpu.CompilerParams(
            dimension_semantics=("arbitrary", "arbitrary", "arbitrary"), vmem_limit_bytes=VMEM_LIMIT),
        name=f"attn_prompt_d{dil}",
    )(qkv, qkv, qkv, qkv, qkv, slot_bias)


def _attn_sample_kernel(a0_ref, a1_ref, a2_ref, c0_ref, c1_ref, c2_ref, s0_ref, s1_ref, s2_ref, sn_ref,
                        o_ref, kn_ref, vn_ref, b0_ref, b1_ref, b2_ref, bn_ref):
    qkv_refs = (a0_ref, a1_ref, a2_ref)
    cache_refs = (c0_ref, c1_ref, c2_ref)
    bias_refs = (b0_ref, b1_ref, b2_ref)

    @pl.when(pl.program_id(0) == 0)
    def _():
        kn_ref[...] = jnp.zeros_like(kn_ref)
        vn_ref[...] = jnp.zeros_like(vn_ref)
        def rotated_rows(vec, width):
            rows = jnp.broadcast_to(vec, (SAMPLE_PAD, vec.shape[1]))
            return pltpu.roll(rows, 0, 1, stride=1, stride_axis=0)[:, :width]
        for g, (s_ref, b_ref) in enumerate(zip((s0_ref, s1_ref, s2_ref), bias_refs)):
            for h in range(ATT_HEADS):
                b_ref[h] = rotated_rows(s_ref[h:h + 1, :], b_ref.shape[2])
                bn_ref[g, h] = rotated_rows(sn_ref[g, h:h + 1, :], ATT_BLOCK)

    for g in range(N_GROUPS):
        gs = slice(g * ATT_OUT_W, (g + 1) * ATT_OUT_W)
        kn_ref[0:SAMPLE_PAD, gs] = qkv_refs[g][0, :, ATT_OUT_W:2 * ATT_OUT_W]
        vn_ref[0:SAMPLE_PAD, gs] = qkv_refs[g][0, :, 2 * ATT_OUT_W:3 * ATT_OUT_W]

    staged = []
    for h in range(ATT_HEADS):
        logits = []
        for g in range(N_GROUPS):
            hs = slice(g * ATT_OUT_W + h * ATT_DH, g * ATT_OUT_W + (h + 1) * ATT_DH)
            qh = qkv_refs[g][0, :, h * ATT_DH:(h + 1) * ATT_DH].astype(BF16)
            logits.append(_dot(qh, cache_refs[g][0, 0, h].astype(BF16)) + bias_refs[g][h])
            logits.append(_nt_dot(qh, kn_ref[:, hs].astype(BF16)) + bn_ref[g, h])
        m = functools.reduce(jnp.maximum, [jnp.max(x, axis=-1, keepdims=True) for x in logits])
        es = [jnp.exp(x - m) for x in logits]
        l = functools.reduce(jnp.add, [jnp.sum(e, axis=-1, keepdims=True) for e in es])
        staged.append(([e.astype(BF16) for e in es], l))
    for h, (es, l) in enumerate(staged):
        acc = jnp.zeros((SAMPLE_PAD, ATT_DH), F32)
        for g in range(N_GROUPS):
            hs = slice(g * ATT_OUT_W + h * ATT_DH, g * ATT_OUT_W + (h + 1) * ATT_DH)
            acc = acc + _nt_dot(es[2 * g], cache_refs[g][0, 1, h].astype(BF16))
            acc = acc + _dot(es[2 * g + 1], vn_ref[:, hs].astype(BF16))
        o_ref[0, :, h * ATT_DH:(h + 1) * ATT_DH] = acc / l


_N_RET_IN, _N_ATT_IN = 10, 10


def _sample_mixers_kernel(*refs):
    n_in = _N_RET_IN + _N_ATT_IN
    ret_in, att_in = refs[:_N_RET_IN], refs[_N_RET_IN:n_in]
    ret_o_ref, state_o_ref, att_o_ref = refs[n_in:n_in + 3]
    kv_o_refs = refs[n_in + 3:n_in + 3 + N_GROUPS]
    kpad_ref, vpad_ref, kn_ref, vn_ref = refs[n_in + 3 + N_GROUPS:n_in + 7 + N_GROUPS]
    _attn_sample_kernel(*att_in, att_o_ref, kn_ref, vn_ref, *refs[n_in + 7 + N_GROUPS:])
    _retention_sample_kernel(*ret_in, ret_o_ref, state_o_ref, kpad_ref, vpad_ref)
    for qkv_ref, kv_ref in zip(att_in[:N_GROUPS], kv_o_refs):
        for tok in range(kv_ref.shape[1]):
            for kv in range(2):
                for h in range(ATT_HEADS):
                    col = (1 + kv) * ATT_OUT_W + h * ATT_DH
                    kv_ref[0, tok, kv, h:h + 1, :] = qkv_ref[0, tok:tok + 1, col:col + ATT_DH]


def _sample_mixers(ret, state, tables, gn, qkvs, caches_t, slots, slots_new, n_tokens):
    b, p, _ = ret.shape
    state_spec = pl.BlockSpec((1, RET_HEADS, RET_DK, RET_DV), lambda bi: (bi, 0, 0, 0))
    ret_specs = _retention_specs(p, lambda bi: (bi, 0)) + [state_spec] + _table_specs(tables + (gn,), 1)
    att_specs = ([pl.BlockSpec((1, p, QKV_W), lambda bi: (bi, 0, 0)) for _ in qkvs]
                 + [pl.BlockSpec((1,) + c.shape[1:], lambda bi: (bi, 0, 0, 0, 0)) for c in caches_t]
                 + [pl.BlockSpec(x.shape, lambda bi: (0, 0)) for x in slots]
                 + [pl.BlockSpec(slots_new.shape, lambda bi: (0, 0, 0))])
    assert len(ret_specs) == _N_RET_IN and len(att_specs) == _N_ATT_IN
    kv_shape = (b, n_tokens, 2, ATT_HEADS, ATT_DH)
    return pl.pallas_call(
        _sample_mixers_kernel,
        grid=(b,),
        in_specs=ret_specs + att_specs,
        out_specs=[pl.BlockSpec((1, p, RET_V_W), lambda bi: (bi, 0, 0)), state_spec,
                   pl.BlockSpec((1, p, ATT_OUT_W), lambda bi: (bi, 0, 0))]
        + [pl.BlockSpec((1,) + kv_shape[1:], lambda bi: (bi, 0, 0, 0, 0))] * N_GROUPS,
        out_shape=[jax.ShapeDtypeStruct((b, p, RET_V_W), ret.dtype), jax.ShapeDtypeStruct(state.shape, F32),
                   jax.ShapeDtypeStruct((b, p, ATT_OUT_W), F32)]
        + [jax.ShapeDtypeStruct(kv_shape, F32)] * N_GROUPS,
        scratch_shapes=[pltpu.VMEM((RET_CHUNK, RET_QK_W), F32), pltpu.VMEM((RET_CHUNK, RET_V_W), F32),
                        pltpu.VMEM((ATT_BLOCK, ATT_W), F32), pltpu.VMEM((ATT_BLOCK, ATT_W), F32)]
        + [pltpu.VMEM((ATT_HEADS, p, c.shape[-1]), F32) for c in caches_t]
        + [pltpu.VMEM((N_GROUPS, ATT_HEADS, p, ATT_BLOCK), F32)],
        compiler_params=pltpu.CompilerParams(
            dimension_semantics=("arbitrary",), vmem_limit_bytes=VMEM_LIMIT),
        name="sample_mixers",
    )(ret, ret, ret, ret, state, *tables, gn, *qkvs, *caches_t, *slots, slots_new)


def _out_proj_kernel(*refs, dils):
    x_ref, ret_ref, ga_ref, gb_ref, ag_ref, wr_ref, wa_ref, wo_ref = refs[:8]
    if dils is None:
        att_ref, o_ref = refs[8:]
        att = att_ref[0]
    else:
        group_refs = refs[8:8 + 2 * N_GROUPS]
        o_ref = refs[8 + 2 * N_GROUPS]
        scratch = refs[9 + 2 * N_GROUPS:]
        tm = x_ref.shape[1]
        os, lses = [], []
        for g, dil in enumerate(dils):
            og_ref, lg_ref = group_refs[2 * g], group_refs[2 * g + 1]
            if dil == 1:
                os.append(og_ref[0, 0]); lses.append(lg_ref[0, 0])
                continue
            so_ref, sl_ref = scratch[2 * (g - 1)], scratch[2 * (g - 1) + 1]
            n_chunks = ATT_OUT_W // LANES
            for c in range(n_chunks):
                cs = slice(c * LANES, (c + 1) * LANES)
                for r in range(dil):
                    so_ref[c, pl.ds(r, tm // dil, stride=dil), :] = og_ref[0, r, :, cs]
                    sl_ref[c, pl.ds(r, tm // dil, stride=dil), :] = lg_ref[0, r, :, cs]
            os.append(jnp.concatenate([so_ref[c] for c in range(n_chunks)], axis=1))
            lses.append(jnp.concatenate([sl_ref[c] for c in range(n_chunks)], axis=1))
        mx = functools.reduce(jnp.maximum, lses)
        ws = [jnp.exp(l - mx) for l in lses]
        att = functools.reduce(jnp.add, [w * o for w, o in zip(ws, os)]) / functools.reduce(jnp.add, ws)
    ag = ag_ref[0].astype(F32)
    u = (ag * jax.nn.sigmoid(ag) * att).astype(BF16)
    o_b = _dot(u, wa_ref[...])
    o_a = _dot(ret_ref[0].astype(BF16), wr_ref[...])
    merged = jax.nn.sigmoid(ga_ref[0].astype(F32)) * o_a + jax.nn.sigmoid(gb_ref[0].astype(F32)) * o_b
    o_ref[0] = x_ref[0] + _dot(merged.astype(BF16), wo_ref[...])


def _out_proj(x, ret, gates, wr, wa, wo, tm, att=None, groups=None):
    b, t, _ = x.shape
    row = lambda w, c: pl.BlockSpec((1, tm, w), lambda bi, i: (bi, i, c))
    full = lambda a: pl.BlockSpec(a.shape, lambda bi, i: (0, 0))
    in_specs = [row(D_MODEL, 0), row(RET_V_W, 0), row(D_MODEL, 0), row(D_MODEL, 1),
                row(ATT_OUT_W, 2 * D_MODEL // ATT_OUT_W), full(wr), full(wa), full(wo)]
    args = [x, ret, gates, gates, gates, wr, wa, wo]
    scratch = []
    if groups is None:
        dils = None
        in_specs.append(row(ATT_OUT_W, 0))
        args.append(att)
    else:
        dils = tuple(o.shape[1] for o, _ in groups)
        for (o, lse), d in zip(groups, dils):
            spec = pl.BlockSpec((1, d, tm // d, ATT_OUT_W), lambda bi, i: (bi, 0, i, 0))
            in_specs += [spec, spec]
            args += [o, lse]
            if d > 1:
                scratch += [pltpu.VMEM((ATT_OUT_W // LANES, tm, LANES), F32)] * 2
    return pl.pallas_call(
        functools.partial(_out_proj_kernel, dils=dils),
        grid=(b, t // tm),
        in_specs=in_specs,
        out_specs=row(D_MODEL, 0),
        out_shape=jax.ShapeDtypeStruct((b, t, D_MODEL), F32),
        scratch_shapes=scratch,
        compiler_params=pltpu.CompilerParams(
            dimension_semantics=("parallel", "parallel"), vmem_limit_bytes=VMEM_LIMIT),
        name="out_proj",
    )(*args)


def _rotary_tables(bases, offsets):
    half = RET_DK // 2
    inv = ROPE_BASE ** (-jnp.arange(half, dtype=F32) / half)
    inv2 = jnp.concatenate([inv, inv])
    sign = jnp.concatenate([-jnp.ones((half,), F32), jnp.ones((half,), F32)])
    a = bases.astype(F32)[:, None, None] * inv2
    b = offsets.astype(F32)[:, None] * inv2
    cb, sb = jnp.cos(b), jnp.sin(b)
    return jnp.cos(a), jnp.sin(a), cb, sb, sign * cb, sign * sb


def _retention_tables(c, rows):
    log_g = jnp.log1p(-(2.0 ** (-5.0 - jnp.arange(RET_HEADS, dtype=F32))))
    i = jnp.arange(c, dtype=F32)
    diff = i[:, None] - i[None, :]
    decay = jnp.where(diff[None] >= 0, jnp.exp(jnp.maximum(diff, 0.0)[None] * log_g[:, None, None]), 0.0)
    q_decay = jnp.exp((i + 1.0)[None, :] * log_g[:, None])
    k_decay = jnp.exp((c - 1.0 - i)[None, :] * log_g[:, None])
    chunk_decay = jnp.exp(c * log_g)
    dec = jnp.zeros((RET_HEADS, rows, RET_CHUNK), F32).at[:, :c, :c].set(decay)
    qd = jnp.zeros((RET_HEADS, rows, 1), F32).at[:, :c, 0].set(q_decay)
    kd = jnp.zeros((RET_HEADS, RET_CHUNK, 1), F32).at[:, :c, 0].set(k_decay)
    cd = jnp.broadcast_to(chunk_decay[:, None, None], (RET_HEADS, 1, RET_DV))
    return dec, qd, kd, cd


def _t5_bucket(dist):
    max_exact = REL_BUCKETS // 2
    d = jnp.maximum(dist.astype(F32), 1.0)
    large = max_exact + (jnp.log(d / max_exact) / math.log(REL_MAX_DIST / max_exact)
                         * (REL_BUCKETS - max_exact)).astype(jnp.int32)
    large = jnp.minimum(large, REL_BUCKETS - 1)
    return jnp.where(dist < max_exact, dist, large)


def _group_bias(rel_bias, g, dil, slots):
    dist = dil * jnp.asarray(slots, dtype=jnp.int32)
    return rel_bias[_t5_bucket(dist)][:, g * ATT_HEADS:(g + 1) * ATT_HEADS].astype(F32).T


def _neg(heads, n):
    return jnp.full((heads, n), NEG, F32)


def _dilate(v, dil):
    heads, n = v.shape
    return jnp.stack([v] + [_neg(heads, n)] * (dil - 1), axis=-1).reshape(heads, n * dil)


def _prompt_slots(tb_rev):
    heads = tb_rev.shape[0]
    return jnp.concatenate([tb_rev, _neg(heads, 4 * ATT_BLOCK - N_KEYS)], axis=1)


def _sample_slots(tb, tb_rev, win, dil):
    heads = tb.shape[0]
    s_c = jnp.concatenate([_dilate(tb_rev[:, :N_KEYS - 1], dil), _neg(heads, ATT_BLOCK)], axis=1)
    back = [tb[:, k // dil:k // dil + 1] if k % dil == 0 else _neg(heads, 1) for k in range(SAMPLE_PAD - 1, 0, -1)]
    s_n = jnp.concatenate([tb[:, 0:1], _neg(heads, 2 * ATT_BLOCK - SAMPLE_PAD)] + back, axis=1)
    return s_c, s_n


def _kv_rows_t(tail_t):
    b, _, _, rows = tail_t.shape
    return jnp.transpose(tail_t.reshape(b, 2, ATT_HEADS, ATT_DH, rows), (0, 4, 1, 2, 3))


def kernel(x_prompt, x_sample, cache_kv_w128, cache_kv_w512, cache_kv_w2048, state_retention,
           w_norm, w_in, q_norm, k_norm, rel_bias, ret_norm, w_proj_ret, w_proj_att, w_out):
    assert w_in.shape[0] == 1
    bp, t, _ = x_prompt.shape
    bs, ts, _ = x_sample.shape
    dils = tuple(d for _, d in ATT_GROUPS)
    assert t % (ATT_BLOCK * max(dils)) == 0 and ts <= SAMPLE_PAD
    caches = (cache_kv_w128[0], cache_kv_w512[0], cache_kv_w2048[0])
    for cch, (win, _) in zip(caches, ATT_GROUPS):
        assert cch.shape[1] == win and win <= PAST_LEN

    wn = w_norm[0].reshape(1, D_MODEL)
    qg =jnp.tile(q_norm[0] * (ATT_DH ** -0.5), TILE // ATT_DH).reshape(1, TILE)
    kg = jnp.tile(k_norm[0], TILE // ATT_DH).reshape(1, TILE)
    gn = ret_norm[0].reshape(1, RET_V_W)
    wr = w_proj_ret[0].astype(BF16)
    wa = w_proj_att[0].astype(BF16)
    wo = w_out[0].astype(BF16)
    hid = jnp.arange(TILE // 2) // ATT_DH
    seg_mean = jnp.where(hid[:, None] == hid[None, :], 1.0 / ATT_DH, 0.0).astype(BF16)
    asc, desc = tuple(range(N_KEYS)), tuple(range(N_KEYS - 1, -1, -1))
    group_bias = [(_group_bias(rel_bias, g, d, asc), _group_bias(rel_bias, g, d, desc)) for g, d in enumerate(dils)]

    pad = SAMPLE_PAD
    ns = bs * pad
    xs = jnp.pad(x_sample, ((0, 0), (0, pad - ts), (0, 0))).reshape(ns, D_MODEL)
    rot_s = _rotary_tables(jnp.full((1,), PAST_LEN), jnp.tile(jnp.arange(pad), bs))
    ret_in_s, *qkv_s, gates_s, w_in_bf = _in_proj(xs, ns, (1,) * N_GROUPS, (), F32, wn, w_in[0], rot_s,
                                                  qg, kg, seg_mean, tm=ns)

    tm_p = 256
    rot_p = _rotary_tables(jnp.arange(0, t, tm_p), jnp.arange(tm_p))
    tail_p_rows = tuple(min(w, t) for w, _ in ATT_GROUPS)
    ret_p, state_p, *rest = _in_proj(
        x_prompt.reshape(bp * t, D_MODEL), t, dils, tail_p_rows, BF16, wn, w_in_bf, rot_p, qg, kg, seg_mean,
        tm=tm_p, retention=(_retention_tables(RET_CHUNK, RET_CHUNK), gn))
    qkv_p, gates_p, tails_p = rest[:N_GROUPS], rest[N_GROUPS], rest[N_GROUPS + 1:]
    groups = [_attn_prompt_group(qkv_p[g], _prompt_slots(group_bias[g][1]), step_blocks=8)
              for g in range(N_GROUPS)]
    y_p = _out_proj(x_prompt, ret_p.reshape(bp, t, RET_V_W), gates_p.reshape(bp, t, GATE_W), wr, wa, wo,
                    tm=512, groups=groups)

    qkv_s =[a.reshape(bs, pad, QKV_W) for a in qkv_s]
    caches_t = [jnp.transpose(c, (0, 2, 3, 4, 1)) for c in caches]
    ss = [_sample_slots(*group_bias[g], win, d) for g, (win, d) in enumerate(ATT_GROUPS)]
    ret_s, state_s, att_s, *kv_s = _sample_mixers(
        ret_in_s.reshape(bs, pad, RET_OUT_W), state_retention[0], _retention_tables(ts, pad), gn,
        qkv_s, caches_t, [c for c, _ in ss], jnp.stack([n for _, n in ss]), ts)
    y_s = _out_proj(xs.reshape(1, ns, D_MODEL), ret_s.reshape(1, ns, RET_V_W), gates_s.reshape(1, ns, GATE_W),
                    wr, wa, wo, tm=ns, att=att_s.reshape(1, ns, ATT_OUT_W))
    y_s = y_s.reshape(bs, pad, D_MODEL)[:, :ts]

    kv_p = [_kv_rows_t(tt)[None] for tt in tails_p]
    return (y_p, y_s, state_p[None], state_s[None], kv_p[0], kv_p[1], kv_p[2],
            kv_s[0][None], kv_s[1][None], kv_s[2][None])
```

```python
import functools
import math

import jax
import jax.numpy as jnp
from jax import lax
from jax.experimental import pallas as pl
from jax.experimental.pallas import tpu as pltpu

D_MODEL = 1024
PAST_LEN = 16384
RET_HEADS = 4
RET_DK = 128
RET_DV = 256
RET_CHUNK = 128
ROPE_BASE = 10000.0
ATT_GROUPS = ((128, 1), (512, 4), (2048, 16))
N_GROUPS = 3
ATT_HEADS = 8
ATT_DH = 64
REL_BUCKETS = 32
REL_MAX_DIST = 2048
EPS = 1e-6

RET_QK_W = RET_HEADS * RET_DK
RET_V_W = RET_HEADS * RET_DV
ATT_W = N_GROUPS * ATT_HEADS * ATT_DH
ATT_OUT_W = ATT_HEADS * ATT_DH
IN_W = 2 * RET_QK_W + 2 * RET_V_W + 3 * ATT_W + ATT_OUT_W + 2 * D_MODEL

LANES = 128
TILE = 512
RET_OUT_W = 2 * RET_QK_W + 2 * RET_V_W
QKV_W = 3 * ATT_OUT_W
GATE_W = 2 * D_MODEL + ATT_OUT_W
ATT_BLOCK = 128
N_KEYS = ATT_BLOCK + 1
SAMPLE_PAD = 8
NEG = -1e30
VMEM_LIMIT = 48 * 1024 * 1024
W_LOAD_SLOTS = 4

F32 = jnp.float32
BF16 = jnp.bfloat16


def _nt_dot(a, b):
    return lax.dot_general(a, b, (((1,), (1,)), ((), ())), preferred_element_type=F32)


def _dot(a, b):
    return jnp.dot(a, b, preferred_element_type=F32)


def _proj_schedule():
    sched = [(0, 0, 0, "rot_q"), (RET_QK_W, 0, RET_QK_W, "rot_k")]
    for k in range(2 * RET_V_W // TILE):
        sched.append((2 * RET_QK_W + k * TILE, 0, 2 * RET_QK_W + k * TILE, "plain"))
    att0 = 2 * RET_QK_W + 2 * RET_V_W
    for g in range(N_GROUPS):
        for kind, epi in enumerate(("norm_q", "norm_k", "plain")):
            sched.append((att0 + kind * ATT_W + g * ATT_OUT_W, 1 + g, kind * ATT_OUT_W, epi))
    gate0 = att0 + 3 * ATT_W
    for k in range(2 * D_MODEL // TILE):
        sched.append((gate0 + ATT_OUT_W + k * TILE, 4, k * TILE, "plain"))
    sched.append((gate0, 4, 2 * D_MODEL, "plain"))
    return sched


_SCHEDULE = _proj_schedule()


def _in_proj_kernel(*refs, dils, per_seq, fuse_retention, n_tails, cast_weights):
    (x_ref, wn_ref, w_ref, ca_ref, sa_ref, cb_ref, sb_ref, cbs_ref, sbs_ref,
     qg_ref, kg_ref, seg_ref) = refs[:12]
    refs = refs[12:]
    if cast_weights:
        *refs, wf_ref, wb_ref, sem_in, sem_out = refs
        wbf_out_ref = refs.pop(-(4 if fuse_retention else 2) - 1)

        load_slots = wf_ref.shape[0]

        def tile_load(step):
            return pltpu.make_async_copy(w_ref.at[:, pl.ds(_SCHEDULE[step][0], TILE)],
                                         wf_ref.at[step % load_slots], sem_in.at[step % load_slots])

        def tile_store(step):
            return pltpu.make_async_copy(wb_ref.at[step % 2],
                                         wbf_out_ref.at[:, pl.ds(_SCHEDULE[step][0], TILE)], sem_out.at[step % 2])

        for ahead in range(load_slots - 1):
            tile_load(ahead).start()
    if fuse_retention:
        dec_ref, qd_ref, kd_ref, cd_ref, gn_ref, ret_ref, s_ref = refs[:7]
        refs = refs[7:]
    else:
        ret_ref = refs[0]
        refs = refs[1:]
    a0_ref, a1_ref, a2_ref, gate_ref = refs[:4]
    tail_refs = refs[4:4 + n_tails]
    h_ref, y_ref = refs[4 + n_tails:6 + n_tails]
    refs = refs[6 + n_tails:]
    r_ref = refs[0] if fuse_retention else ret_ref
    kt_ref = refs[1] if fuse_retention else None
    out_refs = (r_ref, a0_ref, a1_ref, a2_ref, gate_ref)

    if fuse_retention:
        @pl.when(pl.program_id(0) % per_seq == 0)
        def _():
            s_ref[...] = jnp.zeros_like(s_ref)

    x = x_ref[...]
    tm = x.shape[0]
    ms = jnp.mean(x * x, axis=-1, keepdims=True)
    h_ref[...] = (x * lax.rsqrt(ms + EPS) * wn_ref[...]).astype(BF16)

    ca, sa = ca_ref[0], sa_ref[0]
    cos = ca * cb_ref[...] - sa * sb_ref[...]
    sin = sa * cbs_ref[...] + ca * sbs_ref[...]

    def rotary(y, scale):
        parts = []
        for hh in range(TILE // RET_DK):
            yh = y[:, hh * RET_DK:(hh + 1) * RET_DK]
            parts.append((yh * cos + pltpu.roll(yh, RET_DK // 2, axis=1) * sin) * scale)
        return jnp.concatenate(parts, axis=1)

    def head_rms(y, gain):
        y2 = (y * y).astype(BF16)
        half = TILE // 2
        ms = jnp.concatenate([_dot(y2[:, :half], seg_ref[...]), _dot(y2[:, half:], seg_ref[...])], axis=1)
        return y * lax.rsqrt(ms + EPS) * gain

    epilogues = {
        "plain": lambda y: y,
        "rot_q": lambda y: rotary(y, 1.0),
        "rot_k": lambda y: rotary(y, RET_DK ** -0.5),
        "norm_q": lambda y: head_rms(y, qg_ref[...]),
        "norm_k": lambda y: head_rms(y, kg_ref[...]),
    }

    kt_base = jnp.minimum(pl.program_id(0), 0)

    def retention_issue(c, h):
        rows = slice(c * RET_CHUNK, (c + 1) * RET_CHUNK)
        ks = slice(h * RET_DK, (h + 1) * RET_DK)
        ks2 = slice(RET_QK_W + h * RET_DK, RET_QK_W + (h + 1) * RET_DK)
        vs_in = slice(2 * RET_QK_W + h * RET_DV, 2 * RET_QK_W + (h + 1) * RET_DV)
        qb = r_ref[rows, ks].astype(BF16)
        vb = r_ref[rows, vs_in].astype(BF16)
        state = s_ref[0, h]
        scores = (_nt_dot(qb, r_ref[rows, ks2].astype(BF16)) * dec_ref[h]).astype(BF16)
        carried = _dot(qb, state.astype(BF16)) * qd_ref[h]
        s_ref[0, h] = state * cd_ref[h] + _dot(kt_ref[kt_base + c, ks, :], vb)
        return rows, h, scores, vb, carried

    def retention_finish(rows, h, scores, vb, carried):
        vs = slice(h * RET_DV, (h + 1) * RET_DV)
        gs_in = slice(2 * RET_QK_W + RET_V_W + h * RET_DV, 2 * RET_QK_W + RET_V_W + (h + 1) * RET_DV)
        o = _dot(scores, vb) + carried
        ret_ref[rows, vs] = _group_norm_gate(o, gn_ref[:, vs], r_ref[rows, gs_in]).astype(ret_ref.dtype)

    def store_tile(val, out_idx, out_col):
        o_ref = out_refs[out_idx]
        ocs = slice(out_col, out_col + TILE)
        if not 1 <= out_idx <= N_GROUPS:
            o_ref[:, ocs] = val.astype(o_ref.dtype)
            return
        g, kind = out_idx - 1, out_col // ATT_OUT_W
        if kind > 0 and tail_refs:
            t_ref = tail_refs[g]
            t_ref[0, kind - 1] = jnp.transpose(val)[:, tm - t_ref.shape[3]:]
        dil = dils[g]
        if dil == 1:
            o_ref[0, 0, :, ocs] = val.astype(o_ref.dtype)
            return
        for c in range(TILE // LANES):
            y_ref[c] = val[:, c * LANES:(c + 1) * LANES]
            for r in range(dil):
                o_ref[0, r, :, out_col + c * LANES:out_col + (c + 1) * LANES] = (
                    y_ref[c, pl.ds(r, tm // dil, stride=dil), :].astype(o_ref.dtype))

    n_ret_tiles = sum(1 for t in _SCHEDULE if t[1] == 0)
    units = [(c, h) for c in range(tm // RET_CHUNK) for h in range(RET_HEADS)] if fuse_retention else []
    assert len(units) <= len(_SCHEDULE) - n_ret_tiles
    for step, (w_col, out_idx, out_col, epi) in enumerate(_SCHEDULE):
        if fuse_retention and step == 2:
            for c, h in units:
                rows = slice(c * RET_CHUNK, (c + 1) * RET_CHUNK)
                kt_ref[c, h * RET_DK:(h + 1) * RET_DK, :] = _decayed_keys_t(
                    r_ref[rows, RET_QK_W + h * RET_DK:RET_QK_W + (h + 1) * RET_DK], kd_ref[h])
        unit = units[step - n_ret_tiles] if 0 <= step - n_ret_tiles < len(units) else None
        issued = retention_issue(*unit) if unit else None
        if cast_weights:
            if step + load_slots - 1 < len(_SCHEDULE):
                tile_load(step + load_slots - 1).start()
            tile_load(step).wait()
            if step >= 2:
                tile_store(step - 2).wait()
            wb_ref[step % 2] = wf_ref[step % load_slots].astype(BF16)
            tile_store(step).start()
            w_tile = wb_ref[step % 2]
        else:
            w_tile = w_ref[:, w_col:w_col + TILE]
        store_tile(epilogues[epi](_dot(h_ref[...], w_tile)), out_idx, out_col)
        if unit:
            retention_finish(*issued)
    if cast_weights:
        tile_store(len(_SCHEDULE) - 2).wait()
        tile_store(len(_SCHEDULE) - 1).wait()


def _in_proj(x2d, seq_len, dils, tail_rows, out_dtype, wn, w_bf, rot, qg, kg, seg, tm, retention=None):
    n = x2d.shape[0]
    batch = n // seq_len
    per_seq = seq_len // tm
    fuse = retention is not None
    cast_weights = w_bf.dtype == F32
    assert not cast_weights or n == tm
    const = lambda a: pl.BlockSpec(a.shape, lambda i, nd=a.ndim: (0,) * nd)
    base_spec = pl.BlockSpec((1, 1, RET_DK), lambda i: (i % per_seq, 0, 0))
    qkv_spec = lambda d: pl.BlockSpec((1, d, tm // d, QKV_W), lambda i: (i // per_seq, 0, i % per_seq, 0))
    tail_specs, tail_shapes = [], []
    for rows in tail_rows:
        width = min(tm, rows)
        first = (seq_len - rows) // tm if rows >= tm else per_seq
        tail_specs.append(pl.BlockSpec(
            (1, 2, ATT_OUT_W, width),
            lambda i, first=first: (i // per_seq, 0, 0, jnp.maximum(i % per_seq - first, 0))))
        tail_shapes.append(jax.ShapeDtypeStruct((batch, 2, ATT_OUT_W, rows), F32))
    ca, sa, *offset_tables = rot
    args = [x2d, wn, w_bf, ca, sa, *offset_tables, qg, kg, seg]
    in_specs = [
        pl.BlockSpec((tm, D_MODEL), lambda i: (i, 0)),
        const(wn),
        pl.BlockSpec(memory_space=pl.ANY) if cast_weights
        else pl.BlockSpec(w_bf.shape, lambda i: (0, 0), pipeline_mode=pl.Buffered(1)),
        base_spec, base_spec, *[const(t) for t in offset_tables],
        const(qg), const(kg), const(seg),
    ]
    scratch = [pltpu.VMEM((tm, D_MODEL), BF16), pltpu.VMEM((TILE // LANES, tm, LANES), F32)]
    if fuse:
        tables, gn = retention
        args += [*tables, gn]
        in_specs += [const(t) for t in (*tables, gn)]
        ret_specs = [pl.BlockSpec((tm, RET_V_W), lambda i: (i, 0)),
                     pl.BlockSpec((1, RET_HEADS, RET_DK, RET_DV), lambda i: (i // per_seq, 0, 0, 0))]
        ret_shapes = [jax.ShapeDtypeStruct((n, RET_V_W), out_dtype),
                      jax.ShapeDtypeStruct((batch, RET_HEADS, RET_DK, RET_DV), F32)]
        scratch += [pltpu.VMEM((tm, RET_OUT_W), F32), pltpu.VMEM((tm // RET_CHUNK, RET_QK_W, RET_CHUNK), BF16)]
    else:
        ret_specs = [pl.BlockSpec((tm, RET_OUT_W), lambda i: (i, 0))]
        ret_shapes = [jax.ShapeDtypeStruct((n, RET_OUT_W), out_dtype)]
    cast_specs, cast_shapes = [], []
    if cast_weights:
        cast_specs = [pl.BlockSpec(memory_space=pl.ANY)]
        cast_shapes = [jax.ShapeDtypeStruct(w_bf.shape, BF16)]
        scratch += [pltpu.VMEM((W_LOAD_SLOTS, D_MODEL, TILE), F32), pltpu.VMEM((2, D_MODEL, TILE), BF16),
                    pltpu.SemaphoreType.DMA((W_LOAD_SLOTS,)), pltpu.SemaphoreType.DMA((2,))]
    return pl.pallas_call(
        functools.partial(_in_proj_kernel, dils=dils, per_seq=per_seq, fuse_retention=fuse,
                          n_tails=len(tail_rows), cast_weights=cast_weights),
        grid=(n // tm,),
        in_specs=in_specs,
        out_specs=ret_specs + [qkv_spec(d) for d in dils]
        + [pl.BlockSpec((tm, GATE_W), lambda i: (i, 0))] + tail_specs + cast_specs,
        out_shape=ret_shapes
        + [jax.ShapeDtypeStruct((batch, d, seq_len // d, QKV_W), out_dtype) for d in dils]
        + [jax.ShapeDtypeStruct((n, GATE_W), out_dtype)] + tail_shapes + cast_shapes,
        scratch_shapes=scratch,
        compiler_params=pltpu.CompilerParams(
            dimension_semantics=("arbitrary",), vmem_limit_bytes=VMEM_LIMIT),
        name="in_proj",
    )(*args)


def _group_norm_gate(o, gain, gate):
    gate = gate.astype(F32)
    mu = jnp.mean(o, axis=-1, keepdims=True)
    d = o - mu
    var = jnp.mean(d * d, axis=-1, keepdims=True)
    return gate * jax.nn.sigmoid(gate) * (d * lax.rsqrt(var + EPS) * gain)


def _decayed_keys_t(k_keys, kd):
    return jnp.transpose(k_keys * kd).astype(BF16)


def _retention_specs(rows, idx):
    return [
        pl.BlockSpec((1, rows, RET_QK_W), lambda *a: (*idx(*a), 0)),
        pl.BlockSpec((1, rows, RET_QK_W), lambda *a: (*idx(*a), 1)),
        pl.BlockSpec((1, rows, RET_V_W), lambda *a: (*idx(*a), 1)),
        pl.BlockSpec((1, rows, RET_V_W), lambda *a: (*idx(*a), 2)),
    ]


def _table_specs(tables, ndim_grid):
    return [pl.BlockSpec(t.shape, lambda *a, nd=t.ndim: (0,) * nd) for t in tables]


def _retention_sample_kernel(q_ref, k_ref, v_ref, g_ref, s_in_ref, dec_ref, qd_ref, kd_ref, cd_ref,
                             gn_ref, o_ref, s_out_ref, kpad_ref, vpad_ref):
    @pl.when(pl.program_id(0) == 0)
    def _():
        kpad_ref[...] = jnp.zeros_like(kpad_ref)
        vpad_ref[...] = jnp.zeros_like(vpad_ref)

    kpad_ref[0:SAMPLE_PAD, :] = k_ref[0]
    vpad_ref[0:SAMPLE_PAD, :] = v_ref[0]
    issued = []
    for h in range(RET_HEADS):
        ks = slice(h * RET_DK, (h + 1) * RET_DK)
        vs = slice(h * RET_DV, (h + 1) * RET_DV)
        qb = q_ref[0, :, ks].astype(BF16)
        vb = vpad_ref[:, vs].astype(BF16)
        state = s_in_ref[0, h]
        scores = (_nt_dot(qb, kpad_ref[:, ks].astype(BF16)) * dec_ref[h]).astype(BF16)
        carried = _dot(qb, state.astype(BF16)) * qd_ref[h]
        s_out_ref[0, h] = state * cd_ref[h] + _dot(_decayed_keys_t(kpad_ref[:, ks], kd_ref[h]), vb)
        issued.append((scores, vb, carried))
    for h, (scores, vb, carried) in enumerate(issued):
        vs = slice(h * RET_DV, (h + 1) * RET_DV)
        o = _dot(scores, vb) + carried
        o_ref[0, :, vs] = _group_norm_gate(o, gn_ref[:, vs], g_ref[0, :, vs]).astype(o_ref.dtype)


def _attn_prompt_blocks(q_ref, kp_ref, kc_ref, vp_ref, vc_ref, bias_ref, o_ref, lse_ref, first):
    n_seq, n_blocks = q_ref.shape[1], q_ref.shape[2] // ATT_BLOCK
    assert first or n_seq == 1
    pair_w = 2 * ATT_DH
    low_q = lax.broadcasted_iota(jnp.int32, (ATT_BLOCK, pair_w), 1) < ATT_DH
    rows0, rows1 = slice(0, ATT_BLOCK), slice(ATT_BLOCK, 2 * ATT_BLOCK)

    def keys(prev_ref, cur_ref, sq, t, ps):
        if t > 0:
            return cur_ref[0, sq, (t - 1) * ATT_BLOCK:(t + 1) * ATT_BLOCK, ps]
        if first:
            return cur_ref[0, sq, 0:ATT_BLOCK, ps]
        return jnp.concatenate([prev_ref[0, sq, :, ps], cur_ref[0, sq, 0:ATT_BLOCK, ps]], axis=0)

    staged = []
    for sq, t in [(sq, t) for sq in range(n_seq) for t in range(n_blocks)]:
        for p in range(ATT_HEADS // 2):
            ps = slice(p * pair_w, (p + 1) * pair_w)
            qp = q_ref[0, sq, t * ATT_BLOCK:(t + 1) * ATT_BLOCK, ps]
            kcat = keys(kp_ref, kc_ref, sq, t, ps)
            q2 = jnp.concatenate([jnp.where(low_q, qp, 0.0), jnp.where(low_q, 0.0, qp)], axis=0)
            s = _nt_dot(q2, kcat) + bias_ref[p, :, 2 * ATT_BLOCK - kcat.shape[0]:]
            s = s.astype(BF16)
            m = jnp.max(s, axis=-1, keepdims=True)
            staged.append((sq, t, p, m.astype(F32), jnp.exp(s - m)))
    for sq, t, p, m, e in staged:
        ps = slice(p * pair_w, (p + 1) * pair_w)
        vcat = keys(vp_ref, vc_ref, sq, t, ps)
        low_k = lax.broadcasted_iota(jnp.int32, vcat.shape, 1) < ATT_DH
        p0 = _dot(e[rows0], jnp.where(low_k, vcat, 1.0))
        p1 = _dot(e[rows1], jnp.where(low_k, 1.0, vcat))
        den = pltpu.roll(jnp.where(low_q, p1, p0), ATT_DH, axis=1)
        ts = slice(t * ATT_BLOCK, (t + 1) * ATT_BLOCK)
        o_ref[0, sq, ts, ps] = jnp.where(low_q, p0, p1) / den
        lse_ref[0, sq, ts, ps] = jnp.where(low_q, m[rows0], m[rows1]) + jnp.log(den)


def _attn_prompt_kernel(q_ref, kp_ref, kc_ref, vp_ref, vc_ref, slot_ref, o_ref, lse_ref, bias_ref,
                        *, whole_sequences):
    first = pl.program_id(2) == 0

    @pl.when(jnp.logical_and(first, jnp.logical_and(pl.program_id(0) == 0, pl.program_id(1) == 0)))
    def _():
        for h in range(ATT_HEADS):
            slots = jnp.broadcast_to(slot_ref[h:h + 1, :], (ATT_BLOCK, slot_ref.shape[1]))
            rows = pltpu.roll(slots, 0, 1, stride=1, stride_axis=0)
            bias_ref[h // 2, (h % 2) * ATT_BLOCK:(h % 2 + 1) * ATT_BLOCK, :] = rows[:, :2 * ATT_BLOCK]

    if whole_sequences:
        _attn_prompt_blocks(q_ref, kp_ref, kc_ref, vp_ref, vc_ref, bias_ref, o_ref, lse_ref, True)
        return

    @pl.when(first)
    def _():
        _attn_prompt_blocks(q_ref, kp_ref, kc_ref, vp_ref, vc_ref, bias_ref, o_ref, lse_ref, True)

    @pl.when(jnp.logical_not(first))
    def _():
        _attn_prompt_blocks(q_ref, kp_ref, kc_ref, vp_ref, vc_ref, bias_ref, o_ref, lse_ref, False)


def _attn_prompt_group(qkv, slot_bias, step_blocks):
    b, dil, tr, _ = qkv.shape
    n_blocks = min(step_blocks, tr // ATT_BLOCK)
    n_seq = min(step_blocks // n_blocks, dil)
    rows = n_blocks * ATT_BLOCK
    cur = lambda c: pl.BlockSpec((1, n_seq, rows, ATT_OUT_W), lambda bi, r, j: (bi, r, j, c))
    prev = lambda c: pl.BlockSpec((1, n_seq, ATT_BLOCK, ATT_OUT_W),
                                  lambda bi, r, j: (bi, r, jnp.maximum(j * n_blocks - 1, 0), c))
    res_shape = jax.ShapeDtypeStruct((b, dil, tr, ATT_OUT_W), F32)
    return pl.pallas_call(
        functools.partial(_attn_prompt_kernel, whole_sequences=tr == rows),
        grid=(b, dil // n_seq, tr // rows),
        in_specs=[cur(0), prev(1), cur(1), prev(2), cur(2),
                  pl.BlockSpec(slot_bias.shape, lambda bi, r, j: (0, 0))],
        out_specs=[cur(0), cur(0)],
        out_shape=[res_shape, res_shape],
        scratch_shapes=[pltpu.VMEM((ATT_HEADS // 2, 2 * ATT_BLOCK, 2 * ATT_BLOCK), F32)],
        compiler_params=pltpu.CompilerParams(
            dimension_semantics=("arbitrary", "arbitrary", "arbitrary"), vmem_limit_bytes=VMEM_LIMIT),
        name=f"attn_prompt_d{dil}",
    )(qkv, qkv, qkv, qkv, qkv, slot_bias)


def _attn_sample_kernel(a0_ref, a1_ref, a2_ref, c0_ref, c1_ref, c2_ref, s0_ref, s1_ref, s2_ref, sn_ref,
                        o_ref, kn_ref, vn_ref, b0_ref, b1_ref, b2_ref, bn_ref):
    qkv_refs = (a0_ref, a1_ref, a2_ref)
    cache_refs = (c0_ref, c1_ref, c2_ref)
    bias_refs = (b0_ref, b1_ref, b2_ref)

    @pl.when(pl.program_id(0) == 0)
    def _():
        kn_ref[...] = jnp.zeros_like(kn_ref)
        vn_ref[...] = jnp.zeros_like(vn_ref)
        def rotated_rows(vec, width):
            rows = jnp.broadcast_to(vec, (SAMPLE_PAD, vec.shape[1]))
            return pltpu.roll(rows, 0, 1, stride=1, stride_axis=0)[:, :width]
        for g, (s_ref, b_ref) in enumerate(zip((s0_ref, s1_ref, s2_ref), bias_refs)):
            for h in range(ATT_HEADS):
                b_ref[h] = rotated_rows(s_ref[h:h + 1, :], b_ref.shape[2])
                bn_ref[g, h] = rotated_rows(sn_ref[g, h:h + 1, :], ATT_BLOCK)

    for g in range(N_GROUPS):
        gs = slice(g * ATT_OUT_W, (g + 1) * ATT_OUT_W)
        kn_ref[0:SAMPLE_PAD, gs] = qkv_refs[g][0, :, ATT_OUT_W:2 * ATT_OUT_W]
        vn_ref[0:SAMPLE_PAD, gs] = qkv_refs[g][0, :, 2 * ATT_OUT_W:3 * ATT_OUT_W]

    staged = []
    for h in range(ATT_HEADS):
        logits = []
        for g in range(N_GROUPS):
            hs = slice(g * ATT_OUT_W + h * ATT_DH, g * ATT_OUT_W + (h + 1) * ATT_DH)
            qh = qkv_refs[g][0, :, h * ATT_DH:(h + 1) * ATT_DH].astype(BF16)
            logits.append(_dot(qh, cache_refs[g][0, 0, h].astype(BF16)) + bias_refs[g][h])
            logits.append(_nt_dot(qh, kn_ref[:, hs].astype(BF16)) + bn_ref[g, h])
        m = functools.reduce(jnp.maximum, [jnp.max(x, axis=-1, keepdims=True) for x in logits])
        es = [jnp.exp(x - m) for x in logits]
        l = functools.reduce(jnp.add, [jnp.sum(e, axis=-1, keepdims=True) for e in es])
        staged.append(([e.astype(BF16) for e in es], l))
    for h, (es, l) in enumerate(staged):
        acc = jnp.zeros((SAMPLE_PAD, ATT_DH), F32)
        for g in range(N_GROUPS):
            hs = slice(g * ATT_OUT_W + h * ATT_DH, g * ATT_OUT_W + (h + 1) * ATT_DH)
            acc = acc + _nt_dot(es[2 * g], cache_refs[g][0, 1, h].astype(BF16))
            acc = acc + _dot(es[2 * g + 1], vn_ref[:, hs].astype(BF16))
        o_ref[0, :, h * ATT_DH:(h + 1) * ATT_DH] = acc / l


_N_RET_IN, _N_ATT_IN = 10, 10


def _sample_mixers_kernel(*refs):
    n_in = _N_RET_IN + _N_ATT_IN
    ret_in, att_in = refs[:_N_RET_IN], refs[_N_RET_IN:n_in]
    ret_o_ref, state_o_ref, att_o_ref = refs[n_in:n_in + 3]
    kv_o_refs = refs[n_in + 3:n_in + 3 + N_GROUPS]
    kpad_ref, vpad_ref, kn_ref, vn_ref = refs[n_in + 3 + N_GROUPS:n_in + 7 + N_GROUPS]
    _attn_sample_kernel(*att_in, att_o_ref, kn_ref, vn_ref, *refs[n_in + 7 + N_GROUPS:])
    _retention_sample_kernel(*ret_in, ret_o_ref, state_o_ref, kpad_ref, vpad_ref)
    for qkv_ref, kv_ref in zip(att_in[:N_GROUPS], kv_o_refs):
        for tok in range(kv_ref.shape[1]):
            for kv in range(2):
                for h in range(ATT_HEADS):
                    col = (1 + kv) * ATT_OUT_W + h * ATT_DH
                    kv_ref[0, tok, kv, h:h + 1, :] = qkv_ref[0, tok:tok + 1, col:col + ATT_DH]


def _sample_mixers(ret, state, tables, gn, qkvs, caches_t, slots, slots_new, n_tokens):
    b, p, _ = ret.shape
    state_spec = pl.BlockSpec((1, RET_HEADS, RET_DK, RET_DV), lambda bi: (bi, 0, 0, 0))
    ret_specs = _retention_specs(p, lambda bi: (bi, 0)) + [state_spec] + _table_specs(tables + (gn,), 1)
    att_specs = ([pl.BlockSpec((1, p, QKV_W), lambda bi: (bi, 0, 0)) for _ in qkvs]
                 + [pl.BlockSpec((1,) + c.shape[1:], lambda bi: (bi, 0, 0, 0, 0)) for c in caches_t]
                 + [pl.BlockSpec(x.shape, lambda bi: (0, 0)) for x in slots]
                 + [pl.BlockSpec(slots_new.shape, lambda bi: (0, 0, 0))])
    assert len(ret_specs) == _N_RET_IN and len(att_specs) == _N_ATT_IN
    kv_shape = (b, n_tokens, 2, ATT_HEADS, ATT_DH)
    return pl.pallas_call(
        _sample_mixers_kernel,
        grid=(b,),
        in_specs=ret_specs + att_specs,
        out_specs=[pl.BlockSpec((1, p, RET_V_W), lambda bi: (bi, 0, 0)), state_spec,
                   pl.BlockSpec((1, p, ATT_OUT_W), lambda bi: (bi, 0, 0))]
        + [pl.BlockSpec((1,) + kv_shape[1:], lambda bi: (bi, 0, 0, 0, 0))] * N_GROUPS,
        out_shape=[jax.ShapeDtypeStruct((b, p, RET_V_W), ret.dtype), jax.ShapeDtypeStruct(state.shape, F32),
                   jax.ShapeDtypeStruct((b, p, ATT_OUT_W), F32)]
        + [jax.ShapeDtypeStruct(kv_shape, F32)] * N_GROUPS,
        scratch_shapes=[pltpu.VMEM((RET_CHUNK, RET_QK_W), F32), pltpu.VMEM((RET_CHUNK, RET_V_W), F32),
                        pltpu.VMEM((ATT_BLOCK, ATT_W), F32), pltpu.VMEM((ATT_BLOCK, ATT_W), F32)]
        + [pltpu.VMEM((ATT_HEADS, p, c.shape[-1]), F32) for c in caches_t]
        + [pltpu.VMEM((N_GROUPS, ATT_HEADS, p, ATT_BLOCK), F32)],
        compiler_params=pltpu.CompilerParams(
            dimension_semantics=("arbitrary",), vmem_limit_bytes=VMEM_LIMIT),
        name="sample_mixers",
    )(ret, ret, ret, ret, state, *tables, gn, *qkvs, *caches_t, *slots, slots_new)


def _out_proj_kernel(*refs, dils):
    x_ref, ret_ref, ga_ref, gb_ref, ag_ref, wr_ref, wa_ref, wo_ref = refs[:8]
    if dils is None:
        att_ref, o_ref = refs[8:]
        att = att_ref[0]
    else:
        group_refs = refs[8:8 + 2 * N_GROUPS]
        o_ref = refs[8 + 2 * N_GROUPS]
        scratch = refs[9 + 2 * N_GROUPS:]
        tm = x_ref.shape[1]
        os, lses = [], []
        for g, dil in enumerate(dils):
            og_ref, lg_ref = group_refs[2 * g], group_refs[2 * g + 1]
            if dil == 1:
                os.append(og_ref[0, 0]); lses.append(lg_ref[0, 0])
                continue
            so_ref, sl_ref = scratch[2 * (g - 1)], scratch[2 * (g - 1) + 1]
            n_chunks = ATT_OUT_W // LANES
            for c in range(n_chunks):
                cs = slice(c * LANES, (c + 1) * LANES)
                for r in range(dil):
                    so_ref[c, pl.ds(r, tm // dil, stride=dil), :] = og_ref[0, r, :, cs]
                    sl_ref[c, pl.ds(r, tm // dil, stride=dil), :] = lg_ref[0, r, :, cs]
            os.append(jnp.concatenate([so_ref[c] for c in range(n_chunks)], axis=1))
            lses.append(jnp.concatenate([sl_ref[c] for c in range(n_chunks)], axis=1))
        mx = functools.reduce(jnp.maximum, lses)
        ws = [jnp.exp(l - mx) for l in lses]
        att = functools.reduce(jnp.add, [w * o for w, o in zip(ws, os)]) / functools.reduce(jnp.add, ws)
    ag = ag_ref[0].astype(F32)
    u = (ag * jax.nn.sigmoid(ag) * att).astype(BF16)
    o_b = _dot(u, wa_ref[...])
    o_a = _dot(ret_ref[0].astype(BF16), wr_ref[...])
    merged = jax.nn.sigmoid(ga_ref[0].astype(F32)) * o_a + jax.nn.sigmoid(gb_ref[0].astype(F32)) * o_b
    o_ref[0] = x_ref[0] + _dot(merged.astype(BF16), wo_ref[...])


def _out_proj(x, ret, gates, wr, wa, wo, tm, att=None, groups=None):
    b, t, _ = x.shape
    row = lambda w, c: pl.BlockSpec((1, tm, w), lambda bi, i: (bi, i, c))
    full = lambda a: pl.BlockSpec(a.shape, lambda bi, i: (0, 0))
    in_specs = [row(D_MODEL, 0), row(RET_V_W, 0), row(D_MODEL, 0), row(D_MODEL, 1),
                row(ATT_OUT_W, 2 * D_MODEL // ATT_OUT_W), full(wr), full(wa), full(wo)]
    args = [x, ret, gates, gates, gates, wr, wa, wo]
    scratch = []
    if groups is None:
        dils = None
        in_specs.append(row(ATT_OUT_W, 0))
        args.append(att)
    else:
        dils = tuple(o.shape[1] for o, _ in groups)
        for (o, lse), d in zip(groups, dils):
            spec = pl.BlockSpec((1, d, tm // d, ATT_OUT_W), lambda bi, i: (bi, 0, i, 0))
            in_specs += [spec, spec]
            args += [o, lse]
            if d > 1:
                scratch += [pltpu.VMEM((ATT_OUT_W // LANES, tm, LANES), F32)] * 2
    return pl.pallas_call(
        functools.partial(_out_proj_kernel, dils=dils),
        grid=(b, t // tm),
        in_specs=in_specs,
        out_specs=row(D_MODEL, 0),
        out_shape=jax.ShapeDtypeStruct((b, t, D_MODEL), F32),
        scratch_shapes=scratch,
        compiler_params=pltpu.CompilerParams(
            dimension_semantics=("parallel", "parallel"), vmem_limit_bytes=VMEM_LIMIT),
        name="out_proj",
    )(*args)


def _rotary_tables(bases, offsets):
    half = RET_DK // 2
    inv = ROPE_BASE ** (-jnp.arange(half, dtype=F32) / half)
    inv2 = jnp.concatenate([inv, inv])
    sign = jnp.concatenate([-jnp.ones((half,), F32), jnp.ones((half,), F32)])
    a = bases.astype(F32)[:, None, None] * inv2
    b = offsets.astype(F32)[:, None] * inv2
    cb, sb = jnp.cos(b), jnp.sin(b)
    return jnp.cos(a), jnp.sin(a), cb, sb, sign * cb, sign * sb


def _retention_tables(c, rows):
    log_g = jnp.log1p(-(2.0 ** (-5.0 - jnp.arange(RET_HEADS, dtype=F32))))
    i = jnp.arange(c, dtype=F32)
    diff = i[:, None] - i[None, :]
    decay = jnp.where(diff[None] >= 0, jnp.exp(jnp.maximum(diff, 0.0)[None] * log_g[:, None, None]), 0.0)
    q_decay = jnp.exp((i + 1.0)[None, :] * log_g[:, None])
    k_decay = jnp.exp((c - 1.0 - i)[None, :] * log_g[:, None])
    chunk_decay = jnp.exp(c * log_g)
    dec = jnp.zeros((RET_HEADS, rows, RET_CHUNK), F32).at[:, :c, :c].set(decay)
    qd = jnp.zeros((RET_HEADS, rows, 1), F32).at[:, :c, 0].set(q_decay)
    kd = jnp.zeros((RET_HEADS, RET_CHUNK, 1), F32).at[:, :c, 0].set(k_decay)
    cd = jnp.broadcast_to(chunk_decay[:, None, None], (RET_HEADS, 1, RET_DV))
    return dec, qd, kd, cd


def _t5_bucket(dist):
    max_exact = REL_BUCKETS // 2
    d = jnp.maximum(dist.astype(F32), 1.0)
    large = max_exact + (jnp.log(d / max_exact) / math.log(REL_MAX_DIST / max_exact)
                         * (REL_BUCKETS - max_exact)).astype(jnp.int32)
    large = jnp.minimum(large, REL_BUCKETS - 1)
    return jnp.where(dist < max_exact, dist, large)


def _group_bias(rel_bias, g, dil, slots):
    dist = dil * jnp.asarray(slots, dtype=jnp.int32)
    return rel_bias[_t5_bucket(dist)][:, g * ATT_HEADS:(g + 1) * ATT_HEADS].astype(F32).T


def _neg(heads, n):
    return jnp.full((heads, n), NEG, F32)


def _dilate(v, dil):
    heads, n = v.shape
    return jnp.stack([v] + [_neg(heads, n)] * (dil - 1), axis=-1).reshape(heads, n * dil)


def _prompt_slots(tb_rev):
    heads = tb_rev.shape[0]
    return jnp.concatenate([tb_rev, _neg(heads, 4 * ATT_BLOCK - N_KEYS)], axis=1)


def _sample_slots(tb, tb_rev, win, dil):
    heads = tb.shape[0]
    s_c = jnp.concatenate([_dilate(tb_rev[:, :N_KEYS - 1], dil), _neg(heads, ATT_BLOCK)], axis=1)
    back = [tb[:, k // dil:k // dil + 1] if k % dil == 0 else _neg(heads, 1) for k in range(SAMPLE_PAD - 1, 0, -1)]
    s_n = jnp.concatenate([tb[:, 0:1], _neg(heads, 2 * ATT_BLOCK - SAMPLE_PAD)] + back, axis=1)
    return s_c, s_n


def _kv_rows_t(tail_t):
    b, _, _, rows = tail_t.shape
    return jnp.transpose(tail_t.reshape(b, 2, ATT_HEADS, ATT_DH, rows), (0, 4, 1, 2, 3))


def kernel(x_prompt, x_sample, cache_kv_w128, cache_kv_w512, cache_kv_w2048, state_retention,
           w_norm, w_in, q_norm, k_norm, rel_bias, ret_norm, w_proj_ret, w_proj_att, w_out):
    assert w_in.shape[0] == 1
    bp, t, _ = x_prompt.shape
    bs, ts, _ = x_sample.shape
    dils = tuple(d for _, d in ATT_GROUPS)
    assert t % (ATT_BLOCK * max(dils)) == 0 and ts <= SAMPLE_PAD
    caches = (cache_kv_w128[0], cache_kv_w512[0], cache_kv_w2048[0])
    for cch, (win, _) in zip(caches, ATT_GROUPS):
        assert cch.shape[1] == win and win <= PAST_LEN

    wn = w_norm[0].reshape(1, D_MODEL)
    qg =jnp.tile(q_norm[0] * (ATT_DH ** -0.5), TILE // ATT_DH).reshape(1, TILE)
    kg = jnp.tile(k_norm[0], TILE // ATT_DH).reshape(1, TILE)
    gn = ret_norm[0].reshape(1, RET_V_W)
    wr = w_proj_ret[0].astype(BF16)
    wa = w_proj_att[0].astype(BF16)
    wo = w_out[0].astype(BF16)
    hid = jnp.arange(TILE // 2) // ATT_DH
    seg_mean = jnp.where(hid[:, None] == hid[None, :], 1.0 / ATT_DH, 0.0).astype(BF16)
    asc, desc = tuple(range(N_KEYS)), tuple(range(N_KEYS - 1, -1, -1))
    group_bias = [(_group_bias(rel_bias, g, d, asc), _group_bias(rel_bias, g, d, desc)) for g, d in enumerate(dils)]

    pad = SAMPLE_PAD
    ns = bs * pad
    xs = jnp.pad(x_sample, ((0, 0), (0, pad - ts), (0, 0))).reshape(ns, D_MODEL)
    rot_s = _rotary_tables(jnp.full((1,), PAST_LEN), jnp.tile(jnp.arange(pad), bs))
    ret_in_s, *qkv_s, gates_s, w_in_bf = _in_proj(xs, ns, (1,) * N_GROUPS, (), F32, wn, w_in[0], rot_s,
                                                  qg, kg, seg_mean, tm=ns)

    tm_p = 256
    rot_p = _rotary_tables(jnp.arange(0, t, tm_p), jnp.arange(tm_p))
    tail_p_rows = tuple(min(w, t) for w, _ in ATT_GROUPS)
    ret_p, state_p, *rest = _in_proj(
        x_prompt.reshape(bp * t, D_MODEL), t, dils, tail_p_rows, BF16, wn, w_in_bf, rot_p, qg, kg, seg_mean,
        tm=tm_p, retention=(_retention_tables(RET_CHUNK, RET_CHUNK), gn))
    qkv_p, gates_p, tails_p = rest[:N_GROUPS], rest[N_GROUPS], rest[N_GROUPS + 1:]
    groups = [_attn_prompt_group(qkv_p[g], _prompt_slots(group_bias[g][1]), step_blocks=16)
              for g in range(N_GROUPS)]
    y_p = _out_proj(x_prompt, ret_p.reshape(bp, t, RET_V_W), gates_p.reshape(bp, t, GATE_W), wr, wa, wo,
                    tm=512, groups=groups)

    qkv_s =[a.reshape(bs, pad, QKV_W) for a in qkv_s]
    caches_t = [jnp.transpose(c, (0, 2, 3, 4, 1)) for c in caches]
    ss = [_sample_slots(*group_bias[g], win, d) for g, (win, d) in enumerate(ATT_GROUPS)]
    ret_s, state_s, att_s, *kv_s = _sample_mixers(
        ret_in_s.reshape(bs, pad, RET_OUT_W), state_retention[0], _retention_tables(ts, pad), gn,
        qkv_s, caches_t, [c for c, _ in ss], jnp.stack([n for _, n in ss]), ts)
    y_s = _out_proj(xs.reshape(1, ns, D_MODEL), ret_s.reshape(1, ns, RET_V_W), gates_s.reshape(1, ns, GATE_W),
                    wr, wa, wo, tm=ns, att=att_s.reshape(1, ns, ATT_OUT_W))
    y_s = y_s.reshape(bs, pad, D_MODEL)[:, :ts]

    kv_p = [_kv_rows_t(tt)[None] for tt in tails_p]
    return (y_p, y_s, state_p[None], state_s[None], kv_p[0], kv_p[1], kv_p[2],
            kv_s[0][None], kv_s[1][None], kv_s[2][None])
```

```python
import functools
import math

import jax
import jax.numpy as jnp
from jax import lax
from jax.experimental import pallas as pl
from jax.experimental.pallas import tpu as pltpu

D_MODEL = 1024
PAST_LEN = 16384
RET_HEADS = 4
RET_DK = 128
RET_DV = 256
RET_CHUNK = 128
ROPE_BASE = 10000.0
ATT_GROUPS = ((128, 1), (512, 4), (2048, 16))
N_GROUPS = 3
ATT_HEADS = 8
ATT_DH = 64
REL_BUCKETS = 32
REL_MAX_DIST = 2048
EPS = 1e-6

RET_QK_W = RET_HEADS * RET_DK
RET_V_W = RET_HEADS * RET_DV
ATT_W = N_GROUPS * ATT_HEADS * ATT_DH
ATT_OUT_W = ATT_HEADS * ATT_DH
IN_W = 2 * RET_QK_W + 2 * RET_V_W + 3 * ATT_W + ATT_OUT_W + 2 * D_MODEL

LANES = 128
TILE = 512
RET_OUT_W = 2 * RET_QK_W + 2 * RET_V_W
QKV_W = 3 * ATT_OUT_W
GATE_W = 2 * D_MODEL + ATT_OUT_W
ATT_BLOCK = 128
N_KEYS = ATT_BLOCK + 1
SAMPLE_PAD = 8
NEG = -1e30
VMEM_LIMIT = 48 * 1024 * 1024
W_LOAD_SLOTS = 4

F32 = jnp.float32
BF16 = jnp.bfloat16


def _nt_dot(a, b):
    return lax.dot_general(a, b, (((1,), (1,)), ((), ())), preferred_element_type=F32)


def _dot(a, b):
    return jnp.dot(a, b, preferred_element_type=F32)


def _proj_schedule():
    sched = [(0, 0, 0, "rot_q"), (RET_QK_W, 0, RET_QK_W, "rot_k")]
    for k in range(2 * RET_V_W // TILE):
        sched.append((2 * RET_QK_W + k * TILE, 0, 2 * RET_QK_W + k * TILE, "plain"))
    att0 = 2 * RET_QK_W + 2 * RET_V_W
    for g in range(N_GROUPS):
        for kind, epi in enumerate(("norm_q", "norm_k", "plain")):
            sched.append((att0 + kind * ATT_W + g * ATT_OUT_W, 1 + g, kind * ATT_OUT_W, epi))
    gate0 = att0 + 3 * ATT_W
    for k in range(2 * D_MODEL // TILE):
        sched.append((gate0 + ATT_OUT_W + k * TILE, 4, k * TILE, "plain"))
    sched.append((gate0, 4, 2 * D_MODEL, "plain"))
    return sched


_SCHEDULE = _proj_schedule()


def _in_proj_kernel(*refs, dils, per_seq, fuse_retention, n_tails, cast_weights):
    (x_ref, wn_ref, w_ref, ca_ref, sa_ref, cb_ref, sb_ref, cbs_ref, sbs_ref,
     qg_ref, kg_ref, seg_ref) = refs[:12]
    refs = refs[12:]
    if cast_weights:
        *refs, wf_ref, wb_ref, sem_in, sem_out = refs
        wbf_out_ref = refs.pop(-(4 if fuse_retention else 2) - 1)

        load_slots = wf_ref.shape[0]

        def tile_load(step):
            return pltpu.make_async_copy(w_ref.at[:, pl.ds(_SCHEDULE[step][0], TILE)],
                                         wf_ref.at[step % load_slots], sem_in.at[step % load_slots])

        def tile_store(step):
            return pltpu.make_async_copy(wb_ref.at[step % 2],
                                         wbf_out_ref.at[:, pl.ds(_SCHEDULE[step][0], TILE)], sem_out.at[step % 2])

        for ahead in range(load_slots - 1):
            tile_load(ahead).start()
    if fuse_retention:
        dec_ref, qd_ref, kd_ref, cd_ref, gn_ref, ret_ref, s_ref = refs[:7]
        refs = refs[7:]
    else:
        ret_ref = refs[0]
        refs = refs[1:]
    a0_ref, a1_ref, a2_ref, gate_ref = refs[:4]
    tail_refs = refs[4:4 + n_tails]
    h_ref, y_ref = refs[4 + n_tails:6 + n_tails]
    refs = refs[6 + n_tails:]
    r_ref = refs[0] if fuse_retention else ret_ref
    kt_ref = refs[1] if fuse_retention else None
    out_refs = (r_ref, a0_ref, a1_ref, a2_ref, gate_ref)

    if fuse_retention:
        @pl.when(pl.program_id(0) % per_seq == 0)
        def _():
            s_ref[...] = jnp.zeros_like(s_ref)

    x = x_ref[...]
    tm = x.shape[0]
    ms = jnp.mean(x * x, axis=-1, keepdims=True)
    h_ref[...] = (x * lax.rsqrt(ms + EPS) * wn_ref[...]).astype(BF16)

    ca, sa = ca_ref[0], sa_ref[0]
    cos = ca * cb_ref[...] - sa * sb_ref[...]
    sin = sa * cbs_ref[...] + ca * sbs_ref[...]

    def rotary(y, scale):
        parts = []
        for hh in range(TILE // RET_DK):
            yh = y[:, hh * RET_DK:(hh + 1) * RET_DK]
            parts.append((yh * cos + pltpu.roll(yh, RET_DK // 2, axis=1) * sin) * scale)
        return jnp.concatenate(parts, axis=1)

    def head_rms(y, gain):
        y2 = (y * y).astype(BF16)
        half = TILE // 2
        ms = jnp.concatenate([_dot(y2[:, :half], seg_ref[...]), _dot(y2[:, half:], seg_ref[...])], axis=1)
        return y * lax.rsqrt(ms + EPS) * gain

    epilogues = {
        "plain": lambda y: y,
        "rot_q": lambda y: rotary(y, 1.0),
        "rot_k": lambda y: rotary(y, RET_DK ** -0.5),
        "norm_q": lambda y: head_rms(y, qg_ref[...]),
        "norm_k": lambda y: head_rms(y, kg_ref[...]),
    }

    kt_base = jnp.minimum(pl.program_id(0), 0)

    def retention_issue(c, h):
        rows = slice(c * RET_CHUNK, (c + 1) * RET_CHUNK)
        ks = slice(h * RET_DK, (h + 1) * RET_DK)
        ks2 = slice(RET_QK_W + h * RET_DK, RET_QK_W + (h + 1) * RET_DK)
        vs_in = slice(2 * RET_QK_W + h * RET_DV, 2 * RET_QK_W + (h + 1) * RET_DV)
        qb = r_ref[rows, ks].astype(BF16)
        vb = r_ref[rows, vs_in].astype(BF16)
        state = s_ref[0, h]
        scores = (_nt_dot(qb, r_ref[rows, ks2].astype(BF16)) * dec_ref[h]).astype(BF16)
        carried = _dot(qb, state.astype(BF16)) * qd_ref[h]
        s_ref[0, h] = state * cd_ref[h] + _dot(kt_ref[kt_base + c, ks, :], vb)
        return rows, h, scores, vb, carried

    def retention_finish(rows, h, scores, vb, carried):
        vs = slice(h * RET_DV, (h + 1) * RET_DV)
        gs_in = slice(2 * RET_QK_W + RET_V_W + h * RET_DV, 2 * RET_QK_W + RET_V_W + (h + 1) * RET_DV)
        o = _dot(scores, vb) + carried
        ret_ref[rows, vs] = _group_norm_gate(o, gn_ref[:, vs], r_ref[rows, gs_in]).astype(ret_ref.dtype)

    def store_tile(val, out_idx, out_col):
        o_ref = out_refs[out_idx]
        ocs = slice(out_col, out_col + TILE)
        if not 1 <= out_idx <= N_GROUPS:
            o_ref[:, ocs] = val.astype(o_ref.dtype)
            return
        g, kind = out_idx - 1, out_col // ATT_OUT_W
        if kind > 0 and tail_refs:
            t_ref = tail_refs[g]
            t_ref[0, kind - 1] = jnp.transpose(val)[:, tm - t_ref.shape[3]:]
        dil = dils[g]
        if dil == 1:
            o_ref[0, 0, :, ocs] = val.astype(o_ref.dtype)
            return
        for c in range(TILE // LANES):
            y_ref[c] = val[:, c * LANES:(c + 1) * LANES]
            for r in range(dil):
                o_ref[0, r, :, out_col + c * LANES:out_col + (c + 1) * LANES] = (
                    y_ref[c, pl.ds(r, tm // dil, stride=dil), :].astype(o_ref.dtype))

    n_ret_tiles = sum(1 for t in _SCHEDULE if t[1] == 0)
    units = [(c, h) for c in range(tm // RET_CHUNK) for h in range(RET_HEADS)] if fuse_retention else []
    assert len(units) <= len(_SCHEDULE) - n_ret_tiles
    for step, (w_col, out_idx, out_col, epi) in enumerate(_SCHEDULE):
        if fuse_retention and step == 2:
            for c, h in units:
                rows = slice(c * RET_CHUNK, (c + 1) * RET_CHUNK)
                kt_ref[c, h * RET_DK:(h + 1) * RET_DK, :] = _decayed_keys_t(
                    r_ref[rows, RET_QK_W + h * RET_DK:RET_QK_W + (h + 1) * RET_DK], kd_ref[h])
        unit = units[step - n_ret_tiles] if 0 <= step - n_ret_tiles < len(units) else None
        issued = retention_issue(*unit) if unit else None
        if cast_weights:
            if step + load_slots - 1 < len(_SCHEDULE):
                tile_load(step + load_slots - 1).start()
            tile_load(step).wait()
            if step >= 2:
                tile_store(step - 2).wait()
            wb_ref[step % 2] = wf_ref[step % load_slots].astype(BF16)
            tile_store(step).start()
            w_tile = wb_ref[step % 2]
        else:
            w_tile = w_ref[:, w_col:w_col + TILE]
        store_tile(epilogues[epi](_dot(h_ref[...], w_tile)), out_idx, out_col)
        if unit:
            retention_finish(*issued)
    if cast_weights:
        tile_store(len(_SCHEDULE) - 2).wait()
        tile_store(len(_SCHEDULE) - 1).wait()


def _in_proj(x2d, seq_len, dils, tail_rows, out_dtype, wn, w_bf, rot, qg, kg, seg, tm, retention=None):
    n = x2d.shape[0]
    batch = n // seq_len
    per_seq = seq_len // tm
    fuse = retention is not None
    cast_weights = w_bf.dtype == F32
    assert not cast_weights or n == tm
    const = lambda a: pl.BlockSpec(a.shape, lambda i, nd=a.ndim: (0,) * nd)
    base_spec = pl.BlockSpec((1, 1, RET_DK), lambda i: (i % per_seq, 0, 0))
    qkv_spec = lambda d: pl.BlockSpec((1, d, tm // d, QKV_W), lambda i: (i // per_seq, 0, i % per_seq, 0))
    tail_specs, tail_shapes = [], []
    for rows in tail_rows:
        width = min(tm, rows)
        first = (seq_len - rows) // tm if rows >= tm else per_seq
        tail_specs.append(pl.BlockSpec(
            (1, 2, ATT_OUT_W, width),
            lambda i, first=first: (i // per_seq, 0, 0, jnp.maximum(i % per_seq - first, 0))))
        tail_shapes.append(jax.ShapeDtypeStruct((batch, 2, ATT_OUT_W, rows), F32))
    ca, sa, *offset_tables = rot
    args = [x2d, wn, w_bf, ca, sa, *offset_tables, qg, kg, seg]
    in_specs = [
        pl.BlockSpec((tm, D_MODEL), lambda i: (i, 0)),
        const(wn),
        pl.BlockSpec(memory_space=pl.ANY) if cast_weights
        else pl.BlockSpec(w_bf.shape, lambda i: (0, 0), pipeline_mode=pl.Buffered(1)),
        base_spec, base_spec, *[const(t) for t in offset_tables],
        const(qg), const(kg), const(seg),
    ]
    scratch = [pltpu.VMEM((tm, D_MODEL), BF16), pltpu.VMEM((TILE // LANES, tm, LANES), F32)]
    if fuse:
        tables, gn = retention
        args += [*tables, gn]
        in_specs += [const(t) for t in (*tables, gn)]
        ret_specs = [pl.BlockSpec((tm, RET_V_W), lambda i: (i, 0)),
                     pl.BlockSpec((1, RET_HEADS, RET_DK, RET_DV), lambda i: (i // per_seq, 0, 0, 0))]
        ret_shapes = [jax.ShapeDtypeStruct((n, RET_V_W), out_dtype),
                      jax.ShapeDtypeStruct((batch, RET_HEADS, RET_DK, RET_DV), F32)]
        scratch += [pltpu.VMEM((tm, RET_OUT_W), F32), pltpu.VMEM((tm // RET_CHUNK, RET_QK_W, RET_CHUNK), BF16)]
    else:
        ret_specs = [pl.BlockSpec((tm, RET_OUT_W), lambda i: (i, 0))]
        ret_shapes = [jax.ShapeDtypeStruct((n, RET_OUT_W), out_dtype)]
    cast_specs, cast_shapes = [], []
    if cast_weights:
        cast_specs = [pl.BlockSpec(memory_space=pl.ANY)]
        cast_shapes = [jax.ShapeDtypeStruct(w_bf.shape, BF16)]
        scratch += [pltpu.VMEM((W_LOAD_SLOTS, D_MODEL, TILE), F32), pltpu.VMEM((2, D_MODEL, TILE), BF16),
                    pltpu.SemaphoreType.DMA((W_LOAD_SLOTS,)), pltpu.SemaphoreType.DMA((2,))]
    return pl.pallas_call(
        functools.partial(_in_proj_kernel, dils=dils, per_seq=per_seq, fuse_retention=fuse,
                          n_tails=len(tail_rows), cast_weights=cast_weights),
        grid=(n // tm,),
        in_specs=in_specs,
        out_specs=ret_specs + [qkv_spec(d) for d in dils]
        + [pl.BlockSpec((tm, GATE_W), lambda i: (i, 0))] + tail_specs + cast_specs,
        out_shape=ret_shapes
        + [jax.ShapeDtypeStruct((batch, d, seq_len // d, QKV_W), out_dtype) for d in dils]
        + [jax.ShapeDtypeStruct((n, GATE_W), out_dtype)] + tail_shapes + cast_shapes,
        scratch_shapes=scratch,
        compiler_params=pltpu.CompilerParams(
            dimension_semantics=("arbitrary",), vmem_limit_bytes=VMEM_LIMIT),
        name="in_proj",
    )(*args)


def _group_norm_gate(o, gain, gate):
    gate = gate.astype(F32)
    mu = jnp.mean(o, axis=-1, keepdims=True)
    d = o - mu
    var = jnp.mean(d * d, axis=-1, keepdims=True)
    return gate * jax.nn.sigmoid(gate) * (d * lax.rsqrt(var + EPS) * gain)


def _decayed_keys_t(k_keys, kd):
    return jnp.transpose(k_keys * kd).astype(BF16)


def _retention_specs(rows, idx):
    return [
        pl.BlockSpec((1, rows, RET_QK_W), lambda *a: (*idx(*a), 0)),
        pl.BlockSpec((1, rows, RET_QK_W), lambda *a: (*idx(*a), 1)),
        pl.BlockSpec((1, rows, RET_V_W), lambda *a: (*idx(*a), 1)),
        pl.BlockSpec((1, rows, RET_V_W), lambda *a: (*idx(*a), 2)),
    ]


def _table_specs(tables, ndim_grid):
    return [pl.BlockSpec(t.shape, lambda *a, nd=t.ndim: (0,) * nd) for t in tables]


def _retention_sample_kernel(q_ref, k_ref, v_ref, g_ref, s_in_ref, dec_ref, qd_ref, kd_ref, cd_ref,
                             gn_ref, o_ref, s_out_ref, kpad_ref, vpad_ref):
    @pl.when(pl.program_id(0) == 0)
    def _():
        kpad_ref[...] = jnp.zeros_like(kpad_ref)
        vpad_ref[...] = jnp.zeros_like(vpad_ref)

    kpad_ref[0:SAMPLE_PAD, :] = k_ref[0]
    vpad_ref[0:SAMPLE_PAD, :] = v_ref[0]
    issued = []
    for h in range(RET_HEADS):
        ks = slice(h * RET_DK, (h + 1) * RET_DK)
        vs = slice(h * RET_DV, (h + 1) * RET_DV)
        qb = q_ref[0, :, ks].astype(BF16)
        vb = vpad_ref[:, vs].astype(BF16)
        state = s_in_ref[0, h]
        scores = (_nt_dot(qb, kpad_ref[:, ks].astype(BF16)) * dec_ref[h]).astype(BF16)
        carried = _dot(qb, state.astype(BF16)) * qd_ref[h]
        s_out_ref[0, h] = state * cd_ref[h] + _dot(_decayed_keys_t(kpad_ref[:, ks], kd_ref[h]), vb)
        issued.append((scores, vb, carried))
    for h, (scores, vb, carried) in enumerate(issued):
        vs = slice(h * RET_DV, (h + 1) * RET_DV)
        o = _dot(scores, vb) + carried
        o_ref[0, :, vs] = _group_norm_gate(o, gn_ref[:, vs], g_ref[0, :, vs]).astype(o_ref.dtype)


def _attn_prompt_blocks(q_ref, kp_ref, kc_ref, vp_ref, vc_ref, bias_ref, o_ref, lse_ref, first):
    n_seq, n_blocks = q_ref.shape[1], q_ref.shape[2] // ATT_BLOCK
    assert first or n_seq == 1
    pair_w = 2 * ATT_DH
    low_q = lax.broadcasted_iota(jnp.int32, (ATT_BLOCK, pair_w), 1) < ATT_DH
    rows0, rows1 = slice(0, ATT_BLOCK), slice(ATT_BLOCK, 2 * ATT_BLOCK)

    def keys(prev_ref, cur_ref, sq, t, ps):
        if t > 0:
            return cur_ref[0, sq, (t - 1) * ATT_BLOCK:(t + 1) * ATT_BLOCK, ps]
        if first:
            return cur_ref[0, sq, 0:ATT_BLOCK, ps]
        return jnp.concatenate([prev_ref[0, sq, :, ps], cur_ref[0, sq, 0:ATT_BLOCK, ps]], axis=0)

    staged = []
    for sq, t in [(sq, t) for sq in range(n_seq) for t in range(n_blocks)]:
        for p in range(ATT_HEADS // 2):
            ps = slice(p * pair_w, (p + 1) * pair_w)
            qp = q_ref[0, sq, t * ATT_BLOCK:(t + 1) * ATT_BLOCK, ps]
            kcat = keys(kp_ref, kc_ref, sq, t, ps)
            q2 = jnp.concatenate([jnp.where(low_q, qp, 0.0), jnp.where(low_q, 0.0, qp)], axis=0)
            s = _nt_dot(q2, kcat) + bias_ref[p, :, 2 * ATT_BLOCK - kcat.shape[0]:]
            s = s.astype(BF16)
            m = jnp.max(s, axis=-1, keepdims=True)
            staged.append((sq, t, p, m.astype(F32), jnp.exp(s - m)))
    for sq, t, p, m, e in staged:
        ps = slice(p * pair_w, (p + 1) * pair_w)
        vcat = keys(vp_ref, vc_ref, sq, t, ps)
        low_k = lax.broadcasted_iota(jnp.int32, vcat.shape, 1) < ATT_DH
        p0 = _dot(e[rows0], jnp.where(low_k, vcat, 1.0))
        p1 = _dot(e[rows1], jnp.where(low_k, 1.0, vcat))
        den = pltpu.roll(jnp.where(low_q, p1, p0), ATT_DH, axis=1)
        ts = slice(t * ATT_BLOCK, (t + 1) * ATT_BLOCK)
        o_ref[0, sq, ts, ps] = jnp.where(low_q, p0, p1) / den
        lse_ref[0, sq, ts, ps] = jnp.where(low_q, m[rows0], m[rows1]) + jnp.log(den)


def _attn_prompt_kernel(q_ref, kp_ref, kc_ref, vp_ref, vc_ref, slot_ref, o_ref, lse_ref, bias_ref,
                        *, whole_sequences):
    first = pl.program_id(2) == 0

    @pl.when(jnp.logical_and(first, jnp.logical_and(pl.program_id(0) == 0, pl.program_id(1) == 0)))
    def _():
        for h in range(ATT_HEADS):
            slots = jnp.broadcast_to(slot_ref[h:h + 1, :], (ATT_BLOCK, slot_ref.shape[1]))
            rows = pltpu.roll(slots, 0, 1, stride=1, stride_axis=0)
            bias_ref[h // 2, (h % 2) * ATT_BLOCK:(h % 2 + 1) * ATT_BLOCK, :] = rows[:, :2 * ATT_BLOCK]

    if whole_sequences:
        _attn_prompt_blocks(q_ref, kp_ref, kc_ref, vp_ref, vc_ref, bias_ref, o_ref, lse_ref, True)
        return

    @pl.when(first)
    def _():
        _attn_prompt_blocks(q_ref, kp_ref, kc_ref, vp_ref, vc_ref, bias_ref, o_ref, lse_ref, True)

    @pl.when(jnp.logical_not(first))
    def _():
        _attn_prompt_blocks(q_ref, kp_ref, kc_ref, vp_ref, vc_ref, bias_ref, o_ref, lse_ref, False)


def _attn_prompt_group(qkv, slot_bias, step_blocks):
    b, dil, tr, _ = qkv.shape
    n_blocks = min(step_blocks, tr // ATT_BLOCK)
    n_seq = min(step_blocks // n_blocks, dil)
    rows = n_blocks * ATT_BLOCK
    cur = lambda c: pl.BlockSpec((1, n_seq, rows, ATT_OUT_W), lambda bi, r, j: (bi, r, j, c))
    prev = lambda c: pl.BlockSpec((1, n_seq, ATT_BLOCK, ATT_OUT_W),
                                  lambda bi, r, j: (bi, r, jnp.maximum(j * n_blocks - 1, 0), c))
    res_shape = jax.ShapeDtypeStruct((b, dil, tr, ATT_OUT_W), F32)
    return pl.pallas_call(
        functools.partial(_attn_prompt_kernel, whole_sequences=tr == rows),
        grid=(b, dil // n_seq, tr // rows),
        in_specs=[cur(0), prev(1), cur(1), prev(2), cur(2),
                  pl.BlockSpec(slot_bias.shape, lambda bi, r, j: (0, 0))],
        out_specs=[cur(0), cur(0)],
        out_shape=[res_shape, res_shape],
        scratch_shapes=[pltpu.VMEM((ATT_HEADS // 2, 2 * ATT_BLOCK, 2 * ATT_BLOCK), F32)],
        compiler_params=pltpu.CompilerParams(
            dimension_semantics=("arbitrary", "arbitrary", "arbitrary"), vmem_limit_bytes=VMEM_LIMIT),
        name=f"attn_prompt_d{dil}",
    )(qkv, qkv, qkv, qkv, qkv, slot_bias)


def _attn_sample_kernel(a0_ref, a1_ref, a2_ref, c0_ref, c1_ref, c2_ref, s0_ref, s1_ref, s2_ref, sn_ref,
                        o_ref, kn_ref, vn_ref, b0_ref, b1_ref, b2_ref, bn_ref):
    qkv_refs = (a0_ref, a1_ref, a2_ref)
    cache_refs = (c0_ref, c1_ref, c2_ref)
    bias_refs = (b0_ref, b1_ref, b2_ref)

    @pl.when(pl.program_id(0) == 0)
    def _():
        kn_ref[...] = jnp.zeros_like(kn_ref)
        vn_ref[...] = jnp.zeros_like(vn_ref)
        def rotated_rows(vec, width):
            rows = jnp.broadcast_to(vec, (SAMPLE_PAD, vec.shape[1]))
            return pltpu.roll(rows, 0, 1, stride=1, stride_axis=0)[:, :width]
        for g, (s_ref, b_ref) in enumerate(zip((s0_ref, s1_ref, s2_ref), bias_refs)):
            for h in range(ATT_HEADS):
                b_ref[h] = rotated_rows(s_ref[h:h + 1, :], b_ref.shape[2])
                bn_ref[g, h] = rotated_rows(sn_ref[g, h:h + 1, :], ATT_BLOCK)

    for g in range(N_GROUPS):
        gs = slice(g * ATT_OUT_W, (g + 1) * ATT_OUT_W)
        kn_ref[0:SAMPLE_PAD, gs] = qkv_refs[g][0, :, ATT_OUT_W:2 * ATT_OUT_W]
        vn_ref[0:SAMPLE_PAD, gs] = qkv_refs[g][0, :, 2 * ATT_OUT_W:3 * ATT_OUT_W]

    staged = []
    for h in range(ATT_HEADS):
        logits = []
        for g in range(N_GROUPS):
            hs = slice(g * ATT_OUT_W + h * ATT_DH, g * ATT_OUT_W + (h + 1) * ATT_DH)
            qh = qkv_refs[g][0, :, h * ATT_DH:(h + 1) * ATT_DH].astype(BF16)
            logits.append(_dot(qh, cache_refs[g][0, 0, h].astype(BF16)) + bias_refs[g][h])
            logits.append(_nt_dot(qh, kn_ref[:, hs].astype(BF16)) + bn_ref[g, h])
        m = functools.reduce(jnp.maximum, [jnp.max(x, axis=-1, keepdims=True) for x in logits])
        es = [jnp.exp(x - m) for x in logits]
        l = functools.reduce(jnp.add, [jnp.sum(e, axis=-1, keepdims=True) for e in es])
        staged.append(([e.astype(BF16) for e in es], l))
    for h, (es, l) in enumerate(staged):
        acc = jnp.zeros((SAMPLE_PAD, ATT_DH), F32)
        for g in range(N_GROUPS):
            hs = slice(g * ATT_OUT_W + h * ATT_DH, g * ATT_OUT_W + (h + 1) * ATT_DH)
            acc = acc + _nt_dot(es[2 * g], cache_refs[g][0, 1, h].astype(BF16))
            acc = acc + _dot(es[2 * g + 1], vn_ref[:, hs].astype(BF16))
        o_ref[0, :, h * ATT_DH:(h + 1) * ATT_DH] = acc / l


_N_RET_IN, _N_ATT_IN = 10, 10


def _sample_mixers_kernel(*refs):
    n_in = _N_RET_IN + _N_ATT_IN
    ret_in, att_in = refs[:_N_RET_IN], refs[_N_RET_IN:n_in]
    ret_o_ref, state_o_ref, att_o_ref = refs[n_in:n_in + 3]
    kv_o_refs = refs[n_in + 3:n_in + 3 + N_GROUPS]
    kpad_ref, vpad_ref, kn_ref, vn_ref = refs[n_in + 3 + N_GROUPS:n_in + 7 + N_GROUPS]
    _attn_sample_kernel(*att_in, att_o_ref, kn_ref, vn_ref, *refs[n_in + 7 + N_GROUPS:])
    _retention_sample_kernel(*ret_in, ret_o_ref, state_o_ref, kpad_ref, vpad_ref)
    for qkv_ref, kv_ref in zip(att_in[:N_GROUPS], kv_o_refs):
        for tok in range(kv_ref.shape[1]):
            for kv in range(2):
                for h in range(ATT_HEADS):
                    col = (1 + kv) * ATT_OUT_W + h * ATT_DH
                    kv_ref[0, tok, kv, h:h + 1, :] = qkv_ref[0, tok:tok + 1, col:col + ATT_DH]


def _sample_mixers(ret, state, tables, gn, qkvs, caches_t, slots, slots_new, n_tokens):
    b, p, _ = ret.shape
    state_spec = pl.BlockSpec((1, RET_HEADS, RET_DK, RET_DV), lambda bi: (bi, 0, 0, 0))
    ret_specs = _retention_specs(p, lambda bi: (bi, 0)) + [state_spec] + _table_specs(tables + (gn,), 1)
    att_specs = ([pl.BlockSpec((1, p, QKV_W), lambda bi: (bi, 0, 0)) for _ in qkvs]
                 + [pl.BlockSpec((1,) + c.shape[1:], lambda bi: (bi, 0, 0, 0, 0)) for c in caches_t]
                 + [pl.BlockSpec(x.shape, lambda bi: (0, 0)) for x in slots]
                 + [pl.BlockSpec(slots_new.shape, lambda bi: (0, 0, 0))])
    assert len(ret_specs) == _N_RET_IN and len(att_specs) == _N_ATT_IN
    kv_shape = (b, n_tokens, 2, ATT_HEADS, ATT_DH)
    return pl.pallas_call(
        _sample_mixers_kernel,
        grid=(b,),
        in_specs=ret_specs + att_specs,
        out_specs=[pl.BlockSpec((1, p, RET_V_W), lambda bi: (bi, 0, 0)), state_spec,
                   pl.BlockSpec((1, p, ATT_OUT_W), lambda bi: (bi, 0, 0))]
        + [pl.BlockSpec((1,) + kv_shape[1:], lambda bi: (bi, 0, 0, 0, 0))] * N_GROUPS,
        out_shape=[jax.ShapeDtypeStruct((b, p, RET_V_W), ret.dtype), jax.ShapeDtypeStruct(state.shape, F32),
                   jax.ShapeDtypeStruct((b, p, ATT_OUT_W), F32)]
        + [jax.ShapeDtypeStruct(kv_shape, F32)] * N_GROUPS,
        scratch_shapes=[pltpu.VMEM((RET_CHUNK, RET_QK_W), F32), pltpu.VMEM((RET_CHUNK, RET_V_W), F32),
                        pltpu.VMEM((ATT_BLOCK, ATT_W), F32), pltpu.VMEM((ATT_BLOCK, ATT_W), F32)]
        + [pltpu.VMEM((ATT_HEADS, p, c.shape[-1]), F32) for c in caches_t]
        + [pltpu.VMEM((N_GROUPS, ATT_HEADS, p, ATT_BLOCK), F32)],
        compiler_params=pltpu.CompilerParams(
            dimension_semantics=("arbitrary",), vmem_limit_bytes=VMEM_LIMIT),
        name="sample_mixers",
    )(ret, ret, ret, ret, state, *tables, gn, *qkvs, *caches_t, *slots, slots_new)


def _out_proj_kernel(*refs, dils):
    x_ref, ret_ref, gate_ref, wr_ref, wa_ref, wo_ref = refs[:6]
    if dils is None:
        att_ref, o_ref = refs[6:]
        att = att_ref[0]
    else:
        group_refs = refs[6:6 + 2 * N_GROUPS]
        o_ref = refs[6 + 2 * N_GROUPS]
        scratch = refs[7 + 2 * N_GROUPS:]
        tm = x_ref.shape[1]
        os, lses = [], []
        for g, dil in enumerate(dils):
            og_ref, lg_ref = group_refs[2 * g], group_refs[2 * g + 1]
            if dil == 1:
                os.append(og_ref[0, 0]); lses.append(lg_ref[0, 0])
                continue
            so_ref, sl_ref = scratch[2 * (g - 1)], scratch[2 * (g - 1) + 1]
            n_chunks = ATT_OUT_W // LANES
            for c in range(n_chunks):
                cs = slice(c * LANES, (c + 1) * LANES)
                for r in range(dil):
                    so_ref[c, pl.ds(r, tm // dil, stride=dil), :] = og_ref[0, r, :, cs]
                    sl_ref[c, pl.ds(r, tm // dil, stride=dil), :] = lg_ref[0, r, :, cs]
            os.append(jnp.concatenate([so_ref[c] for c in range(n_chunks)], axis=1))
            lses.append(jnp.concatenate([sl_ref[c] for c in range(n_chunks)], axis=1))
        mx = functools.reduce(jnp.maximum, lses)
        ws = [jnp.exp(l - mx) for l in lses]
        att = functools.reduce(jnp.add, [w * o for w, o in zip(ws, os)]) / functools.reduce(jnp.add, ws)
    ag = gate_ref[0, :, 2 * D_MODEL:].astype(F32)
    u = (ag * jax.nn.sigmoid(ag) * att).astype(BF16)
    o_b = _dot(u, wa_ref[...])
    o_a = _dot(ret_ref[0].astype(BF16), wr_ref[...])
    merged = (jax.nn.sigmoid(gate_ref[0, :, :D_MODEL].astype(F32)) * o_a
              + jax.nn.sigmoid(gate_ref[0, :, D_MODEL:2 * D_MODEL].astype(F32)) * o_b)
    o_ref[0] = x_ref[0] + _dot(merged.astype(BF16), wo_ref[...])


def _out_proj(x, ret, gates, wr, wa, wo, tm, att=None, groups=None):
    b, t, _ = x.shape
    row = lambda w, c: pl.BlockSpec((1, tm, w), lambda bi, i: (bi, i, c))
    full = lambda a: pl.BlockSpec(a.shape, lambda bi, i: (0, 0))
    in_specs = [row(D_MODEL, 0), row(RET_V_W, 0), row(GATE_W, 0), full(wr), full(wa), full(wo)]
    args = [x, ret, gates, wr, wa, wo]
    scratch = []
    if groups is None:
        dils = None
        in_specs.append(row(ATT_OUT_W, 0))
        args.append(att)
    else:
        dils = tuple(o.shape[1] for o, _ in groups)
        for (o, lse), d in zip(groups, dils):
            spec = pl.BlockSpec((1, d, tm // d, ATT_OUT_W), lambda bi, i: (bi, 0, i, 0))
            in_specs += [spec, spec]
            args += [o, lse]
            if d > 1:
                scratch += [pltpu.VMEM((ATT_OUT_W // LANES, tm, LANES), F32)] * 2
    return pl.pallas_call(
        functools.partial(_out_proj_kernel, dils=dils),
        grid=(b, t // tm),
        in_specs=in_specs,
        out_specs=row(D_MODEL, 0),
        out_shape=jax.ShapeDtypeStruct((b, t, D_MODEL), F32),
        scratch_shapes=scratch,
        compiler_params=pltpu.CompilerParams(
            dimension_semantics=("parallel", "parallel"), vmem_limit_bytes=VMEM_LIMIT),
        name="out_proj",
    )(*args)


def _rotary_tables(bases, offsets):
    half = RET_DK // 2
    inv = ROPE_BASE ** (-jnp.arange(half, dtype=F32) / half)
    inv2 = jnp.concatenate([inv, inv])
    sign = jnp.concatenate([-jnp.ones((half,), F32), jnp.ones((half,), F32)])
    a = bases.astype(F32)[:, None, None] * inv2
    b = offsets.astype(F32)[:, None] * inv2
    cb, sb = jnp.cos(b), jnp.sin(b)
    return jnp.cos(a), jnp.sin(a), cb, sb, sign * cb, sign * sb


def _retention_tables(c, rows):
    log_g = jnp.log1p(-(2.0 ** (-5.0 - jnp.arange(RET_HEADS, dtype=F32))))
    i = jnp.arange(c, dtype=F32)
    diff = i[:, None] - i[None, :]
    decay = jnp.where(diff[None] >= 0, jnp.exp(jnp.maximum(diff, 0.0)[None] * log_g[:, None, None]), 0.0)
    q_decay = jnp.exp((i + 1.0)[None, :] * log_g[:, None])
    k_decay = jnp.exp((c - 1.0 - i)[None, :] * log_g[:, None])
    chunk_decay = jnp.exp(c * log_g)
    dec = jnp.zeros((RET_HEADS, rows, RET_CHUNK), F32).at[:, :c, :c].set(decay)
    qd = jnp.zeros((RET_HEADS, rows, 1), F32).at[:, :c, 0].set(q_decay)
    kd = jnp.zeros((RET_HEADS, RET_CHUNK, 1), F32).at[:, :c, 0].set(k_decay)
    cd = jnp.broadcast_to(chunk_decay[:, None, None], (RET_HEADS, 1, RET_DV))
    return dec, qd, kd, cd


def _t5_bucket(dist):
    max_exact = REL_BUCKETS // 2
    d = jnp.maximum(dist.astype(F32), 1.0)
    large = max_exact + (jnp.log(d / max_exact) / math.log(REL_MAX_DIST / max_exact)
                         * (REL_BUCKETS - max_exact)).astype(jnp.int32)
    large = jnp.minimum(large, REL_BUCKETS - 1)
    return jnp.where(dist < max_exact, dist, large)


def _group_bias(rel_bias, g, dil, slots):
    dist = dil * jnp.asarray(slots, dtype=jnp.int32)
    return rel_bias[_t5_bucket(dist)][:, g * ATT_HEADS:(g + 1) * ATT_HEADS].astype(F32).T


def _neg(heads, n):
    return jnp.full((heads, n), NEG, F32)


def _dilate(v, dil):
    heads, n = v.shape
    return jnp.stack([v] + [_neg(heads, n)] * (dil - 1), axis=-1).reshape(heads, n * dil)


def _prompt_slots(tb_rev):
    heads = tb_rev.shape[0]
    return jnp.concatenate([tb_rev, _neg(heads, 4 * ATT_BLOCK - N_KEYS)], axis=1)


def _sample_slots(tb, tb_rev, win, dil):
    heads = tb.shape[0]
    s_c = jnp.concatenate([_dilate(tb_rev[:, :N_KEYS - 1], dil), _neg(heads, ATT_BLOCK)], axis=1)
    back = [tb[:, k // dil:k // dil + 1] if k % dil == 0 else _neg(heads, 1) for k in range(SAMPLE_PAD - 1, 0, -1)]
    s_n = jnp.concatenate([tb[:, 0:1], _neg(heads, 2 * ATT_BLOCK - SAMPLE_PAD)] + back, axis=1)
    return s_c, s_n


def _kv_rows_t(tail_t):
    b, _, _, rows = tail_t.shape
    return jnp.transpose(tail_t.reshape(b, 2, ATT_HEADS, ATT_DH, rows), (0, 4, 1, 2, 3))


def kernel(x_prompt, x_sample, cache_kv_w128, cache_kv_w512, cache_kv_w2048, state_retention,
           w_norm, w_in, q_norm, k_norm, rel_bias, ret_norm, w_proj_ret, w_proj_att, w_out):
    assert w_in.shape[0] == 1
    bp, t, _ = x_prompt.shape
    bs, ts, _ = x_sample.shape
    dils = tuple(d for _, d in ATT_GROUPS)
    assert t % (ATT_BLOCK * max(dils)) == 0 and ts <= SAMPLE_PAD
    caches = (cache_kv_w128[0], cache_kv_w512[0], cache_kv_w2048[0])
    for cch, (win, _) in zip(caches, ATT_GROUPS):
        assert cch.shape[1] == win and win <= PAST_LEN

    wn = w_norm[0].reshape(1, D_MODEL)
    qg =jnp.tile(q_norm[0] * (ATT_DH ** -0.5), TILE // ATT_DH).reshape(1, TILE)
    kg = jnp.tile(k_norm[0], TILE // ATT_DH).reshape(1, TILE)
    gn = ret_norm[0].reshape(1, RET_V_W)
    wr = w_proj_ret[0].astype(BF16)
    wa = w_proj_att[0].astype(BF16)
    wo = w_out[0].astype(BF16)
    hid = jnp.arange(TILE // 2) // ATT_DH
    seg_mean = jnp.where(hid[:, None] == hid[None, :], 1.0 / ATT_DH, 0.0).astype(BF16)
    asc, desc = tuple(range(N_KEYS)), tuple(range(N_KEYS - 1, -1, -1))
    group_bias = [(_group_bias(rel_bias, g, d, asc), _group_bias(rel_bias, g, d, desc)) for g, d in enumerate(dils)]

    pad = SAMPLE_PAD
    ns = bs * pad
    xs = jnp.pad(x_sample, ((0, 0), (0, pad - ts), (0, 0))).reshape(ns, D_MODEL)
    rot_s = _rotary_tables(jnp.full((1,), PAST_LEN), jnp.tile(jnp.arange(pad), bs))
    ret_in_s, *qkv_s, gates_s, w_in_bf = _in_proj(xs, ns, (1,) * N_GROUPS, (), F32, wn, w_in[0], rot_s,
                                                  qg, kg, seg_mean, tm=ns)

    tm_p = 256
    rot_p = _rotary_tables(jnp.arange(0, t, tm_p), jnp.arange(tm_p))
    tail_p_rows = tuple(min(w, t) for w, _ in ATT_GROUPS)
    ret_p, state_p, *rest = _in_proj(
        x_prompt.reshape(bp * t, D_MODEL), t, dils, tail_p_rows, BF16, wn, w_in_bf, rot_p, qg, kg, seg_mean,
        tm=tm_p, retention=(_retention_tables(RET_CHUNK, RET_CHUNK), gn))
    qkv_p, gates_p, tails_p = rest[:N_GROUPS], rest[N_GROUPS], rest[N_GROUPS + 1:]
    groups = [_attn_prompt_group(qkv_p[g], _prompt_slots(group_bias[g][1]), step_blocks=16)
              for g in range(N_GROUPS)]
    y_p = _out_proj(x_prompt, ret_p.reshape(bp, t, RET_V_W), gates_p.reshape(bp, t, GATE_W), wr, wa, wo,
                    tm=512, groups=groups)

    qkv_s =[a.reshape(bs, pad, QKV_W) for a in qkv_s]
    caches_t = [jnp.transpose(c, (0, 2, 3, 4, 1)) for c in caches]
    ss = [_sample_slots(*group_bias[g], win, d) for g, (win, d) in enumerate(ATT_GROUPS)]
    ret_s, state_s, att_s, *kv_s = _sample_mixers(
        ret_in_s.reshape(bs, pad, RET_OUT_W), state_retention[0], _retention_tables(ts, pad), gn,
        qkv_s, caches_t, [c for c, _ in ss], jnp.stack([n for _, n in ss]), ts)
    y_s = _out_proj(xs.reshape(1, ns, D_MODEL), ret_s.reshape(1, ns, RET_V_W), gates_s.reshape(1, ns, GATE_W),
                    wr, wa, wo, tm=ns, att=att_s.reshape(1, ns, ATT_OUT_W))
    y_s = y_s.reshape(bs, pad, D_MODEL)[:, :ts]

    kv_p = [_kv_rows_t(tt)[None] for tt in tails_p]
    return (y_p, y_s, state_p[None], state_s[None], kv_p[0], kv_p[1], kv_p[2],
            kv_s[0][None], kv_s[1][None], kv_s[2][None])
```

```python
import functools
import math

import jax
import jax.numpy as jnp
from jax import lax
from jax.experimental import pallas as pl
from jax.experimental.pallas import tpu as pltpu

D_MODEL = 1024
PAST_LEN = 16384
RET_HEADS = 4
RET_DK = 128
RET_DV = 256
RET_CHUNK = 128
ROPE_BASE = 10000.0
ATT_GROUPS = ((128, 1), (512, 4), (2048, 16))
N_GROUPS = 3
ATT_HEADS = 8
ATT_DH = 64
REL_BUCKETS = 32
REL_MAX_DIST = 2048
EPS = 1e-6

RET_QK_W = RET_HEADS * RET_DK
RET_V_W = RET_HEADS * RET_DV
ATT_W = N_GROUPS * ATT_HEADS * ATT_DH
ATT_OUT_W = ATT_HEADS * ATT_DH
IN_W = 2 * RET_QK_W + 2 * RET_V_W + 3 * ATT_W + ATT_OUT_W + 2 * D_MODEL

LANES = 128
TILE = 512
RET_OUT_W = 2 * RET_QK_W + 2 * RET_V_W
QKV_W = 3 * ATT_OUT_W
GATE_W = 2 * D_MODEL + ATT_OUT_W
ATT_BLOCK = 128
N_KEYS = ATT_BLOCK + 1
SAMPLE_PAD = 8
NEG = -1e30
VMEM_LIMIT = 48 * 1024 * 1024
CACHE_RING = 3
W_LOAD_SLOTS = 4

F32 = jnp.float32
BF16 = jnp.bfloat16


def _nt_dot(a, b):
    return lax.dot_general(a, b, (((1,), (1,)), ((), ())), preferred_element_type=F32)


def _dot(a, b):
    return jnp.dot(a, b, preferred_element_type=F32)


def _proj_schedule():
    sched = [(0, 0, 0, "rot_q"), (RET_QK_W, 0, RET_QK_W, "rot_k")]
    for k in range(2 * RET_V_W // TILE):
        sched.append((2 * RET_QK_W + k * TILE, 0, 2 * RET_QK_W + k * TILE, "plain"))
    att0 = 2 * RET_QK_W + 2 * RET_V_W
    for g in range(N_GROUPS):
        for kind, epi in enumerate(("norm_q", "norm_k", "plain")):
            sched.append((att0 + kind * ATT_W + g * ATT_OUT_W, 1 + g, kind * ATT_OUT_W, epi))
    gate0 = att0 + 3 * ATT_W
    for k in range(2 * D_MODEL // TILE):
        sched.append((gate0 + ATT_OUT_W + k * TILE, 4, k * TILE, "plain"))
    sched.append((gate0, 4, 2 * D_MODEL, "plain"))
    return sched


_SCHEDULE = _proj_schedule()


def _in_proj_kernel(*refs, dils, per_seq, fuse_retention, n_tails, cast_weights):
    (x_ref, wn_ref, w_ref, ca_ref, sa_ref, cb_ref, sb_ref, cbs_ref, sbs_ref,
     qg_ref, kg_ref, seg_ref) = refs[:12]
    refs = refs[12:]
    if cast_weights:
        *refs, wf_ref, wb_ref, sem_in, sem_out = refs
        wbf_out_ref = refs.pop(-(4 if fuse_retention else 2) - 1)

        load_slots = wf_ref.shape[0]

        def tile_load(step):
            return pltpu.make_async_copy(w_ref.at[:, pl.ds(_SCHEDULE[step][0], TILE)],
                                         wf_ref.at[step % load_slots], sem_in.at[step % load_slots])

        def tile_store(step):
            return pltpu.make_async_copy(wb_ref.at[step % 2],
                                         wbf_out_ref.at[:, pl.ds(_SCHEDULE[step][0], TILE)], sem_out.at[step % 2])

        for ahead in range(load_slots - 1):
            tile_load(ahead).start()
    if fuse_retention:
        dec_ref, qd_ref, kd_ref, cd_ref, gn_ref, ret_ref, s_ref = refs[:7]
        refs = refs[7:]
    else:
        ret_ref = refs[0]
        refs = refs[1:]
    a0_ref, a1_ref, a2_ref, gate_ref = refs[:4]
    tail_refs = refs[4:4 + n_tails]
    h_ref, y_ref = refs[4 + n_tails:6 + n_tails]
    refs = refs[6 + n_tails:]
    r_ref = refs[0] if fuse_retention else ret_ref
    kt_ref = refs[1] if fuse_retention else None
    out_refs = (r_ref, a0_ref, a1_ref, a2_ref, gate_ref)

    if fuse_retention:
        @pl.when(pl.program_id(0) % per_seq == 0)
        def _():
            s_ref[...] = jnp.zeros_like(s_ref)

    x = x_ref[...]
    tm = x.shape[0]
    ms = jnp.mean(x * x, axis=-1, keepdims=True)
    h_ref[...] = (x * lax.rsqrt(ms + EPS) * wn_ref[...]).astype(BF16)

    ca, sa = ca_ref[0], sa_ref[0]
    cos = ca * cb_ref[...] - sa * sb_ref[...]
    sin = sa * cbs_ref[...] + ca * sbs_ref[...]

    def rotary(y, scale):
        parts = []
        for hh in range(TILE // RET_DK):
            yh = y[:, hh * RET_DK:(hh + 1) * RET_DK]
            parts.append((yh * cos + pltpu.roll(yh, RET_DK // 2, axis=1) * sin) * scale)
        return jnp.concatenate(parts, axis=1)

    def head_rms(y, gain):
        y2 = (y * y).astype(BF16)
        half = TILE // 2
        ms = jnp.concatenate([_dot(y2[:, :half], seg_ref[...]), _dot(y2[:, half:], seg_ref[...])], axis=1)
        return y * lax.rsqrt(ms + EPS) * gain

    epilogues = {
        "plain": lambda y: y,
        "rot_q": lambda y: rotary(y, 1.0),
        "rot_k": lambda y: rotary(y, RET_DK ** -0.5),
        "norm_q": lambda y: head_rms(y, qg_ref[...]),
        "norm_k": lambda y: head_rms(y, kg_ref[...]),
    }

    kt_base = jnp.minimum(pl.program_id(0), 0)

    def retention_issue(c, h):
        rows = slice(c * RET_CHUNK, (c + 1) * RET_CHUNK)
        ks = slice(h * RET_DK, (h + 1) * RET_DK)
        ks2 = slice(RET_QK_W + h * RET_DK, RET_QK_W + (h + 1) * RET_DK)
        vs_in = slice(2 * RET_QK_W + h * RET_DV, 2 * RET_QK_W + (h + 1) * RET_DV)
        qb = r_ref[rows, ks].astype(BF16)
        vb = r_ref[rows, vs_in].astype(BF16)
        state = s_ref[0, h]
        scores = (_nt_dot(qb, r_ref[rows, ks2].astype(BF16)) * dec_ref[h]).astype(BF16)
        carried = _dot(qb, state.astype(BF16)) * qd_ref[h]
        s_ref[0, h] = state * cd_ref[h] + _dot(kt_ref[kt_base + c, ks, :], vb)
        return rows, h, scores, vb, carried

    def retention_finish(rows, h, scores, vb, carried):
        vs = slice(h * RET_DV, (h + 1) * RET_DV)
        gs_in = slice(2 * RET_QK_W + RET_V_W + h * RET_DV, 2 * RET_QK_W + RET_V_W + (h + 1) * RET_DV)
        o = _dot(scores, vb) + carried
        ret_ref[rows, vs] = _group_norm_gate(o, gn_ref[:, vs], r_ref[rows, gs_in]).astype(ret_ref.dtype)

    def store_tile(val, out_idx, out_col):
        o_ref = out_refs[out_idx]
        ocs = slice(out_col, out_col + TILE)
        if not 1 <= out_idx <= N_GROUPS:
            o_ref[:, ocs] = val.astype(o_ref.dtype)
            return
        g, kind = out_idx - 1, out_col // ATT_OUT_W
        if kind > 0 and tail_refs:
            t_ref = tail_refs[g]
            t_ref[0, kind - 1] = jnp.transpose(val)[:, tm - t_ref.shape[3]:]
        dil = dils[g]
        if dil == 1:
            o_ref[0, 0, :, ocs] = val.astype(o_ref.dtype)
            return
        for c in range(TILE // LANES):
            y_ref[c] = val[:, c * LANES:(c + 1) * LANES]
            for r in range(dil):
                o_ref[0, r, :, out_col + c * LANES:out_col + (c + 1) * LANES] = (
                    y_ref[c, pl.ds(r, tm // dil, stride=dil), :].astype(o_ref.dtype))

    n_ret_tiles = sum(1 for t in _SCHEDULE if t[1] == 0)
    units = [(c, h) for c in range(tm // RET_CHUNK) for h in range(RET_HEADS)] if fuse_retention else []
    assert len(units) <= len(_SCHEDULE) - n_ret_tiles
    for step, (w_col, out_idx, out_col, epi) in enumerate(_SCHEDULE):
        if fuse_retention and step == 2:
            for c, h in units:
                rows = slice(c * RET_CHUNK, (c + 1) * RET_CHUNK)
                kt_ref[c, h * RET_DK:(h + 1) * RET_DK, :] = _decayed_keys_t(
                    r_ref[rows, RET_QK_W + h * RET_DK:RET_QK_W + (h + 1) * RET_DK], kd_ref[h])
        unit = units[step - n_ret_tiles] if 0 <= step - n_ret_tiles < len(units) else None
        issued = retention_issue(*unit) if unit else None
        if cast_weights:
            if step + load_slots - 1 < len(_SCHEDULE):
                tile_load(step + load_slots - 1).start()
            tile_load(step).wait()
            if step >= 2:
                tile_store(step - 2).wait()
            wb_ref[step % 2] = wf_ref[step % load_slots].astype(BF16)
            tile_store(step).start()
            w_tile = wb_ref[step % 2]
        else:
            w_tile = w_ref[:, w_col:w_col + TILE]
        store_tile(epilogues[epi](_dot(h_ref[...], w_tile)), out_idx, out_col)
        if unit:
            retention_finish(*issued)
    if cast_weights:
        tile_store(len(_SCHEDULE) - 2).wait()
        tile_store(len(_SCHEDULE) - 1).wait()


def _in_proj(x2d, seq_len, dils, tail_rows, out_dtype, wn, w_bf, rot, qg, kg, seg, tm, retention=None):
    n = x2d.shape[0]
    batch = n // seq_len
    per_seq = seq_len // tm
    fuse = retention is not None
    cast_weights = w_bf.dtype == F32
    assert not cast_weights or n == tm
    const = lambda a: pl.BlockSpec(a.shape, lambda i, nd=a.ndim: (0,) * nd)
    base_spec = pl.BlockSpec((1, 1, RET_DK), lambda i: (i % per_seq, 0, 0))
    qkv_spec = lambda d: pl.BlockSpec((1, d, tm // d, QKV_W), lambda i: (i // per_seq, 0, i % per_seq, 0))
    tail_specs, tail_shapes = [], []
    for rows in tail_rows:
        width = min(tm, rows)
        first = (seq_len - rows) // tm if rows >= tm else per_seq
        tail_specs.append(pl.BlockSpec(
            (1, 2, ATT_OUT_W, width),
            lambda i, first=first: (i // per_seq, 0, 0, jnp.maximum(i % per_seq - first, 0))))
        tail_shapes.append(jax.ShapeDtypeStruct((batch, 2, ATT_OUT_W, rows), F32))
    ca, sa, *offset_tables = rot
    args = [x2d, wn, w_bf, ca, sa, *offset_tables, qg, kg, seg]
    in_specs = [
        pl.BlockSpec((tm, D_MODEL), lambda i: (i, 0)),
        const(wn),
        pl.BlockSpec(memory_space=pl.ANY) if cast_weights
        else pl.BlockSpec(w_bf.shape, lambda i: (0, 0), pipeline_mode=pl.Buffered(1)),
        base_spec, base_spec, *[const(t) for t in offset_tables],
        const(qg), const(kg), const(seg),
    ]
    scratch = [pltpu.VMEM((tm, D_MODEL), BF16), pltpu.VMEM((TILE // LANES, tm, LANES), F32)]
    if fuse:
        tables, gn = retention
        args += [*tables, gn]
        in_specs += [const(t) for t in (*tables, gn)]
        ret_specs = [pl.BlockSpec((tm, RET_V_W), lambda i: (i, 0)),
                     pl.BlockSpec((1, RET_HEADS, RET_DK, RET_DV), lambda i: (i // per_seq, 0, 0, 0))]
        ret_shapes = [jax.ShapeDtypeStruct((n, RET_V_W), out_dtype),
                      jax.ShapeDtypeStruct((batch, RET_HEADS, RET_DK, RET_DV), F32)]
        scratch += [pltpu.VMEM((tm, RET_OUT_W), F32), pltpu.VMEM((tm // RET_CHUNK, RET_QK_W, RET_CHUNK), BF16)]
    else:
        ret_specs = [pl.BlockSpec((tm, RET_OUT_W), lambda i: (i, 0))]
        ret_shapes = [jax.ShapeDtypeStruct((n, RET_OUT_W), out_dtype)]
    cast_specs, cast_shapes = [], []
    if cast_weights:
        cast_specs = [pl.BlockSpec(memory_space=pl.ANY)]
        cast_shapes = [jax.ShapeDtypeStruct(w_bf.shape, BF16)]
        scratch += [pltpu.VMEM((W_LOAD_SLOTS, D_MODEL, TILE), F32), pltpu.VMEM((2, D_MODEL, TILE), BF16),
                    pltpu.SemaphoreType.DMA((W_LOAD_SLOTS,)), pltpu.SemaphoreType.DMA((2,))]
    return pl.pallas_call(
        functools.partial(_in_proj_kernel, dils=dils, per_seq=per_seq, fuse_retention=fuse,
                          n_tails=len(tail_rows), cast_weights=cast_weights),
        grid=(n // tm,),
        in_specs=in_specs,
        out_specs=ret_specs + [qkv_spec(d) for d in dils]
        + [pl.BlockSpec((tm, GATE_W), lambda i: (i, 0))] + tail_specs + cast_specs,
        out_shape=ret_shapes
        + [jax.ShapeDtypeStruct((batch, d, seq_len // d, QKV_W), out_dtype) for d in dils]
        + [jax.ShapeDtypeStruct((n, GATE_W), out_dtype)] + tail_shapes + cast_shapes,
        scratch_shapes=scratch,
        compiler_params=pltpu.CompilerParams(
            dimension_semantics=("arbitrary",), vmem_limit_bytes=VMEM_LIMIT),
        name="in_proj",
    )(*args)


def _group_norm_gate(o, gain, gate):
    gate = gate.astype(F32)
    mu = jnp.mean(o, axis=-1, keepdims=True)
    d = o - mu
    var = jnp.mean(d * d, axis=-1, keepdims=True)
    return gate * jax.nn.sigmoid(gate) * (d * lax.rsqrt(var + EPS) * gain)


def _decayed_keys_t(k_keys, kd):
    return jnp.transpose(k_keys * kd).astype(BF16)


def _retention_specs(rows, idx):
    return [
        pl.BlockSpec((1, rows, RET_QK_W), lambda *a: (*idx(*a), 0)),
        pl.BlockSpec((1, rows, RET_QK_W), lambda *a: (*idx(*a), 1)),
        pl.BlockSpec((1, rows, RET_V_W), lambda *a: (*idx(*a), 1)),
        pl.BlockSpec((1, rows, RET_V_W), lambda *a: (*idx(*a), 2)),
    ]


def _table_specs(tables, ndim_grid):
    return [pl.BlockSpec(t.shape, lambda *a, nd=t.ndim: (0,) * nd) for t in tables]


def _retention_sample_kernel(q_ref, k_ref, v_ref, g_ref, s_in_ref, dec_ref, qd_ref, kd_ref, cd_ref,
                             gn_ref, o_ref, s_out_ref, kpad_ref, vpad_ref):
    @pl.when(pl.program_id(0) == 0)
    def _():
        kpad_ref[...] = jnp.zeros_like(kpad_ref)
        vpad_ref[...] = jnp.zeros_like(vpad_ref)

    kpad_ref[0:SAMPLE_PAD, :] = k_ref[0]
    vpad_ref[0:SAMPLE_PAD, :] = v_ref[0]
    issued = []
    for h in range(RET_HEADS):
        ks = slice(h * RET_DK, (h + 1) * RET_DK)
        vs = slice(h * RET_DV, (h + 1) * RET_DV)
        qb = q_ref[0, :, ks].astype(BF16)
        vb = vpad_ref[:, vs].astype(BF16)
        state = s_in_ref[0, h]
        scores = (_nt_dot(qb, kpad_ref[:, ks].astype(BF16)) * dec_ref[h]).astype(BF16)
        carried = _dot(qb, state.astype(BF16)) * qd_ref[h]
        s_out_ref[0, h] = state * cd_ref[h] + _dot(_decayed_keys_t(kpad_ref[:, ks], kd_ref[h]), vb)
        issued.append((scores, vb, carried))
    for h, (scores, vb, carried) in enumerate(issued):
        vs = slice(h * RET_DV, (h + 1) * RET_DV)
        o = _dot(scores, vb) + carried
        o_ref[0, :, vs] = _group_norm_gate(o, gn_ref[:, vs], g_ref[0, :, vs]).astype(o_ref.dtype)


def _attn_prompt_blocks(q_ref, kp_ref, kc_ref, vp_ref, vc_ref, bias_ref, o_ref, lse_ref, first):
    n_seq, n_blocks = q_ref.shape[1], q_ref.shape[2] // ATT_BLOCK
    assert first or n_seq == 1
    pair_w = 2 * ATT_DH
    low_q = lax.broadcasted_iota(jnp.int32, (ATT_BLOCK, pair_w), 1) < ATT_DH
    rows0, rows1 = slice(0, ATT_BLOCK), slice(ATT_BLOCK, 2 * ATT_BLOCK)

    def keys(prev_ref, cur_ref, sq, t, ps):
        if t > 0:
            return cur_ref[0, sq, (t - 1) * ATT_BLOCK:(t + 1) * ATT_BLOCK, ps]
        if first:
            return cur_ref[0, sq, 0:ATT_BLOCK, ps]
        return jnp.concatenate([prev_ref[0, sq, :, ps], cur_ref[0, sq, 0:ATT_BLOCK, ps]], axis=0)

    staged = []
    for sq, t in [(sq, t) for sq in range(n_seq) for t in range(n_blocks)]:
        for p in range(ATT_HEADS // 2):
            ps = slice(p * pair_w, (p + 1) * pair_w)
            qp = q_ref[0, sq, t * ATT_BLOCK:(t + 1) * ATT_BLOCK, ps]
            kcat = keys(kp_ref, kc_ref, sq, t, ps)
            q2 = jnp.concatenate([jnp.where(low_q, qp, 0.0), jnp.where(low_q, 0.0, qp)], axis=0)
            s = _nt_dot(q2, kcat) + bias_ref[p, :, 2 * ATT_BLOCK - kcat.shape[0]:]
            s = s.astype(BF16)
            m = jnp.max(s, axis=-1, keepdims=True)
            staged.append((sq, t, p, m.astype(F32), jnp.exp(s - m)))
    for sq, t, p, m, e in staged:
        ps = slice(p * pair_w, (p + 1) * pair_w)
        vcat = keys(vp_ref, vc_ref, sq, t, ps)
        low_k = lax.broadcasted_iota(jnp.int32, vcat.shape, 1) < ATT_DH
        p0 = _dot(e[rows0], jnp.where(low_k, vcat, 1.0))
        p1 = _dot(e[rows1], jnp.where(low_k, 1.0, vcat))
        den = pltpu.roll(jnp.where(low_q, p1, p0), ATT_DH, axis=1)
        ts = slice(t * ATT_BLOCK, (t + 1) * ATT_BLOCK)
        o_ref[0, sq, ts, ps] = jnp.where(low_q, p0, p1) / den
        lse_ref[0, sq, ts, ps] = jnp.where(low_q, m[rows0], m[rows1]) + jnp.log(den)


def _attn_prompt_kernel(q_ref, kp_ref, kc_ref, vp_ref, vc_ref, slot_ref, o_ref, lse_ref, bias_ref,
                        *, whole_sequences):
    first = pl.program_id(2) == 0

    @pl.when(jnp.logical_and(first, jnp.logical_and(pl.program_id(0) == 0, pl.program_id(1) == 0)))
    def _():
        for h in range(ATT_HEADS):
            slots = jnp.broadcast_to(slot_ref[h:h + 1, :], (ATT_BLOCK, slot_ref.shape[1]))
            rows = pltpu.roll(slots, 0, 1, stride=1, stride_axis=0)
            bias_ref[h // 2, (h % 2) * ATT_BLOCK:(h % 2 + 1) * ATT_BLOCK, :] = rows[:, :2 * ATT_BLOCK]

    if whole_sequences:
        _attn_prompt_blocks(q_ref, kp_ref, kc_ref, vp_ref, vc_ref, bias_ref, o_ref, lse_ref, True)
        return

    @pl.when(first)
    def _():
        _attn_prompt_blocks(q_ref, kp_ref, kc_ref, vp_ref, vc_ref, bias_ref, o_ref, lse_ref, True)

    @pl.when(jnp.logical_not(first))
    def _():
        _attn_prompt_blocks(q_ref, kp_ref, kc_ref, vp_ref, vc_ref, bias_ref, o_ref, lse_ref, False)


def _attn_prompt_group(qkv, slot_bias, step_blocks):
    b, dil, tr, _ = qkv.shape
    n_blocks = min(step_blocks, tr // ATT_BLOCK)
    n_seq = min(step_blocks // n_blocks, dil)
    rows = n_blocks * ATT_BLOCK
    cur = lambda c: pl.BlockSpec((1, n_seq, rows, ATT_OUT_W), lambda bi, r, j: (bi, r, j, c))
    prev = lambda c: pl.BlockSpec((1, n_seq, ATT_BLOCK, ATT_OUT_W),
                                  lambda bi, r, j: (bi, r, jnp.maximum(j * n_blocks - 1, 0), c))
    res_shape = jax.ShapeDtypeStruct((b, dil, tr, ATT_OUT_W), F32)
    return pl.pallas_call(
        functools.partial(_attn_prompt_kernel, whole_sequences=tr == rows),
        grid=(b, dil // n_seq, tr // rows),
        in_specs=[cur(0), prev(1), cur(1), prev(2), cur(2),
                  pl.BlockSpec(slot_bias.shape, lambda bi, r, j: (0, 0))],
        out_specs=[cur(0), cur(0)],
        out_shape=[res_shape, res_shape],
        scratch_shapes=[pltpu.VMEM((ATT_HEADS // 2, 2 * ATT_BLOCK, 2 * ATT_BLOCK), F32)],
        compiler_params=pltpu.CompilerParams(
            dimension_semantics=("arbitrary", "arbitrary", "arbitrary"), vmem_limit_bytes=VMEM_LIMIT),
        name=f"attn_prompt_d{dil}",
    )(qkv, qkv, qkv, qkv, qkv, slot_bias)


def _attn_sample_kernel(a0_ref, a1_ref, a2_ref, c0_ref, c1_ref, c2_ref, s0_ref, s1_ref, s2_ref, sn_ref,
                        o_ref, kn_ref, vn_ref, b0_ref, b1_ref, b2_ref, bn_ref, *, slot):
    qkv_refs = (a0_ref, a1_ref, a2_ref)
    cache_refs = (c0_ref, c1_ref, c2_ref)
    bias_refs = (b0_ref, b1_ref, b2_ref)

    @pl.when(pl.program_id(0) == 0)
    def _():
        kn_ref[...] = jnp.zeros_like(kn_ref)
        vn_ref[...] = jnp.zeros_like(vn_ref)
        def rotated_rows(vec, width):
            rows = jnp.broadcast_to(vec, (SAMPLE_PAD, vec.shape[1]))
            return pltpu.roll(rows, 0, 1, stride=1, stride_axis=0)[:, :width]
        for g, (s_ref, b_ref) in enumerate(zip((s0_ref, s1_ref, s2_ref), bias_refs)):
            for h in range(ATT_HEADS):
                b_ref[h] = rotated_rows(s_ref[h:h + 1, :], b_ref.shape[2])
                bn_ref[g, h] = rotated_rows(sn_ref[g, h:h + 1, :], ATT_BLOCK)

    for g in range(N_GROUPS):
        gs = slice(g * ATT_OUT_W, (g + 1) * ATT_OUT_W)
        kn_ref[0:SAMPLE_PAD, gs] = qkv_refs[g][0, :, ATT_OUT_W:2 * ATT_OUT_W]
        vn_ref[0:SAMPLE_PAD, gs] = qkv_refs[g][0, :, 2 * ATT_OUT_W:3 * ATT_OUT_W]

    staged = []
    for h in range(ATT_HEADS):
        logits = []
        for g in range(N_GROUPS):
            hs = slice(g * ATT_OUT_W + h * ATT_DH, g * ATT_OUT_W + (h + 1) * ATT_DH)
            qh = qkv_refs[g][0, :, h * ATT_DH:(h + 1) * ATT_DH].astype(BF16)
            logits.append(_dot(qh, cache_refs[g][slot, 0, h].astype(BF16)) + bias_refs[g][h])
            logits.append(_nt_dot(qh, kn_ref[:, hs].astype(BF16)) + bn_ref[g, h])
        m = functools.reduce(jnp.maximum, [jnp.max(x, axis=-1, keepdims=True) for x in logits])
        es = [jnp.exp(x - m) for x in logits]
        l = functools.reduce(jnp.add, [jnp.sum(e, axis=-1, keepdims=True) for e in es])
        staged.append(([e.astype(BF16) for e in es], l))
    for h, (es, l) in enumerate(staged):
        acc = jnp.zeros((SAMPLE_PAD, ATT_DH), F32)
        for g in range(N_GROUPS):
            hs = slice(g * ATT_OUT_W + h * ATT_DH, g * ATT_OUT_W + (h + 1) * ATT_DH)
            acc = acc + _nt_dot(es[2 * g], cache_refs[g][slot, 1, h].astype(BF16))
            acc = acc + _dot(es[2 * g + 1], vn_ref[:, hs].astype(BF16))
        o_ref[0, :, h * ATT_DH:(h + 1) * ATT_DH] = acc / l


_N_RET_IN, _N_ATT_IN = 10, 10


def _sample_mixers_kernel(*refs):
    n_in = _N_RET_IN + _N_ATT_IN
    ret_in, att_in = refs[:_N_RET_IN], refs[_N_RET_IN:n_in]
    ret_o_ref, state_o_ref, att_o_ref = refs[n_in:n_in + 3]
    kv_o_refs = refs[n_in + 3:n_in + 3 + N_GROUPS]
    kpad_ref, vpad_ref, kn_ref, vn_ref = refs[n_in + 3 + N_GROUPS:n_in + 7 + N_GROUPS]
    *bias_refs, sem = refs[n_in + 7 + N_GROUPS:]
    ring_refs, bias_refs = bias_refs[N_GROUPS + 1:], bias_refs[:N_GROUPS + 1]
    cache_refs = att_in[N_GROUPS:2 * N_GROUPS]

    depth = ring_refs[0].shape[0]
    step, n_steps = pl.program_id(0), pl.num_programs(0)

    def cache_copy(row, g):
        return pltpu.make_async_copy(cache_refs[g].at[row], ring_refs[g].at[row % depth], sem.at[row % depth, g])

    @pl.when(step == 0)
    def _():
        for row in range(depth - 1):
            for g in range(N_GROUPS):
                cache_copy(row, g).start()

    @pl.when(step + depth - 1 < n_steps)
    def _():
        for g in range(N_GROUPS):
            cache_copy(step + depth - 1, g).start()

    _retention_sample_kernel(*ret_in, ret_o_ref, state_o_ref, kpad_ref, vpad_ref)
    for qkv_ref, kv_ref in zip(att_in[:N_GROUPS], kv_o_refs):
        for tok in range(kv_ref.shape[1]):
            for kv in range(2):
                for h in range(ATT_HEADS):
                    col = (1 + kv) * ATT_OUT_W + h * ATT_DH
                    kv_ref[0, tok, kv, h:h + 1, :] = qkv_ref[0, tok:tok + 1, col:col + ATT_DH]
    for g in range(N_GROUPS):
        cache_copy(step, g).wait()
    _attn_sample_kernel(*att_in[:N_GROUPS], *ring_refs, *att_in[2 * N_GROUPS:], att_o_ref, kn_ref, vn_ref,
                        *bias_refs, slot=step % depth)


def _sample_mixers(ret, state, tables, gn, qkvs, caches_t, slots, slots_new, n_tokens):
    b, p, _ = ret.shape
    state_spec = pl.BlockSpec((1, RET_HEADS, RET_DK, RET_DV), lambda bi: (bi, 0, 0, 0))
    ret_specs = _retention_specs(p, lambda bi: (bi, 0)) + [state_spec] + _table_specs(tables + (gn,), 1)
    att_specs = ([pl.BlockSpec((1, p, QKV_W), lambda bi: (bi, 0, 0)) for _ in qkvs]
                 + [pl.BlockSpec(memory_space=pl.ANY) for _ in caches_t]
                 + [pl.BlockSpec(x.shape, lambda bi: (0, 0)) for x in slots]
                 + [pl.BlockSpec(slots_new.shape, lambda bi: (0, 0, 0))])
    assert len(ret_specs) == _N_RET_IN and len(att_specs) == _N_ATT_IN
    kv_shape = (b, n_tokens, 2, ATT_HEADS, ATT_DH)
    return pl.pallas_call(
        _sample_mixers_kernel,
        grid=(b,),
        in_specs=ret_specs + att_specs,
        out_specs=[pl.BlockSpec((1, p, RET_V_W), lambda bi: (bi, 0, 0)), state_spec,
                   pl.BlockSpec((1, p, ATT_OUT_W), lambda bi: (bi, 0, 0))]
        + [pl.BlockSpec((1,) + kv_shape[1:], lambda bi: (bi, 0, 0, 0, 0))] * N_GROUPS,
        out_shape=[jax.ShapeDtypeStruct((b, p, RET_V_W), ret.dtype), jax.ShapeDtypeStruct(state.shape, F32),
                   jax.ShapeDtypeStruct((b, p, ATT_OUT_W), F32)]
        + [jax.ShapeDtypeStruct(kv_shape, F32)] * N_GROUPS,
        scratch_shapes=[pltpu.VMEM((RET_CHUNK, RET_QK_W), F32), pltpu.VMEM((RET_CHUNK, RET_V_W), F32),
                        pltpu.VMEM((ATT_BLOCK, ATT_W), F32), pltpu.VMEM((ATT_BLOCK, ATT_W), F32)]
        + [pltpu.VMEM((ATT_HEADS, p, c.shape[-1]), F32) for c in caches_t]
        + [pltpu.VMEM((N_GROUPS, ATT_HEADS, p, ATT_BLOCK), F32)]
        + [pltpu.VMEM((CACHE_RING,) + c.shape[1:], c.dtype) for c in caches_t]
        + [pltpu.SemaphoreType.DMA((CACHE_RING, N_GROUPS))],
        compiler_params=pltpu.CompilerParams(
            dimension_semantics=("arbitrary",), vmem_limit_bytes=VMEM_LIMIT),
        name="sample_mixers",
    )(ret, ret, ret, ret, state, *tables, gn, *qkvs, *caches_t, *slots, slots_new)


def _out_proj_kernel(*refs, dils):
    x_ref, ret_ref, ga_ref, gb_ref, ag_ref, wr_ref, wa_ref, wo_ref = refs[:8]
    if dils is None:
        att_ref, o_ref = refs[8:]
        att = att_ref[0]
    else:
        group_refs = refs[8:8 + 2 * N_GROUPS]
        o_ref = refs[8 + 2 * N_GROUPS]
        scratch = refs[9 + 2 * N_GROUPS:]
        tm = x_ref.shape[1]
        os, lses = [], []
        for g, dil in enumerate(dils):
            og_ref, lg_ref = group_refs[2 * g], group_refs[2 * g + 1]
            if dil == 1:
                os.append(og_ref[0, 0]); lses.append(lg_ref[0, 0])
                continue
            so_ref, sl_ref = scratch[2 * (g - 1)], scratch[2 * (g - 1) + 1]
            n_chunks = ATT_OUT_W // LANES
            for c in range(n_chunks):
                cs = slice(c * LANES, (c + 1) * LANES)
                for r in range(dil):
                    so_ref[c, pl.ds(r, tm // dil, stride=dil), :] = og_ref[0, r, :, cs]
                    sl_ref[c, pl.ds(r, tm // dil, stride=dil), :] = lg_ref[0, r, :, cs]
            os.append(jnp.concatenate([so_ref[c] for c in range(n_chunks)], axis=1))
            lses.append(jnp.concatenate([sl_ref[c] for c in range(n_chunks)], axis=1))
        mx = functools.reduce(jnp.maximum, lses)
        ws = [jnp.exp(l - mx) for l in lses]
        att = functools.reduce(jnp.add, [w * o for w, o in zip(ws, os)]) / functools.reduce(jnp.add, ws)
    ag = ag_ref[0].astype(F32)
    u = (ag * jax.nn.sigmoid(ag) * att).astype(BF16)
    o_b = _dot(u, wa_ref[...])
    o_a = _dot(ret_ref[0].astype(BF16), wr_ref[...])
    merged = jax.nn.sigmoid(ga_ref[0].astype(F32)) * o_a + jax.nn.sigmoid(gb_ref[0].astype(F32)) * o_b
    o_ref[0] = x_ref[0] + _dot(merged.astype(BF16), wo_ref[...])


def _out_proj(x, ret, gates, wr, wa, wo, tm, att=None, groups=None):
    b, t, _ = x.shape
    row = lambda w, c: pl.BlockSpec((1, tm, w), lambda bi, i: (bi, i, c))
    full = lambda a: pl.BlockSpec(a.shape, lambda bi, i: (0, 0))
    in_specs = [row(D_MODEL, 0), row(RET_V_W, 0), row(D_MODEL, 0), row(D_MODEL, 1),
                row(ATT_OUT_W, 2 * D_MODEL // ATT_OUT_W), full(wr), full(wa), full(wo)]
    args = [x, ret, gates, gates, gates, wr, wa, wo]
    scratch = []
    if groups is None:
        dils = None
        in_specs.append(row(ATT_OUT_W, 0))
        args.append(att)
    else:
        dils = tuple(o.shape[1] for o, _ in groups)
        for (o, lse), d in zip(groups, dils):
            spec = pl.BlockSpec((1, d, tm // d, ATT_OUT_W), lambda bi, i: (bi, 0, i, 0))
            in_specs += [spec, spec]
            args += [o, lse]
            if d > 1:
                scratch += [pltpu.VMEM((ATT_OUT_W // LANES, tm, LANES), F32)] * 2
    return pl.pallas_call(
        functools.partial(_out_proj_kernel, dils=dils),
        grid=(b, t // tm),
        in_specs=in_specs,
        out_specs=row(D_MODEL, 0),
        out_shape=jax.ShapeDtypeStruct((b, t, D_MODEL), F32),
        scratch_shapes=scratch,
        compiler_params=pltpu.CompilerParams(
            dimension_semantics=("parallel", "parallel"), vmem_limit_bytes=VMEM_LIMIT),
        name="out_proj",
    )(*args)


def _rotary_tables(bases, offsets):
    half = RET_DK // 2
    inv = ROPE_BASE ** (-jnp.arange(half, dtype=F32) / half)
    inv2 = jnp.concatenate([inv, inv])
    sign = jnp.concatenate([-jnp.ones((half,), F32), jnp.ones((half,), F32)])
    a = bases.astype(F32)[:, None, None] * inv2
    b = offsets.astype(F32)[:, None] * inv2
    cb, sb = jnp.cos(b), jnp.sin(b)
    return jnp.cos(a), jnp.sin(a), cb, sb, sign * cb, sign * sb


def _retention_tables(c, rows):
    log_g = jnp.log1p(-(2.0 ** (-5.0 - jnp.arange(RET_HEADS, dtype=F32))))
    i = jnp.arange(c, dtype=F32)
    diff = i[:, None] - i[None, :]
    decay = jnp.where(diff[None] >= 0, jnp.exp(jnp.maximum(diff, 0.0)[None] * log_g[:, None, None]), 0.0)
    q_decay = jnp.exp((i + 1.0)[None, :] * log_g[:, None])
    k_decay = jnp.exp((c - 1.0 - i)[None, :] * log_g[:, None])
    chunk_decay = jnp.exp(c * log_g)
    dec = jnp.zeros((RET_HEADS, rows, RET_CHUNK), F32).at[:, :c, :c].set(decay)
    qd = jnp.zeros((RET_HEADS, rows, 1), F32).at[:, :c, 0].set(q_decay)
    kd = jnp.zeros((RET_HEADS, RET_CHUNK, 1), F32).at[:, :c, 0].set(k_decay)
    cd = jnp.broadcast_to(chunk_decay[:, None, None], (RET_HEADS, 1, RET_DV))
    return dec, qd, kd, cd


def _t5_bucket(dist):
    max_exact = REL_BUCKETS // 2
    d = jnp.maximum(dist.astype(F32), 1.0)
    large = max_exact + (jnp.log(d / max_exact) / math.log(REL_MAX_DIST / max_exact)
                         * (REL_BUCKETS - max_exact)).astype(jnp.int32)
    large = jnp.minimum(large, REL_BUCKETS - 1)
    return jnp.where(dist < max_exact, dist, large)


def _group_bias(rel_bias, g, dil, slots):
    dist = dil * jnp.asarray(slots, dtype=jnp.int32)
    return rel_bias[_t5_bucket(dist)][:, g * ATT_HEADS:(g + 1) * ATT_HEADS].astype(F32).T


def _neg(heads, n):
    return jnp.full((heads, n), NEG, F32)


def _dilate(v, dil):
    heads, n = v.shape
    return jnp.stack([v] + [_neg(heads, n)] * (dil - 1), axis=-1).reshape(heads, n * dil)


def _prompt_slots(tb_rev):
    heads = tb_rev.shape[0]
    return jnp.concatenate([tb_rev, _neg(heads, 4 * ATT_BLOCK - N_KEYS)], axis=1)


def _sample_slots(tb, tb_rev, win, dil):
    heads = tb.shape[0]
    s_c = jnp.concatenate([_dilate(tb_rev[:, :N_KEYS - 1], dil), _neg(heads, ATT_BLOCK)], axis=1)
    back = [tb[:, k // dil:k // dil + 1] if k % dil == 0 else _neg(heads, 1) for k in range(SAMPLE_PAD - 1, 0, -1)]
    s_n = jnp.concatenate([tb[:, 0:1], _neg(heads, 2 * ATT_BLOCK - SAMPLE_PAD)] + back, axis=1)
    return s_c, s_n


def _kv_rows_t(tail_t):
    b, _, _, rows = tail_t.shape
    return jnp.transpose(tail_t.reshape(b, 2, ATT_HEADS, ATT_DH, rows), (0, 4, 1, 2, 3))


def kernel(x_prompt, x_sample, cache_kv_w128, cache_kv_w512, cache_kv_w2048, state_retention,
           w_norm, w_in, q_norm, k_norm, rel_bias, ret_norm, w_proj_ret, w_proj_att, w_out):
    assert w_in.shape[0] == 1
    bp, t, _ = x_prompt.shape
    bs, ts, _ = x_sample.shape
    dils = tuple(d for _, d in ATT_GROUPS)
    assert t % (ATT_BLOCK * max(dils)) == 0 and ts <= SAMPLE_PAD
    caches = (cache_kv_w128[0], cache_kv_w512[0], cache_kv_w2048[0])
    for cch, (win, _) in zip(caches, ATT_GROUPS):
        assert cch.shape[1] == win and win <= PAST_LEN

    wn = w_norm[0].reshape(1, D_MODEL)
    qg =jnp.tile(q_norm[0] * (ATT_DH ** -0.5), TILE // ATT_DH).reshape(1, TILE)
    kg = jnp.tile(k_norm[0], TILE // ATT_DH).reshape(1, TILE)
    gn = ret_norm[0].reshape(1, RET_V_W)
    wr = w_proj_ret[0].astype(BF16)
    wa = w_proj_att[0].astype(BF16)
    wo = w_out[0].astype(BF16)
    hid = jnp.arange(TILE // 2) // ATT_DH
    seg_mean = jnp.where(hid[:, None] == hid[None, :], 1.0 / ATT_DH, 0.0).astype(BF16)
    asc, desc = tuple(range(N_KEYS)), tuple(range(N_KEYS - 1, -1, -1))
    group_bias = [(_group_bias(rel_bias, g, d, asc), _group_bias(rel_bias, g, d, desc)) for g, d in enumerate(dils)]

    pad = SAMPLE_PAD
    ns = bs * pad
    xs = jnp.pad(x_sample, ((0, 0), (0, pad - ts), (0, 0))).reshape(ns, D_MODEL)
    rot_s = _rotary_tables(jnp.full((1,), PAST_LEN), jnp.tile(jnp.arange(pad), bs))
    ret_in_s, *qkv_s, gates_s, w_in_bf = _in_proj(xs, ns, (1,) * N_GROUPS, (), F32, wn, w_in[0], rot_s,
                                                  qg, kg, seg_mean, tm=ns)

    tm_p = 256
    rot_p = _rotary_tables(jnp.arange(0, t, tm_p), jnp.arange(tm_p))
    tail_p_rows = tuple(min(w, t) for w, _ in ATT_GROUPS)
    ret_p, state_p, *rest = _in_proj(
        x_prompt.reshape(bp * t, D_MODEL), t, dils, tail_p_rows, BF16, wn, w_in_bf, rot_p, qg, kg, seg_mean,
        tm=tm_p, retention=(_retention_tables(RET_CHUNK, RET_CHUNK), gn))
    qkv_p, gates_p, tails_p = rest[:N_GROUPS], rest[N_GROUPS], rest[N_GROUPS + 1:]
    groups = [_attn_prompt_group(qkv_p[g], _prompt_slots(group_bias[g][1]), step_blocks=16)
              for g in range(N_GROUPS)]
    y_p = _out_proj(x_prompt, ret_p.reshape(bp, t, RET_V_W), gates_p.reshape(bp, t, GATE_W), wr, wa, wo,
                    tm=512, groups=groups)

    qkv_s =[a.reshape(bs, pad, QKV_W) for a in qkv_s]
    caches_t = [jnp.transpose(c, (0, 2, 3, 4, 1)) for c in caches]
    ss = [_sample_slots(*group_bias[g], win, d) for g, (win, d) in enumerate(ATT_GROUPS)]
    ret_s, state_s, att_s, *kv_s = _sample_mixers(
        ret_in_s.reshape(bs, pad, RET_OUT_W), state_retention[0], _retention_tables(ts, pad), gn,
        qkv_s, caches_t, [c for c, _ in ss], jnp.stack([n for _, n in ss]), ts)
    y_s = _out_proj(xs.reshape(1, ns, D_MODEL), ret_s.reshape(1, ns, RET_V_W), gates_s.reshape(1, ns, GATE_W),
                    wr, wa, wo, tm=ns, att=att_s.reshape(1, ns, ATT_OUT_W))
    y_s = y_s.reshape(bs, pad, D_MODEL)[:, :ts]

    kv_p = [_kv_rows_t(tt)[None] for tt in tails_p]
    return (y_p, y_s, state_p[None], state_s[None], kv_p[0], kv_p[1], kv_p[2],
            kv_s[0][None], kv_s[1][None], kv_s[2][None])
```

```python
import functools
import math

import jax
import jax.numpy as jnp
from jax import lax
from jax.experimental import pallas as pl
from jax.experimental.pallas import tpu as pltpu

D_MODEL = 1024
PAST_LEN = 16384
RET_HEADS = 4
RET_DK = 128
RET_DV = 256
RET_CHUNK = 128
ROPE_BASE = 10000.0
ATT_GROUPS = ((128, 1), (512, 4), (2048, 16))
N_GROUPS = 3
ATT_HEADS = 8
ATT_DH = 64
REL_BUCKETS = 32
REL_MAX_DIST = 2048
EPS = 1e-6

RET_QK_W = RET_HEADS * RET_DK
RET_V_W = RET_HEADS * RET_DV
ATT_W = N_GROUPS * ATT_HEADS * ATT_DH
ATT_OUT_W = ATT_HEADS * ATT_DH
IN_W = 2 * RET_QK_W + 2 * RET_V_W + 3 * ATT_W + ATT_OUT_W + 2 * D_MODEL

LANES = 128
TILE = 512
RET_OUT_W = 2 * RET_QK_W + 2 * RET_V_W
QKV_W = 3 * ATT_OUT_W
GATE_W = 2 * D_MODEL + ATT_OUT_W
ATT_BLOCK = 128
N_KEYS = ATT_BLOCK + 1
SAMPLE_PAD = 8
NEG = -1e30
VMEM_LIMIT = 48 * 1024 * 1024
CACHE_RING = 3
W_LOAD_SLOTS = 4

F32 = jnp.float32
BF16 = jnp.bfloat16


def _nt_dot(a, b):
    return lax.dot_general(a, b, (((1,), (1,)), ((), ())), preferred_element_type=F32)


def _dot(a, b):
    return jnp.dot(a, b, preferred_element_type=F32)


def _proj_schedule():
    sched = [(0, 0, 0, "rot_q"), (RET_QK_W, 0, RET_QK_W, "rot_k")]
    for k in range(2 * RET_V_W // TILE):
        sched.append((2 * RET_QK_W + k * TILE, 0, 2 * RET_QK_W + k * TILE, "plain"))
    att0 = 2 * RET_QK_W + 2 * RET_V_W
    for g in range(N_GROUPS):
        for kind, epi in enumerate(("norm_q", "norm_k", "plain")):
            sched.append((att0 + kind * ATT_W + g * ATT_OUT_W, 1 + g, kind * ATT_OUT_W, epi))
    gate0 = att0 + 3 * ATT_W
    for k in range(2 * D_MODEL // TILE):
        sched.append((gate0 + ATT_OUT_W + k * TILE, 4, k * TILE, "plain"))
    sched.append((gate0, 4, 2 * D_MODEL, "plain"))
    return sched


_SCHEDULE = _proj_schedule()


def _in_proj_kernel(*refs, dils, per_seq, fuse_retention, n_tails, cast_weights):
    (x_ref, wn_ref, w_ref, ca_ref, sa_ref, cb_ref, sb_ref, cbs_ref, sbs_ref,
     qg_ref, kg_ref, seg_ref) = refs[:12]
    refs = refs[12:]
    if cast_weights:
        *refs, wf_ref, wb_ref, sem_in, sem_out = refs
        wbf_out_ref = refs.pop(-(4 if fuse_retention else 2) - 1)

        load_slots = wf_ref.shape[0]

        def tile_load(step):
            return pltpu.make_async_copy(w_ref.at[:, pl.ds(_SCHEDULE[step][0], TILE)],
                                         wf_ref.at[step % load_slots], sem_in.at[step % load_slots])

        def tile_store(step):
            return pltpu.make_async_copy(wb_ref.at[step % 2],
                                         wbf_out_ref.at[:, pl.ds(_SCHEDULE[step][0], TILE)], sem_out.at[step % 2])

        for ahead in range(load_slots - 1):
            tile_load(ahead).start(priority=ahead % 2)
    if fuse_retention:
        dec_ref, qd_ref, kd_ref, cd_ref, gn_ref, ret_ref, s_ref = refs[:7]
        refs = refs[7:]
    else:
        ret_ref = refs[0]
        refs = refs[1:]
    a0_ref, a1_ref, a2_ref, gate_ref = refs[:4]
    tail_refs = refs[4:4 + n_tails]
    h_ref, y_ref = refs[4 + n_tails:6 + n_tails]
    refs = refs[6 + n_tails:]
    r_ref = refs[0] if fuse_retention else ret_ref
    kt_ref = refs[1] if fuse_retention else None
    out_refs = (r_ref, a0_ref, a1_ref, a2_ref, gate_ref)

    if fuse_retention:
        @pl.when(pl.program_id(0) % per_seq == 0)
        def _():
            s_ref[...] = jnp.zeros_like(s_ref)

    x = x_ref[...]
    tm = x.shape[0]
    ms = jnp.mean(x * x, axis=-1, keepdims=True)
    h_ref[...] = (x * lax.rsqrt(ms + EPS) * wn_ref[...]).astype(BF16)

    ca, sa = ca_ref[0], sa_ref[0]
    cos = ca * cb_ref[...] - sa * sb_ref[...]
    sin = sa * cbs_ref[...] + ca * sbs_ref[...]

    def rotary(y, scale):
        parts = []
        for hh in range(TILE // RET_DK):
            yh = y[:, hh * RET_DK:(hh + 1) * RET_DK]
            parts.append((yh * cos + pltpu.roll(yh, RET_DK // 2, axis=1) * sin) * scale)
        return jnp.concatenate(parts, axis=1)

    def head_rms(y, gain):
        y2 = (y * y).astype(BF16)
        half = TILE // 2
        ms = jnp.concatenate([_dot(y2[:, :half], seg_ref[...]), _dot(y2[:, half:], seg_ref[...])], axis=1)
        return y * lax.rsqrt(ms + EPS) * gain

    epilogues = {
        "plain": lambda y: y,
        "rot_q": lambda y: rotary(y, 1.0),
        "rot_k": lambda y: rotary(y, RET_DK ** -0.5),
        "norm_q": lambda y: head_rms(y, qg_ref[...]),
        "norm_k": lambda y: head_rms(y, kg_ref[...]),
    }

    kt_base = jnp.minimum(pl.program_id(0), 0)

    def retention_issue(c, h):
        rows = slice(c * RET_CHUNK, (c + 1) * RET_CHUNK)
        ks = slice(h * RET_DK, (h + 1) * RET_DK)
        ks2 = slice(RET_QK_W + h * RET_DK, RET_QK_W + (h + 1) * RET_DK)
        vs_in = slice(2 * RET_QK_W + h * RET_DV, 2 * RET_QK_W + (h + 1) * RET_DV)
        qb = r_ref[rows, ks].astype(BF16)
        vb = r_ref[rows, vs_in].astype(BF16)
        state = s_ref[0, h]
        scores = (_nt_dot(qb, r_ref[rows, ks2].astype(BF16)) * dec_ref[h]).astype(BF16)
        carried = _dot(qb, state.astype(BF16)) * qd_ref[h]
        s_ref[0, h] = state * cd_ref[h] + _dot(kt_ref[kt_base + c, ks, :], vb)
        return rows, h, scores, vb, carried

    def retention_finish(rows, h, scores, vb, carried):
        vs = slice(h * RET_DV, (h + 1) * RET_DV)
        gs_in = slice(2 * RET_QK_W + RET_V_W + h * RET_DV, 2 * RET_QK_W + RET_V_W + (h + 1) * RET_DV)
        o = _dot(scores, vb) + carried
        ret_ref[rows, vs] = _group_norm_gate(o, gn_ref[:, vs], r_ref[rows, gs_in]).astype(ret_ref.dtype)

    def store_tile(val, out_idx, out_col):
        o_ref = out_refs[out_idx]
        ocs = slice(out_col, out_col + TILE)
        if not 1 <= out_idx <= N_GROUPS:
            o_ref[:, ocs] = val.astype(o_ref.dtype)
            return
        g, kind = out_idx - 1, out_col // ATT_OUT_W
        if kind > 0 and tail_refs:
            t_ref = tail_refs[g]
            t_ref[0, kind - 1] = jnp.transpose(val)[:, tm - t_ref.shape[3]:]
        dil = dils[g]
        if dil == 1:
            o_ref[0, 0, :, ocs] = val.astype(o_ref.dtype)
            return
        for c in range(TILE // LANES):
            y_ref[c] = val[:, c * LANES:(c + 1) * LANES]
            for r in range(dil):
                o_ref[0, r, :, out_col + c * LANES:out_col + (c + 1) * LANES] = (
                    y_ref[c, pl.ds(r, tm // dil, stride=dil), :].astype(o_ref.dtype))

    n_ret_tiles = sum(1 for t in _SCHEDULE if t[1] == 0)
    units = [(c, h) for c in range(tm // RET_CHUNK) for h in range(RET_HEADS)] if fuse_retention else []
    assert len(units) <= len(_SCHEDULE) - n_ret_tiles
    for step, (w_col, out_idx, out_col, epi) in enumerate(_SCHEDULE):
        if fuse_retention and step == 2:
            for c, h in units:
                rows = slice(c * RET_CHUNK, (c + 1) * RET_CHUNK)
                kt_ref[c, h * RET_DK:(h + 1) * RET_DK, :] = _decayed_keys_t(
                    r_ref[rows, RET_QK_W + h * RET_DK:RET_QK_W + (h + 1) * RET_DK], kd_ref[h])
        unit = units[step - n_ret_tiles] if 0 <= step - n_ret_tiles < len(units) else None
        issued = retention_issue(*unit) if unit else None
        if cast_weights:
            if step + load_slots - 1 < len(_SCHEDULE):
                tile_load(step + load_slots - 1).start(priority=(step + load_slots - 1) % 2)
            tile_load(step).wait()
            if step >= 2:
                tile_store(step - 2).wait()
            wb_ref[step % 2] = wf_ref[step % load_slots].astype(BF16)
            tile_store(step).start()
            w_tile = wb_ref[step % 2]
        else:
            w_tile = w_ref[:, w_col:w_col + TILE]
        store_tile(epilogues[epi](_dot(h_ref[...], w_tile)), out_idx, out_col)
        if unit:
            retention_finish(*issued)
    if cast_weights:
        tile_store(len(_SCHEDULE) - 2).wait()
        tile_store(len(_SCHEDULE) - 1).wait()


def _in_proj(x2d, seq_len, dils, tail_rows, out_dtype, wn, w_bf, rot, qg, kg, seg, tm, retention=None):
    n = x2d.shape[0]
    batch = n // seq_len
    per_seq = seq_len // tm
    fuse = retention is not None
    cast_weights = w_bf.dtype == F32
    assert not cast_weights or n == tm
    const = lambda a: pl.BlockSpec(a.shape, lambda i, nd=a.ndim: (0,) * nd)
    base_spec = pl.BlockSpec((1, 1, RET_DK), lambda i: (i % per_seq, 0, 0))
    qkv_spec = lambda d: pl.BlockSpec((1, d, tm // d, QKV_W), lambda i: (i // per_seq, 0, i % per_seq, 0))
    tail_specs, tail_shapes = [], []
    for rows in tail_rows:
        width = min(tm, rows)
        first = (seq_len - rows) // tm if rows >= tm else per_seq
        tail_specs.append(pl.BlockSpec(
            (1, 2, ATT_OUT_W, width),
            lambda i, first=first: (i // per_seq, 0, 0, jnp.maximum(i % per_seq - first, 0))))
        tail_shapes.append(jax.ShapeDtypeStruct((batch, 2, ATT_OUT_W, rows), F32))
    ca, sa, *offset_tables = rot
    args = [x2d, wn, w_bf, ca, sa, *offset_tables, qg, kg, seg]
    in_specs = [
        pl.BlockSpec((tm, D_MODEL), lambda i: (i, 0)),
        const(wn),
        pl.BlockSpec(memory_space=pl.ANY) if cast_weights
        else pl.BlockSpec(w_bf.shape, lambda i: (0, 0), pipeline_mode=pl.Buffered(1)),
        base_spec, base_spec, *[const(t) for t in offset_tables],
        const(qg), const(kg), const(seg),
    ]
    scratch = [pltpu.VMEM((tm, D_MODEL), BF16), pltpu.VMEM((TILE // LANES, tm, LANES), F32)]
    if fuse:
        tables, gn = retention
        args += [*tables, gn]
        in_specs += [const(t) for t in (*tables, gn)]
        ret_specs = [pl.BlockSpec((tm, RET_V_W), lambda i: (i, 0)),
                     pl.BlockSpec((1, RET_HEADS, RET_DK, RET_DV), lambda i: (i // per_seq, 0, 0, 0))]
        ret_shapes = [jax.ShapeDtypeStruct((n, RET_V_W), out_dtype),
                      jax.ShapeDtypeStruct((batch, RET_HEADS, RET_DK, RET_DV), F32)]
        scratch += [pltpu.VMEM((tm, RET_OUT_W), F32), pltpu.VMEM((tm // RET_CHUNK, RET_QK_W, RET_CHUNK), BF16)]
    else:
        ret_specs = [pl.BlockSpec((tm, RET_OUT_W), lambda i: (i, 0))]
        ret_shapes = [jax.ShapeDtypeStruct((n, RET_OUT_W), out_dtype)]
    cast_specs, cast_shapes = [], []
    if cast_weights:
        cast_specs = [pl.BlockSpec(memory_space=pl.ANY)]
        cast_shapes = [jax.ShapeDtypeStruct(w_bf.shape, BF16)]
        scratch += [pltpu.VMEM((W_LOAD_SLOTS, D_MODEL, TILE), F32), pltpu.VMEM((2, D_MODEL, TILE), BF16),
                    pltpu.SemaphoreType.DMA((W_LOAD_SLOTS,)), pltpu.SemaphoreType.DMA((2,))]
    return pl.pallas_call(
        functools.partial(_in_proj_kernel, dils=dils, per_seq=per_seq, fuse_retention=fuse,
                          n_tails=len(tail_rows), cast_weights=cast_weights),
        grid=(n // tm,),
        in_specs=in_specs,
        out_specs=ret_specs + [qkv_spec(d) for d in dils]
        + [pl.BlockSpec((tm, GATE_W), lambda i: (i, 0))] + tail_specs + cast_specs,
        out_shape=ret_shapes
        + [jax.ShapeDtypeStruct((batch, d, seq_len // d, QKV_W), out_dtype) for d in dils]
        + [jax.ShapeDtypeStruct((n, GATE_W), out_dtype)] + tail_shapes + cast_shapes,
        scratch_shapes=scratch,
        compiler_params=pltpu.CompilerParams(
            dimension_semantics=("arbitrary",), vmem_limit_bytes=VMEM_LIMIT),
        name="in_proj",
    )(*args)


def _group_norm_gate(o, gain, gate):
    gate = gate.astype(F32)
    mu = jnp.mean(o, axis=-1, keepdims=True)
    d = o - mu
    var = jnp.mean(d * d, axis=-1, keepdims=True)
    return gate * jax.nn.sigmoid(gate) * (d * lax.rsqrt(var + EPS) * gain)


def _decayed_keys_t(k_keys, kd):
    return jnp.transpose(k_keys * kd).astype(BF16)


def _retention_specs(rows, idx):
    return [
        pl.BlockSpec((1, rows, RET_QK_W), lambda *a: (*idx(*a), 0)),
        pl.BlockSpec((1, rows, RET_QK_W), lambda *a: (*idx(*a), 1)),
        pl.BlockSpec((1, rows, RET_V_W), lambda *a: (*idx(*a), 1)),
        pl.BlockSpec((1, rows, RET_V_W), lambda *a: (*idx(*a), 2)),
    ]


def _table_specs(tables, ndim_grid):
    return [pl.BlockSpec(t.shape, lambda *a, nd=t.ndim: (0,) * nd) for t in tables]


def _retention_sample_kernel(q_ref, k_ref, v_ref, g_ref, s_in_ref, dec_ref, qd_ref, kd_ref, cd_ref,
                             gn_ref, o_ref, s_out_ref, kpad_ref, vpad_ref):
    @pl.when(pl.program_id(0) == 0)
    def _():
        kpad_ref[...] = jnp.zeros_like(kpad_ref)
        vpad_ref[...] = jnp.zeros_like(vpad_ref)

    kpad_ref[0:SAMPLE_PAD, :] = k_ref[0]
    vpad_ref[0:SAMPLE_PAD, :] = v_ref[0]
    issued = []
    for h in range(RET_HEADS):
        ks = slice(h * RET_DK, (h + 1) * RET_DK)
        vs = slice(h * RET_DV, (h + 1) * RET_DV)
        qb = q_ref[0, :, ks].astype(BF16)
        vb = vpad_ref[:, vs].astype(BF16)
        state = s_in_ref[0, h]
        scores = (_nt_dot(qb, kpad_ref[:, ks].astype(BF16)) * dec_ref[h]).astype(BF16)
        carried = _dot(qb, state.astype(BF16)) * qd_ref[h]
        s_out_ref[0, h] = state * cd_ref[h] + _dot(_decayed_keys_t(kpad_ref[:, ks], kd_ref[h]), vb)
        issued.append((scores, vb, carried))
    for h, (scores, vb, carried) in enumerate(issued):
        vs = slice(h * RET_DV, (h + 1) * RET_DV)
        o = _dot(scores, vb) + carried
        o_ref[0, :, vs] = _group_norm_gate(o, gn_ref[:, vs], g_ref[0, :, vs]).astype(o_ref.dtype)


def _attn_prompt_blocks(q_ref, kp_ref, kc_ref, vp_ref, vc_ref, bias_ref, o_ref, lse_ref, first):
    n_seq, n_blocks = q_ref.shape[1], q_ref.shape[2] // ATT_BLOCK
    assert first or n_seq == 1
    pair_w = 2 * ATT_DH
    low_q = lax.broadcasted_iota(jnp.int32, (ATT_BLOCK, pair_w), 1) < ATT_DH
    rows0, rows1 = slice(0, ATT_BLOCK), slice(ATT_BLOCK, 2 * ATT_BLOCK)

    def keys(prev_ref, cur_ref, sq, t, ps):
        if t > 0:
            return cur_ref[0, sq, (t - 1) * ATT_BLOCK:(t + 1) * ATT_BLOCK, ps]
        if first:
            return cur_ref[0, sq, 0:ATT_BLOCK, ps]
        return jnp.concatenate([prev_ref[0, sq, :, ps], cur_ref[0, sq, 0:ATT_BLOCK, ps]], axis=0)

    staged = []
    for sq, t in [(sq, t) for sq in range(n_seq) for t in range(n_blocks)]:
        for p in range(ATT_HEADS // 2):
            ps = slice(p * pair_w, (p + 1) * pair_w)
            qp = q_ref[0, sq, t * ATT_BLOCK:(t + 1) * ATT_BLOCK, ps]
            kcat = keys(kp_ref, kc_ref, sq, t, ps)
            q2 = jnp.concatenate([jnp.where(low_q, qp, 0.0), jnp.where(low_q, 0.0, qp)], axis=0)
            s = _nt_dot(q2, kcat) + bias_ref[p, :, 2 * ATT_BLOCK - kcat.shape[0]:]
            s = s.astype(BF16)
            m = jnp.max(s, axis=-1, keepdims=True)
            staged.append((sq, t, p, m.astype(F32), jnp.exp(s - m)))
    for sq, t, p, m, e in staged:
        ps = slice(p * pair_w, (p + 1) * pair_w)
        vcat = keys(vp_ref, vc_ref, sq, t, ps)
        low_k = lax.broadcasted_iota(jnp.int32, vcat.shape, 1) < ATT_DH
        p0 = _dot(e[rows0], jnp.where(low_k, vcat, 1.0))
        p1 = _dot(e[rows1], jnp.where(low_k, 1.0, vcat))
        den = pltpu.roll(jnp.where(low_q, p1, p0), ATT_DH, axis=1)
        ts = slice(t * ATT_BLOCK, (t + 1) * ATT_BLOCK)
        o_ref[0, sq, ts, ps] = jnp.where(low_q, p0, p1) / den
        lse_ref[0, sq, ts, ps] = jnp.where(low_q, m[rows0], m[rows1]) + jnp.log(den)


def _attn_prompt_kernel(q_ref, kp_ref, kc_ref, vp_ref, vc_ref, slot_ref, o_ref, lse_ref, bias_ref,
                        *, whole_sequences):
    first = pl.program_id(2) == 0

    @pl.when(jnp.logical_and(first, jnp.logical_and(pl.program_id(0) == 0, pl.program_id(1) == 0)))
    def _():
        for h in range(ATT_HEADS):
            slots = jnp.broadcast_to(slot_ref[h:h + 1, :], (ATT_BLOCK, slot_ref.shape[1]))
            rows = pltpu.roll(slots, 0, 1, stride=1, stride_axis=0)
            bias_ref[h // 2, (h % 2) * ATT_BLOCK:(h % 2 + 1) * ATT_BLOCK, :] = rows[:, :2 * ATT_BLOCK]

    if whole_sequences:
        _attn_prompt_blocks(q_ref, kp_ref, kc_ref, vp_ref, vc_ref, bias_ref, o_ref, lse_ref, True)
        return

    @pl.when(first)
    def _():
        _attn_prompt_blocks(q_ref, kp_ref, kc_ref, vp_ref, vc_ref, bias_ref, o_ref, lse_ref, True)

    @pl.when(jnp.logical_not(first))
    def _():
        _attn_prompt_blocks(q_ref, kp_ref, kc_ref, vp_ref, vc_ref, bias_ref, o_ref, lse_ref, False)


def _attn_prompt_group(qkv, slot_bias, step_blocks):
    b, dil, tr, _ = qkv.shape
    n_blocks = min(step_blocks, tr // ATT_BLOCK)
    n_seq = min(step_blocks // n_blocks, dil)
    rows = n_blocks * ATT_BLOCK
    cur = lambda c: pl.BlockSpec((1, n_seq, rows, ATT_OUT_W), lambda bi, r, j: (bi, r, j, c))
    prev = lambda c: pl.BlockSpec((1, n_seq, ATT_BLOCK, ATT_OUT_W),
                                  lambda bi, r, j: (bi, r, jnp.maximum(j * n_blocks - 1, 0), c))
    res_shape = jax.ShapeDtypeStruct((b, dil, tr, ATT_OUT_W), F32)
    return pl.pallas_call(
        functools.partial(_attn_prompt_kernel, whole_sequences=tr == rows),
        grid=(b, dil // n_seq, tr // rows),
        in_specs=[cur(0), prev(1), cur(1), prev(2), cur(2),
                  pl.BlockSpec(slot_bias.shape, lambda bi, r, j: (0, 0))],
        out_specs=[cur(0), cur(0)],
        out_shape=[res_shape, res_shape],
        scratch_shapes=[pltpu.VMEM((ATT_HEADS // 2, 2 * ATT_BLOCK, 2 * ATT_BLOCK), F32)],
        compiler_params=pltpu.CompilerParams(
            dimension_semantics=("arbitrary", "arbitrary", "arbitrary"), vmem_limit_bytes=VMEM_LIMIT),
        name=f"attn_prompt_d{dil}",
    )(qkv, qkv, qkv, qkv, qkv, slot_bias)


def _attn_sample_kernel(a0_ref, a1_ref, a2_ref, c0_ref, c1_ref, c2_ref, s0_ref, s1_ref, s2_ref, sn_ref,
                        o_ref, kn_ref, vn_ref, b0_ref, b1_ref, b2_ref, bn_ref, *, slot):
    qkv_refs = (a0_ref, a1_ref, a2_ref)
    cache_refs = (c0_ref, c1_ref, c2_ref)
    bias_refs = (b0_ref, b1_ref, b2_ref)

    @pl.when(pl.program_id(0) == 0)
    def _():
        kn_ref[...] = jnp.zeros_like(kn_ref)
        vn_ref[...] = jnp.zeros_like(vn_ref)
        def rotated_rows(vec, width):
            rows = jnp.broadcast_to(vec, (SAMPLE_PAD, vec.shape[1]))
            return pltpu.roll(rows, 0, 1, stride=1, stride_axis=0)[:, :width]
        for g, (s_ref, b_ref) in enumerate(zip((s0_ref, s1_ref, s2_ref), bias_refs)):
            for h in range(ATT_HEADS):
                b_ref[h] = rotated_rows(s_ref[h:h + 1, :], b_ref.shape[2])
                bn_ref[g, h] = rotated_rows(sn_ref[g, h:h + 1, :], ATT_BLOCK)

    for g in range(N_GROUPS):
        gs = slice(g * ATT_OUT_W, (g + 1) * ATT_OUT_W)
        kn_ref[0:SAMPLE_PAD, gs] = qkv_refs[g][0, :, ATT_OUT_W:2 * ATT_OUT_W]
        vn_ref[0:SAMPLE_PAD, gs] = qkv_refs[g][0, :, 2 * ATT_OUT_W:3 * ATT_OUT_W]

    staged = []
    for h in range(ATT_HEADS):
        logits = []
        for g in range(N_GROUPS):
            hs = slice(g * ATT_OUT_W + h * ATT_DH, g * ATT_OUT_W + (h + 1) * ATT_DH)
            qh = qkv_refs[g][0, :, h * ATT_DH:(h + 1) * ATT_DH].astype(BF16)
            logits.append(_dot(qh, cache_refs[g][slot, 0, h].astype(BF16)) + bias_refs[g][h])
            logits.append(_nt_dot(qh, kn_ref[:, hs].astype(BF16)) + bn_ref[g, h])
        m = functools.reduce(jnp.maximum, [jnp.max(x, axis=-1, keepdims=True) for x in logits])
        es = [jnp.exp(x - m) for x in logits]
        l = functools.reduce(jnp.add, [jnp.sum(e, axis=-1, keepdims=True) for e in es])
        staged.append(([e.astype(BF16) for e in es], l))
    for h, (es, l) in enumerate(staged):
        acc = jnp.zeros((SAMPLE_PAD, ATT_DH), F32)
        for g in range(N_GROUPS):
            hs = slice(g * ATT_OUT_W + h * ATT_DH, g * ATT_OUT_W + (h + 1) * ATT_DH)
            acc = acc + _nt_dot(es[2 * g], cache_refs[g][slot, 1, h].astype(BF16))
            acc = acc + _dot(es[2 * g + 1], vn_ref[:, hs].astype(BF16))
        o_ref[0, :, h * ATT_DH:(h + 1) * ATT_DH] = acc / l


_N_RET_IN, _N_ATT_IN = 10, 10


def _sample_mixers_kernel(*refs):
    n_in = _N_RET_IN + _N_ATT_IN
    ret_in, att_in = refs[:_N_RET_IN], refs[_N_RET_IN:n_in]
    ret_o_ref, state_o_ref, att_o_ref = refs[n_in:n_in + 3]
    kv_o_refs = refs[n_in + 3:n_in + 3 + N_GROUPS]
    kpad_ref, vpad_ref, kn_ref, vn_ref = refs[n_in + 3 + N_GROUPS:n_in + 7 + N_GROUPS]
    *bias_refs, sem = refs[n_in + 7 + N_GROUPS:]
    ring_refs, bias_refs = bias_refs[N_GROUPS + 1:], bias_refs[:N_GROUPS + 1]
    cache_refs = att_in[N_GROUPS:2 * N_GROUPS]

    depth = ring_refs[0].shape[0]
    step, n_steps = pl.program_id(0), pl.num_programs(0)

    def cache_copy(row, g):
        return pltpu.make_async_copy(cache_refs[g].at[row], ring_refs[g].at[row % depth], sem.at[row % depth, g])

    @pl.when(step == 0)
    def _():
        for row in range(depth - 1):
            for g in range(N_GROUPS):
                cache_copy(row, g).start(priority=g // 2)

    @pl.when(step + depth - 1 < n_steps)
    def _():
        for g in range(N_GROUPS):
            cache_copy(step + depth - 1, g).start(priority=g // 2)

    _retention_sample_kernel(*ret_in, ret_o_ref, state_o_ref, kpad_ref, vpad_ref)
    for qkv_ref, kv_ref in zip(att_in[:N_GROUPS], kv_o_refs):
        for tok in range(kv_ref.shape[1]):
            for kv in range(2):
                for h in range(ATT_HEADS):
                    col = (1 + kv) * ATT_OUT_W + h * ATT_DH
                    kv_ref[0, tok, kv, h:h + 1, :] = qkv_ref[0, tok:tok + 1, col:col + ATT_DH]
    for g in range(N_GROUPS):
        cache_copy(step, g).wait()
    _attn_sample_kernel(*att_in[:N_GROUPS], *ring_refs, *att_in[2 * N_GROUPS:], att_o_ref, kn_ref, vn_ref,
                        *bias_refs, slot=step % depth)


def _sample_mixers(ret, state, tables, gn, qkvs, caches_t, slots, slots_new, n_tokens):
    b, p, _ = ret.shape
    state_spec = pl.BlockSpec((1, RET_HEADS, RET_DK, RET_DV), lambda bi: (bi, 0, 0, 0))
    ret_specs = _retention_specs(p, lambda bi: (bi, 0)) + [state_spec] + _table_specs(tables + (gn,), 1)
    att_specs = ([pl.BlockSpec((1, p, QKV_W), lambda bi: (bi, 0, 0)) for _ in qkvs]
                 + [pl.BlockSpec(memory_space=pl.ANY) for _ in caches_t]
                 + [pl.BlockSpec(x.shape, lambda bi: (0, 0)) for x in slots]
                 + [pl.BlockSpec(slots_new.shape, lambda bi: (0, 0, 0))])
    assert len(ret_specs) == _N_RET_IN and len(att_specs) == _N_ATT_IN
    kv_shape = (b, n_tokens, 2, ATT_HEADS, ATT_DH)
    return pl.pallas_call(
        _sample_mixers_kernel,
        grid=(b,),
        in_specs=ret_specs + att_specs,
        out_specs=[pl.BlockSpec((1, p, RET_V_W), lambda bi: (bi, 0, 0)), state_spec,
                   pl.BlockSpec((1, p, ATT_OUT_W), lambda bi: (bi, 0, 0))]
        + [pl.BlockSpec((1,) + kv_shape[1:], lambda bi: (bi, 0, 0, 0, 0))] * N_GROUPS,
        out_shape=[jax.ShapeDtypeStruct((b, p, RET_V_W), ret.dtype), jax.ShapeDtypeStruct(state.shape, F32),
                   jax.ShapeDtypeStruct((b, p, ATT_OUT_W), F32)]
        + [jax.ShapeDtypeStruct(kv_shape, F32)] * N_GROUPS,
        scratch_shapes=[pltpu.VMEM((RET_CHUNK, RET_QK_W), F32), pltpu.VMEM((RET_CHUNK, RET_V_W), F32),
                        pltpu.VMEM((ATT_BLOCK, ATT_W), F32), pltpu.VMEM((ATT_BLOCK, ATT_W), F32)]
        + [pltpu.VMEM((ATT_HEADS, p, c.shape[-1]), F32) for c in caches_t]
        + [pltpu.VMEM((N_GROUPS, ATT_HEADS, p, ATT_BLOCK), F32)]
        + [pltpu.VMEM((CACHE_RING,) + c.shape[1:], c.dtype) for c in caches_t]
        + [pltpu.SemaphoreType.DMA((CACHE_RING, N_GROUPS))],
        compiler_params=pltpu.CompilerParams(
            dimension_semantics=("arbitrary",), vmem_limit_bytes=VMEM_LIMIT),
        name="sample_mixers",
    )(ret, ret, ret, ret, state, *tables, gn, *qkvs, *caches_t, *slots, slots_new)


def _out_proj_kernel(*refs, dils):
    x_ref, ret_ref, ga_ref, gb_ref, ag_ref, wr_ref, wa_ref, wo_ref = refs[:8]
    if dils is None:
        att_ref, o_ref = refs[8:]
        att = att_ref[0]
    else:
        group_refs = refs[8:8 + 2 * N_GROUPS]
        o_ref = refs[8 + 2 * N_GROUPS]
        scratch = refs[9 + 2 * N_GROUPS:]
        tm = x_ref.shape[1]
        os, lses = [], []
        for g, dil in enumerate(dils):
            og_ref, lg_ref = group_refs[2 * g], group_refs[2 * g + 1]
            if dil == 1:
                os.append(og_ref[0, 0]); lses.append(lg_ref[0, 0])
                continue
            so_ref, sl_ref = scratch[2 * (g - 1)], scratch[2 * (g - 1) + 1]
            n_chunks = ATT_OUT_W // LANES
            for c in range(n_chunks):
                cs = slice(c * LANES, (c + 1) * LANES)
                for r in range(dil):
                    so_ref[c, pl.ds(r, tm // dil, stride=dil), :] = og_ref[0, r, :, cs]
                    sl_ref[c, pl.ds(r, tm // dil, stride=dil), :] = lg_ref[0, r, :, cs]
            os.append(jnp.concatenate([so_ref[c] for c in range(n_chunks)], axis=1))
            lses.append(jnp.concatenate([sl_ref[c] for c in range(n_chunks)], axis=1))
        mx = functools.reduce(jnp.maximum, lses)
        ws = [jnp.exp(l - mx) for l in lses]
        att = functools.reduce(jnp.add, [w * o for w, o in zip(ws, os)]) / functools.reduce(jnp.add, ws)
    ag = ag_ref[0].astype(F32)
    u = (ag * jax.nn.sigmoid(ag) * att).astype(BF16)
    o_b = _dot(u, wa_ref[...])
    o_a = _dot(ret_ref[0].astype(BF16), wr_ref[...])
    merged = jax.nn.sigmoid(ga_ref[0].astype(F32)) * o_a + jax.nn.sigmoid(gb_ref[0].astype(F32)) * o_b
    o_ref[0] = x_ref[0] + _dot(merged.astype(BF16), wo_ref[...])


def _out_proj(x, ret, gates, wr, wa, wo, tm, att=None, groups=None):
    b, t, _ = x.shape
    row = lambda w, c: pl.BlockSpec((1, tm, w), lambda bi, i: (bi, i, c))
    full = lambda a: pl.BlockSpec(a.shape, lambda bi, i: (0, 0))
    in_specs = [row(D_MODEL, 0), row(RET_V_W, 0), row(D_MODEL, 0), row(D_MODEL, 1),
                row(ATT_OUT_W, 2 * D_MODEL // ATT_OUT_W), full(wr), full(wa), full(wo)]
    args = [x, ret, gates, gates, gates, wr, wa, wo]
    scratch = []
    if groups is None:
        dils = None
        in_specs.append(row(ATT_OUT_W, 0))
        args.append(att)
    else:
        dils = tuple(o.shape[1] for o, _ in groups)
        for (o, lse), d in zip(groups, dils):
            spec = pl.BlockSpec((1, d, tm // d, ATT_OUT_W), lambda bi, i: (bi, 0, i, 0))
            in_specs += [spec, spec]
            args += [o, lse]
            if d > 1:
                scratch += [pltpu.VMEM((ATT_OUT_W // LANES, tm, LANES), F32)] * 2
    return pl.pallas_call(
        functools.partial(_out_proj_kernel, dils=dils),
        grid=(b, t // tm),
        in_specs=in_specs,
        out_specs=row(D_MODEL, 0),
        out_shape=jax.ShapeDtypeStruct((b, t, D_MODEL), F32),
        scratch_shapes=scratch,
        compiler_params=pltpu.CompilerParams(
            dimension_semantics=("parallel", "parallel"), vmem_limit_bytes=VMEM_LIMIT),
        name="out_proj",
    )(*args)


def _rotary_tables(bases, offsets):
    half = RET_DK // 2
    inv = ROPE_BASE ** (-jnp.arange(half, dtype=F32) / half)
    inv2 = jnp.concatenate([inv, inv])
    sign = jnp.concatenate([-jnp.ones((half,), F32), jnp.ones((half,), F32)])
    a = bases.astype(F32)[:, None, None] * inv2
    b = offsets.astype(F32)[:, None] * inv2
    cb, sb = jnp.cos(b), jnp.sin(b)
    return jnp.cos(a), jnp.sin(a), cb, sb, sign * cb, sign * sb


def _retention_tables(c, rows):
    log_g = jnp.log1p(-(2.0 ** (-5.0 - jnp.arange(RET_HEADS, dtype=F32))))
    i = jnp.arange(c, dtype=F32)
    diff = i[:, None] - i[None, :]
    decay = jnp.where(diff[None] >= 0, jnp.exp(jnp.maximum(diff, 0.0)[None] * log_g[:, None, None]), 0.0)
    q_decay = jnp.exp((i + 1.0)[None, :] * log_g[:, None])
    k_decay = jnp.exp((c - 1.0 - i)[None, :] * log_g[:, None])
    chunk_decay = jnp.exp(c * log_g)
    dec = jnp.zeros((RET_HEADS, rows, RET_CHUNK), F32).at[:, :c, :c].set(decay)
    qd = jnp.zeros((RET_HEADS, rows, 1), F32).at[:, :c, 0].set(q_decay)
    kd = jnp.zeros((RET_HEADS, RET_CHUNK, 1), F32).at[:, :c, 0].set(k_decay)
    cd = jnp.broadcast_to(chunk_decay[:, None, None], (RET_HEADS, 1, RET_DV))
    return dec, qd, kd, cd


def _t5_bucket(dist):
    max_exact = REL_BUCKETS // 2
    d = jnp.maximum(dist.astype(F32), 1.0)
    large = max_exact + (jnp.log(d / max_exact) / math.log(REL_MAX_DIST / max_exact)
                         * (REL_BUCKETS - max_exact)).astype(jnp.int32)
    large = jnp.minimum(large, REL_BUCKETS - 1)
    return jnp.where(dist < max_exact, dist, large)


def _group_bias(rel_bias, g, dil, slots):
    dist = dil * jnp.asarray(slots, dtype=jnp.int32)
    return rel_bias[_t5_bucket(dist)][:, g * ATT_HEADS:(g + 1) * ATT_HEADS].astype(F32).T


def _neg(heads, n):
    return jnp.full((heads, n), NEG, F32)


def _dilate(v, dil):
    heads, n = v.shape
    return jnp.stack([v] + [_neg(heads, n)] * (dil - 1), axis=-1).reshape(heads, n * dil)


def _prompt_slots(tb_rev):
    heads = tb_rev.shape[0]
    return jnp.concatenate([tb_rev, _neg(heads, 4 * ATT_BLOCK - N_KEYS)], axis=1)


def _sample_slots(tb, tb_rev, win, dil):
    heads = tb.shape[0]
    s_c = jnp.concatenate([_dilate(tb_rev[:, :N_KEYS - 1], dil), _neg(heads, ATT_BLOCK)], axis=1)
    back = [tb[:, k // dil:k // dil + 1] if k % dil == 0 else _neg(heads, 1) for k in range(SAMPLE_PAD - 1, 0, -1)]
    s_n = jnp.concatenate([tb[:, 0:1], _neg(heads, 2 * ATT_BLOCK - SAMPLE_PAD)] + back, axis=1)
    return s_c, s_n


def _kv_rows_t(tail_t):
    b, _, _, rows = tail_t.shape
    return jnp.transpose(tail_t.reshape(b, 2, ATT_HEADS, ATT_DH, rows), (0, 4, 1, 2, 3))


def kernel(x_prompt, x_sample, cache_kv_w128, cache_kv_w512, cache_kv_w2048, state_retention,
           w_norm, w_in, q_norm, k_norm, rel_bias, ret_norm, w_proj_ret, w_proj_att, w_out):
    assert w_in.shape[0] == 1
    bp, t, _ = x_prompt.shape
    bs, ts, _ = x_sample.shape
    dils = tuple(d for _, d in ATT_GROUPS)
    assert t % (ATT_BLOCK * max(dils)) == 0 and ts <= SAMPLE_PAD
    caches = (cache_kv_w128[0], cache_kv_w512[0], cache_kv_w2048[0])
    for cch, (win, _) in zip(caches, ATT_GROUPS):
        assert cch.shape[1] == win and win <= PAST_LEN

    wn = w_norm[0].reshape(1, D_MODEL)
    qg =jnp.tile(q_norm[0] * (ATT_DH ** -0.5), TILE // ATT_DH).reshape(1, TILE)
    kg = jnp.tile(k_norm[0], TILE // ATT_DH).reshape(1, TILE)
    gn = ret_norm[0].reshape(1, RET_V_W)
    wr = w_proj_ret[0].astype(BF16)
    wa = w_proj_att[0].astype(BF16)
    wo = w_out[0].astype(BF16)
    hid = jnp.arange(TILE // 2) // ATT_DH
    seg_mean = jnp.where(hid[:, None] == hid[None, :], 1.0 / ATT_DH, 0.0).astype(BF16)
    asc, desc = tuple(range(N_KEYS)), tuple(range(N_KEYS - 1, -1, -1))
    group_bias = [(_group_bias(rel_bias, g, d, asc), _group_bias(rel_bias, g, d, desc)) for g, d in enumerate(dils)]

    pad = SAMPLE_PAD
    ns = bs * pad
    xs = jnp.pad(x_sample, ((0, 0), (0, pad - ts), (0, 0))).reshape(ns, D_MODEL)
    rot_s = _rotary_tables(jnp.full((1,), PAST_LEN), jnp.tile(jnp.arange(pad), bs))
    ret_in_s, *qkv_s, gates_s, w_in_bf = _in_proj(xs, ns, (1,) * N_GROUPS, (), F32, wn, w_in[0], rot_s,
                                                  qg, kg, seg_mean, tm=ns)

    tm_p = 256
    rot_p = _rotary_tables(jnp.arange(0, t, tm_p), jnp.arange(tm_p))
    tail_p_rows = tuple(min(w, t) for w, _ in ATT_GROUPS)
    ret_p, state_p, *rest = _in_proj(
        x_prompt.reshape(bp * t, D_MODEL), t, dils, tail_p_rows, BF16, wn, w_in_bf, rot_p, qg, kg, seg_mean,
        tm=tm_p, retention=(_retention_tables(RET_CHUNK, RET_CHUNK), gn))
    qkv_p, gates_p, tails_p = rest[:N_GROUPS], rest[N_GROUPS], rest[N_GROUPS + 1:]
    groups = [_attn_prompt_group(qkv_p[g], _prompt_slots(group_bias[g][1]), step_blocks=16)
              for g in range(N_GROUPS)]
    y_p = _out_proj(x_prompt, ret_p.reshape(bp, t, RET_V_W), gates_p.reshape(bp, t, GATE_W), wr, wa, wo,
                    tm=512, groups=groups)

    qkv_s =[a.reshape(bs, pad, QKV_W) for a in qkv_s]
    caches_t = [jnp.transpose(c, (0, 2, 3, 4, 1)) for c in caches]
    ss = [_sample_slots(*group_bias[g], win, d) for g, (win, d) in enumerate(ATT_GROUPS)]
    ret_s, state_s, att_s, *kv_s = _sample_mixers(
        ret_in_s.reshape(bs, pad, RET_OUT_W), state_retention[0], _retention_tables(ts, pad), gn,
        qkv_s, caches_t, [c for c, _ in ss], jnp.stack([n for _, n in ss]), ts)
    y_s = _out_proj(xs.reshape(1, ns, D_MODEL), ret_s.reshape(1, ns, RET_V_W), gates_s.reshape(1, ns, GATE_W),
                    wr, wa, wo, tm=ns, att=att_s.reshape(1, ns, ATT_OUT_W))
    y_s = y_s.reshape(bs, pad, D_MODEL)[:, :ts]

    kv_p = [_kv_rows_t(tt)[None] for tt in tails_p]
    return (y_p, y_s, state_p[None], state_s[None], kv_p[0], kv_p[1], kv_p[2],
            kv_s[0][None], kv_s[1][None], kv_s[2][None])
```

```python
import functools
import math

import jax
import jax.numpy as jnp
from jax import lax
from jax.experimental import pallas as pl
from jax.experimental.pallas import tpu as pltpu

D_MODEL = 1024
PAST_LEN = 16384
RET_HEADS = 4
RET_DK = 128
RET_DV = 256
RET_CHUNK = 128
ROPE_BASE = 10000.0
ATT_GROUPS = ((128, 1), (512, 4), (2048, 16))
N_GROUPS = 3
ATT_HEADS = 8
ATT_DH = 64
REL_BUCKETS = 32
REL_MAX_DIST = 2048
EPS = 1e-6

RET_QK_W = RET_HEADS * RET_DK
RET_V_W = RET_HEADS * RET_DV
ATT_W = N_GROUPS * ATT_HEADS * ATT_DH
ATT_OUT_W = ATT_HEADS * ATT_DH
IN_W = 2 * RET_QK_W + 2 * RET_V_W + 3 * ATT_W + ATT_OUT_W + 2 * D_MODEL

LANES = 128
TILE = 512
RET_OUT_W = 2 * RET_QK_W + 2 * RET_V_W
QKV_W = 3 * ATT_OUT_W
GATE_W = 2 * D_MODEL + ATT_OUT_W
ATT_BLOCK = 128
N_KEYS = ATT_BLOCK + 1
SAMPLE_PAD = 8
NEG = -1e30
VMEM_LIMIT = 48 * 1024 * 1024
CACHE_RING = 3
W_LOAD_SLOTS = 4

F32 = jnp.float32
BF16 = jnp.bfloat16


def _nt_dot(a, b):
    return lax.dot_general(a, b, (((1,), (1,)), ((), ())), preferred_element_type=F32)


def _dot(a, b):
    return jnp.dot(a, b, preferred_element_type=F32)


def _proj_schedule():
    sched = [(0, 0, 0, "rot_q"), (RET_QK_W, 0, RET_QK_W, "rot_k")]
    for k in range(2 * RET_V_W // TILE):
        sched.append((2 * RET_QK_W + k * TILE, 0, 2 * RET_QK_W + k * TILE, "plain"))
    att0 = 2 * RET_QK_W + 2 * RET_V_W
    for g in range(N_GROUPS):
        for kind, epi in enumerate(("norm_q", "norm_k", "plain")):
            sched.append((att0 + kind * ATT_W + g * ATT_OUT_W, 1 + g, kind * ATT_OUT_W, epi))
    gate0 = att0 + 3 * ATT_W
    for k in range(2 * D_MODEL // TILE):
        sched.append((gate0 + ATT_OUT_W + k * TILE, 4, k * TILE, "plain"))
    sched.append((gate0, 4, 2 * D_MODEL, "plain"))
    return sched


_SCHEDULE = _proj_schedule()


def _in_proj_kernel(*refs, dils, per_seq, fuse_retention, n_tails, cast_weights):
    (x_ref, wn_ref, w_ref, ca_ref, sa_ref, cb_ref, sb_ref, cbs_ref, sbs_ref,
     qg_ref, kg_ref, seg_ref) = refs[:12]
    refs = refs[12:]
    if cast_weights:
        *refs, wf_ref, wb_ref, sem_in, sem_out = refs
        wbf_out_ref = refs.pop(-(4 if fuse_retention else 2) - 1)

        load_slots = wf_ref.shape[0]

        def tile_load(step):
            return pltpu.make_async_copy(w_ref.at[:, pl.ds(_SCHEDULE[step][0], TILE)],
                                         wf_ref.at[step % load_slots], sem_in.at[step % load_slots])

        def tile_store(step):
            return pltpu.make_async_copy(wb_ref.at[step % 2],
                                         wbf_out_ref.at[:, pl.ds(_SCHEDULE[step][0], TILE)], sem_out.at[step % 2])

        for ahead in range(load_slots - 1):
            tile_load(ahead).start()
    if fuse_retention:
        dec_ref, qd_ref, kd_ref, cd_ref, gn_ref, ret_ref, s_ref = refs[:7]
        refs = refs[7:]
    else:
        ret_ref = refs[0]
        refs = refs[1:]
    a0_ref, a1_ref, a2_ref, gate_ref = refs[:4]
    tail_refs = refs[4:4 + n_tails]
    h_ref, y_ref = refs[4 + n_tails:6 + n_tails]
    refs = refs[6 + n_tails:]
    r_ref = refs[0] if fuse_retention else ret_ref
    kt_ref = refs[1] if fuse_retention else None
    out_refs = (r_ref, a0_ref, a1_ref, a2_ref, gate_ref)

    if fuse_retention:
        @pl.when(pl.program_id(0) % per_seq == 0)
        def _():
            s_ref[...] = jnp.zeros_like(s_ref)

    x = x_ref[...]
    tm = x.shape[0]
    ms = jnp.mean(x * x, axis=-1, keepdims=True)
    h_ref[...] = (x * lax.rsqrt(ms + EPS) * wn_ref[...]).astype(BF16)

    ca, sa = ca_ref[0], sa_ref[0]
    cos = ca * cb_ref[...] - sa * sb_ref[...]
    sin = sa * cbs_ref[...] + ca * sbs_ref[...]

    def rotary(y, scale):
        parts = []
        for hh in range(TILE // RET_DK):
            yh = y[:, hh * RET_DK:(hh + 1) * RET_DK]
            parts.append((yh * cos + pltpu.roll(yh, RET_DK // 2, axis=1) * sin) * scale)
        return jnp.concatenate(parts, axis=1)

    def head_rms(y, gain):
        y2 = (y * y).astype(BF16)
        half = TILE // 2
        ms = jnp.concatenate([_dot(y2[:, :half], seg_ref[...]), _dot(y2[:, half:], seg_ref[...])], axis=1)
        return y * lax.rsqrt(ms + EPS) * gain

    epilogues = {
        "plain": lambda y: y,
        "rot_q": lambda y: rotary(y, 1.0),
        "rot_k": lambda y: rotary(y, RET_DK ** -0.5),
        "norm_q": lambda y: head_rms(y, qg_ref[...]),
        "norm_k": lambda y: head_rms(y, kg_ref[...]),
    }

    kt_base = jnp.minimum(pl.program_id(0), 0)

    def retention_issue(c, h):
        rows = slice(c * RET_CHUNK, (c + 1) * RET_CHUNK)
        ks = slice(h * RET_DK, (h + 1) * RET_DK)
        ks2 = slice(RET_QK_W + h * RET_DK, RET_QK_W + (h + 1) * RET_DK)
        vs_in = slice(2 * RET_QK_W + h * RET_DV, 2 * RET_QK_W + (h + 1) * RET_DV)
        qb = r_ref[rows, ks].astype(BF16)
        vb = r_ref[rows, vs_in].astype(BF16)
        state = s_ref[0, h]
        scores = (_nt_dot(qb, r_ref[rows, ks2].astype(BF16)) * dec_ref[h]).astype(BF16)
        carried = _dot(qb, state.astype(BF16)) * qd_ref[h]
        s_ref[0, h] = state * cd_ref[h] + _dot(kt_ref[kt_base + c, ks, :], vb)
        return rows, h, scores, vb, carried

    def retention_finish(rows, h, scores, vb, carried):
        vs = slice(h * RET_DV, (h + 1) * RET_DV)
        gs_in = slice(2 * RET_QK_W + RET_V_W + h * RET_DV, 2 * RET_QK_W + RET_V_W + (h + 1) * RET_DV)
        o = _dot(scores, vb) + carried
        ret_ref[rows, vs] = _group_norm_gate(o, gn_ref[:, vs], r_ref[rows, gs_in]).astype(ret_ref.dtype)

    def store_tile(val, out_idx, out_col):
        o_ref = out_refs[out_idx]
        ocs = slice(out_col, out_col + TILE)
        if not 1 <= out_idx <= N_GROUPS:
            o_ref[:, ocs] = val.astype(o_ref.dtype)
            return
        g, kind = out_idx - 1, out_col // ATT_OUT_W
        if kind > 0 and tail_refs:
            t_ref = tail_refs[g]
            t_ref[0, kind - 1] = jnp.transpose(val)[:, tm - t_ref.shape[3]:]
        dil = dils[g]
        if dil == 1:
            o_ref[0, 0, :, ocs] = val.astype(o_ref.dtype)
            return
        for c in range(TILE // LANES):
            y_ref[c] = val[:, c * LANES:(c + 1) * LANES]
            for r in range(dil):
                o_ref[0, r, :, out_col + c * LANES:out_col + (c + 1) * LANES] = (
                    y_ref[c, pl.ds(r, tm // dil, stride=dil), :].astype(o_ref.dtype))

    n_ret_tiles = sum(1 for t in _SCHEDULE if t[1] == 0)
    units = [(c, h) for c in range(tm // RET_CHUNK) for h in range(RET_HEADS)] if fuse_retention else []
    assert len(units) <= len(_SCHEDULE) - n_ret_tiles
    for step, (w_col, out_idx, out_col, epi) in enumerate(_SCHEDULE):
        if fuse_retention and step == 2:
            for c, h in units:
                rows = slice(c * RET_CHUNK, (c + 1) * RET_CHUNK)
                kt_ref[c, h * RET_DK:(h + 1) * RET_DK, :] = _decayed_keys_t(
                    r_ref[rows, RET_QK_W + h * RET_DK:RET_QK_W + (h + 1) * RET_DK], kd_ref[h])
        unit = units[step - n_ret_tiles] if 0 <= step - n_ret_tiles < len(units) else None
        issued = retention_issue(*unit) if unit else None
        if cast_weights:
            tile_load(step).wait()
            if step >= 2:
                tile_store(step - 2).wait()
            wb_ref[step % 2] = wf_ref[step % load_slots].astype(BF16)
            w_tile = wb_ref[step % 2]
        else:
            w_tile = w_ref[:, w_col:w_col + TILE]
        store_tile(epilogues[epi](_dot(h_ref[...], w_tile)), out_idx, out_col)
        if unit:
            retention_finish(*issued)
        if cast_weights:
            tile_store(step).start()
            if step + load_slots - 1 < len(_SCHEDULE):
                tile_load(step + load_slots - 1).start()
    if cast_weights:
        tile_store(len(_SCHEDULE) - 2).wait()
        tile_store(len(_SCHEDULE) - 1).wait()


def _in_proj(x2d, seq_len, dils, tail_rows, out_dtype, wn, w_bf, rot, qg, kg, seg, tm, retention=None):
    n = x2d.shape[0]
    batch = n // seq_len
    per_seq = seq_len // tm
    fuse = retention is not None
    cast_weights = w_bf.dtype == F32
    assert not cast_weights or n == tm
    const = lambda a: pl.BlockSpec(a.shape, lambda i, nd=a.ndim: (0,) * nd)
    base_spec = pl.BlockSpec((1, 1, RET_DK), lambda i: (i % per_seq, 0, 0))
    qkv_spec = lambda d: pl.BlockSpec((1, d, tm // d, QKV_W), lambda i: (i // per_seq, 0, i % per_seq, 0))
    tail_specs, tail_shapes = [], []
    for rows in tail_rows:
        width = min(tm, rows)
        first = (seq_len - rows) // tm if rows >= tm else per_seq
        tail_specs.append(pl.BlockSpec(
            (1, 2, ATT_OUT_W, width),
            lambda i, first=first: (i // per_seq, 0, 0, jnp.maximum(i % per_seq - first, 0))))
        tail_shapes.append(jax.ShapeDtypeStruct((batch, 2, ATT_OUT_W, rows), F32))
    ca, sa, *offset_tables = rot
    args = [x2d, wn, w_bf, ca, sa, *offset_tables, qg, kg, seg]
    in_specs = [
        pl.BlockSpec((tm, D_MODEL), lambda i: (i, 0)),
        const(wn),
        pl.BlockSpec(memory_space=pl.ANY) if cast_weights
        else pl.BlockSpec(w_bf.shape, lambda i: (0, 0), pipeline_mode=pl.Buffered(1)),
        base_spec, base_spec, *[const(t) for t in offset_tables],
        const(qg), const(kg), const(seg),
    ]
    scratch = [pltpu.VMEM((tm, D_MODEL), BF16), pltpu.VMEM((TILE // LANES, tm, LANES), F32)]
    if fuse:
        tables, gn = retention
        args += [*tables, gn]
        in_specs += [const(t) for t in (*tables, gn)]
        ret_specs = [pl.BlockSpec((tm, RET_V_W), lambda i: (i, 0)),
                     pl.BlockSpec((1, RET_HEADS, RET_DK, RET_DV), lambda i: (i // per_seq, 0, 0, 0))]
        ret_shapes = [jax.ShapeDtypeStruct((n, RET_V_W), out_dtype),
                      jax.ShapeDtypeStruct((batch, RET_HEADS, RET_DK, RET_DV), F32)]
        scratch += [pltpu.VMEM((tm, RET_OUT_W), F32), pltpu.VMEM((tm // RET_CHUNK, RET_QK_W, RET_CHUNK), BF16)]
    else:
        ret_specs = [pl.BlockSpec((tm, RET_OUT_W), lambda i: (i, 0))]
        ret_shapes = [jax.ShapeDtypeStruct((n, RET_OUT_W), out_dtype)]
    cast_specs, cast_shapes = [], []
    if cast_weights:
        cast_specs = [pl.BlockSpec(memory_space=pl.ANY)]
        cast_shapes = [jax.ShapeDtypeStruct(w_bf.shape, BF16)]
        scratch += [pltpu.VMEM((W_LOAD_SLOTS, D_MODEL, TILE), F32), pltpu.VMEM((2, D_MODEL, TILE), BF16),
                    pltpu.SemaphoreType.DMA((W_LOAD_SLOTS,)), pltpu.SemaphoreType.DMA((2,))]
    return pl.pallas_call(
        functools.partial(_in_proj_kernel, dils=dils, per_seq=per_seq, fuse_retention=fuse,
                          n_tails=len(tail_rows), cast_weights=cast_weights),
        grid=(n // tm,),
        in_specs=in_specs,
        out_specs=ret_specs + [qkv_spec(d) for d in dils]
        + [pl.BlockSpec((tm, GATE_W), lambda i: (i, 0))] + tail_specs + cast_specs,
        out_shape=ret_shapes
        + [jax.ShapeDtypeStruct((batch, d, seq_len // d, QKV_W), out_dtype) for d in dils]
        + [jax.ShapeDtypeStruct((n, GATE_W), out_dtype)] + tail_shapes + cast_shapes,
        scratch_shapes=scratch,
        compiler_params=pltpu.CompilerParams(
            dimension_semantics=("arbitrary",), vmem_limit_bytes=VMEM_LIMIT),
        name="in_proj",
    )(*args)


def _group_norm_gate(o, gain, gate):
    gate = gate.astype(F32)
    mu = jnp.mean(o, axis=-1, keepdims=True)
    d = o - mu
    var = jnp.mean(d * d, axis=-1, keepdims=True)
    return gate * jax.nn.sigmoid(gate) * (d * lax.rsqrt(var + EPS) * gain)


def _decayed_keys_t(k_keys, kd):
    return jnp.transpose(k_keys * kd).astype(BF16)


def _retention_specs(rows, idx):
    return [
        pl.BlockSpec((1, rows, RET_QK_W), lambda *a: (*idx(*a), 0)),
        pl.BlockSpec((1, rows, RET_QK_W), lambda *a: (*idx(*a), 1)),
        pl.BlockSpec((1, rows, RET_V_W), lambda *a: (*idx(*a), 1)),
        pl.BlockSpec((1, rows, RET_V_W), lambda *a: (*idx(*a), 2)),
    ]


def _table_specs(tables, ndim_grid):
    return [pl.BlockSpec(t.shape, lambda *a, nd=t.ndim: (0,) * nd) for t in tables]


def _retention_sample_kernel(q_ref, k_ref, v_ref, g_ref, s_in_ref, dec_ref, qd_ref, kd_ref, cd_ref,
                             gn_ref, o_ref, s_out_ref, kpad_ref, vpad_ref):
    @pl.when(pl.program_id(0) == 0)
    def _():
        kpad_ref[...] = jnp.zeros_like(kpad_ref)
        vpad_ref[...] = jnp.zeros_like(vpad_ref)

    kpad_ref[0:SAMPLE_PAD, :] = k_ref[0]
    vpad_ref[0:SAMPLE_PAD, :] = v_ref[0]
    issued = []
    for h in range(RET_HEADS):
        ks = slice(h * RET_DK, (h + 1) * RET_DK)
        vs = slice(h * RET_DV, (h + 1) * RET_DV)
        qb = q_ref[0, :, ks].astype(BF16)
        vb = vpad_ref[:, vs].astype(BF16)
        state = s_in_ref[0, h]
        scores = (_nt_dot(qb, kpad_ref[:, ks].astype(BF16)) * dec_ref[h]).astype(BF16)
        carried = _dot(qb, state.astype(BF16)) * qd_ref[h]
        s_out_ref[0, h] = state * cd_ref[h] + _dot(_decayed_keys_t(kpad_ref[:, ks], kd_ref[h]), vb)
        issued.append((scores, vb, carried))
    for h, (scores, vb, carried) in enumerate(issued):
        vs = slice(h * RET_DV, (h + 1) * RET_DV)
        o = _dot(scores, vb) + carried
        o_ref[0, :, vs] = _group_norm_gate(o, gn_ref[:, vs], g_ref[0, :, vs]).astype(o_ref.dtype)


def _attn_prompt_blocks(q_ref, kp_ref, kc_ref, vp_ref, vc_ref, bias_ref, o_ref, lse_ref, first):
    n_seq, n_blocks = q_ref.shape[1], q_ref.shape[2] // ATT_BLOCK
    assert first or n_seq == 1
    pair_w = 2 * ATT_DH
    low_q = lax.broadcasted_iota(jnp.int32, (ATT_BLOCK, pair_w), 1) < ATT_DH
    rows0, rows1 = slice(0, ATT_BLOCK), slice(ATT_BLOCK, 2 * ATT_BLOCK)

    def keys(prev_ref, cur_ref, sq, t, ps):
        if t > 0:
            return cur_ref[0, sq, (t - 1) * ATT_BLOCK:(t + 1) * ATT_BLOCK, ps]
        if first:
            return cur_ref[0, sq, 0:ATT_BLOCK, ps]
        return jnp.concatenate([prev_ref[0, sq, :, ps], cur_ref[0, sq, 0:ATT_BLOCK, ps]], axis=0)

    staged = []
    for sq, t in [(sq, t) for sq in range(n_seq) for t in range(n_blocks)]:
        for p in range(ATT_HEADS // 2):
            ps = slice(p * pair_w, (p + 1) * pair_w)
            qp = q_ref[0, sq, t * ATT_BLOCK:(t + 1) * ATT_BLOCK, ps]
            kcat = keys(kp_ref, kc_ref, sq, t, ps)
            q2 = jnp.concatenate([jnp.where(low_q, qp, 0.0), jnp.where(low_q, 0.0, qp)], axis=0)
            s = _nt_dot(q2, kcat) + bias_ref[p, :, 2 * ATT_BLOCK - kcat.shape[0]:]
            s = s.astype(BF16)
            m = jnp.max(s, axis=-1, keepdims=True)
            staged.append((sq, t, p, m.astype(F32), jnp.exp(s - m)))
    for sq, t, p, m, e in staged:
        ps = slice(p * pair_w, (p + 1) * pair_w)
        vcat = keys(vp_ref, vc_ref, sq, t, ps)
        low_k = lax.broadcasted_iota(jnp.int32, vcat.shape, 1) < ATT_DH
        p0 = _dot(e[rows0], jnp.where(low_k, vcat, 1.0))
        p1 = _dot(e[rows1], jnp.where(low_k, 1.0, vcat))
        den = pltpu.roll(jnp.where(low_q, p1, p0), ATT_DH, axis=1)
        ts = slice(t * ATT_BLOCK, (t + 1) * ATT_BLOCK)
        o_ref[0, sq, ts, ps] = jnp.where(low_q, p0, p1) / den
        lse_ref[0, sq, ts, ps] = jnp.where(low_q, m[rows0], m[rows1]) + jnp.log(den)


def _attn_prompt_kernel(q_ref, kp_ref, kc_ref, vp_ref, vc_ref, slot_ref, o_ref, lse_ref, bias_ref,
                        *, whole_sequences):
    first = pl.program_id(2) == 0

    @pl.when(jnp.logical_and(first, jnp.logical_and(pl.program_id(0) == 0, pl.program_id(1) == 0)))
    def _():
        for h in range(ATT_HEADS):
            slots = jnp.broadcast_to(slot_ref[h:h + 1, :], (ATT_BLOCK, slot_ref.shape[1]))
            rows = pltpu.roll(slots, 0, 1, stride=1, stride_axis=0)
            bias_ref[h // 2, (h % 2) * ATT_BLOCK:(h % 2 + 1) * ATT_BLOCK, :] = rows[:, :2 * ATT_BLOCK]

    if whole_sequences:
        _attn_prompt_blocks(q_ref, kp_ref, kc_ref, vp_ref, vc_ref, bias_ref, o_ref, lse_ref, True)
        return

    @pl.when(first)
    def _():
        _attn_prompt_blocks(q_ref, kp_ref, kc_ref, vp_ref, vc_ref, bias_ref, o_ref, lse_ref, True)

    @pl.when(jnp.logical_not(first))
    def _():
        _attn_prompt_blocks(q_ref, kp_ref, kc_ref, vp_ref, vc_ref, bias_ref, o_ref, lse_ref, False)


def _attn_prompt_group(qkv, slot_bias, step_blocks):
    b, dil, tr, _ = qkv.shape
    n_blocks = min(step_blocks, tr // ATT_BLOCK)
    n_seq = min(step_blocks // n_blocks, dil)
    rows = n_blocks * ATT_BLOCK
    cur = lambda c: pl.BlockSpec((1, n_seq, rows, ATT_OUT_W), lambda bi, r, j: (bi, r, j, c))
    prev = lambda c: pl.BlockSpec((1, n_seq, ATT_BLOCK, ATT_OUT_W),
                                  lambda bi, r, j: (bi, r, jnp.maximum(j * n_blocks - 1, 0), c))
    res_shape = jax.ShapeDtypeStruct((b, dil, tr, ATT_OUT_W), F32)
    return pl.pallas_call(
        functools.partial(_attn_prompt_kernel, whole_sequences=tr == rows),
        grid=(b, dil // n_seq, tr // rows),
        in_specs=[cur(0), prev(1), cur(1), prev(2), cur(2),
                  pl.BlockSpec(slot_bias.shape, lambda bi, r, j: (0, 0))],
        out_specs=[cur(0), cur(0)],
        out_shape=[res_shape, res_shape],
        scratch_shapes=[pltpu.VMEM((ATT_HEADS // 2, 2 * ATT_BLOCK, 2 * ATT_BLOCK), F32)],
        compiler_params=pltpu.CompilerParams(
            dimension_semantics=("arbitrary", "arbitrary", "arbitrary"), vmem_limit_bytes=VMEM_LIMIT),
        name=f"attn_prompt_d{dil}",
    )(qkv, qkv, qkv, qkv, qkv, slot_bias)


def _attn_sample_kernel(a0_ref, a1_ref, a2_ref, c0_ref, c1_ref, c2_ref, s0_ref, s1_ref, s2_ref, sn_ref,
                        o_ref, kn_ref, vn_ref, b0_ref, b1_ref, b2_ref, bn_ref, *, slot):
    qkv_refs = (a0_ref, a1_ref, a2_ref)
    cache_refs = (c0_ref, c1_ref, c2_ref)
    bias_refs = (b0_ref, b1_ref, b2_ref)

    @pl.when(pl.program_id(0) == 0)
    def _():
        kn_ref[...] = jnp.zeros_like(kn_ref)
        vn_ref[...] = jnp.zeros_like(vn_ref)
        def rotated_rows(vec, width):
            rows = jnp.broadcast_to(vec, (SAMPLE_PAD, vec.shape[1]))
            return pltpu.roll(rows, 0, 1, stride=1, stride_axis=0)[:, :width]
        for g, (s_ref, b_ref) in enumerate(zip((s0_ref, s1_ref, s2_ref), bias_refs)):
            for h in range(ATT_HEADS):
                b_ref[h] = rotated_rows(s_ref[h:h + 1, :], b_ref.shape[2])
                bn_ref[g, h] = rotated_rows(sn_ref[g, h:h + 1, :], ATT_BLOCK)

    for g in range(N_GROUPS):
        gs = slice(g * ATT_OUT_W, (g + 1) * ATT_OUT_W)
        kn_ref[0:SAMPLE_PAD, gs] = qkv_refs[g][0, :, ATT_OUT_W:2 * ATT_OUT_W]
        vn_ref[0:SAMPLE_PAD, gs] = qkv_refs[g][0, :, 2 * ATT_OUT_W:3 * ATT_OUT_W]

    staged = []
    for h in range(ATT_HEADS):
        logits = []
        for g in range(N_GROUPS):
            hs = slice(g * ATT_OUT_W + h * ATT_DH, g * ATT_OUT_W + (h + 1) * ATT_DH)
            qh = qkv_refs[g][0, :, h * ATT_DH:(h + 1) * ATT_DH].astype(BF16)
            logits.append(_dot(qh, cache_refs[g][slot, 0, h].astype(BF16)) + bias_refs[g][h])
            logits.append(_nt_dot(qh, kn_ref[:, hs].astype(BF16)) + bn_ref[g, h])
        m = functools.reduce(jnp.maximum, [jnp.max(x, axis=-1, keepdims=True) for x in logits])
        es = [jnp.exp(x - m) for x in logits]
        l = functools.reduce(jnp.add, [jnp.sum(e, axis=-1, keepdims=True) for e in es])
        staged.append(([e.astype(BF16) for e in es], l))
    for h, (es, l) in enumerate(staged):
        acc = jnp.zeros((SAMPLE_PAD, ATT_DH), F32)
        for g in range(N_GROUPS):
            hs = slice(g * ATT_OUT_W + h * ATT_DH, g * ATT_OUT_W + (h + 1) * ATT_DH)
            acc = acc + _nt_dot(es[2 * g], cache_refs[g][slot, 1, h].astype(BF16))
            acc = acc + _dot(es[2 * g + 1], vn_ref[:, hs].astype(BF16))
        o_ref[0, :, h * ATT_DH:(h + 1) * ATT_DH] = acc / l


_N_RET_IN, _N_ATT_IN = 10, 10


def _sample_mixers_kernel(*refs):
    n_in = _N_RET_IN + _N_ATT_IN
    ret_in, att_in = refs[:_N_RET_IN], refs[_N_RET_IN:n_in]
    ret_o_ref, state_o_ref, att_o_ref = refs[n_in:n_in + 3]
    kv_o_refs = refs[n_in + 3:n_in + 3 + N_GROUPS]
    kpad_ref, vpad_ref, kn_ref, vn_ref = refs[n_in + 3 + N_GROUPS:n_in + 7 + N_GROUPS]
    *bias_refs, sem = refs[n_in + 7 + N_GROUPS:]
    ring_refs, bias_refs = bias_refs[N_GROUPS + 1:], bias_refs[:N_GROUPS + 1]
    cache_refs = att_in[N_GROUPS:2 * N_GROUPS]

    depth = ring_refs[0].shape[0]
    step, n_steps = pl.program_id(0), pl.num_programs(0)

    def cache_copy(row, g):
        return pltpu.make_async_copy(cache_refs[g].at[row], ring_refs[g].at[row % depth], sem.at[row % depth, g])

    @pl.when(step == 0)
    def _():
        for row in range(depth - 1):
            for g in range(N_GROUPS):
                cache_copy(row, g).start()

    @pl.when(step + depth - 1 < n_steps)
    def _():
        for g in range(N_GROUPS):
            cache_copy(step + depth - 1, g).start()

    _retention_sample_kernel(*ret_in, ret_o_ref, state_o_ref, kpad_ref, vpad_ref)
    for qkv_ref, kv_ref in zip(att_in[:N_GROUPS], kv_o_refs):
        for tok in range(kv_ref.shape[1]):
            for kv in range(2):
                for h in range(ATT_HEADS):
                    col = (1 + kv) * ATT_OUT_W + h * ATT_DH
                    kv_ref[0, tok, kv, h:h + 1, :] = qkv_ref[0, tok:tok + 1, col:col + ATT_DH]
    for g in range(N_GROUPS):
        cache_copy(step, g).wait()
    _attn_sample_kernel(*att_in[:N_GROUPS], *ring_refs, *att_in[2 * N_GROUPS:], att_o_ref, kn_ref, vn_ref,
                        *bias_refs, slot=step % depth)


def _sample_mixers(ret, state, tables, gn, qkvs, caches_t, slots, slots_new, n_tokens):
    b, p, _ = ret.shape
    state_spec = pl.BlockSpec((1, RET_HEADS, RET_DK, RET_DV), lambda bi: (bi, 0, 0, 0))
    ret_specs = _retention_specs(p, lambda bi: (bi, 0)) + [state_spec] + _table_specs(tables + (gn,), 1)
    att_specs = ([pl.BlockSpec((1, p, QKV_W), lambda bi: (bi, 0, 0)) for _ in qkvs]
                 + [pl.BlockSpec(memory_space=pl.ANY) for _ in caches_t]
                 + [pl.BlockSpec(x.shape, lambda bi: (0, 0)) for x in slots]
                 + [pl.BlockSpec(slots_new.shape, lambda bi: (0, 0, 0))])
    assert len(ret_specs) == _N_RET_IN and len(att_specs) == _N_ATT_IN
    kv_shape = (b, n_tokens, 2, ATT_HEADS, ATT_DH)
    return pl.pallas_call(
        _sample_mixers_kernel,
        grid=(b,),
        in_specs=ret_specs + att_specs,
        out_specs=[pl.BlockSpec((1, p, RET_V_W), lambda bi: (bi, 0, 0)), state_spec,
                   pl.BlockSpec((1, p, ATT_OUT_W), lambda bi: (bi, 0, 0))]
        + [pl.BlockSpec((1,) + kv_shape[1:], lambda bi: (bi, 0, 0, 0, 0))] * N_GROUPS,
        out_shape=[jax.ShapeDtypeStruct((b, p, RET_V_W), ret.dtype), jax.ShapeDtypeStruct(state.shape, F32),
                   jax.ShapeDtypeStruct((b, p, ATT_OUT_W), F32)]
        + [jax.ShapeDtypeStruct(kv_shape, F32)] * N_GROUPS,
        scratch_shapes=[pltpu.VMEM((RET_CHUNK, RET_QK_W), F32), pltpu.VMEM((RET_CHUNK, RET_V_W), F32),
                        pltpu.VMEM((ATT_BLOCK, ATT_W), F32), pltpu.VMEM((ATT_BLOCK, ATT_W), F32)]
        + [pltpu.VMEM((ATT_HEADS, p, c.shape[-1]), F32) for c in caches_t]
        + [pltpu.VMEM((N_GROUPS, ATT_HEADS, p, ATT_BLOCK), F32)]
        + [pltpu.VMEM((CACHE_RING,) + c.shape[1:], c.dtype) for c in caches_t]
        + [pltpu.SemaphoreType.DMA((CACHE_RING, N_GROUPS))],
        compiler_params=pltpu.CompilerParams(
            dimension_semantics=("arbitrary",), vmem_limit_bytes=VMEM_LIMIT),
        name="sample_mixers",
    )(ret, ret, ret, ret, state, *tables, gn, *qkvs, *caches_t, *slots, slots_new)


def _out_proj_kernel(*refs, dils):
    x_ref, ret_ref, ga_ref, gb_ref, ag_ref, wr_ref, wa_ref, wo_ref = refs[:8]
    if dils is None:
        att_ref, o_ref = refs[8:]
        att = att_ref[0]
    else:
        group_refs = refs[8:8 + 2 * N_GROUPS]
        o_ref = refs[8 + 2 * N_GROUPS]
        scratch = refs[9 + 2 * N_GROUPS:]
        tm = x_ref.shape[1]
        os, lses = [], []
        for g, dil in enumerate(dils):
            og_ref, lg_ref = group_refs[2 * g], group_refs[2 * g + 1]
            if dil == 1:
                os.append(og_ref[0, 0]); lses.append(lg_ref[0, 0])
                continue
            so_ref, sl_ref = scratch[2 * (g - 1)], scratch[2 * (g - 1) + 1]
            n_chunks = ATT_OUT_W // LANES
            for c in range(n_chunks):
                cs = slice(c * LANES, (c + 1) * LANES)
                for r in range(dil):
                    so_ref[c, pl.ds(r, tm // dil, stride=dil), :] = og_ref[0, r, :, cs]
                    sl_ref[c, pl.ds(r, tm // dil, stride=dil), :] = lg_ref[0, r, :, cs]
            os.append(jnp.concatenate([so_ref[c] for c in range(n_chunks)], axis=1))
            lses.append(jnp.concatenate([sl_ref[c] for c in range(n_chunks)], axis=1))
        mx = functools.reduce(jnp.maximum, lses)
        ws = [jnp.exp(l - mx) for l in lses]
        att = functools.reduce(jnp.add, [w * o for w, o in zip(ws, os)]) / functools.reduce(jnp.add, ws)
    ag = ag_ref[0].astype(F32)
    u = (ag * jax.nn.sigmoid(ag) * att).astype(BF16)
    o_b = _dot(u, wa_ref[...])
    o_a = _dot(ret_ref[0].astype(BF16), wr_ref[...])
    merged = jax.nn.sigmoid(ga_ref[0].astype(F32)) * o_a + jax.nn.sigmoid(gb_ref[0].astype(F32)) * o_b
    o_ref[0] = x_ref[0] + _dot(merged.astype(BF16), wo_ref[...])


def _out_proj(x, ret, gates, wr, wa, wo, tm, att=None, groups=None):
    b, t, _ = x.shape
    row = lambda w, c: pl.BlockSpec((1, tm, w), lambda bi, i: (bi, i, c))
    full = lambda a: pl.BlockSpec(a.shape, lambda bi, i: (0, 0))
    in_specs = [row(D_MODEL, 0), row(RET_V_W, 0), row(D_MODEL, 0), row(D_MODEL, 1),
                row(ATT_OUT_W, 2 * D_MODEL // ATT_OUT_W), full(wr), full(wa), full(wo)]
    args = [x, ret, gates, gates, gates, wr, wa, wo]
    scratch = []
    if groups is None:
        dils = None
        in_specs.append(row(ATT_OUT_W, 0))
        args.append(att)
    else:
        dils = tuple(o.shape[1] for o, _ in groups)
        for (o, lse), d in zip(groups, dils):
            spec = pl.BlockSpec((1, d, tm // d, ATT_OUT_W), lambda bi, i: (bi, 0, i, 0))
            in_specs += [spec, spec]
            args += [o, lse]
            if d > 1:
                scratch += [pltpu.VMEM((ATT_OUT_W // LANES, tm, LANES), F32)] * 2
    return pl.pallas_call(
        functools.partial(_out_proj_kernel, dils=dils),
        grid=(b, t // tm),
        in_specs=in_specs,
        out_specs=row(D_MODEL, 0),
        out_shape=jax.ShapeDtypeStruct((b, t, D_MODEL), F32),
        scratch_shapes=scratch,
        compiler_params=pltpu.CompilerParams(
            dimension_semantics=("parallel", "parallel"), vmem_limit_bytes=VMEM_LIMIT),
        name="out_proj",
    )(*args)


def _rotary_tables(bases, offsets):
    half = RET_DK // 2
    inv = ROPE_BASE ** (-jnp.arange(half, dtype=F32) / half)
    inv2 = jnp.concatenate([inv, inv])
    sign = jnp.concatenate([-jnp.ones((half,), F32), jnp.ones((half,), F32)])
    a = bases.astype(F32)[:, None, None] * inv2
    b = offsets.astype(F32)[:, None] * inv2
    cb, sb = jnp.cos(b), jnp.sin(b)
    return jnp.cos(a), jnp.sin(a), cb, sb, sign * cb, sign * sb


def _retention_tables(c, rows):
    log_g = jnp.log1p(-(2.0 ** (-5.0 - jnp.arange(RET_HEADS, dtype=F32))))
    i = jnp.arange(c, dtype=F32)
    diff = i[:, None] - i[None, :]
    decay = jnp.where(diff[None] >= 0, jnp.exp(jnp.maximum(diff, 0.0)[None] * log_g[:, None, None]), 0.0)
    q_decay = jnp.exp((i + 1.0)[None, :] * log_g[:, None])
    k_decay = jnp.exp((c - 1.0 - i)[None, :] * log_g[:, None])
    chunk_decay = jnp.exp(c * log_g)
    dec = jnp.zeros((RET_HEADS, rows, RET_CHUNK), F32).at[:, :c, :c].set(decay)
    qd = jnp.zeros((RET_HEADS, rows, 1), F32).at[:, :c, 0].set(q_decay)
    kd = jnp.zeros((RET_HEADS, RET_CHUNK, 1), F32).at[:, :c, 0].set(k_decay)
    cd = jnp.broadcast_to(chunk_decay[:, None, None], (RET_HEADS, 1, RET_DV))
    return dec, qd, kd, cd


def _t5_bucket(dist):
    max_exact = REL_BUCKETS // 2
    d = jnp.maximum(dist.astype(F32), 1.0)
    large = max_exact + (jnp.log(d / max_exact) / math.log(REL_MAX_DIST / max_exact)
                         * (REL_BUCKETS - max_exact)).astype(jnp.int32)
    large = jnp.minimum(large, REL_BUCKETS - 1)
    return jnp.where(dist < max_exact, dist, large)


def _group_bias(rel_bias, g, dil, slots):
    dist = dil * jnp.asarray(slots, dtype=jnp.int32)
    return rel_bias[_t5_bucket(dist)][:, g * ATT_HEADS:(g + 1) * ATT_HEADS].astype(F32).T


def _neg(heads, n):
    return jnp.full((heads, n), NEG, F32)


def _dilate(v, dil):
    heads, n = v.shape
    return jnp.stack([v] + [_neg(heads, n)] * (dil - 1), axis=-1).reshape(heads, n * dil)


def _prompt_slots(tb_rev):
    heads = tb_rev.shape[0]
    return jnp.concatenate([tb_rev, _neg(heads, 4 * ATT_BLOCK - N_KEYS)], axis=1)


def _sample_slots(tb, tb_rev, win, dil):
    heads = tb.shape[0]
    s_c = jnp.concatenate([_dilate(tb_rev[:, :N_KEYS - 1], dil), _neg(heads, ATT_BLOCK)], axis=1)
    back = [tb[:, k // dil:k // dil + 1] if k % dil == 0 else _neg(heads, 1) for k in range(SAMPLE_PAD - 1, 0, -1)]
    s_n = jnp.concatenate([tb[:, 0:1], _neg(heads, 2 * ATT_BLOCK - SAMPLE_PAD)] + back, axis=1)
    return s_c, s_n


def _kv_rows_t(tail_t):
    b, _, _, rows = tail_t.shape
    return jnp.transpose(tail_t.reshape(b, 2, ATT_HEADS, ATT_DH, rows), (0, 4, 1, 2, 3))


def kernel(x_prompt, x_sample, cache_kv_w128, cache_kv_w512, cache_kv_w2048, state_retention,
           w_norm, w_in, q_norm, k_norm, rel_bias, ret_norm, w_proj_ret, w_proj_att, w_out):
    assert w_in.shape[0] == 1
    bp, t, _ = x_prompt.shape
    bs, ts, _ = x_sample.shape
    dils = tuple(d for _, d in ATT_GROUPS)
    assert t % (ATT_BLOCK * max(dils)) == 0 and ts <= SAMPLE_PAD
    caches = (cache_kv_w128[0], cache_kv_w512[0], cache_kv_w2048[0])
    for cch, (win, _) in zip(caches, ATT_GROUPS):
        assert cch.shape[1] == win and win <= PAST_LEN

    wn = w_norm[0].reshape(1, D_MODEL)
    qg =jnp.tile(q_norm[0] * (ATT_DH ** -0.5), TILE // ATT_DH).reshape(1, TILE)
    kg = jnp.tile(k_norm[0], TILE // ATT_DH).reshape(1, TILE)
    gn = ret_norm[0].reshape(1, RET_V_W)
    wr = w_proj_ret[0].astype(BF16)
    wa = w_proj_att[0].astype(BF16)
    wo = w_out[0].astype(BF16)
    hid = jnp.arange(TILE // 2) // ATT_DH
    seg_mean = jnp.where(hid[:, None] == hid[None, :], 1.0 / ATT_DH, 0.0).astype(BF16)
    asc, desc = tuple(range(N_KEYS)), tuple(range(N_KEYS - 1, -1, -1))
    group_bias = [(_group_bias(rel_bias, g, d, asc), _group_bias(rel_bias, g, d, desc)) for g, d in enumerate(dils)]

    pad = SAMPLE_PAD
    ns = bs * pad
    xs = jnp.pad(x_sample, ((0, 0), (0, pad - ts), (0, 0))).reshape(ns, D_MODEL)
    rot_s = _rotary_tables(jnp.full((1,), PAST_LEN), jnp.tile(jnp.arange(pad), bs))
    ret_in_s, *qkv_s, gates_s, w_in_bf = _in_proj(xs, ns, (1,) * N_GROUPS, (), F32, wn, w_in[0], rot_s,
                                                  qg, kg, seg_mean, tm=ns)

    tm_p = 256
    rot_p = _rotary_tables(jnp.arange(0, t, tm_p), jnp.arange(tm_p))
    tail_p_rows = tuple(min(w, t) for w, _ in ATT_GROUPS)
    ret_p, state_p, *rest = _in_proj(
        x_prompt.reshape(bp * t, D_MODEL), t, dils, tail_p_rows, BF16, wn, w_in_bf, rot_p, qg, kg, seg_mean,
        tm=tm_p, retention=(_retention_tables(RET_CHUNK, RET_CHUNK), gn))
    qkv_p, gates_p, tails_p = rest[:N_GROUPS], rest[N_GROUPS], rest[N_GROUPS + 1:]
    groups = [_attn_prompt_group(qkv_p[g], _prompt_slots(group_bias[g][1]), step_blocks=16)
              for g in range(N_GROUPS)]
    y_p = _out_proj(x_prompt, ret_p.reshape(bp, t, RET_V_W), gates_p.reshape(bp, t, GATE_W), wr, wa, wo,
                    tm=512, groups=groups)

    qkv_s =[a.reshape(bs, pad, QKV_W) for a in qkv_s]
    caches_t = [jnp.transpose(c, (0, 2, 3, 4, 1)) for c in caches]
    ss = [_sample_slots(*group_bias[g], win, d) for g, (win, d) in enumerate(ATT_GROUPS)]
    ret_s, state_s, att_s, *kv_s = _sample_mixers(
        ret_in_s.reshape(bs, pad, RET_OUT_W), state_retention[0], _retention_tables(ts, pad), gn,
        qkv_s, caches_t, [c for c, _ in ss], jnp.stack([n for _, n in ss]), ts)
    y_s = _out_proj(xs.reshape(1, ns, D_MODEL), ret_s.reshape(1, ns, RET_V_W), gates_s.reshape(1, ns, GATE_W),
                    wr, wa, wo, tm=ns, att=att_s.reshape(1, ns, ATT_OUT_W))
    y_s = y_s.reshape(bs, pad, D_MODEL)[:, :ts]

    kv_p = [_kv_rows_t(tt)[None] for tt in tails_p]
    return (y_p, y_s, state_p[None], state_s[None], kv_p[0], kv_p[1], kv_p[2],
            kv_s[0][None], kv_s[1][None], kv_s[2][None])
```
